```python
import math
import jax
import jax.numpy as jnp
from jax import lax
import numpy as np

D_MODEL = 1024
BATCH = 4
SEQ = 4096
DEPTH = 2

GRID_W = 64
RET_HEADS = 4
RET_DK = 64
RET_DV = 64
NAT_HEADS = 4
NAT_DIM = 64
NAT_KH = 8
NAT_KW = 16
HGRN_HEADS = 4
HGRN_DK = 64
HGRN_DV = 64
DIFF_HEADS = 4
DIFF_DQK = 64
DIFF_DV = 128
N_BRANCH = 4
CHUNK = 64
Q_BLOCK = 128
N_EXPERTS = 64
TOP_K = 8
N_GROUPS = 8
TOPK_GROUPS = 4
EXPERT_HIDDEN = 256
SHARED_HIDDEN = 256
ROUTED_SCALE = 2.5
MOE_BLOCK = 128
EPS = 1e-5
MASK_VALUE = -1e30
ALPHA = (2.0 * DEPTH) ** 0.25
BETA = (8.0 * DEPTH) ** -0.25

RET_QK = RET_HEADS * RET_DK
RET_V = RET_HEADS * RET_DV
NAT_W = NAT_HEADS * NAT_DIM
HGRN_K = HGRN_HEADS * HGRN_DK
HGRN_V = HGRN_HEADS * HGRN_DV
DIFF_QK = DIFF_HEADS * 2 * DIFF_DQK
DIFF_V = DIFF_HEADS * DIFF_DV
SPLITS = (RET_QK, RET_QK, RET_V, RET_V,
          NAT_W, NAT_W, NAT_W,
          HGRN_K, HGRN_K, HGRN_K, HGRN_V, HGRN_V,
          DIFF_QK, DIFF_QK, DIFF_V)
IN_WIDTH = sum(SPLITS)

kernel_name = 'hybrid_gated_encoder_block'


def layer_norm(x, g, b):
    xf = x.astype(jnp.float32)
    mu = jnp.mean(xf, axis=-1, keepdims=True)
    var = jnp.mean(jnp.square(xf - mu), axis=-1, keepdims=True)
    return ((xf - mu) * lax.rsqrt(var + EPS) * g + b).astype(x.dtype)


def rms_norm(x, g):
    xf = x.astype(jnp.float32)
    return (xf * lax.rsqrt(jnp.mean(jnp.square(xf), axis=-1, keepdims=True) + EPS) * g).astype(x.dtype)


def split_heads(a, n_heads):
    b, t, _ = a.shape
    return a.reshape(b, t, n_heads, -1).transpose(0, 2, 1, 3)


def merge_heads(a):
    b, h, t, d = a.shape
    return a.transpose(0, 2, 1, 3).reshape(b, t, h * d)


def split_columns(p):
    offsets = np.cumsum(SPLITS)[:-1].tolist()
    return jnp.split(p, offsets, axis=-1)


def chunk_recurrence(q, k, v, log_f):
    b, h, t, dk = q.shape
    dv = v.shape[-1]
    n = t // CHUNK

    def to_chunks(a):
        return a.reshape(b, h, n, CHUNK, a.shape[-1]).transpose(2, 0, 1, 3, 4)

    lower_tri = jnp.tril(jnp.ones((CHUNK, CHUNK), dtype=bool))[:, :, None]

    def step(state, inp):
        qc, kc, vc, fc = inp
        g = jnp.cumsum(fc.astype(jnp.float32), axis=-2)
        diff = g[..., :, None, :] - g[..., None, :, :]
        decay = jnp.where(lower_tri, jnp.exp(jnp.where(lower_tri, diff, 0.0)), 0.0)
        scores = jnp.einsum('bhik,bhjk,bhijk->bhij', qc, kc, decay)
        intra = jnp.einsum('bhij,bhjv->bhiv', scores, vc)
        inter = jnp.einsum('bhik,bhkv->bhiv', qc * jnp.exp(g), state)
        g_last = g[..., -1, :]
        new_state = (jnp.exp(g_last)[..., None] * state
                     + jnp.einsum('bhjk,bhjv->bhkv', kc * jnp.exp(g_last[..., None, :] - g), vc))
        return new_state, intra + inter

    s0 = jnp.zeros((b, h, dk, dv), jnp.float32)
    _, out = lax.scan(step, s0, (to_chunks(q), to_chunks(k), to_chunks(v), to_chunks(log_f)))
    return out.transpose(1, 2, 0, 3, 4).reshape(b, h, t, dv).astype(v.dtype)


def bidirectional_recurrence(q, k_fwd, k_bwd, v, log_f_fwd, log_f_bwd):
    rev = lambda a: jnp.flip(a, axis=2)
    fwd = chunk_recurrence(q, k_fwd, v, log_f_fwd)
    bwd = rev(chunk_recurrence(rev(q), rev(k_bwd), rev(v), rev(log_f_bwd)))
    return fwd + bwd


def retention(q, k, v, g, gn):
    b, t, _ = q.shape
    qh = split_heads(q, RET_HEADS)
    kh = split_heads(k, RET_HEADS) * (RET_DK ** -0.5)
    vh = split_heads(v, RET_HEADS)
    idx = jnp.arange(RET_HEADS, dtype=jnp.float32)
    shape = (b, RET_HEADS, t, RET_DK)
    log_fwd = jnp.broadcast_to(jnp.log1p(-jnp.exp2(-5.0 - 2.0 * idx))[None, :, None, None], shape)
    log_bwd = jnp.broadcast_to(jnp.log1p(-jnp.exp2(-6.0 - 2.0 * idx))[None, :, None, None], shape)
    o = bidirectional_recurrence(qh, kh, kh, vh, log_fwd, log_bwd)
    return merge_heads(rms_norm(o, gn[:, None, :])) * jax.nn.silu(g)


def neighbourhood_attention(q, k, v, rpb):
    b, t, _ = q.shape
    rows = t // GRID_W
    kh = min(NAT_KH, rows)
    qg = split_heads(q, NAT_HEADS).reshape(b, NAT_HEADS, rows, GRID_W, NAT_DIM)
    kg = split_heads(k, NAT_HEADS).reshape(b, NAT_HEADS, rows, GRID_W, NAT_DIM)
    vg = split_heads(v, NAT_HEADS).reshape(b, NAT_HEADS, rows, GRID_W, NAT_DIM)
    r = jnp.arange(rows)
    row_start = jnp.clip(r - kh // 2, 0, rows - kh)
    row_idx = row_start[:, None] + jnp.arange(kh)[None, :]
    k_blk = kg[:, :, row_idx]
    v_blk = vg[:, :, row_idx]
    c = jnp.arange(GRID_W)
    col_start = jnp.clip(c - NAT_KW // 2, 0, GRID_W - NAT_KW)
    col_mask = (c[None, :] >= col_start[:, None]) & (c[None, :] < col_start[:, None] + NAT_KW)
    dr = row_idx - r[:, None] + (NAT_KH - 1)
    dc = jnp.clip(c[None, :] - c[:, None], -(NAT_KW - 1), NAT_KW - 1) + (NAT_KW - 1)
    bias = rpb[:, dr[:, None, :, None], dc[None, :, None, :]]
    s = jnp.einsum('bhrqd,bhrkcd->bhrqkc', qg, k_blk).astype(jnp.float32) * (NAT_DIM ** -0.5) + bias
    s = jnp.where(col_mask[:, None, :], s, MASK_VALUE)
    p = jax.nn.softmax(s.reshape(b, NAT_HEADS, rows, GRID_W, kh * GRID_W), axis=-1)
    p = p.reshape(b, NAT_HEADS, rows, GRID_W, kh, GRID_W)
    p = jnp.where(col_mask[:, None, :], p, 0.0).astype(v.dtype)
    o = jnp.einsum('bhrqkc,bhrkcd->bhrqd', p, v_blk)
    return merge_heads(o.reshape(b, NAT_HEADS, t, NAT_DIM))


def hgrn2(q, f_fwd, f_bwd, i, g, lower, gn):
    def gate(f_pre, lb):
        f_pre = f_pre.astype(jnp.float32)
        f = lb + (1.0 - lb) * jax.nn.sigmoid(f_pre)
        log_f = jnp.log(f)
        k = (1.0 - lb) * jax.nn.sigmoid(-f_pre)
        return split_heads(k, HGRN_HEADS), split_heads(log_f, HGRN_HEADS)

    k_f, lf_f = gate(f_fwd, lower[0])
    k_b, lf_b = gate(f_bwd, lower[1])
    qh = split_heads(jax.nn.silu(q), HGRN_HEADS)
    vh = split_heads(i, HGRN_HEADS)
    o = bidirectional_recurrence(qh, k_f, k_b, vh, lf_f, lf_b)
    return merge_heads(rms_norm(o, gn[:, None, :])) * jax.nn.silu(g)


def diff_attention(q, k, v, lam_params, subln, layer):
    b, t, _ = q.shape
    lam_init = 0.8 - 0.6 * math.exp(-0.3 * layer)
    qh = split_heads(q, DIFF_HEADS)
    kh = split_heads(k, DIFF_HEADS)
    vh = split_heads(v, DIFF_HEADS)
    q1, q2 = qh[..., :DIFF_DQK], qh[..., DIFF_DQK:]
    k1, k2 = kh[..., :DIFF_DQK], kh[..., DIFF_DQK:]
    lam = (jnp.exp(jnp.sum(lam_params[0] * lam_params[1]))
           - jnp.exp(jnp.sum(lam_params[2] * lam_params[3])) + lam_init)
    slopes = jnp.exp2(-8.0 * (jnp.arange(DIFF_HEADS, dtype=jnp.float32) + 1.0) / DIFF_HEADS)
    scale = DIFF_DQK ** -0.5
    kpos = jnp.arange(t)
    nb = t // Q_BLOCK

    def to_blocks(a):
        return a.reshape(b, DIFF_HEADS, nb, Q_BLOCK, DIFF_DQK).transpose(2, 0, 1, 3, 4)

    def block(inp):
        q1c, q2c, qpos = inp
        bias = -slopes[:, None, None] * jnp.abs(qpos[:, None] - kpos[None, :]).astype(jnp.float32)
        s1 = jnp.einsum('bhqd,bhkd->bhqk', q1c, k1).astype(jnp.float32) * scale + bias
        s2 = jnp.einsum('bhqd,bhkd->bhqk', q2c, k2).astype(jnp.float32) * scale + bias
        a = jax.nn.softmax(s1, axis=-1) - lam * jax.nn.softmax(s2, axis=-1)
        return jnp.einsum('bhqk,bhkd->bhqd', a.astype(vh.dtype), vh)

    o = lax.map(block, (to_blocks(q1), to_blocks(q2), kpos.reshape(nb, Q_BLOCK)))
    o = o.transpose(1, 2, 0, 3, 4).reshape(b, DIFF_HEADS, t, DIFF_DV)
    return merge_heads(rms_norm(o, subln[:, None, :]) * (1.0 - lam_init))


def mixer(h, w_in, w_gate, b_gate, w_br_ret, w_br_nat, w_br_hgrn, w_br_diff, w_out,
          ret_gn, nat_rpb, hgrn_lower, hgrn_gn, diff_lambda, diff_subln, layer):
    (rq, rk, rv, rg, nq, nk, nv, hq, hff, hfb, hi, hg, dq, dk, dv) = split_columns(h @ w_in)
    y_ret = retention(rq, rk, rv, rg, ret_gn)
    y_nat = neighbourhood_attention(nq, nk, nv, nat_rpb)
    y_hgrn = hgrn2(hq, hff, hfb, hi, hg, hgrn_lower, hgrn_gn)
    y_diff = diff_attention(dq, dk, dv, diff_lambda, diff_subln, layer)
    gates = jax.nn.sigmoid(h @ w_gate + b_gate).reshape(h.shape[0], h.shape[1], N_BRANCH, D_MODEL)
    merged = (gates[:, :, 0] * (y_ret @ w_br_ret) + gates[:, :, 1] * (y_nat @ w_br_nat)
              + gates[:, :, 2] * (y_hgrn @ w_br_hgrn) + gates[:, :, 3] * (y_diff @ w_br_diff))
    return merged @ w_out


def moe(h, w_router, router_bias, w_e_gate, w_e_up, w_e_down, w_s_gate, w_s_up, w_s_down):
    b, t, d = h.shape
    xt = h.reshape(b * t, d)
    n = xt.shape[0]
    scores = jax.nn.sigmoid((xt @ w_router).astype(jnp.float32))
    biased = scores + router_bias.astype(jnp.float32)
    grp = biased.reshape(n, N_GROUPS, N_EXPERTS // N_GROUPS)
    grp_score = jnp.sum(lax.top_k(grp, 2)[0], axis=-1)
    _, top_grp = lax.top_k(grp_score, TOPK_GROUPS)
    grp_mask = jnp.any(jax.nn.one_hot(top_grp, N_GROUPS, dtype=jnp.bool_), axis=-2)
    expert_mask = jnp.repeat(grp_mask, N_EXPERTS // N_GROUPS, axis=-1)
    _, top_idx = lax.top_k(jnp.where(expert_mask, biased, MASK_VALUE), TOP_K)
    top_w = jnp.take_along_axis(scores, top_idx, axis=-1)
    top_w = top_w / (jnp.sum(top_w, axis=-1, keepdims=True) + 1e-20) * ROUTED_SCALE
    combine = jnp.einsum('nk,nke->ne', top_w, jax.nn.one_hot(top_idx, N_EXPERTS, dtype=top_w.dtype))
    nblk = n // MOE_BLOCK

    def block(inp):
        xb, cb = inp
        hid = jax.nn.silu(jnp.einsum('nd,edh->neh', xb, w_e_gate)) * jnp.einsum('nd,edh->neh', xb, w_e_up)
        hid = hid * cb[..., None].astype(hid.dtype)
        return jnp.einsum('neh,ehd->nd', hid, w_e_down)

    routed = lax.map(block, (xt.reshape(nblk, MOE_BLOCK, d), combine.reshape(nblk, MOE_BLOCK, N_EXPERTS)))
    shared = (jax.nn.silu(xt @ w_s_gate) * (xt @ w_s_up)) @ w_s_down
    return (routed.reshape(n, d) + shared).reshape(b, t, d)


def setup_inputs(seed: int = 0) -> dict:
    key = jax.random.key(seed)
    ks = iter(jax.random.split(key, 32))
    nrm = lambda shape, scale: scale * jax.random.normal(next(ks), shape, jnp.float32)
    L, D = DEPTH, D_MODEL
    return {
        'x': nrm((BATCH, SEQ, D), 1.0),
        'ln_in_g': 1.0 + nrm((D,), 0.02),
        'ln_in_b': nrm((D,), 0.02),
        'w_in': nrm((L, D, IN_WIDTH), D ** -0.5),
        'w_gate': nrm((L, D, N_BRANCH * D), D ** -0.5),
        'b_gate': nrm((L, N_BRANCH * D), 0.1),
        'w_br_ret': nrm((L, RET_V, D), RET_V ** -0.5),
        'w_br_nat': nrm((L, NAT_W, D), NAT_W ** -0.5),
        'w_br_hgrn': nrm((L, HGRN_V, D), HGRN_V ** -0.5),
        'w_br_diff': nrm((L, DIFF_V, D), DIFF_V ** -0.5),
        'w_out': nrm((L, D, D), BETA * D ** -0.5),
        'ret_gn': 1.0 + nrm((L, RET_HEADS, RET_DV), 0.02),
        'nat_rpb': nrm((L, NAT_HEADS, 2 * NAT_KH - 1, 2 * NAT_KW - 1), 0.1),
        'hgrn_lb': nrm((L, 2, HGRN_K), 1.0),
        'hgrn_gn': 1.0 + nrm((L, HGRN_HEADS, HGRN_DV), 0.02),
        'diff_lambda': nrm((L, 4, DIFF_DQK), 0.1),
        'diff_subln': 1.0 + nrm((L, DIFF_HEADS, DIFF_DV), 0.02),
        'ln1_g': 1.0 + nrm((L, D), 0.02),
        'ln1_b': nrm((L, D), 0.02),
        'w_router': nrm((L, D, N_EXPERTS), D ** -0.5),
        'router_bias': nrm((L, N_EXPERTS), 0.01),
        'w_e_gate': nrm((L, N_EXPERTS, D, EXPERT_HIDDEN), D ** -0.5),
        'w_e_up': nrm((L, N_EXPERTS, D, EXPERT_HIDDEN), D ** -0.5),
        'w_e_down': nrm((L, N_EXPERTS, EXPERT_HIDDEN, D), BETA * EXPERT_HIDDEN ** -0.5),
        'w_s_gate': nrm((L, D, SHARED_HIDDEN), D ** -0.5),
        'w_s_up': nrm((L, D, SHARED_HIDDEN), D ** -0.5),
        'w_s_down': nrm((L, SHARED_HIDDEN, D), BETA * SHARED_HIDDEN ** -0.5),
        'ln2_g': 1.0 + nrm((L, D), 0.02),
        'ln2_b': nrm((L, D), 0.02),
    }


def reference(x, ln_in_g, ln_in_b, w_in, w_gate, b_gate, w_br_ret, w_br_nat, w_br_hgrn,
              w_br_diff, w_out, ret_gn, nat_rpb, hgrn_lb, hgrn_gn, diff_lambda, diff_subln,
              ln1_g, ln1_b, w_router, router_bias, w_e_gate, w_e_up, w_e_down,
              w_s_gate, w_s_up, w_s_down, ln2_g, ln2_b):
    p = jax.nn.softmax(hgrn_lb.astype(jnp.float32), axis=0)
    hgrn_lower = jnp.cumsum(p, axis=0) - p[0]
    h = layer_norm(x, ln_in_g, ln_in_b)
    for l in range(DEPTH):
        m = mixer(h, w_in[l], w_gate[l], b_gate[l], w_br_ret[l], w_br_nat[l], w_br_hgrn[l],
                  w_br_diff[l], w_out[l], ret_gn[l], nat_rpb[l], hgrn_lower[l], hgrn_gn[l],
                  diff_lambda[l], diff_subln[l], l)
        h = layer_norm(ALPHA * h + m, ln1_g[l], ln1_b[l])
        f = moe(h, w_router[l], router_bias[l], w_e_gate[l], w_e_up[l], w_e_down[l],
                w_s_gate[l], w_s_up[l], w_s_down[l])
        h = layer_norm(ALPHA * h + f, ln2_g[l], ln2_b[l])
    return h
```

```python
import functools
import math

import numpy as np
import jax
import jax.numpy as jnp
from jax import lax
from jax.experimental import pallas as pl
from jax.experimental.pallas import tpu as pltpu

F32 = jnp.float32
BF16 = jnp.bfloat16

D_MODEL = 1024
DEPTH = 2
GRID_W = 64
HEAD_DIM = 64
N_HEADS = 4
BR_W = N_HEADS * HEAD_DIM
NAT_KH = 8
NAT_KW = 16
DIFF_DV = 128
DIFF_W = N_HEADS * DIFF_DV
N_EXPERTS = 64
TOP_K = 8
N_GROUPS = 8
TOPK_GROUPS = 4
GROUP_SIZE = N_EXPERTS // N_GROUPS
EXPERT_HIDDEN = 256
ROUTED_SCALE = 2.5
EPS = 1e-5
MASK_VALUE = -1e30
ALPHA = (2.0 * DEPTH) ** 0.25
IN_WIDTH = 4608
COL_RET = 0
COL_NAT = 1024
COL_HGRN = 1792
COL_DIFF = 3072

VMEM_LIMIT = 56 * 1024 * 1024

RET_CHUNK = 256
HGRN_CHUNK = 64
DIFF_TILE = 512
NAT_ROWS_PER_STEP = 8


def _cparams(*sem):
    return pltpu.CompilerParams(dimension_semantics=sem, vmem_limit_bytes=VMEM_LIMIT)


def _nt(a, b):
    return lax.dot_general(a, b, (((1,), (1,)), ((), ())), preferred_element_type=F32)


def _tn(a, b):
    return lax.dot_general(a, b, (((0,), (0,)), ((), ())), preferred_element_type=F32)


def _dot(a, b):
    return jnp.dot(a, b, preferred_element_type=F32)


def _dot_split(x, w_bf16, terms=3):
    acc = None
    rem = x
    for _ in range(terms):
        piece = rem.astype(BF16)
        part = _dot(piece, w_bf16)
        acc = part if acc is None else acc + part
        rem = rem - piece.astype(F32)
    return acc


def _layer_norm(z, g, b):
    mu = jnp.mean(z, axis=-1, keepdims=True)
    zc = z - mu
    var = jnp.mean(zc * zc, axis=-1, keepdims=True)
    return zc * lax.rsqrt(var + EPS) * g + b


def _sigmoid(x):
    return 1.0 / (1.0 + jnp.exp(-x))


def _head_id(shape, axis):
    return lax.shift_right_logical(lax.broadcasted_iota(jnp.int32, shape, axis), 6)


def _block_diag_mask(n):
    return _head_id((n, n), 0) == _head_id((n, n), 1)


def _ln_kernel(x_ref, g_ref, b_ref, hf_ref, hb_ref):
    h = _layer_norm(x_ref[...], g_ref[...], b_ref[...])
    hf_ref[...] = h
    hb_ref[...] = h.astype(BF16)


def input_layer_norm(x2, g, b, tm=1024):
    n, d = x2.shape
    return pl.pallas_call(
        _ln_kernel,
        grid=(n // tm,),
        in_specs=[pl.BlockSpec((tm, d), lambda i: (i, 0)),
                  pl.BlockSpec((1, d), lambda i: (0, 0)),
                  pl.BlockSpec((1, d), lambda i: (0, 0))],
        out_specs=[pl.BlockSpec((tm, d), lambda i: (i, 0)),
                   pl.BlockSpec((tm, d), lambda i: (i, 0))],
        out_shape=[jax.ShapeDtypeStruct((n, d), F32), jax.ShapeDtypeStruct((n, d), BF16)],
        compiler_params=_cparams("parallel"),
        name="input_ln",
    )(x2, g.reshape(1, d), b.reshape(1, d))


def _matmul_kernel(h_ref, w_ref, o_ref):
    o_ref[...] = _dot(h_ref[...], w_ref[...]).astype(o_ref.dtype)


def in_projection(hb, w_bf16, tm=2048, tn=768):
    n, d = hb.shape
    w = w_bf16.shape[1]
    return pl.pallas_call(
        _matmul_kernel,
        grid=(n // tm, w // tn),
        in_specs=[pl.BlockSpec((tm, d), lambda i, j: (i, 0)),
                  pl.BlockSpec((d, tn), lambda i, j: (0, j))],
        out_specs=pl.BlockSpec((tm, tn), lambda i, j: (i, j)),
        out_shape=jax.ShapeDtypeStruct((n, w), BF16),
        compiler_params=_cparams("parallel", "arbitrary"),
        name="in_proj",
    )(hb, w_bf16)


def _retention_tables(c):
    idx = np.arange(N_HEADS, dtype=np.float64)
    lg = [np.log1p(-np.exp2(-5.0 - 2.0 * idx)), np.log1p(-np.exp2(-6.0 - 2.0 * idx))]
    i = np.arange(c, dtype=np.float64)
    diff = i[:, None] - i[None, :]
    lane_head = np.repeat(np.arange(N_HEADS), HEAD_DIM)
    dmat = np.zeros((2, N_HEADS, c, c), np.float32)
    oscale = np.zeros((2, c, BR_W), np.float32)
    kscale = np.zeros((2, c, BR_W), np.float32)
    sdecay = np.zeros((2, 1, BR_W), np.float32)
    for h in range(N_HEADS):
        dmat[0, h] = np.where(diff >= 0, np.exp(lg[0][h] * np.maximum(diff, 0)), 0.0)
        dmat[1, h] = np.where(diff <= 0, np.exp(lg[1][h] * np.maximum(-diff, 0)), 0.0)
    oscale[0] = np.exp(lg[0][lane_head][None, :] * (i[:, None] + 1.0))
    oscale[1] = np.exp(lg[1][lane_head][None, :] * (c - i[:, None]))
    kscale[0] = np.exp(lg[0][lane_head][None, :] * (c - 1.0 - i[:, None]))
    kscale[1] = np.exp(lg[1][lane_head][None, :] * i[:, None])
    sdecay[0, 0] = np.exp(lg[0][lane_head] * c)
    sdecay[1, 0] = np.exp(lg[1][lane_head] * c)
    return (jnp.asarray(dmat), jnp.asarray(oscale), jnp.asarray(kscale), jnp.asarray(sdecay))


def _retention_kernel(qf, kf, vf, qb, kb, vb, dmat, oscale, kscale, sdecay, of_ref, ob_ref, state):
    @pl.when(pl.program_id(1) == 0)
    def _():
        state[...] = jnp.zeros_like(state)

    c = qf.shape[0]
    lane_head = _head_id((c, BR_W), 1)
    bd = _block_diag_mask(BR_W)
    for dirn, (q_ref, k_ref, v_ref, o_ref) in enumerate(((qf, kf, vf, of_ref), (qb, kb, vb, ob_ref))):
        q = q_ref[...]
        k = k_ref[...] * jnp.asarray(HEAD_DIM ** -0.5, BF16)
        v = v_ref[...]
        s_old = state[dirn]
        out = _dot(q, s_old.astype(BF16)) * oscale[dirn]
        for h in range(N_HEADS):
            qz = jnp.where(lane_head == h, q, jnp.zeros_like(q))
            a = (_nt(qz, k) * dmat[dirn, h]).astype(BF16)
            out = out + jnp.where(lane_head == h, _dot(a, v), 0.0)
        o_ref[...] = out
        kw = (k.astype(F32) * kscale[dirn]).astype(BF16)
        state[dirn] = s_old * sdecay[dirn] + jnp.where(bd, _tn(kw, v), 0.0)


def retention(p, batch, seq):
    c = RET_CHUNK
    nc = seq // c
    dmat, oscale, kscale, sdecay = _retention_tables(c)
    cb = COL_RET // BR_W

    def fwd(col):
        return pl.BlockSpec((c, BR_W), lambda b, j: (b * nc + j, col))

    def bwd(col):
        return pl.BlockSpec((c, BR_W), lambda b, j: (b * nc + nc - 1 - j, col))

    def whole(a):
        nd = a.ndim
        return pl.BlockSpec(a.shape, lambda b, j: (0,) * nd)

    n = batch * seq
    return pl.pallas_call(
        _retention_kernel,
        grid=(batch, nc),
        in_specs=[fwd(cb), fwd(cb + 1), fwd(cb + 2), bwd(cb), bwd(cb + 1), bwd(cb + 2),
                  whole(dmat), whole(oscale), whole(kscale), whole(sdecay)],
        out_specs=[pl.BlockSpec((c, BR_W), lambda b, j: (b * nc + j, 0)),
                   pl.BlockSpec((c, BR_W), lambda b, j: (b * nc + nc - 1 - j, 0))],
        out_shape=[jax.ShapeDtypeStruct((n, BR_W), F32)] * 2,
        scratch_shapes=[pltpu.VMEM((2, BR_W, BR_W), F32)],
        compiler_params=_cparams("parallel", "arbitrary"),
        name="retention",
    )(p, p, p, p, p, p, dmat, oscale, kscale, sdecay)


def _hgrn_kernel(lb_ref, qf, ff, vf, qb, fb, vb, of_ref, ob_ref, state, kpad, gpad, vpad, *, layer):
    c = qf.shape[0]

    @pl.when(pl.program_id(1) == 0)
    def _():
        state[...] = jnp.zeros_like(state)
        kpad[...] = jnp.zeros_like(kpad)
        gpad[...] = jnp.zeros_like(gpad)
        vpad[...] = jnp.zeros_like(vpad)

    lb = lb_ref[...]
    e = jnp.exp(lb - jnp.max(lb, axis=0, keepdims=True))
    prob = e / jnp.sum(e, axis=0, keepdims=True)
    lower = jnp.sum(prob[:layer + 1], axis=0) - prob[0]

    row = lax.broadcasted_iota(jnp.int32, (c, c), 0)
    col = lax.broadcasted_iota(jnp.int32, (c, c), 1)
    bd = _block_diag_mask(BR_W)
    seg_ones = jnp.where(bd, 1.0, 0.0).astype(BF16)

    for dirn, (q_ref, f_ref, v_ref, o_ref) in enumerate(((qf, ff, vf, of_ref), (qb, fb, vb, ob_ref))):
        rev = dirn == 1
        lo = lower[dirn:dirn + 1, :]
        fpre = f_ref[...].astype(F32)
        logf = jnp.log(lo + (1.0 - lo) * _sigmoid(fpre))
        kk = (1.0 - lo) * _sigmoid(-fpre)
        qx = q_ref[...].astype(F32)
        q = qx * _sigmoid(qx)
        v_bf = v_ref[...]
        v = v_bf.astype(F32)
        tri = jnp.where((col >= row) if rev else (col <= row), 1.0, 0.0).astype(BF16)
        g = _tri_cumsum(tri, logf)
        g_end = g[0:1, :] if rev else g[c - 1:c, :]
        s_old = state[dirn]
        inter = _nt((q * jnp.exp(g)).astype(BF16), s_old.astype(BF16))
        kd = (kk * jnp.exp(g_end - g)).astype(BF16)
        state[dirn] = s_old * jnp.exp(g_end) + jnp.where(bd, _tn(v_bf, kd), 0.0)
        for b in range(8):
            lo_row = (c - b) if rev else (c + b)
            kpad[b, dirn, lo_row:lo_row + c, :] = kk
            gpad[b, dirn, lo_row:lo_row + c, :] = g
            vpad[b, dirn, lo_row:lo_row + c, :] = v

        intra = jnp.zeros((c, BR_W), F32)
        for b in range(8):
            def offset_step(a, acc, b=b, dirn=dirn, rev=rev, q=q, g=g):
                start = pl.multiple_of((c + 8 * a) if rev else (c - 8 * a), 8)
                ks = kpad[b, dirn, pl.ds(start, c), :]
                gs = gpad[b, dirn, pl.ds(start, c), :]
                vs = vpad[b, dirn, pl.ds(start, c), :]
                term = q * ks * jnp.exp(g - gs)
                return acc + _dot(term.astype(BF16), seg_ones) * vs

            intra = lax.fori_loop(0, c // 8, offset_step, intra)
        o_ref[...] = inter + intra


def _tri_cumsum(tri_bf16, x):
    acc = None
    rem = x
    for _ in range(3):
        piece = rem.astype(BF16)
        part = _dot(tri_bf16, piece)
        acc = part if acc is None else acc + part
        rem = rem - piece.astype(F32)
    return acc


def hgrn(p, hgrn_lb, layer, batch, seq):
    c = HGRN_CHUNK
    nc = seq // c
    cb = COL_HGRN // BR_W

    def fwd(col):
        return pl.BlockSpec((c, BR_W), lambda b, j: (b * nc + j, col))

    def bwd(col):
        return pl.BlockSpec((c, BR_W), lambda b, j: (b * nc + nc - 1 - j, col))

    n = batch * seq
    return pl.pallas_call(
        functools.partial(_hgrn_kernel, layer=layer),
        grid=(batch, nc),
        in_specs=[pl.BlockSpec(hgrn_lb.shape, lambda b, j: (0, 0, 0)),
                  fwd(cb), fwd(cb + 1), fwd(cb + 3), bwd(cb), bwd(cb + 2), bwd(cb + 3)],
        out_specs=[pl.BlockSpec((c, BR_W), lambda b, j: (b * nc + j, 0)),
                   pl.BlockSpec((c, BR_W), lambda b, j: (b * nc + nc - 1 - j, 0))],
        out_shape=[jax.ShapeDtypeStruct((n, BR_W), F32)] * 2,
        scratch_shapes=[pltpu.VMEM((2, BR_W, BR_W), F32),
                        pltpu.VMEM((8, 2, 3 * c, BR_W), F32),
                        pltpu.VMEM((8, 2, 3 * c, BR_W), F32),
                        pltpu.VMEM((8, 2, 3 * c, BR_W), F32)],
        compiler_params=_cparams("parallel", "arbitrary"),
        name="hgrn",
    )(hgrn_lb, p, p, p, p, p, p)


def _nat_bias_table(rpb):
    cq = np.arange(GRID_W)
    ck = np.arange(GRID_W)
    col_start = np.clip(cq - NAT_KW // 2, 0, GRID_W - NAT_KW)
    col_mask = (ck[None, :] >= col_start[:, None]) & (ck[None, :] < col_start[:, None] + NAT_KW)
    dc = np.clip(ck[None, :] - cq[:, None], -(NAT_KW - 1), NAT_KW - 1) + (NAT_KW - 1)
    dr = np.arange(NAT_KH)[:, None] + np.arange(NAT_KH)[None, :]
    t = rpb[:, dr[:, :, None, None], dc[None, None, :, :]]
    t = jnp.where(jnp.asarray(col_mask)[None, None, None], t, MASK_VALUE)
    t = t.transpose(0, 1, 3, 2, 4).reshape(N_HEADS, NAT_KH, GRID_W, NAT_KH * GRID_W)
    return t.astype(F32)


def _nat_kernel(q_ref, k_ref, v_ref, bias_ref, o_ref, *, rows_per_step, n_rows):
    j = pl.program_id(1)
    lane_head = _head_id((GRID_W, BR_W), 1)
    win = NAT_KH * GRID_W

    def body(i, carry):
        r = j * rows_per_step + i
        rs = jnp.clip(r - NAT_KH // 2, 0, n_rows - NAT_KH)
        base = rs - r + (NAT_KH - 1)
        q = q_ref[pl.ds(pl.multiple_of(i * GRID_W, GRID_W), GRID_W), :] * jnp.asarray(HEAD_DIM ** -0.5, BF16)
        kw = k_ref[pl.ds(pl.multiple_of(rs * GRID_W, GRID_W), win), :]
        vw = v_ref[pl.ds(pl.multiple_of(rs * GRID_W, GRID_W), win), :]
        out = jnp.zeros((GRID_W, BR_W), F32)
        for h in range(N_HEADS):
            qz = jnp.where(lane_head == h, q, jnp.zeros_like(q))
            s = _nt(qz, kw) + bias_ref[h, base]
            m = jnp.max(s, axis=-1, keepdims=True)
            pr = jnp.exp(s - m)
            l = jnp.sum(pr, axis=-1, keepdims=True)
            o = _dot(pr.astype(BF16), vw) / l
            out = out + jnp.where(lane_head == h, o, 0.0)
        o_ref[pl.ds(pl.multiple_of(i * GRID_W, GRID_W), GRID_W), :] = out.astype(o_ref.dtype)
        return carry

    lax.fori_loop(0, rows_per_step, body, 0)


def neighbourhood_attention(p, rpb, batch, seq):
    n_rows = seq // GRID_W
    rps = NAT_ROWS_PER_STEP
    steps = n_rows // rps
    bias = _nat_bias_table(rpb)
    cb = COL_NAT // BR_W
    tq = rps * GRID_W
    return pl.pallas_call(
        functools.partial(_nat_kernel, rows_per_step=rps, n_rows=n_rows),
        grid=(batch, steps),
        in_specs=[pl.BlockSpec((tq, BR_W), lambda b, j: (b * steps + j, cb)),
                  pl.BlockSpec((seq, BR_W), lambda b, j: (b, cb + 1)),
                  pl.BlockSpec((seq, BR_W), lambda b, j: (b, cb + 2)),
                  pl.BlockSpec(bias.shape, lambda b, j: (0, 0, 0, 0))],
        out_specs=pl.BlockSpec((tq, BR_W), lambda b, j: (b * steps + j, 0)),
        out_shape=jax.ShapeDtypeStruct((batch * seq, BR_W), BF16),
        compiler_params=_cparams("parallel", "arbitrary"),
        name="nat",
    )(p, p, p, bias)


def _diff_kernel(lam_ref, slope_ref, sub_ref, q_ref, k_ref, v_ref, o_ref,
                 acc1, acc2, m1, l1, m2, l2, below, diag, *, lam_init, n_tiles):
    t = q_ref.shape[0]
    qi = pl.program_id(2)
    slope = slope_ref[0, 0:1, 0:1]
    lp = lam_ref[...]
    lam = (jnp.exp(jnp.sum(lp[0:1] * lp[1:2], axis=-1, keepdims=True))
           - jnp.exp(jnp.sum(lp[2:3] * lp[3:4], axis=-1, keepdims=True)) + lam_init)

    rel = (lax.broadcasted_iota(jnp.int32, (t, t), 0)
           - lax.broadcasted_iota(jnp.int32, (t, t), 1)).astype(F32)
    below[...] = -slope * rel
    diag[...] = -slope * jnp.abs(rel)

    q = q_ref[...] * jnp.asarray(0.125, BF16)
    lane = lax.broadcasted_iota(jnp.int32, q.shape, 1)
    q1 = jnp.where(lane < 64, q, jnp.zeros_like(q))
    q2 = jnp.where(lane >= 64, q, jnp.zeros_like(q))

    acc1[...] = jnp.zeros_like(acc1)
    acc2[...] = jnp.zeros_like(acc2)
    l1[...] = jnp.zeros_like(l1)
    l2[...] = jnp.zeros_like(l2)
    m1[...] = jnp.full_like(m1, -jnp.inf)
    m2[...] = jnp.full_like(m2, -jnp.inf)

    def tile(kj, bias):
        start = pl.multiple_of(kj * t, t)
        k = k_ref[pl.ds(start, t), :]
        v = v_ref[pl.ds(start, t), :]
        for qx, acc, m_ref, l_ref in ((q1, acc1, m1, l1), (q2, acc2, m2, l2)):
            s = _nt(qx, k) + bias
            m_old = m_ref[...]
            m_new = jnp.maximum(m_old, jnp.max(s, axis=-1, keepdims=True))
            a = jnp.exp(m_old - m_new)
            pr = jnp.exp(s - m_new)
            l_ref[...] = a * l_ref[...] + jnp.sum(pr, axis=-1, keepdims=True)
            acc[...] = a * acc[...] + _dot(pr.astype(BF16), v)
            m_ref[...] = m_new

    def before(kj, carry):
        dist = jnp.full((1, 1), (qi - kj) * t, jnp.int32).astype(F32)
        tile(kj, below[...] - slope * dist)
        return carry

    def after(kj, carry):
        dist = jnp.full((1, 1), (kj - qi) * t, jnp.int32).astype(F32)
        tile(kj, -below[...] - slope * dist)
        return carry

    lax.fori_loop(0, qi, before, 0)
    tile(qi, diag[...])
    lax.fori_loop(qi + 1, n_tiles, after, 0)

    o = acc1[...] / l1[...] - lam * (acc2[...] / l2[...])
    o = o * lax.rsqrt(jnp.mean(o * o, axis=-1, keepdims=True) + EPS) * sub_ref[0]
    o_ref[...] = (o * (1.0 - lam_init)).astype(o_ref.dtype)


def diff_attention(p, lam_params, subln, layer, batch, seq):
    t = DIFF_TILE
    nt = seq // t
    lam_init = 0.8 - 0.6 * math.exp(-0.3 * layer)
    slopes = np.exp2(-8.0 * (np.arange(N_HEADS, dtype=np.float64) + 1.0) / N_HEADS)
    slope_tab = jnp.asarray(np.broadcast_to(slopes[:, None, None], (N_HEADS, 8, 128)).astype(np.float32))
    qc, kc, vc = (COL_DIFF // DIFF_DV, (COL_DIFF + DIFF_W) // DIFF_DV, (COL_DIFF + 2 * DIFF_W) // DIFF_DV)
    return pl.pallas_call(
        functools.partial(_diff_kernel, lam_init=lam_init, n_tiles=nt),
        grid=(batch, N_HEADS, nt),
        in_specs=[pl.BlockSpec(lam_params.shape, lambda b, h, i: (0, 0)),
                  pl.BlockSpec((1, 8, 128), lambda b, h, i: (h, 0, 0)),
                  pl.BlockSpec((1, 1, DIFF_DV), lambda b, h, i: (h, 0, 0)),
                  pl.BlockSpec((t, DIFF_DV), lambda b, h, i: (b * nt + i, qc + h)),
                  pl.BlockSpec((seq, DIFF_DV), lambda b, h, i: (b, kc + h)),
                  pl.BlockSpec((seq, DIFF_DV), lambda b, h, i: (b, vc + h))],
        out_specs=pl.BlockSpec((t, DIFF_DV), lambda b, h, i: (b * nt + i, h)),
        out_shape=jax.ShapeDtypeStruct((batch * seq, DIFF_W), BF16),
        scratch_shapes=[pltpu.VMEM((t, DIFF_DV), F32), pltpu.VMEM((t, DIFF_DV), F32),
                        pltpu.VMEM((t, 1), F32), pltpu.VMEM((t, 1), F32),
                        pltpu.VMEM((t, 1), F32), pltpu.VMEM((t, 1), F32),
                        pltpu.VMEM((t, t), F32), pltpu.VMEM((t, t), F32)],
        compiler_params=_cparams("parallel", "parallel", "arbitrary"),
        name="diff_attn",
    )(lam_params, slope_tab, subln.reshape(N_HEADS, 1, DIFF_DV), p, p, p)


def _merge_kernel(hf_ref, hb_ref, rof, rob, rg, ynat, hof, hob, hg, ydiff,
                  wg_ref, bg_ref, wr_ref, wn_ref, wh_ref, wd_ref, wo_ref,
                  rgn_ref, hgn_ref, lng_ref, lnb_ref, of_ref, ob_ref):
    seg_mean = jnp.where(_block_diag_mask(BR_W), 1.0 / HEAD_DIM, 0.0).astype(BF16)

    def head_norm_gate(o, gn, gate):
        ms = _dot_split(o * o, seg_mean, terms=2)
        gx = gate.astype(F32)
        return (o * lax.rsqrt(ms + EPS) * gn * (gx * _sigmoid(gx))).astype(BF16)

    y_ret = head_norm_gate(rof[...] + rob[...], rgn_ref[...], rg[...])
    y_hgrn = head_norm_gate(hof[...] + hob[...], hgn_ref[...], hg[...])
    hb = hb_ref[...]
    d = hf_ref.shape[1]
    merged = None
    for i, (y, w_ref) in enumerate(((y_ret, wr_ref), (ynat[...], wn_ref), (y_hgrn, wh_ref), (ydiff[...], wd_ref))):
        gate = _sigmoid(_dot(hb, wg_ref[:, i * d:(i + 1) * d]) + bg_ref[:, i * d:(i + 1) * d])
        part = gate * _dot(y, w_ref[...])
        merged = part if merged is None else merged + part
    z = ALPHA * hf_ref[...] + _dot(merged.astype(BF16), wo_ref[...])
    h = _layer_norm(z, lng_ref[...], lnb_ref[...])
    of_ref[...] = h
    ob_ref[...] = h.astype(BF16)


def merge_and_norm(hf, hb, p, ret_f, ret_b, y_nat, hg_f, hg_b, y_diff,
                   w_gate, b_gate, w_br_ret, w_br_nat, w_br_hgrn, w_br_diff, w_out,
                   ret_gn, hgrn_gn, ln_g, ln_b, tm=512):
    n, d = hf.shape

    def rows(width, col=0):
        return pl.BlockSpec((tm, width), lambda i: (i, col))

    def whole(a):
        nd = a.ndim
        return pl.BlockSpec(a.shape, lambda i: (0,) * nd)

    consts = [w_gate, b_gate.reshape(1, -1), w_br_ret, w_br_nat, w_br_hgrn, w_br_diff, w_out,
              ret_gn.reshape(1, BR_W), hgrn_gn.reshape(1, BR_W), ln_g.reshape(1, d), ln_b.reshape(1, d)]
    return pl.pallas_call(
        _merge_kernel,
        grid=(n // tm,),
        in_specs=[rows(d), rows(d),
                  rows(BR_W), rows(BR_W), rows(BR_W, COL_RET // BR_W + 3),
                  rows(BR_W),
                  rows(BR_W), rows(BR_W), rows(BR_W, COL_HGRN // BR_W + 4),
                  rows(DIFF_W)] + [whole(a) for a in consts],
        out_specs=[rows(d), rows(d)],
        out_shape=[jax.ShapeDtypeStruct((n, d), F32), jax.ShapeDtypeStruct((n, d), BF16)],
        compiler_params=_cparams("parallel"),
        name="merge",
    )(hf, hb, ret_f, ret_b, p, y_nat, hg_f, hg_b, p, y_diff, *consts)


def _router_kernel(h_ref, w_ref, bias_ref, c_ref):
    w = w_ref[...]
    w_hi = w.astype(BF16)
    w_lo = (w - w_hi.astype(F32)).astype(BF16)
    h = h_ref[...]
    logits = _dot_split(h, w_hi, terms=3) + _dot_split(h, w_lo, terms=2)
    scores = _sigmoid(logits)
    biased = scores + bias_ref[...]
    shape = scores.shape
    lane = lax.broadcasted_iota(jnp.int32, shape, 1).astype(F32)
    group = lax.shift_right_logical(lax.broadcasted_iota(jnp.int32, shape, 1), 3).astype(F32)
    big = jnp.asarray(1e9, F32)
    neg = jnp.asarray(-jnp.inf, F32)

    def first_max(x, ids):
        m = jnp.max(x, axis=-1, keepdims=True)
        return m, jnp.min(jnp.where(x == m, ids, big), axis=-1, keepdims=True)

    gscore = jnp.zeros(shape, F32)
    for g in range(N_GROUPS):
        xg = jnp.where(group == g, biased, neg)
        m1, i1 = first_max(xg, lane)
        m2 = jnp.max(jnp.where(lane == i1, neg, xg), axis=-1, keepdims=True)
        gscore = jnp.where(group == g, m1 + m2, gscore)
    gsel = jnp.zeros(shape, jnp.bool_)
    for _ in range(TOPK_GROUPS):
        _, gi = first_max(gscore, group)
        hit = group == gi
        gsel = gsel | hit
        gscore = jnp.where(hit, neg, gscore)
    cand = jnp.where(gsel, biased, MASK_VALUE)
    esel = jnp.zeros(shape, jnp.bool_)
    for _ in range(TOP_K):
        _, ei = first_max(cand, lane)
        hit = lane == ei
        esel = esel | hit
        cand = jnp.where(hit, neg, cand)
    top = jnp.where(esel, scores, 0.0)
    c_ref[...] = top / (jnp.sum(top, axis=-1, keepdims=True) + 1e-20) * ROUTED_SCALE


def router(hf, w_router, router_bias, tm=512):
    n, d = hf.shape
    e = w_router.shape[1]
    return pl.pallas_call(
        _router_kernel,
        grid=(n // tm,),
        in_specs=[pl.BlockSpec((tm, d), lambda i: (i, 0)),
                  pl.BlockSpec((d, e), lambda i: (0, 0)),
                  pl.BlockSpec((1, e), lambda i: (0, 0))],
        out_specs=pl.BlockSpec((tm, e), lambda i: (i, 0)),
        out_shape=jax.ShapeDtypeStruct((n, e), F32),
        compiler_params=_cparams("parallel"),
        name="router",
    )(hf, w_router, router_bias.reshape(1, e))


def _moe_kernel(hf_ref, hb_ref, c_ref, wg_ref, wu_ref, wd_ref, sg_ref, su_ref, sd_ref,
                lng_ref, lnb_ref, of_ref, ob_ref, acc, *, sub):
    e = pl.program_id(1)
    tm = hf_ref.shape[0]
    n_sub = tm // sub

    def swiglu(x, wg, wu):
        a = _dot(x, wg)
        return a * _sigmoid(a) * _dot(x, wu)

    @pl.when(e == 0)
    def _():
        for s in range(n_sub):
            x = hb_ref[s * sub:(s + 1) * sub, :]
            hid = swiglu(x, sg_ref[...], su_ref[...])
            acc[s * sub:(s + 1) * sub, :] = _dot(hid.astype(BF16), sd_ref[...])

    wg = wg_ref[0].astype(BF16)
    wu = wu_ref[0].astype(BF16)
    wd = wd_ref[0].astype(BF16)
    lane = lax.broadcasted_iota(jnp.int32, (sub, c_ref.shape[1]), 1)
    for s in range(n_sub):
        x = hb_ref[s * sub:(s + 1) * sub, :]
        ce = jnp.sum(jnp.where(lane == e, c_ref[s * sub:(s + 1) * sub, :], 0.0), axis=-1, keepdims=True)
        hid = swiglu(x, wg, wu) * ce
        acc[s * sub:(s + 1) * sub, :] += _dot(hid.astype(BF16), wd)

    @pl.when(e == pl.num_programs(1) - 1)
    def _():
        h = _layer_norm(ALPHA * hf_ref[...] + acc[...], lng_ref[...], lnb_ref[...])
        of_ref[...] = h
        ob_ref[...] = h.astype(BF16)


def moe_and_norm(hf, hb, combine, w_e_gate, w_e_up, w_e_down, w_s_gate, w_s_up, w_s_down,
                 ln_g, ln_b, tm=1024, sub=256):
    n, d = hf.shape
    ne, _, hid = w_e_gate.shape
    return pl.pallas_call(
        functools.partial(_moe_kernel, sub=sub),
        grid=(n // tm, ne),
        in_specs=[pl.BlockSpec((tm, d), lambda i, e: (i, 0)),
                  pl.BlockSpec((tm, d), lambda i, e: (i, 0)),
                  pl.BlockSpec((tm, ne), lambda i, e: (i, 0)),
                  pl.BlockSpec((1, d, hid), lambda i, e: (e, 0, 0)),
                  pl.BlockSpec((1, d, hid), lambda i, e: (e, 0, 0)),
                  pl.BlockSpec((1, hid, d), lambda i, e: (e, 0, 0)),
                  pl.BlockSpec(w_s_gate.shape, lambda i, e: (0, 0)),
                  pl.BlockSpec(w_s_up.shape, lambda i, e: (0, 0)),
                  pl.BlockSpec(w_s_down.shape, lambda i, e: (0, 0)),
                  pl.BlockSpec((1, d), lambda i, e: (0, 0)),
                  pl.BlockSpec((1, d), lambda i, e: (0, 0))],
        out_specs=[pl.BlockSpec((tm, d), lambda i, e: (i, 0)),
                   pl.BlockSpec((tm, d), lambda i, e: (i, 0))],
        out_shape=[jax.ShapeDtypeStruct((n, d), F32), jax.ShapeDtypeStruct((n, d), BF16)],
        scratch_shapes=[pltpu.VMEM((tm, d), F32)],
        compiler_params=_cparams("parallel", "arbitrary"),
        name="moe",
    )(hf, hb, combine, w_e_gate, w_e_up, w_e_down, w_s_gate, w_s_up, w_s_down,
      ln_g.reshape(1, d), ln_b.reshape(1, d))


def kernel(x, ln_in_g, ln_in_b, w_in, w_gate, b_gate, w_br_ret, w_br_nat, w_br_hgrn, w_br_diff, w_out,
           ret_gn, nat_rpb, hgrn_lb, hgrn_gn, diff_lambda, diff_subln, ln1_g, ln1_b, w_router,
           router_bias, w_e_gate, w_e_up, w_e_down, w_s_gate, w_s_up, w_s_down, ln2_g, ln2_b):
    batch, seq, d = x.shape
    bf = lambda a: a.astype(BF16)
    hf, hb = input_layer_norm(x.reshape(batch * seq, d), ln_in_g, ln_in_b)
    for l in range(DEPTH):
        p = in_projection(hb, bf(w_in[l]))
        ret_f, ret_b = retention(p, batch, seq)
        y_nat = neighbourhood_attention(p, nat_rpb[l], batch, seq)
        hg_f, hg_b = hgrn(p, hgrn_lb, l, batch, seq)
        y_diff = diff_attention(p, diff_lambda[l], diff_subln[l], l, batch, seq)
        hf, hb = merge_and_norm(hf, hb, p, ret_f, ret_b, y_nat, hg_f, hg_b, y_diff,
                                bf(w_gate[l]), b_gate[l], bf(w_br_ret[l]), bf(w_br_nat[l]),
                                bf(w_br_hgrn[l]), bf(w_br_diff[l]), bf(w_out[l]),
                                ret_gn[l], hgrn_gn[l], ln1_g[l], ln1_b[l])
        combine = router(hf, w_router[l], router_bias[l])
        hf, hb = moe_and_norm(hf, hb, combine, w_e_gate[l], w_e_up[l], w_e_down[l],
                              bf(w_s_gate[l]), bf(w_s_up[l]), bf(w_s_down[l]), ln2_g[l], ln2_b[l])
    return hf.reshape(batch, seq, d)
```

```python
import functools
import math

import numpy as np
import jax
import jax.numpy as jnp
from jax import lax
from jax.experimental import pallas as pl
from jax.experimental.pallas import tpu as pltpu

F32 = jnp.float32
BF16 = jnp.bfloat16

D_MODEL = 1024
DEPTH = 2
GRID_W = 64
HEAD_DIM = 64
N_HEADS = 4
BR_W = N_HEADS * HEAD_DIM
NAT_KH = 8
NAT_KW = 16
DIFF_DV = 128
DIFF_W = N_HEADS * DIFF_DV
N_EXPERTS = 64
TOP_K = 8
N_GROUPS = 8
TOPK_GROUPS = 4
GROUP_SIZE = N_EXPERTS // N_GROUPS
EXPERT_HIDDEN = 256
ROUTED_SCALE = 2.5
EPS = 1e-5
MASK_VALUE = -1e30
ALPHA = (2.0 * DEPTH) ** 0.25
IN_WIDTH = 4608
COL_RET = 0
COL_NAT = 1024
COL_HGRN = 1792
COL_DIFF = 3072

VMEM_LIMIT = 56 * 1024 * 1024

RET_CHUNK = 256
HGRN_CHUNK = 64
DIFF_TILE = 512
NAT_ROWS_PER_STEP = 8


def _cparams(*sem):
    return pltpu.CompilerParams(dimension_semantics=sem, vmem_limit_bytes=VMEM_LIMIT)


def _nt(a, b):
    return lax.dot_general(a, b, (((1,), (1,)), ((), ())), preferred_element_type=F32)


def _tn(a, b):
    return lax.dot_general(a, b, (((0,), (0,)), ((), ())), preferred_element_type=F32)


def _dot(a, b):
    return jnp.dot(a, b, preferred_element_type=F32)


def _dot_split(x, w_bf16, terms=3):
    acc = None
    rem = x
    for _ in range(terms):
        piece = rem.astype(BF16)
        part = _dot(piece, w_bf16)
        acc = part if acc is None else acc + part
        rem = rem - piece.astype(F32)
    return acc


def _layer_norm(z, g, b):
    mu = jnp.mean(z, axis=-1, keepdims=True)
    zc = z - mu
    var = jnp.mean(zc * zc, axis=-1, keepdims=True)
    return zc * lax.rsqrt(var + EPS) * g + b


def _sigmoid(x):
    return 1.0 / (1.0 + jnp.exp(-x))


def _head_id(shape, axis):
    return lax.shift_right_logical(lax.broadcasted_iota(jnp.int32, shape, axis), 6)


def _block_diag_mask(n):
    return _head_id((n, n), 0) == _head_id((n, n), 1)


def _ln_kernel(x_ref, g_ref, b_ref, hf_ref, hb_ref):
    h = _layer_norm(x_ref[...], g_ref[...], b_ref[...])
    hf_ref[...] = h
    hb_ref[...] = h.astype(BF16)


def input_layer_norm(x2, g, b, tm=1024):
    n, d = x2.shape
    return pl.pallas_call(
        _ln_kernel,
        grid=(n // tm,),
        in_specs=[pl.BlockSpec((tm, d), lambda i: (i, 0)),
                  pl.BlockSpec((1, d), lambda i: (0, 0)),
                  pl.BlockSpec((1, d), lambda i: (0, 0))],
        out_specs=[pl.BlockSpec((tm, d), lambda i: (i, 0)),
                   pl.BlockSpec((tm, d), lambda i: (i, 0))],
        out_shape=[jax.ShapeDtypeStruct((n, d), F32), jax.ShapeDtypeStruct((n, d), BF16)],
        compiler_params=_cparams("parallel"),
        name="input_ln",
    )(x2, g.reshape(1, d), b.reshape(1, d))


def _matmul_kernel(h_ref, w_ref, o_ref):
    o_ref[...] = _dot(h_ref[...], w_ref[...]).astype(o_ref.dtype)


def in_projection(hb, w_bf16, tm=2048, tn=768):
    n, d = hb.shape
    w = w_bf16.shape[1]
    return pl.pallas_call(
        _matmul_kernel,
        grid=(n // tm, w // tn),
        in_specs=[pl.BlockSpec((tm, d), lambda i, j: (i, 0)),
                  pl.BlockSpec((d, tn), lambda i, j: (0, j))],
        out_specs=pl.BlockSpec((tm, tn), lambda i, j: (i, j)),
        out_shape=jax.ShapeDtypeStruct((n, w), BF16),
        compiler_params=_cparams("parallel", "arbitrary"),
        name="in_proj",
    )(hb, w_bf16)


def _retention_tables(c):
    idx = np.arange(N_HEADS, dtype=np.float64)
    lg = [np.log1p(-np.exp2(-5.0 - 2.0 * idx)), np.log1p(-np.exp2(-6.0 - 2.0 * idx))]
    i = np.arange(c, dtype=np.float64)
    diff = i[:, None] - i[None, :]
    lane_head = np.repeat(np.arange(N_HEADS), HEAD_DIM)
    dmat = np.zeros((2, N_HEADS, c, c), np.float32)
    oscale = np.zeros((2, c, BR_W), np.float32)
    kscale = np.zeros((2, c, BR_W), np.float32)
    sdecay = np.zeros((2, 1, BR_W), np.float32)
    for h in range(N_HEADS):
        dmat[0, h] = np.where(diff >= 0, np.exp(lg[0][h] * np.maximum(diff, 0)), 0.0)
        dmat[1, h] = np.where(diff <= 0, np.exp(lg[1][h] * np.maximum(-diff, 0)), 0.0)
    oscale[0] = np.exp(lg[0][lane_head][None, :] * (i[:, None] + 1.0))
    oscale[1] = np.exp(lg[1][lane_head][None, :] * (c - i[:, None]))
    kscale[0] = np.exp(lg[0][lane_head][None, :] * (c - 1.0 - i[:, None]))
    kscale[1] = np.exp(lg[1][lane_head][None, :] * i[:, None])
    sdecay[0, 0] = np.exp(lg[0][lane_head] * c)
    sdecay[1, 0] = np.exp(lg[1][lane_head] * c)
    return (jnp.asarray(dmat), jnp.asarray(oscale), jnp.asarray(kscale), jnp.asarray(sdecay))


def _retention_kernel(qf, kf, vf, qb, kb, vb, dmat, oscale, kscale, sdecay, of_ref, ob_ref, state):
    @pl.when(pl.program_id(1) == 0)
    def _():
        state[...] = jnp.zeros_like(state)

    c = qf.shape[0]
    lane_head = _head_id((c, BR_W), 1)
    bd = _block_diag_mask(BR_W)
    for dirn, (q_ref, k_ref, v_ref, o_ref) in enumerate(((qf, kf, vf, of_ref), (qb, kb, vb, ob_ref))):
        q = q_ref[...]
        k = k_ref[...] * jnp.asarray(HEAD_DIM ** -0.5, BF16)
        v = v_ref[...]
        s_old = state[dirn]
        out = _dot(q, s_old.astype(BF16)) * oscale[dirn]
        for h in range(N_HEADS):
            qz = jnp.where(lane_head == h, q, jnp.zeros_like(q))
            a = (_nt(qz, k) * dmat[dirn, h]).astype(BF16)
            out = out + jnp.where(lane_head == h, _dot(a, v), 0.0)
        o_ref[...] = out
        kw = (k.astype(F32) * kscale[dirn]).astype(BF16)
        state[dirn] = s_old * sdecay[dirn] + jnp.where(bd, _tn(kw, v), 0.0)


def retention(p, batch, seq):
    c = RET_CHUNK
    nc = seq // c
    dmat, oscale, kscale, sdecay = _retention_tables(c)
    cb = COL_RET // BR_W

    def fwd(col):
        return pl.BlockSpec((c, BR_W), lambda b, j: (b * nc + j, col))

    def bwd(col):
        return pl.BlockSpec((c, BR_W), lambda b, j: (b * nc + nc - 1 - j, col))

    def whole(a):
        nd = a.ndim
        return pl.BlockSpec(a.shape, lambda b, j: (0,) * nd)

    n = batch * seq
    return pl.pallas_call(
        _retention_kernel,
        grid=(batch, nc),
        in_specs=[fwd(cb), fwd(cb + 1), fwd(cb + 2), bwd(cb), bwd(cb + 1), bwd(cb + 2),
                  whole(dmat), whole(oscale), whole(kscale), whole(sdecay)],
        out_specs=[pl.BlockSpec((c, BR_W), lambda b, j: (b * nc + j, 0)),
                   pl.BlockSpec((c, BR_W), lambda b, j: (b * nc + nc - 1 - j, 0))],
        out_shape=[jax.ShapeDtypeStruct((n, BR_W), F32)] * 2,
        scratch_shapes=[pltpu.VMEM((2, BR_W, BR_W), F32)],
        compiler_params=_cparams("parallel", "arbitrary"),
        name="retention",
    )(p, p, p, p, p, p, dmat, oscale, kscale, sdecay)


def _hgrn_kernel(lb_ref, qf, ff, vf, qb, fb, vb, of_ref, ob_ref, state, kpad, gpad, vpad, *, layer):
    c = qf.shape[0]

    @pl.when(pl.program_id(1) == 0)
    def _():
        state[...] = jnp.zeros_like(state)
        kpad[...] = jnp.zeros_like(kpad)
        gpad[...] = jnp.zeros_like(gpad)
        vpad[...] = jnp.zeros_like(vpad)

    lb = lb_ref[...]
    e = jnp.exp(lb - jnp.max(lb, axis=0, keepdims=True))
    prob = e / jnp.sum(e, axis=0, keepdims=True)
    lower = jnp.sum(prob[:layer + 1], axis=0) - prob[0]

    row = lax.broadcasted_iota(jnp.int32, (c, c), 0)
    col = lax.broadcasted_iota(jnp.int32, (c, c), 1)
    bd = _block_diag_mask(BR_W)
    seg_ones = jnp.where(bd, 1.0, 0.0).astype(BF16)

    for dirn, (q_ref, f_ref, v_ref, o_ref) in enumerate(((qf, ff, vf, of_ref), (qb, fb, vb, ob_ref))):
        rev = dirn == 1
        lo = lower[dirn:dirn + 1, :]
        fpre = f_ref[...].astype(F32)
        logf = jnp.log(lo + (1.0 - lo) * _sigmoid(fpre))
        kk = (1.0 - lo) * _sigmoid(-fpre)
        qx = q_ref[...].astype(F32)
        q = qx * _sigmoid(qx)
        v_bf = v_ref[...]
        v = v_bf.astype(F32)
        tri = jnp.where((col >= row) if rev else (col <= row), 1.0, 0.0).astype(BF16)
        g = _tri_cumsum(tri, logf)
        g_end = g[0:1, :] if rev else g[c - 1:c, :]
        s_old = state[dirn]
        inter = _nt((q * jnp.exp(g)).astype(BF16), s_old.astype(BF16))
        kd = (kk * jnp.exp(g_end - g)).astype(BF16)
        state[dirn] = s_old * jnp.exp(g_end) + jnp.where(bd, _tn(v_bf, kd), 0.0)
        for b in range(8):
            lo_row = (c - b) if rev else (c + b)
            kpad[b, dirn, lo_row:lo_row + c, :] = kk
            gpad[b, dirn, lo_row:lo_row + c, :] = g
            vpad[b, dirn, lo_row:lo_row + c, :] = v

        intra = jnp.zeros((c, BR_W), F32)
        for b in range(8):
            for a in range(c // 8):
                start = (c + 8 * a) if rev else (c - 8 * a)
                ks = kpad[b, dirn, start:start + c, :]
                gs = gpad[b, dirn, start:start + c, :]
                vs = vpad[b, dirn, start:start + c, :]
                term = q * ks * jnp.exp(g - gs)
                intra = intra + _dot(term.astype(BF16), seg_ones) * vs
        o_ref[...] = inter + intra


def _tri_cumsum(tri_bf16, x):
    acc = None
    rem = x
    for _ in range(3):
        piece = rem.astype(BF16)
        part = _dot(tri_bf16, piece)
        acc = part if acc is None else acc + part
        rem = rem - piece.astype(F32)
    return acc


def hgrn(p, hgrn_lb, layer, batch, seq):
    c = HGRN_CHUNK
    nc = seq // c
    cb = COL_HGRN // BR_W

    def fwd(col):
        return pl.BlockSpec((c, BR_W), lambda b, j: (b * nc + j, col))

    def bwd(col):
        return pl.BlockSpec((c, BR_W), lambda b, j: (b * nc + nc - 1 - j, col))

    n = batch * seq
    return pl.pallas_call(
        functools.partial(_hgrn_kernel, layer=layer),
        grid=(batch, nc),
        in_specs=[pl.BlockSpec(hgrn_lb.shape, lambda b, j: (0, 0, 0)),
                  fwd(cb), fwd(cb + 1), fwd(cb + 3), bwd(cb), bwd(cb + 2), bwd(cb + 3)],
        out_specs=[pl.BlockSpec((c, BR_W), lambda b, j: (b * nc + j, 0)),
                   pl.BlockSpec((c, BR_W), lambda b, j: (b * nc + nc - 1 - j, 0))],
        out_shape=[jax.ShapeDtypeStruct((n, BR_W), F32)] * 2,
        scratch_shapes=[pltpu.VMEM((2, BR_W, BR_W), F32),
                        pltpu.VMEM((8, 2, 3 * c, BR_W), F32),
                        pltpu.VMEM((8, 2, 3 * c, BR_W), F32),
                        pltpu.VMEM((8, 2, 3 * c, BR_W), F32)],
        compiler_params=_cparams("parallel", "arbitrary"),
        name="hgrn",
    )(hgrn_lb, p, p, p, p, p, p)


def _nat_bias_table(rpb):
    cq = np.arange(GRID_W)
    ck = np.arange(GRID_W)
    col_start = np.clip(cq - NAT_KW // 2, 0, GRID_W - NAT_KW)
    col_mask = (ck[None, :] >= col_start[:, None]) & (ck[None, :] < col_start[:, None] + NAT_KW)
    dc = np.clip(ck[None, :] - cq[:, None], -(NAT_KW - 1), NAT_KW - 1) + (NAT_KW - 1)
    pick = (dc[None, :, :] == np.arange(2 * NAT_KW - 1)[:, None, None]).astype(np.float32)
    tiles = jnp.einsum('hrd,dqk->hrqk', rpb.astype(F32), jnp.asarray(pick),
                       precision=lax.Precision.HIGHEST)
    tiles = jnp.where(jnp.asarray(col_mask)[None, None], tiles, MASK_VALUE)
    t = jnp.stack([tiles[:, b:b + NAT_KH] for b in range(NAT_KH)], axis=1)
    return t.transpose(0, 1, 3, 2, 4).reshape(N_HEADS, NAT_KH, GRID_W, NAT_KH * GRID_W)


def _nat_kernel(q_ref, k_ref, v_ref, bias_ref, o_ref, *, rows_per_step, n_rows):
    j = pl.program_id(1)
    lane_head = _head_id((GRID_W, BR_W), 1)
    win = NAT_KH * GRID_W

    def body(i, carry):
        r = j * rows_per_step + i
        rs = jnp.clip(r - NAT_KH // 2, 0, n_rows - NAT_KH)
        base = rs - r + (NAT_KH - 1)
        q = q_ref[pl.ds(pl.multiple_of(i * GRID_W, GRID_W), GRID_W), :] * jnp.asarray(HEAD_DIM ** -0.5, BF16)
        kw = k_ref[pl.ds(pl.multiple_of(rs * GRID_W, GRID_W), win), :]
        vw = v_ref[pl.ds(pl.multiple_of(rs * GRID_W, GRID_W), win), :]
        out = jnp.zeros((GRID_W, BR_W), F32)
        for h in range(N_HEADS):
            qz = jnp.where(lane_head == h, q, jnp.zeros_like(q))
            s = _nt(qz, kw) + bias_ref[h, base]
            m = jnp.max(s, axis=-1, keepdims=True)
            pr = jnp.exp(s - m)
            l = jnp.sum(pr, axis=-1, keepdims=True)
            o = _dot(pr.astype(BF16), vw) / l
            out = out + jnp.where(lane_head == h, o, 0.0)
        o_ref[pl.ds(pl.multiple_of(i * GRID_W, GRID_W), GRID_W), :] = out.astype(o_ref.dtype)
        return carry

    lax.fori_loop(0, rows_per_step, body, 0)


def neighbourhood_attention(p, rpb, batch, seq):
    n_rows = seq // GRID_W
    rps = NAT_ROWS_PER_STEP
    steps = n_rows // rps
    bias = _nat_bias_table(rpb)
    cb = COL_NAT // BR_W
    tq = rps * GRID_W
    return pl.pallas_call(
        functools.partial(_nat_kernel, rows_per_step=rps, n_rows=n_rows),
        grid=(batch, steps),
        in_specs=[pl.BlockSpec((tq, BR_W), lambda b, j: (b * steps + j, cb)),
                  pl.BlockSpec((seq, BR_W), lambda b, j: (b, cb + 1)),
                  pl.BlockSpec((seq, BR_W), lambda b, j: (b, cb + 2)),
                  pl.BlockSpec(bias.shape, lambda b, j: (0, 0, 0, 0))],
        out_specs=pl.BlockSpec((tq, BR_W), lambda b, j: (b * steps + j, 0)),
        out_shape=jax.ShapeDtypeStruct((batch * seq, BR_W), BF16),
        compiler_params=_cparams("parallel", "arbitrary"),
        name="nat",
    )(p, p, p, bias)


def _diff_kernel(lam_ref, slope_ref, sub_ref, q_ref, k_ref, v_ref, o_ref,
                 acc1, acc2, m1, l1, m2, l2, below, diag, *, lam_init, n_tiles):
    t = q_ref.shape[0]
    qi = pl.program_id(2)
    slope = slope_ref[0, 0:1, 0:1]
    lp = lam_ref[...]
    lam = (jnp.exp(jnp.sum(lp[0:1] * lp[1:2], axis=-1, keepdims=True))
           - jnp.exp(jnp.sum(lp[2:3] * lp[3:4], axis=-1, keepdims=True)) + lam_init)

    rel = (lax.broadcasted_iota(jnp.int32, (t, t), 0)
           - lax.broadcasted_iota(jnp.int32, (t, t), 1)).astype(F32)
    below[...] = -slope * rel
    diag[...] = -slope * jnp.abs(rel)

    q = q_ref[...] * jnp.asarray(0.125, BF16)
    lane = lax.broadcasted_iota(jnp.int32, q.shape, 1)
    q1 = jnp.where(lane < 64, q, jnp.zeros_like(q))
    q2 = jnp.where(lane >= 64, q, jnp.zeros_like(q))

    acc1[...] = jnp.zeros_like(acc1)
    acc2[...] = jnp.zeros_like(acc2)
    l1[...] = jnp.zeros_like(l1)
    l2[...] = jnp.zeros_like(l2)
    m1[...] = jnp.full_like(m1, -jnp.inf)
    m2[...] = jnp.full_like(m2, -jnp.inf)

    def tile(kj, bias):
        start = pl.multiple_of(kj * t, t)
        k = k_ref[pl.ds(start, t), :]
        v = v_ref[pl.ds(start, t), :]
        for qx, acc, m_ref, l_ref in ((q1, acc1, m1, l1), (q2, acc2, m2, l2)):
            s = _nt(qx, k) + bias
            m_old = m_ref[...]
            m_new = jnp.maximum(m_old, jnp.max(s, axis=-1, keepdims=True))
            a = jnp.exp(m_old - m_new)
            pr = jnp.exp(s - m_new)
            l_ref[...] = a * l_ref[...] + jnp.sum(pr, axis=-1, keepdims=True)
            acc[...] = a * acc[...] + _dot(pr.astype(BF16), v)
            m_ref[...] = m_new

    def before(kj, carry):
        dist = jnp.full((1, 1), (qi - kj) * t, jnp.int32).astype(F32)
        tile(kj, below[...] - slope * dist)
        return carry

    def after(kj, carry):
        dist = jnp.full((1, 1), (kj - qi) * t, jnp.int32).astype(F32)
        tile(kj, -below[...] - slope * dist)
        return carry

    lax.fori_loop(0, qi, before, 0)
    tile(qi, diag[...])
    lax.fori_loop(qi + 1, n_tiles, after, 0)

    o = acc1[...] / l1[...] - lam * (acc2[...] / l2[...])
    o = o * lax.rsqrt(jnp.mean(o * o, axis=-1, keepdims=True) + EPS) * sub_ref[0]
    o_ref[...] = (o * (1.0 - lam_init)).astype(o_ref.dtype)


def diff_attention(p, lam_params, subln, layer, batch, seq):
    t = DIFF_TILE
    nt = seq // t
    lam_init = 0.8 - 0.6 * math.exp(-0.3 * layer)
    slopes = np.exp2(-8.0 * (np.arange(N_HEADS, dtype=np.float64) + 1.0) / N_HEADS)
    slope_tab = jnp.asarray(np.broadcast_to(slopes[:, None, None], (N_HEADS, 8, 128)).astype(np.float32))
    qc, kc, vc = (COL_DIFF // DIFF_DV, (COL_DIFF + DIFF_W) // DIFF_DV, (COL_DIFF + 2 * DIFF_W) // DIFF_DV)
    return pl.pallas_call(
        functools.partial(_diff_kernel, lam_init=lam_init, n_tiles=nt),
        grid=(batch, N_HEADS, nt),
        in_specs=[pl.BlockSpec(lam_params.shape, lambda b, h, i: (0, 0)),
                  pl.BlockSpec((1, 8, 128), lambda b, h, i: (h, 0, 0)),
                  pl.BlockSpec((1, 1, DIFF_DV), lambda b, h, i: (h, 0, 0)),
                  pl.BlockSpec((t, DIFF_DV), lambda b, h, i: (b * nt + i, qc + h)),
                  pl.BlockSpec((seq, DIFF_DV), lambda b, h, i: (b, kc + h)),
                  pl.BlockSpec((seq, DIFF_DV), lambda b, h, i: (b, vc + h))],
        out_specs=pl.BlockSpec((t, DIFF_DV), lambda b, h, i: (b * nt + i, h)),
        out_shape=jax.ShapeDtypeStruct((batch * seq, DIFF_W), BF16),
        scratch_shapes=[pltpu.VMEM((t, DIFF_DV), F32), pltpu.VMEM((t, DIFF_DV), F32),
                        pltpu.VMEM((t, 1), F32), pltpu.VMEM((t, 1), F32),
                        pltpu.VMEM((t, 1), F32), pltpu.VMEM((t, 1), F32),
                        pltpu.VMEM((t, t), F32), pltpu.VMEM((t, t), F32)],
        compiler_params=_cparams("parallel", "parallel", "arbitrary"),
        name="diff_attn",
    )(lam_params, slope_tab, subln.reshape(N_HEADS, 1, DIFF_DV), p, p, p)


def _merge_kernel(hf_ref, hb_ref, rof, rob, rg, ynat, hof, hob, hg, ydiff,
                  wg_ref, bg_ref, wr_ref, wn_ref, wh_ref, wd_ref, wo_ref,
                  rgn_ref, hgn_ref, lng_ref, lnb_ref, of_ref, ob_ref):
    seg_mean = jnp.where(_block_diag_mask(BR_W), 1.0 / HEAD_DIM, 0.0).astype(BF16)

    def head_norm_gate(o, gn, gate):
        ms = _dot_split(o * o, seg_mean, terms=2)
        gx = gate.astype(F32)
        return (o * lax.rsqrt(ms + EPS) * gn * (gx * _sigmoid(gx))).astype(BF16)

    y_ret = head_norm_gate(rof[...] + rob[...], rgn_ref[...], rg[...])
    y_hgrn = head_norm_gate(hof[...] + hob[...], hgn_ref[...], hg[...])
    hb = hb_ref[...]
    d = hf_ref.shape[1]
    merged = None
    for i, (y, w_ref) in enumerate(((y_ret, wr_ref), (ynat[...], wn_ref), (y_hgrn, wh_ref), (ydiff[...], wd_ref))):
        gate = _sigmoid(_dot(hb, wg_ref[:, i * d:(i + 1) * d]) + bg_ref[:, i * d:(i + 1) * d])
        part = gate * _dot(y, w_ref[...])
        merged = part if merged is None else merged + part
    z = ALPHA * hf_ref[...] + _dot(merged.astype(BF16), wo_ref[...])
    h = _layer_norm(z, lng_ref[...], lnb_ref[...])
    of_ref[...] = h
    ob_ref[...] = h.astype(BF16)


def merge_and_norm(hf, hb, p, ret_f, ret_b, y_nat, hg_f, hg_b, y_diff,
                   w_gate, b_gate, w_br_ret, w_br_nat, w_br_hgrn, w_br_diff, w_out,
                   ret_gn, hgrn_gn, ln_g, ln_b, tm=512):
    n, d = hf.shape

    def rows(width, col=0):
        return pl.BlockSpec((tm, width), lambda i: (i, col))

    def whole(a):
        nd = a.ndim
        return pl.BlockSpec(a.shape, lambda i: (0,) * nd)

    consts = [w_gate, b_gate.reshape(1, -1), w_br_ret, w_br_nat, w_br_hgrn, w_br_diff, w_out,
              ret_gn.reshape(1, BR_W), hgrn_gn.reshape(1, BR_W), ln_g.reshape(1, d), ln_b.reshape(1, d)]
    return pl.pallas_call(
        _merge_kernel,
        grid=(n // tm,),
        in_specs=[rows(d), rows(d),
                  rows(BR_W), rows(BR_W), rows(BR_W, COL_RET // BR_W + 3),
                  rows(BR_W),
                  rows(BR_W), rows(BR_W), rows(BR_W, COL_HGRN // BR_W + 4),
                  rows(DIFF_W)] + [whole(a) for a in consts],
        out_specs=[rows(d), rows(d)],
        out_shape=[jax.ShapeDtypeStruct((n, d), F32), jax.ShapeDtypeStruct((n, d), BF16)],
        compiler_params=_cparams("parallel"),
        name="merge",
    )(hf, hb, ret_f, ret_b, p, y_nat, hg_f, hg_b, p, y_diff, *consts)


def _router_kernel(h_ref, w_ref, bias_ref, c_ref):
    w = w_ref[...]
    w_hi = w.astype(BF16)
    w_lo = (w - w_hi.astype(F32)).astype(BF16)
    h = h_ref[...]
    logits = _dot_split(h, w_hi, terms=3) + _dot_split(h, w_lo, terms=2)
    scores = _sigmoid(logits)
    biased = scores + bias_ref[...]
    shape = scores.shape
    lane = lax.broadcasted_iota(jnp.int32, shape, 1).astype(F32)
    group = lax.shift_right_logical(lax.broadcasted_iota(jnp.int32, shape, 1), 3).astype(F32)
    big = jnp.asarray(1e9, F32)
    neg = jnp.asarray(-jnp.inf, F32)

    def first_max(x, ids):
        m = jnp.max(x, axis=-1, keepdims=True)
        return m, jnp.min(jnp.where(x == m, ids, big), axis=-1, keepdims=True)

    gscore = jnp.zeros(shape, F32)
    for g in range(N_GROUPS):
        xg = jnp.where(group == g, biased, neg)
        m1, i1 = first_max(xg, lane)
        m2 = jnp.max(jnp.where(lane == i1, neg, xg), axis=-1, keepdims=True)
        gscore = jnp.where(group == g, m1 + m2, gscore)
    gsel = jnp.zeros(shape, jnp.bool_)
    for _ in range(TOPK_GROUPS):
        _, gi = first_max(gscore, group)
        hit = group == gi
        gsel = gsel | hit
        gscore = jnp.where(hit, neg, gscore)
    cand = jnp.where(gsel, biased, MASK_VALUE)
    esel = jnp.zeros(shape, jnp.bool_)
    for _ in range(TOP_K):
        _, ei = first_max(cand, lane)
        hit = lane == ei
        esel = esel | hit
        cand = jnp.where(hit, neg, cand)
    top = jnp.where(esel, scores, 0.0)
    c_ref[...] = top / (jnp.sum(top, axis=-1, keepdims=True) + 1e-20) * ROUTED_SCALE


def router(hf, w_router, router_bias, tm=512):
    n, d = hf.shape
    e = w_router.shape[1]
    return pl.pallas_call(
        _router_kernel,
        grid=(n // tm,),
        in_specs=[pl.BlockSpec((tm, d), lambda i: (i, 0)),
                  pl.BlockSpec((d, e), lambda i: (0, 0)),
                  pl.BlockSpec((1, e), lambda i: (0, 0))],
        out_specs=pl.BlockSpec((tm, e), lambda i: (i, 0)),
        out_shape=jax.ShapeDtypeStruct((n, e), F32),
        compiler_params=_cparams("parallel"),
        name="router",
    )(hf, w_router, router_bias.reshape(1, e))


def _moe_kernel(hf_ref, hb_ref, c_ref, wg_ref, wu_ref, wd_ref, sg_ref, su_ref, sd_ref,
                lng_ref, lnb_ref, of_ref, ob_ref, acc, *, sub):
    e = pl.program_id(1)
    tm = hf_ref.shape[0]
    n_sub = tm // sub

    def swiglu(x, wg, wu):
        a = _dot(x, wg)
        return a * _sigmoid(a) * _dot(x, wu)

    @pl.when(e == 0)
    def _():
        for s in range(n_sub):
            x = hb_ref[s * sub:(s + 1) * sub, :]
            hid = swiglu(x, sg_ref[...], su_ref[...])
            acc[s * sub:(s + 1) * sub, :] = _dot(hid.astype(BF16), sd_ref[...])

    wg = wg_ref[0].astype(BF16)
    wu = wu_ref[0].astype(BF16)
    wd = wd_ref[0].astype(BF16)
    lane = lax.broadcasted_iota(jnp.int32, (sub, c_ref.shape[1]), 1)
    for s in range(n_sub):
        x = hb_ref[s * sub:(s + 1) * sub, :]
        ce = jnp.sum(jnp.where(lane == e, c_ref[s * sub:(s + 1) * sub, :], 0.0), axis=-1, keepdims=True)
        hid = swiglu(x, wg, wu) * ce
        acc[s * sub:(s + 1) * sub, :] += _dot(hid.astype(BF16), wd)

    @pl.when(e == pl.num_programs(1) - 1)
    def _():
        h = _layer_norm(ALPHA * hf_ref[...] + acc[...], lng_ref[...], lnb_ref[...])
        of_ref[...] = h
        ob_ref[...] = h.astype(BF16)


def moe_and_norm(hf, hb, combine, w_e_gate, w_e_up, w_e_down, w_s_gate, w_s_up, w_s_down,
                 ln_g, ln_b, tm=1024, sub=256):
    n, d = hf.shape
    ne, _, hid = w_e_gate.shape
    return pl.pallas_call(
        functools.partial(_moe_kernel, sub=sub),
        grid=(n // tm, ne),
        in_specs=[pl.BlockSpec((tm, d), lambda i, e: (i, 0)),
                  pl.BlockSpec((tm, d), lambda i, e: (i, 0)),
                  pl.BlockSpec((tm, ne), lambda i, e: (i, 0)),
                  pl.BlockSpec((1, d, hid), lambda i, e: (e, 0, 0)),
                  pl.BlockSpec((1, d, hid), lambda i, e: (e, 0, 0)),
                  pl.BlockSpec((1, hid, d), lambda i, e: (e, 0, 0)),
                  pl.BlockSpec(w_s_gate.shape, lambda i, e: (0, 0)),
                  pl.BlockSpec(w_s_up.shape, lambda i, e: (0, 0)),
                  pl.BlockSpec(w_s_down.shape, lambda i, e: (0, 0)),
                  pl.BlockSpec((1, d), lambda i, e: (0, 0)),
                  pl.BlockSpec((1, d), lambda i, e: (0, 0))],
        out_specs=[pl.BlockSpec((tm, d), lambda i, e: (i, 0)),
                   pl.BlockSpec((tm, d), lambda i, e: (i, 0))],
        out_shape=[jax.ShapeDtypeStruct((n, d), F32), jax.ShapeDtypeStruct((n, d), BF16)],
        scratch_shapes=[pltpu.VMEM((tm, d), F32)],
        compiler_params=_cparams("parallel", "arbitrary"),
        name="moe",
    )(hf, hb, combine, w_e_gate, w_e_up, w_e_down, w_s_gate, w_s_up, w_s_down,
      ln_g.reshape(1, d), ln_b.reshape(1, d))


def kernel(x, ln_in_g, ln_in_b, w_in, w_gate, b_gate, w_br_ret, w_br_nat, w_br_hgrn, w_br_diff, w_out,
           ret_gn, nat_rpb, hgrn_lb, hgrn_gn, diff_lambda, diff_subln, ln1_g, ln1_b, w_router,
           router_bias, w_e_gate, w_e_up, w_e_down, w_s_gate, w_s_up, w_s_down, ln2_g, ln2_b):
    batch, seq, d = x.shape
    bf = lambda a: a.astype(BF16)
    hf, hb = input_layer_norm(x.reshape(batch * seq, d), ln_in_g, ln_in_b)
    for l in range(DEPTH):
        p = in_projection(hb, bf(w_in[l]))
        ret_f, ret_b = retention(p, batch, seq)
        y_nat = neighbourhood_attention(p, nat_rpb[l], batch, seq)
        hg_f, hg_b = hgrn(p, hgrn_lb, l, batch, seq)
        y_diff = diff_attention(p, diff_lambda[l], diff_subln[l], l, batch, seq)
        hf, hb = merge_and_norm(hf, hb, p, ret_f, ret_b, y_nat, hg_f, hg_b, y_diff,
                                bf(w_gate[l]), b_gate[l], bf(w_br_ret[l]), bf(w_br_nat[l]),
                                bf(w_br_hgrn[l]), bf(w_br_diff[l]), bf(w_out[l]),
                                ret_gn[l], hgrn_gn[l], ln1_g[l], ln1_b[l])
        combine = router(hf, w_router[l], router_bias[l])
        hf, hb = moe_and_norm(hf, hb, combine, w_e_gate[l], w_e_up[l], w_e_down[l],
                              bf(w_s_gate[l]), bf(w_s_up[l]), bf(w_s_down[l]), ln2_g[l], ln2_b[l])
    return hf.reshape(batch, seq, d)
```

```python
import functools
import math

import numpy as np
import jax
import jax.numpy as jnp
from jax import lax
from jax.experimental import pallas as pl
from jax.experimental.pallas import tpu as pltpu

F32 = jnp.float32
BF16 = jnp.bfloat16

D_MODEL = 1024
DEPTH = 2
GRID_W = 64
HEAD_DIM = 64
N_HEADS = 4
BR_W = N_HEADS * HEAD_DIM
NAT_KH = 8
NAT_KW = 16
DIFF_DV = 128
DIFF_W = N_HEADS * DIFF_DV
N_EXPERTS = 64
TOP_K = 8
N_GROUPS = 8
TOPK_GROUPS = 4
GROUP_SIZE = N_EXPERTS // N_GROUPS
EXPERT_HIDDEN = 256
ROUTED_SCALE = 2.5
EPS = 1e-5
MASK_VALUE = -1e30
ALPHA = (2.0 * DEPTH) ** 0.25
IN_WIDTH = 4608
COL_RET = 0
COL_NAT = 1024
COL_HGRN = 1792
COL_DIFF = 3072

VMEM_LIMIT = 56 * 1024 * 1024

RET_CHUNK = 256
HGRN_CHUNK = 64
DIFF_TILE = 512
NAT_ROWS_PER_STEP = 8


def _cparams(*sem):
    return pltpu.CompilerParams(dimension_semantics=sem, vmem_limit_bytes=VMEM_LIMIT)


def _nt(a, b):
    return lax.dot_general(a, b, (((1,), (1,)), ((), ())), preferred_element_type=F32)


def _tn(a, b):
    return lax.dot_general(a, b, (((0,), (0,)), ((), ())), preferred_element_type=F32)


def _dot(a, b):
    return jnp.dot(a, b, preferred_element_type=F32)


def _dot_split(x, w_bf16, terms=3):
    acc = None
    rem = x
    for _ in range(terms):
        piece = rem.astype(BF16)
        part = _dot(piece, w_bf16)
        acc = part if acc is None else acc + part
        rem = rem - piece.astype(F32)
    return acc


def _layer_norm(z, g, b):
    mu = jnp.mean(z, axis=-1, keepdims=True)
    zc = z - mu
    var = jnp.mean(zc * zc, axis=-1, keepdims=True)
    return zc * lax.rsqrt(var + EPS) * g + b


def _sigmoid(x):
    return 1.0 / (1.0 + jnp.exp(-x))


def _head_id(shape, axis):
    return lax.shift_right_logical(lax.broadcasted_iota(jnp.int32, shape, axis), 6)


def _block_diag_mask(n):
    return _head_id((n, n), 0) == _head_id((n, n), 1)


def _ln_kernel(x_ref, g_ref, b_ref, hf_ref, hb_ref):
    h = _layer_norm(x_ref[...], g_ref[...], b_ref[...])
    hf_ref[...] = h
    hb_ref[...] = h.astype(BF16)


def input_layer_norm(x2, g, b, tm=1024):
    n, d = x2.shape
    return pl.pallas_call(
        _ln_kernel,
        grid=(n // tm,),
        in_specs=[pl.BlockSpec((tm, d), lambda i: (i, 0)),
                  pl.BlockSpec((1, d), lambda i: (0, 0)),
                  pl.BlockSpec((1, d), lambda i: (0, 0))],
        out_specs=[pl.BlockSpec((tm, d), lambda i: (i, 0)),
                   pl.BlockSpec((tm, d), lambda i: (i, 0))],
        out_shape=[jax.ShapeDtypeStruct((n, d), F32), jax.ShapeDtypeStruct((n, d), BF16)],
        compiler_params=_cparams("parallel"),
        name="input_ln",
    )(x2, g.reshape(1, d), b.reshape(1, d))


def _matmul_kernel(h_ref, w_ref, o_ref):
    o_ref[...] = _dot(h_ref[...], w_ref[...]).astype(o_ref.dtype)


def in_projection(hb, w_bf16, tm=2048, tn=768):
    n, d = hb.shape
    w = w_bf16.shape[1]
    return pl.pallas_call(
        _matmul_kernel,
        grid=(n // tm, w // tn),
        in_specs=[pl.BlockSpec((tm, d), lambda i, j: (i, 0)),
                  pl.BlockSpec((d, tn), lambda i, j: (0, j))],
        out_specs=pl.BlockSpec((tm, tn), lambda i, j: (i, j)),
        out_shape=jax.ShapeDtypeStruct((n, w), BF16),
        compiler_params=_cparams("parallel", "arbitrary"),
        name="in_proj",
    )(hb, w_bf16)


def _retention_tables(c):
    idx = np.arange(N_HEADS, dtype=np.float64)
    lg = [np.log1p(-np.exp2(-5.0 - 2.0 * idx)), np.log1p(-np.exp2(-6.0 - 2.0 * idx))]
    i = np.arange(c, dtype=np.float64)
    diff = i[:, None] - i[None, :]
    lane_head = np.repeat(np.arange(N_HEADS), HEAD_DIM)
    dmat = np.zeros((2, N_HEADS, c, c), np.float32)
    oscale = np.zeros((2, c, BR_W), np.float32)
    kscale = np.zeros((2, c, BR_W), np.float32)
    sdecay = np.zeros((2, 1, BR_W), np.float32)
    for h in range(N_HEADS):
        dmat[0, h] = np.where(diff >= 0, np.exp(lg[0][h] * np.maximum(diff, 0)), 0.0)
        dmat[1, h] = np.where(diff <= 0, np.exp(lg[1][h] * np.maximum(-diff, 0)), 0.0)
    oscale[0] = np.exp(lg[0][lane_head][None, :] * (i[:, None] + 1.0))
    oscale[1] = np.exp(lg[1][lane_head][None, :] * (c - i[:, None]))
    kscale[0] = np.exp(lg[0][lane_head][None, :] * (c - 1.0 - i[:, None]))
    kscale[1] = np.exp(lg[1][lane_head][None, :] * i[:, None])
    sdecay[0, 0] = np.exp(lg[0][lane_head] * c)
    sdecay[1, 0] = np.exp(lg[1][lane_head] * c)
    return (jnp.asarray(dmat), jnp.asarray(oscale), jnp.asarray(kscale), jnp.asarray(sdecay))


def _retention_kernel(qf, kf, vf, qb, kb, vb, dmat, oscale, kscale, sdecay, of_ref, ob_ref, state):
    @pl.when(pl.program_id(1) == 0)
    def _():
        state[...] = jnp.zeros_like(state)

    c = qf.shape[0]
    lane_head = _head_id((c, BR_W), 1)
    bd = _block_diag_mask(BR_W)
    for dirn, (q_ref, k_ref, v_ref, o_ref) in enumerate(((qf, kf, vf, of_ref), (qb, kb, vb, ob_ref))):
        q = q_ref[...]
        k = k_ref[...] * jnp.asarray(HEAD_DIM ** -0.5, BF16)
        v = v_ref[...]
        s_old = state[dirn]
        out = _dot(q, s_old.astype(BF16)) * oscale[dirn]
        for h in range(N_HEADS):
            qz = jnp.where(lane_head == h, q, jnp.zeros_like(q))
            a = (_nt(qz, k) * dmat[dirn, h]).astype(BF16)
            out = out + jnp.where(lane_head == h, _dot(a, v), 0.0)
        o_ref[...] = out
        kw = (k.astype(F32) * kscale[dirn]).astype(BF16)
        state[dirn] = s_old * sdecay[dirn] + jnp.where(bd, _tn(kw, v), 0.0)


def retention(p, batch, seq):
    c = RET_CHUNK
    nc = seq // c
    dmat, oscale, kscale, sdecay = _retention_tables(c)
    cb = COL_RET // BR_W

    def fwd(col):
        return pl.BlockSpec((c, BR_W), lambda b, j: (b * nc + j, col))

    def bwd(col):
        return pl.BlockSpec((c, BR_W), lambda b, j: (b * nc + nc - 1 - j, col))

    def whole(a):
        nd = a.ndim
        return pl.BlockSpec(a.shape, lambda b, j: (0,) * nd)

    n = batch * seq
    return pl.pallas_call(
        _retention_kernel,
        grid=(batch, nc),
        in_specs=[fwd(cb), fwd(cb + 1), fwd(cb + 2), bwd(cb), bwd(cb + 1), bwd(cb + 2),
                  whole(dmat), whole(oscale), whole(kscale), whole(sdecay)],
        out_specs=[pl.BlockSpec((c, BR_W), lambda b, j: (b * nc + j, 0)),
                   pl.BlockSpec((c, BR_W), lambda b, j: (b * nc + nc - 1 - j, 0))],
        out_shape=[jax.ShapeDtypeStruct((n, BR_W), F32)] * 2,
        scratch_shapes=[pltpu.VMEM((2, BR_W, BR_W), F32)],
        compiler_params=_cparams("parallel", "arbitrary"),
        name="retention",
    )(p, p, p, p, p, p, dmat, oscale, kscale, sdecay)


def _hgrn_kernel(lb_ref, qf, ff, vf, qb, fb, vb, of_ref, ob_ref, state, kpad, gpad, vpad, *, layer):
    c = qf.shape[0]

    @pl.when(pl.program_id(1) == 0)
    def _():
        state[...] = jnp.zeros_like(state)
        kpad[...] = jnp.zeros_like(kpad)
        gpad[...] = jnp.zeros_like(gpad)
        vpad[...] = jnp.zeros_like(vpad)

    lb = lb_ref[...]
    e = jnp.exp(lb - jnp.max(lb, axis=0, keepdims=True))
    prob = e / jnp.sum(e, axis=0, keepdims=True)
    lower = jnp.sum(prob[:layer + 1], axis=0) - prob[0]

    row = lax.broadcasted_iota(jnp.int32, (c, c), 0)
    col = lax.broadcasted_iota(jnp.int32, (c, c), 1)
    bd = _block_diag_mask(BR_W)
    seg_ones = jnp.where(bd, 1.0, 0.0).astype(BF16)

    for dirn, (q_ref, f_ref, v_ref, o_ref) in enumerate(((qf, ff, vf, of_ref), (qb, fb, vb, ob_ref))):
        rev = dirn == 1
        lo = lower[dirn:dirn + 1, :]
        fpre = f_ref[...].astype(F32)
        logf = jnp.log(lo + (1.0 - lo) * _sigmoid(fpre))
        kk = (1.0 - lo) * _sigmoid(-fpre)
        qx = q_ref[...].astype(F32)
        q = qx * _sigmoid(qx)
        v_bf = v_ref[...]
        v = v_bf.astype(F32)
        tri = jnp.where((col >= row) if rev else (col <= row), 1.0, 0.0).astype(BF16)
        g = _tri_cumsum(tri, logf)
        g_end = g[0:1, :] if rev else g[c - 1:c, :]
        s_old = state[dirn]
        inter = _nt((q * jnp.exp(g)).astype(BF16), s_old.astype(BF16))
        kd = (kk * jnp.exp(g_end - g)).astype(BF16)
        state[dirn] = s_old * jnp.exp(g_end) + jnp.where(bd, _tn(v_bf, kd), 0.0)
        for b in range(8):
            lo_row = (c - b) if rev else (c + b)
            kpad[b, dirn, lo_row:lo_row + c, :] = kk
            gpad[b, dirn, lo_row:lo_row + c, :] = g
            vpad[b, dirn, lo_row:lo_row + c, :] = v

        intra = jnp.zeros((c, BR_W), F32)
        for b in range(8):
            for a in range(c // 8):
                start = (c + 8 * a) if rev else (c - 8 * a)
                ks = kpad[b, dirn, start:start + c, :]
                gs = gpad[b, dirn, start:start + c, :]
                vs = vpad[b, dirn, start:start + c, :]
                term = q * ks * jnp.exp(g - gs)
                intra = intra + _dot(term.astype(BF16), seg_ones) * vs
        o_ref[...] = inter + intra


def _tri_cumsum(tri_bf16, x):
    acc = None
    rem = x
    for _ in range(3):
        piece = rem.astype(BF16)
        part = _dot(tri_bf16, piece)
        acc = part if acc is None else acc + part
        rem = rem - piece.astype(F32)
    return acc


def hgrn(p, hgrn_lb, layer, batch, seq):
    c = HGRN_CHUNK
    nc = seq // c
    cb = COL_HGRN // BR_W

    def fwd(col):
        return pl.BlockSpec((c, BR_W), lambda b, j: (b * nc + j, col))

    def bwd(col):
        return pl.BlockSpec((c, BR_W), lambda b, j: (b * nc + nc - 1 - j, col))

    n = batch * seq
    return pl.pallas_call(
        functools.partial(_hgrn_kernel, layer=layer),
        grid=(batch, nc),
        in_specs=[pl.BlockSpec(hgrn_lb.shape, lambda b, j: (0, 0, 0)),
                  fwd(cb), fwd(cb + 1), fwd(cb + 3), bwd(cb), bwd(cb + 2), bwd(cb + 3)],
        out_specs=[pl.BlockSpec((c, BR_W), lambda b, j: (b * nc + j, 0)),
                   pl.BlockSpec((c, BR_W), lambda b, j: (b * nc + nc - 1 - j, 0))],
        out_shape=[jax.ShapeDtypeStruct((n, BR_W), F32)] * 2,
        scratch_shapes=[pltpu.VMEM((2, BR_W, BR_W), F32),
                        pltpu.VMEM((8, 2, 3 * c, BR_W), F32),
                        pltpu.VMEM((8, 2, 3 * c, BR_W), F32),
                        pltpu.VMEM((8, 2, 3 * c, BR_W), F32)],
        compiler_params=_cparams("parallel", "arbitrary"),
        name="hgrn",
    )(hgrn_lb, p, p, p, p, p, p)


def _nat_bias_table(rpb):
    cq = np.arange(GRID_W)
    ck = np.arange(GRID_W)
    col_start = np.clip(cq - NAT_KW // 2, 0, GRID_W - NAT_KW)
    col_mask = (ck[None, :] >= col_start[:, None]) & (ck[None, :] < col_start[:, None] + NAT_KW)
    dc = np.clip(ck[None, :] - cq[:, None], -(NAT_KW - 1), NAT_KW - 1) + (NAT_KW - 1)
    pick = (dc[None, :, :] == np.arange(2 * NAT_KW - 1)[:, None, None]).astype(np.float32)
    tiles = jnp.einsum('hrd,dqk->hrqk', rpb.astype(F32), jnp.asarray(pick),
                       precision=lax.Precision.HIGHEST)
    tiles = jnp.where(jnp.asarray(col_mask)[None, None], tiles, MASK_VALUE)
    t = jnp.stack([tiles[:, b:b + NAT_KH] for b in range(NAT_KH)], axis=1)
    return t.transpose(0, 1, 3, 2, 4).reshape(N_HEADS, NAT_KH, GRID_W, NAT_KH * GRID_W)


def _nat_kernel(q_ref, k_ref, v_ref, bias_ref, o_ref, *, rows_per_step, n_rows):
    j = pl.program_id(1)
    lane_head = _head_id((GRID_W, BR_W), 1)
    win = NAT_KH * GRID_W

    def body(i, carry):
        r = j * rows_per_step + i
        rs = jnp.clip(r - NAT_KH // 2, 0, n_rows - NAT_KH)
        base = rs - r + (NAT_KH - 1)
        q = q_ref[pl.ds(pl.multiple_of(i * GRID_W, GRID_W), GRID_W), :] * jnp.asarray(HEAD_DIM ** -0.5, BF16)
        kw = k_ref[pl.ds(pl.multiple_of(rs * GRID_W, GRID_W), win), :]
        vw = v_ref[pl.ds(pl.multiple_of(rs * GRID_W, GRID_W), win), :]
        out = jnp.zeros((GRID_W, BR_W), F32)
        for h in range(N_HEADS):
            qz = jnp.where(lane_head == h, q, jnp.zeros_like(q))
            s = _nt(qz, kw) + bias_ref[h, base]
            m = jnp.max(s, axis=-1, keepdims=True)
            pr = jnp.exp(s - m)
            l = jnp.sum(pr, axis=-1, keepdims=True)
            o = _dot(pr.astype(BF16), vw) / l
            out = out + jnp.where(lane_head == h, o, 0.0)
        o_ref[pl.ds(pl.multiple_of(i * GRID_W, GRID_W), GRID_W), :] = out.astype(o_ref.dtype)
        return carry

    lax.fori_loop(0, rows_per_step, body, 0, unroll=True)


def neighbourhood_attention(p, rpb, batch, seq):
    n_rows = seq // GRID_W
    rps = NAT_ROWS_PER_STEP
    steps = n_rows // rps
    bias = _nat_bias_table(rpb)
    cb = COL_NAT // BR_W
    tq = rps * GRID_W
    return pl.pallas_call(
        functools.partial(_nat_kernel, rows_per_step=rps, n_rows=n_rows),
        grid=(batch, steps),
        in_specs=[pl.BlockSpec((tq, BR_W), lambda b, j: (b * steps + j, cb)),
                  pl.BlockSpec((seq, BR_W), lambda b, j: (b, cb + 1)),
                  pl.BlockSpec((seq, BR_W), lambda b, j: (b, cb + 2)),
                  pl.BlockSpec(bias.shape, lambda b, j: (0, 0, 0, 0))],
        out_specs=pl.BlockSpec((tq, BR_W), lambda b, j: (b * steps + j, 0)),
        out_shape=jax.ShapeDtypeStruct((batch * seq, BR_W), BF16),
        compiler_params=_cparams("parallel", "arbitrary"),
        name="nat",
    )(p, p, p, bias)


def _diff_kernel(lam_ref, slope_ref, sub_ref, q_ref, k_ref, v_ref, o_ref,
                 acc1, acc2, m1, l1, m2, l2, *, lam_init, n_tiles):
    t = q_ref.shape[0]
    reps = t // 128
    qi = pl.program_id(2)
    neg_slope = -slope_ref[0, 0:1, 0:1]
    lp = lam_ref[...]
    lam = (jnp.exp(jnp.sum(lp[0:1] * lp[1:2], axis=-1, keepdims=True))
           - jnp.exp(jnp.sum(lp[2:3] * lp[3:4], axis=-1, keepdims=True)) + lam_init)

    q = q_ref[...] * jnp.asarray(0.125, BF16)
    lane = lax.broadcasted_iota(jnp.int32, q.shape, 1)
    q1 = jnp.where(lane < 64, q, jnp.zeros_like(q))
    q2 = jnp.where(lane >= 64, q, jnp.zeros_like(q))
    qpos = (qi * t + lax.broadcasted_iota(jnp.int32, (t, 1), 0)).astype(F32)

    acc1[...] = jnp.zeros_like(acc1)
    acc2[...] = jnp.zeros_like(acc2)
    l1[...] = jnp.zeros_like(l1)
    l2[...] = jnp.zeros_like(l2)
    m1[...] = jnp.full_like(m1, -jnp.inf)
    m2[...] = jnp.full_like(m2, -jnp.inf)

    def tile(kj, carry):
        start = pl.multiple_of(kj * t, t)
        k = k_ref[pl.ds(start, t), :]
        v = v_ref[pl.ds(start, t), :]
        kpos = (kj * t + lax.broadcasted_iota(jnp.int32, (1, t), 1)).astype(F32)
        bias = neg_slope * jnp.abs(qpos - kpos)
        for qx, acc, m_ref, l_ref in ((q1, acc1, m1, l1), (q2, acc2, m2, l2)):
            s = _nt(qx, k) + bias
            m_old = m_ref[...]
            m_new = jnp.maximum(m_old, jnp.max(s, axis=-1, keepdims=True))
            a = jnp.exp(m_old - m_new)
            pr = jnp.exp(s - jnp.concatenate([m_new] * reps, axis=1))
            l_ref[...] = a * l_ref[...] + jnp.sum(pr, axis=-1, keepdims=True)
            acc[...] = a * acc[...] + _dot(pr.astype(BF16), v)
            m_ref[...] = m_new
        return carry

    lax.fori_loop(0, n_tiles, tile, 0)

    o = acc1[...] / l1[...] - lam * (acc2[...] / l2[...])
    o = o * lax.rsqrt(jnp.mean(o * o, axis=-1, keepdims=True) + EPS) * sub_ref[0]
    o_ref[...] = (o * (1.0 - lam_init)).astype(o_ref.dtype)


def diff_attention(p, lam_params, subln, layer, batch, seq):
    t = DIFF_TILE
    nt = seq // t
    lam_init = 0.8 - 0.6 * math.exp(-0.3 * layer)
    slopes = np.exp2(-8.0 * (np.arange(N_HEADS, dtype=np.float64) + 1.0) / N_HEADS)
    slope_tab = jnp.asarray(np.broadcast_to(slopes[:, None, None], (N_HEADS, 8, 128)).astype(np.float32))
    qc, kc, vc = (COL_DIFF // DIFF_DV, (COL_DIFF + DIFF_W) // DIFF_DV, (COL_DIFF + 2 * DIFF_W) // DIFF_DV)
    return pl.pallas_call(
        functools.partial(_diff_kernel, lam_init=lam_init, n_tiles=nt),
        grid=(batch, N_HEADS, nt),
        in_specs=[pl.BlockSpec(lam_params.shape, lambda b, h, i: (0, 0)),
                  pl.BlockSpec((1, 8, 128), lambda b, h, i: (h, 0, 0)),
                  pl.BlockSpec((1, 1, DIFF_DV), lambda b, h, i: (h, 0, 0)),
                  pl.BlockSpec((t, DIFF_DV), lambda b, h, i: (b * nt + i, qc + h)),
                  pl.BlockSpec((seq, DIFF_DV), lambda b, h, i: (b, kc + h)),
                  pl.BlockSpec((seq, DIFF_DV), lambda b, h, i: (b, vc + h))],
        out_specs=pl.BlockSpec((t, DIFF_DV), lambda b, h, i: (b * nt + i, h)),
        out_shape=jax.ShapeDtypeStruct((batch * seq, DIFF_W), BF16),
        scratch_shapes=[pltpu.VMEM((t, DIFF_DV), F32)] * 6,
        compiler_params=_cparams("parallel", "parallel", "arbitrary"),
        name="diff_attn",
    )(lam_params, slope_tab, subln.reshape(N_HEADS, 1, DIFF_DV), p, p, p)


def _merge_kernel(hf_ref, hb_ref, rof, rob, rg, ynat, hof, hob, hg, ydiff,
                  wg_ref, bg_ref, wr_ref, wn_ref, wh_ref, wd_ref, wo_ref,
                  rgn_ref, hgn_ref, lng_ref, lnb_ref, of_ref, ob_ref):
    seg_mean = jnp.where(_block_diag_mask(BR_W), 1.0 / HEAD_DIM, 0.0).astype(BF16)

    def head_norm_gate(o, gn, gate):
        ms = _dot_split(o * o, seg_mean, terms=2)
        gx = gate.astype(F32)
        return (o * lax.rsqrt(ms + EPS) * gn * (gx * _sigmoid(gx))).astype(BF16)

    y_ret = head_norm_gate(rof[...] + rob[...], rgn_ref[...], rg[...])
    y_hgrn = head_norm_gate(hof[...] + hob[...], hgn_ref[...], hg[...])
    hb = hb_ref[...]
    d = hf_ref.shape[1]
    merged = None
    for i, (y, w_ref) in enumerate(((y_ret, wr_ref), (ynat[...], wn_ref), (y_hgrn, wh_ref), (ydiff[...], wd_ref))):
        gate = _sigmoid(_dot(hb, wg_ref[:, i * d:(i + 1) * d]) + bg_ref[:, i * d:(i + 1) * d])
        part = gate * _dot(y, w_ref[...])
        merged = part if merged is None else merged + part
    z = ALPHA * hf_ref[...] + _dot(merged.astype(BF16), wo_ref[...])
    h = _layer_norm(z, lng_ref[...], lnb_ref[...])
    of_ref[...] = h
    ob_ref[...] = h.astype(BF16)


def merge_and_norm(hf, hb, p, ret_f, ret_b, y_nat, hg_f, hg_b, y_diff,
                   w_gate, b_gate, w_br_ret, w_br_nat, w_br_hgrn, w_br_diff, w_out,
                   ret_gn, hgrn_gn, ln_g, ln_b, tm=512):
    n, d = hf.shape

    def rows(width, col=0):
        return pl.BlockSpec((tm, width), lambda i: (i, col))

    def whole(a):
        nd = a.ndim
        return pl.BlockSpec(a.shape, lambda i: (0,) * nd)

    consts = [w_gate, b_gate.reshape(1, -1), w_br_ret, w_br_nat, w_br_hgrn, w_br_diff, w_out,
              ret_gn.reshape(1, BR_W), hgrn_gn.reshape(1, BR_W), ln_g.reshape(1, d), ln_b.reshape(1, d)]
    return pl.pallas_call(
        _merge_kernel,
        grid=(n // tm,),
        in_specs=[rows(d), rows(d),
                  rows(BR_W), rows(BR_W), rows(BR_W, COL_RET // BR_W + 3),
                  rows(BR_W),
                  rows(BR_W), rows(BR_W), rows(BR_W, COL_HGRN // BR_W + 4),
                  rows(DIFF_W)] + [whole(a) for a in consts],
        out_specs=[rows(d), rows(d)],
        out_shape=[jax.ShapeDtypeStruct((n, d), F32), jax.ShapeDtypeStruct((n, d), BF16)],
        compiler_params=_cparams("parallel"),
        name="merge",
    )(hf, hb, ret_f, ret_b, p, y_nat, hg_f, hg_b, p, y_diff, *consts)


def _router_kernel(h_ref, w_ref, bias_ref, c_ref):
    w = w_ref[...]
    w_hi = w.astype(BF16)
    w_lo = (w - w_hi.astype(F32)).astype(BF16)
    h = h_ref[...]
    logits = _dot_split(h, w_hi, terms=3) + _dot_split(h, w_lo, terms=2)
    scores = _sigmoid(logits)
    biased = scores + bias_ref[...]
    shape = scores.shape
    lane = lax.broadcasted_iota(jnp.int32, shape, 1).astype(F32)
    group = lax.shift_right_logical(lax.broadcasted_iota(jnp.int32, shape, 1), 3).astype(F32)
    big = jnp.asarray(1e9, F32)
    neg = jnp.asarray(-jnp.inf, F32)

    def first_max(x, ids):
        m = jnp.max(x, axis=-1, keepdims=True)
        return m, jnp.min(jnp.where(x == m, ids, big), axis=-1, keepdims=True)

    gscore = jnp.zeros(shape, F32)
    for g in range(N_GROUPS):
        xg = jnp.where(group == g, biased, neg)
        m1, i1 = first_max(xg, lane)
        m2 = jnp.max(jnp.where(lane == i1, neg, xg), axis=-1, keepdims=True)
        gscore = jnp.where(group == g, m1 + m2, gscore)
    gsel = jnp.zeros(shape, jnp.bool_)
    for _ in range(TOPK_GROUPS):
        _, gi = first_max(gscore, group)
        hit = group == gi
        gsel = gsel | hit
        gscore = jnp.where(hit, neg, gscore)
    cand = jnp.where(gsel, biased, MASK_VALUE)
    esel = jnp.zeros(shape, jnp.bool_)
    for _ in range(TOP_K):
        _, ei = first_max(cand, lane)
        hit = lane == ei
        esel = esel | hit
        cand = jnp.where(hit, neg, cand)
    top = jnp.where(esel, scores, 0.0)
    c_ref[...] = top / (jnp.sum(top, axis=-1, keepdims=True) + 1e-20) * ROUTED_SCALE


def router(hf, w_router, router_bias, tm=512):
    n, d = hf.shape
    e = w_router.shape[1]
    return pl.pallas_call(
        _router_kernel,
        grid=(n // tm,),
        in_specs=[pl.BlockSpec((tm, d), lambda i: (i, 0)),
                  pl.BlockSpec((d, e), lambda i: (0, 0)),
                  pl.BlockSpec((1, e), lambda i: (0, 0))],
        out_specs=pl.BlockSpec((tm, e), lambda i: (i, 0)),
        out_shape=jax.ShapeDtypeStruct((n, e), F32),
        compiler_params=_cparams("parallel"),
        name="router",
    )(hf, w_router, router_bias.reshape(1, e))


def _moe_kernel(hf_ref, hb_ref, c_ref, wg_ref, wu_ref, wd_ref, sg_ref, su_ref, sd_ref,
                lng_ref, lnb_ref, of_ref, ob_ref, acc, *, sub):
    e = pl.program_id(1)
    tm = hf_ref.shape[0]
    n_sub = tm // sub

    def swiglu(x, wg, wu):
        a = _dot(x, wg)
        return a * _sigmoid(a) * _dot(x, wu)

    @pl.when(e == 0)
    def _():
        for s in range(n_sub):
            x = hb_ref[s * sub:(s + 1) * sub, :]
            hid = swiglu(x, sg_ref[...], su_ref[...])
            acc[s * sub:(s + 1) * sub, :] = _dot(hid.astype(BF16), sd_ref[...])

    wg = wg_ref[0, 0].astype(BF16)
    wu = wu_ref[0, 0].astype(BF16)
    wd = wd_ref[0, 0].astype(BF16)
    lane = lax.broadcasted_iota(jnp.int32, (sub, c_ref.shape[1]), 1)
    for s in range(n_sub):
        x = hb_ref[s * sub:(s + 1) * sub, :]
        ce = jnp.sum(jnp.where(lane == e, c_ref[s * sub:(s + 1) * sub, :], 0.0), axis=-1, keepdims=True)
        hid = swiglu(x, wg, wu) * ce
        acc[s * sub:(s + 1) * sub, :] += _dot(hid.astype(BF16), wd)

    @pl.when(e == pl.num_programs(1) - 1)
    def _():
        h = _layer_norm(ALPHA * hf_ref[...] + acc[...], lng_ref[...], lnb_ref[...])
        of_ref[...] = h
        ob_ref[...] = h.astype(BF16)


def moe_and_norm(hf, hb, combine, w_e_gate, w_e_up, w_e_down, w_s_gate, w_s_up, w_s_down,
                 ln_g, ln_b, layer, tm=1024, sub=256):
    n, d = hf.shape
    _, ne, _, hid = w_e_gate.shape
    return pl.pallas_call(
        functools.partial(_moe_kernel, sub=sub),
        grid=(n // tm, ne),
        in_specs=[pl.BlockSpec((tm, d), lambda i, e: (i, 0)),
                  pl.BlockSpec((tm, d), lambda i, e: (i, 0)),
                  pl.BlockSpec((tm, ne), lambda i, e: (i, 0)),
                  pl.BlockSpec((1, 1, d, hid), lambda i, e: (layer, e, 0, 0)),
                  pl.BlockSpec((1, 1, d, hid), lambda i, e: (layer, e, 0, 0)),
                  pl.BlockSpec((1, 1, hid, d), lambda i, e: (layer, e, 0, 0)),
                  pl.BlockSpec(w_s_gate.shape, lambda i, e: (0, 0)),
                  pl.BlockSpec(w_s_up.shape, lambda i, e: (0, 0)),
                  pl.BlockSpec(w_s_down.shape, lambda i, e: (0, 0)),
                  pl.BlockSpec((1, d), lambda i, e: (0, 0)),
                  pl.BlockSpec((1, d), lambda i, e: (0, 0))],
        out_specs=[pl.BlockSpec((tm, d), lambda i, e: (i, 0)),
                   pl.BlockSpec((tm, d), lambda i, e: (i, 0))],
        out_shape=[jax.ShapeDtypeStruct((n, d), F32), jax.ShapeDtypeStruct((n, d), BF16)],
        scratch_shapes=[pltpu.VMEM((tm, d), F32)],
        compiler_params=_cparams("parallel", "arbitrary"),
        name="moe",
    )(hf, hb, combine, w_e_gate, w_e_up, w_e_down, w_s_gate, w_s_up, w_s_down,
      ln_g.reshape(1, d), ln_b.reshape(1, d))


def kernel(x, ln_in_g, ln_in_b, w_in, w_gate, b_gate, w_br_ret, w_br_nat, w_br_hgrn, w_br_diff, w_out,
           ret_gn, nat_rpb, hgrn_lb, hgrn_gn, diff_lambda, diff_subln, ln1_g, ln1_b, w_router,
           router_bias, w_e_gate, w_e_up, w_e_down, w_s_gate, w_s_up, w_s_down, ln2_g, ln2_b):
    batch, seq, d = x.shape
    bf = lambda a: a.astype(BF16)
    hf, hb = input_layer_norm(x.reshape(batch * seq, d), ln_in_g, ln_in_b)
    for l in range(DEPTH):
        p = in_projection(hb, bf(w_in[l]))
        ret_f, ret_b = retention(p, batch, seq)
        y_nat = neighbourhood_attention(p, nat_rpb[l], batch, seq)
        hg_f, hg_b = hgrn(p, hgrn_lb, l, batch, seq)
        y_diff = diff_attention(p, diff_lambda[l], diff_subln[l], l, batch, seq)
        hf, hb = merge_and_norm(hf, hb, p, ret_f, ret_b, y_nat, hg_f, hg_b, y_diff,
                                bf(w_gate[l]), b_gate[l], bf(w_br_ret[l]), bf(w_br_nat[l]),
                                bf(w_br_hgrn[l]), bf(w_br_diff[l]), bf(w_out[l]),
                                ret_gn[l], hgrn_gn[l], ln1_g[l], ln1_b[l])
        combine = router(hf, w_router[l], router_bias[l])
        hf, hb = moe_and_norm(hf, hb, combine, w_e_gate, w_e_up, w_e_down,
                              bf(w_s_gate[l]), bf(w_s_up[l]), bf(w_s_down[l]), ln2_g[l], ln2_b[l], l)
    return hf.reshape(batch, seq, d)
```

```python
import functools
import math

import numpy as np
import jax
import jax.numpy as jnp
from jax import lax
from jax.experimental import pallas as pl
from jax.experimental.pallas import tpu as pltpu

F32 = jnp.float32
BF16 = jnp.bfloat16

D_MODEL = 1024
DEPTH = 2
GRID_W = 64
HEAD_DIM = 64
N_HEADS = 4
BR_W = N_HEADS * HEAD_DIM
NAT_KH = 8
NAT_KW = 16
DIFF_DV = 128
DIFF_W = N_HEADS * DIFF_DV
N_EXPERTS = 64
TOP_K = 8
N_GROUPS = 8
TOPK_GROUPS = 4
GROUP_SIZE = N_EXPERTS // N_GROUPS
EXPERT_HIDDEN = 256
ROUTED_SCALE = 2.5
EPS = 1e-5
MASK_VALUE = -1e30
ALPHA = (2.0 * DEPTH) ** 0.25
IN_WIDTH = 4608
COL_RET = 0
COL_NAT = 1024
COL_HGRN = 1792
COL_DIFF = 3072

VMEM_LIMIT = 56 * 1024 * 1024

RET_CHUNK = 256
HGRN_CHUNK = 64
DIFF_TILE = 512
NAT_ROWS_PER_STEP = 8


def _cparams(*sem):
    return pltpu.CompilerParams(dimension_semantics=sem, vmem_limit_bytes=VMEM_LIMIT)


def _nt(a, b):
    return lax.dot_general(a, b, (((1,), (1,)), ((), ())), preferred_element_type=F32)


def _tn(a, b):
    return lax.dot_general(a, b, (((0,), (0,)), ((), ())), preferred_element_type=F32)


def _dot(a, b):
    return jnp.dot(a, b, preferred_element_type=F32)


def _dot_split(x, w_bf16, terms=3):
    acc = None
    rem = x
    for _ in range(terms):
        piece = rem.astype(BF16)
        part = _dot(piece, w_bf16)
        acc = part if acc is None else acc + part
        rem = rem - piece.astype(F32)
    return acc


def _layer_norm(z, g, b):
    mu = jnp.mean(z, axis=-1, keepdims=True)
    zc = z - mu
    var = jnp.mean(zc * zc, axis=-1, keepdims=True)
    return zc * lax.rsqrt(var + EPS) * g + b


def _sigmoid(x):
    return 1.0 / (1.0 + jnp.exp(-x))


def _head_id(shape, axis):
    return lax.shift_right_logical(lax.broadcasted_iota(jnp.int32, shape, axis), 6)


def _block_diag_mask(n):
    return _head_id((n, n), 0) == _head_id((n, n), 1)


def _ln_kernel(x_ref, g_ref, b_ref, hf_ref, hb_ref):
    h = _layer_norm(x_ref[...], g_ref[...], b_ref[...])
    hf_ref[...] = h
    hb_ref[...] = h.astype(BF16)


def input_layer_norm(x2, g, b, tm=1024):
    n, d = x2.shape
    return pl.pallas_call(
        _ln_kernel,
        grid=(n // tm,),
        in_specs=[pl.BlockSpec((tm, d), lambda i: (i, 0)),
                  pl.BlockSpec((1, d), lambda i: (0, 0)),
                  pl.BlockSpec((1, d), lambda i: (0, 0))],
        out_specs=[pl.BlockSpec((tm, d), lambda i: (i, 0)),
                   pl.BlockSpec((tm, d), lambda i: (i, 0))],
        out_shape=[jax.ShapeDtypeStruct((n, d), F32), jax.ShapeDtypeStruct((n, d), BF16)],
        compiler_params=_cparams("parallel"),
        name="input_ln",
    )(x2, g.reshape(1, d), b.reshape(1, d))


def _matmul_kernel(h_ref, w_ref, o_ref):
    o_ref[...] = _dot(h_ref[...], w_ref[...]).astype(o_ref.dtype)


def in_projection(hb, w_bf16, tm=2048, tn=768):
    n, d = hb.shape
    w = w_bf16.shape[1]
    return pl.pallas_call(
        _matmul_kernel,
        grid=(n // tm, w // tn),
        in_specs=[pl.BlockSpec((tm, d), lambda i, j: (i, 0)),
                  pl.BlockSpec((d, tn), lambda i, j: (0, j))],
        out_specs=pl.BlockSpec((tm, tn), lambda i, j: (i, j)),
        out_shape=jax.ShapeDtypeStruct((n, w), BF16),
        compiler_params=_cparams("parallel", "arbitrary"),
        name="in_proj",
    )(hb, w_bf16)


def _retention_tables(c):
    idx = np.arange(N_HEADS, dtype=np.float64)
    lg = [np.log1p(-np.exp2(-5.0 - 2.0 * idx)), np.log1p(-np.exp2(-6.0 - 2.0 * idx))]
    i = np.arange(c, dtype=np.float64)
    diff = i[:, None] - i[None, :]
    lane_head = np.repeat(np.arange(N_HEADS), HEAD_DIM)
    dmat = np.zeros((2, N_HEADS, c, c), np.float32)
    oscale = np.zeros((2, c, BR_W), np.float32)
    kscale = np.zeros((2, c, BR_W), np.float32)
    sdecay = np.zeros((2, 1, BR_W), np.float32)
    for h in range(N_HEADS):
        dmat[0, h] = np.where(diff >= 0, np.exp(lg[0][h] * np.maximum(diff, 0)), 0.0)
        dmat[1, h] = np.where(diff <= 0, np.exp(lg[1][h] * np.maximum(-diff, 0)), 0.0)
    oscale[0] = np.exp(lg[0][lane_head][None, :] * (i[:, None] + 1.0))
    oscale[1] = np.exp(lg[1][lane_head][None, :] * (c - i[:, None]))
    kscale[0] = np.exp(lg[0][lane_head][None, :] * (c - 1.0 - i[:, None]))
    kscale[1] = np.exp(lg[1][lane_head][None, :] * i[:, None])
    sdecay[0, 0] = np.exp(lg[0][lane_head] * c)
    sdecay[1, 0] = np.exp(lg[1][lane_head] * c)
    return (jnp.asarray(dmat), jnp.asarray(oscale), jnp.asarray(kscale), jnp.asarray(sdecay))


def _retention_kernel(qf, kf, vf, qb, kb, vb, dmat, oscale, kscale, sdecay, of_ref, ob_ref, state):
    @pl.when(pl.program_id(1) == 0)
    def _():
        state[...] = jnp.zeros_like(state)

    c = qf.shape[0]
    lane_head = _head_id((c, BR_W), 1)
    bd = _block_diag_mask(BR_W)
    for dirn, (q_ref, k_ref, v_ref, o_ref) in enumerate(((qf, kf, vf, of_ref), (qb, kb, vb, ob_ref))):
        q = q_ref[...]
        k = k_ref[...] * jnp.asarray(HEAD_DIM ** -0.5, BF16)
        v = v_ref[...]
        s_old = state[dirn]
        out = _dot(q, s_old.astype(BF16)) * oscale[dirn]
        for h in range(N_HEADS):
            qz = jnp.where(lane_head == h, q, jnp.zeros_like(q))
            a = (_nt(qz, k) * dmat[dirn, h]).astype(BF16)
            out = out + jnp.where(lane_head == h, _dot(a, v), 0.0)
        o_ref[...] = out
        kw = (k.astype(F32) * kscale[dirn]).astype(BF16)
        state[dirn] = s_old * sdecay[dirn] + jnp.where(bd, _tn(kw, v), 0.0)


def retention(p, batch, seq):
    c = RET_CHUNK
    nc = seq // c
    dmat, oscale, kscale, sdecay = _retention_tables(c)
    cb = COL_RET // BR_W

    def fwd(col):
        return pl.BlockSpec((c, BR_W), lambda b, j: (b * nc + j, col))

    def bwd(col):
        return pl.BlockSpec((c, BR_W), lambda b, j: (b * nc + nc - 1 - j, col))

    def whole(a):
        nd = a.ndim
        return pl.BlockSpec(a.shape, lambda b, j: (0,) * nd)

    n = batch * seq
    return pl.pallas_call(
        _retention_kernel,
        grid=(batch, nc),
        in_specs=[fwd(cb), fwd(cb + 1), fwd(cb + 2), bwd(cb), bwd(cb + 1), bwd(cb + 2),
                  whole(dmat), whole(oscale), whole(kscale), whole(sdecay)],
        out_specs=[pl.BlockSpec((c, BR_W), lambda b, j: (b * nc + j, 0)),
                   pl.BlockSpec((c, BR_W), lambda b, j: (b * nc + nc - 1 - j, 0))],
        out_shape=[jax.ShapeDtypeStruct((n, BR_W), F32)] * 2,
        scratch_shapes=[pltpu.VMEM((2, BR_W, BR_W), F32)],
        compiler_params=_cparams("parallel", "arbitrary"),
        name="retention",
    )(p, p, p, p, p, p, dmat, oscale, kscale, sdecay)


def _hgrn_kernel(lb_ref, qf, ff, vf, qb, fb, vb, of_ref, ob_ref, state, kpad, gpad, vpad, *, layer):
    c = qf.shape[0]

    @pl.when(pl.program_id(1) == 0)
    def _():
        state[...] = jnp.zeros_like(state)
        kpad[...] = jnp.zeros_like(kpad)
        gpad[...] = jnp.zeros_like(gpad)
        vpad[...] = jnp.zeros_like(vpad)

    lb = lb_ref[...]
    e = jnp.exp(lb - jnp.max(lb, axis=0, keepdims=True))
    prob = e / jnp.sum(e, axis=0, keepdims=True)
    lower = jnp.sum(prob[:layer + 1], axis=0) - prob[0]

    row = lax.broadcasted_iota(jnp.int32, (c, c), 0)
    col = lax.broadcasted_iota(jnp.int32, (c, c), 1)
    bd = _block_diag_mask(BR_W)
    seg_ones = jnp.where(bd, 1.0, 0.0).astype(BF16)

    for dirn, (q_ref, f_ref, v_ref, o_ref) in enumerate(((qf, ff, vf, of_ref), (qb, fb, vb, ob_ref))):
        rev = dirn == 1
        lo = lower[dirn:dirn + 1, :]
        fpre = f_ref[...].astype(F32)
        logf = jnp.log(lo + (1.0 - lo) * _sigmoid(fpre))
        kk = (1.0 - lo) * _sigmoid(-fpre)
        qx = q_ref[...].astype(F32)
        q = qx * _sigmoid(qx)
        v_bf = v_ref[...]
        v = v_bf.astype(F32)
        tri = jnp.where((col >= row) if rev else (col <= row), 1.0, 0.0).astype(BF16)
        g = _tri_cumsum(tri, logf)
        g_end = g[0:1, :] if rev else g[c - 1:c, :]
        s_old = state[dirn]
        inter = _nt((q * jnp.exp(g)).astype(BF16), s_old.astype(BF16))
        kd = (kk * jnp.exp(g_end - g)).astype(BF16)
        state[dirn] = s_old * jnp.exp(g_end) + jnp.where(bd, _tn(v_bf, kd), 0.0)
        for b in range(8):
            lo_row = (c - b) if rev else (c + b)
            kpad[b, dirn, lo_row:lo_row + c, :] = kk
            gpad[b, dirn, lo_row:lo_row + c, :] = g
            vpad[b, dirn, lo_row:lo_row + c, :] = v

        intra = jnp.zeros((c, BR_W), F32)
        for b in range(8):
            for a in range(c // 8):
                start = (c + 8 * a) if rev else (c - 8 * a)
                ks = kpad[b, dirn, start:start + c, :]
                gs = gpad[b, dirn, start:start + c, :]
                vs = vpad[b, dirn, start:start + c, :]
                term = q * ks * jnp.exp(g - gs)
                intra = intra + _dot(term.astype(BF16), seg_ones) * vs
        o_ref[...] = inter + intra


def _tri_cumsum(tri_bf16, x):
    acc = None
    rem = x
    for _ in range(3):
        piece = rem.astype(BF16)
        part = _dot(tri_bf16, piece)
        acc = part if acc is None else acc + part
        rem = rem - piece.astype(F32)
    return acc


def hgrn(p, hgrn_lb, layer, batch, seq):
    c = HGRN_CHUNK
    nc = seq // c
    cb = COL_HGRN // BR_W

    def fwd(col):
        return pl.BlockSpec((c, BR_W), lambda b, j: (b * nc + j, col))

    def bwd(col):
        return pl.BlockSpec((c, BR_W), lambda b, j: (b * nc + nc - 1 - j, col))

    n = batch * seq
    return pl.pallas_call(
        functools.partial(_hgrn_kernel, layer=layer),
        grid=(batch, nc),
        in_specs=[pl.BlockSpec(hgrn_lb.shape, lambda b, j: (0, 0, 0)),
                  fwd(cb), fwd(cb + 1), fwd(cb + 3), bwd(cb), bwd(cb + 2), bwd(cb + 3)],
        out_specs=[pl.BlockSpec((c, BR_W), lambda b, j: (b * nc + j, 0)),
                   pl.BlockSpec((c, BR_W), lambda b, j: (b * nc + nc - 1 - j, 0))],
        out_shape=[jax.ShapeDtypeStruct((n, BR_W), F32)] * 2,
        scratch_shapes=[pltpu.VMEM((2, BR_W, BR_W), F32),
                        pltpu.VMEM((8, 2, 3 * c, BR_W), F32),
                        pltpu.VMEM((8, 2, 3 * c, BR_W), F32),
                        pltpu.VMEM((8, 2, 3 * c, BR_W), F32)],
        compiler_params=_cparams("parallel", "arbitrary"),
        name="hgrn",
    )(hgrn_lb, p, p, p, p, p, p)


def _nat_bias_table(rpb):
    cq = np.arange(GRID_W)
    ck = np.arange(GRID_W)
    col_start = np.clip(cq - NAT_KW // 2, 0, GRID_W - NAT_KW)
    col_mask = (ck[None, :] >= col_start[:, None]) & (ck[None, :] < col_start[:, None] + NAT_KW)
    dc = np.clip(ck[None, :] - cq[:, None], -(NAT_KW - 1), NAT_KW - 1) + (NAT_KW - 1)
    pick = (dc[None, :, :] == np.arange(2 * NAT_KW - 1)[:, None, None]).astype(np.float32)
    tiles = jnp.einsum('hrd,dqk->hrqk', rpb.astype(F32), jnp.asarray(pick),
                       precision=lax.Precision.HIGHEST)
    tiles = jnp.where(jnp.asarray(col_mask)[None, None], tiles, MASK_VALUE)
    t = jnp.stack([tiles[:, b:b + NAT_KH] for b in range(NAT_KH)], axis=1)
    return t.transpose(0, 1, 3, 2, 4).reshape(N_HEADS, NAT_KH, GRID_W, NAT_KH * GRID_W)


def _nat_kernel(q_ref, k_ref, v_ref, bias_ref, o_ref, *, rows_per_step, n_rows):
    j = pl.program_id(1)
    lane_head = _head_id((GRID_W, BR_W), 1)
    win = NAT_KH * GRID_W

    def body(i, carry):
        r = j * rows_per_step + i
        rs = jnp.clip(r - NAT_KH // 2, 0, n_rows - NAT_KH)
        base = rs - r + (NAT_KH - 1)
        q = q_ref[pl.ds(pl.multiple_of(i * GRID_W, GRID_W), GRID_W), :] * jnp.asarray(HEAD_DIM ** -0.5, BF16)
        kw = k_ref[pl.ds(pl.multiple_of(rs * GRID_W, GRID_W), win), :]
        vw = v_ref[pl.ds(pl.multiple_of(rs * GRID_W, GRID_W), win), :]
        out = jnp.zeros((GRID_W, BR_W), F32)
        for h in range(N_HEADS):
            qz = jnp.where(lane_head == h, q, jnp.zeros_like(q))
            s = _nt(qz, kw) + bias_ref[h, base]
            m = jnp.max(s, axis=-1, keepdims=True)
            pr = jnp.exp(s - m)
            l = jnp.sum(pr, axis=-1, keepdims=True)
            o = _dot(pr.astype(BF16), vw) / l
            out = out + jnp.where(lane_head == h, o, 0.0)
        o_ref[pl.ds(pl.multiple_of(i * GRID_W, GRID_W), GRID_W), :] = out.astype(o_ref.dtype)
        return carry

    lax.fori_loop(0, rows_per_step, body, 0, unroll=True)


def neighbourhood_attention(p, rpb, batch, seq):
    n_rows = seq // GRID_W
    rps = NAT_ROWS_PER_STEP
    steps = n_rows // rps
    bias = _nat_bias_table(rpb)
    cb = COL_NAT // BR_W
    tq = rps * GRID_W
    return pl.pallas_call(
        functools.partial(_nat_kernel, rows_per_step=rps, n_rows=n_rows),
        grid=(batch, steps),
        in_specs=[pl.BlockSpec((tq, BR_W), lambda b, j: (b * steps + j, cb)),
                  pl.BlockSpec((seq, BR_W), lambda b, j: (b, cb + 1)),
                  pl.BlockSpec((seq, BR_W), lambda b, j: (b, cb + 2)),
                  pl.BlockSpec(bias.shape, lambda b, j: (0, 0, 0, 0))],
        out_specs=pl.BlockSpec((tq, BR_W), lambda b, j: (b * steps + j, 0)),
        out_shape=jax.ShapeDtypeStruct((batch * seq, BR_W), BF16),
        compiler_params=_cparams("parallel", "arbitrary"),
        name="nat",
    )(p, p, p, bias)


def _diff_kernel(lam_ref, slope_ref, sub_ref, q_ref, k_ref, v_ref, o_ref,
                 acc1, acc2, m1, l1, m2, l2, *, lam_init, n_tiles):
    t = q_ref.shape[0]
    reps = t // 128
    qi = pl.program_id(2)
    neg_slope = -slope_ref[0, 0:1, 0:1]
    lp = lam_ref[...]
    lam = (jnp.exp(jnp.sum(lp[0:1] * lp[1:2], axis=-1, keepdims=True))
           - jnp.exp(jnp.sum(lp[2:3] * lp[3:4], axis=-1, keepdims=True)) + lam_init)

    q = q_ref[...] * jnp.asarray(0.125, BF16)
    lane = lax.broadcasted_iota(jnp.int32, q.shape, 1)
    q1 = jnp.where(lane < 64, q, jnp.zeros_like(q))
    q2 = jnp.where(lane >= 64, q, jnp.zeros_like(q))
    qpos = (qi * t + lax.broadcasted_iota(jnp.int32, (t, 1), 0)).astype(F32)

    acc1[...] = jnp.zeros_like(acc1)
    acc2[...] = jnp.zeros_like(acc2)
    l1[...] = jnp.zeros_like(l1)
    l2[...] = jnp.zeros_like(l2)
    m1[...] = jnp.full_like(m1, -jnp.inf)
    m2[...] = jnp.full_like(m2, -jnp.inf)

    def tile(kj, carry):
        start = pl.multiple_of(kj * t, t)
        k = k_ref[pl.ds(start, t), :]
        v = v_ref[pl.ds(start, t), :]
        kpos = (kj * t + lax.broadcasted_iota(jnp.int32, (1, t), 1)).astype(F32)
        bias = neg_slope * jnp.abs(qpos - kpos)
        for qx, acc, m_ref, l_ref in ((q1, acc1, m1, l1), (q2, acc2, m2, l2)):
            s = _nt(qx, k) + bias
            m_old = m_ref[...]
            m_new = jnp.maximum(m_old, jnp.max(s, axis=-1, keepdims=True))
            a = jnp.exp(m_old - m_new)
            pr = jnp.exp(s - jnp.concatenate([m_new] * reps, axis=1))
            l_ref[...] = a * l_ref[...] + jnp.sum(pr, axis=-1, keepdims=True)
            acc[...] = a * acc[...] + _dot(pr.astype(BF16), v)
            m_ref[...] = m_new
        return carry

    lax.fori_loop(0, n_tiles, tile, 0)

    o = acc1[...] / l1[...] - lam * (acc2[...] / l2[...])
    o = o * lax.rsqrt(jnp.mean(o * o, axis=-1, keepdims=True) + EPS) * sub_ref[0]
    o_ref[...] = (o * (1.0 - lam_init)).astype(o_ref.dtype)


def diff_attention(p, lam_params, subln, layer, batch, seq):
    t = DIFF_TILE
    nt = seq // t
    lam_init = 0.8 - 0.6 * math.exp(-0.3 * layer)
    slopes = np.exp2(-8.0 * (np.arange(N_HEADS, dtype=np.float64) + 1.0) / N_HEADS)
    slope_tab = jnp.asarray(np.broadcast_to(slopes[:, None, None], (N_HEADS, 8, 128)).astype(np.float32))
    qc, kc, vc = (COL_DIFF // DIFF_DV, (COL_DIFF + DIFF_W) // DIFF_DV, (COL_DIFF + 2 * DIFF_W) // DIFF_DV)
    return pl.pallas_call(
        functools.partial(_diff_kernel, lam_init=lam_init, n_tiles=nt),
        grid=(batch, N_HEADS, nt),
        in_specs=[pl.BlockSpec(lam_params.shape, lambda b, h, i: (0, 0)),
                  pl.BlockSpec((1, 8, 128), lambda b, h, i: (h, 0, 0)),
                  pl.BlockSpec((1, 1, DIFF_DV), lambda b, h, i: (h, 0, 0)),
                  pl.BlockSpec((t, DIFF_DV), lambda b, h, i: (b * nt + i, qc + h)),
                  pl.BlockSpec((seq, DIFF_DV), lambda b, h, i: (b, kc + h)),
                  pl.BlockSpec((seq, DIFF_DV), lambda b, h, i: (b, vc + h))],
        out_specs=pl.BlockSpec((t, DIFF_DV), lambda b, h, i: (b * nt + i, h)),
        out_shape=jax.ShapeDtypeStruct((batch * seq, DIFF_W), BF16),
        scratch_shapes=[pltpu.VMEM((t, DIFF_DV), F32)] * 6,
        compiler_params=_cparams("parallel", "parallel", "arbitrary"),
        name="diff_attn",
    )(lam_params, slope_tab, subln.reshape(N_HEADS, 1, DIFF_DV), p, p, p)


def _merge_kernel(hf_ref, hb_ref, rof, rob, rg, ynat, hof, hob, hg, ydiff,
                  wg_ref, bg_ref, wr_ref, wn_ref, wh_ref, wd_ref, wo_ref,
                  rgn_ref, hgn_ref, lng_ref, lnb_ref, of_ref, ob_ref, op_ref):
    seg_mean = jnp.where(_block_diag_mask(BR_W), 1.0 / HEAD_DIM, 0.0).astype(BF16)

    def head_norm_gate(o, gn, gate):
        ms = _dot_split(o * o, seg_mean, terms=2)
        gx = gate.astype(F32)
        return (o * lax.rsqrt(ms + EPS) * gn * (gx * _sigmoid(gx))).astype(BF16)

    y_ret = head_norm_gate(rof[...] + rob[...], rgn_ref[...], rg[...])
    y_hgrn = head_norm_gate(hof[...] + hob[...], hgn_ref[...], hg[...])
    hb = hb_ref[...]
    d = hf_ref.shape[1]
    merged = None
    for i, (y, w_ref) in enumerate(((y_ret, wr_ref), (ynat[...], wn_ref), (y_hgrn, wh_ref), (ydiff[...], wd_ref))):
        gate = _sigmoid(_dot(hb, wg_ref[:, i * d:(i + 1) * d]) + bg_ref[:, i * d:(i + 1) * d])
        part = gate * _dot(y, w_ref[...])
        merged = part if merged is None else merged + part
    z = ALPHA * hf_ref[...] + _dot(merged.astype(BF16), wo_ref[...])
    h = _layer_norm(z, lng_ref[...], lnb_ref[...])
    of_ref[...] = h
    ob_ref[...] = h.astype(BF16)
    op_ref[...] = _pack_pairs(h)


def merge_and_norm(hf, hb, p, ret_f, ret_b, y_nat, hg_f, hg_b, y_diff,
                   w_gate, b_gate, w_br_ret, w_br_nat, w_br_hgrn, w_br_diff, w_out,
                   ret_gn, hgrn_gn, ln_g, ln_b, tm=512):
    n, d = hf.shape

    def rows(width, col=0):
        return pl.BlockSpec((tm, width), lambda i: (i, col))

    def whole(a):
        nd = a.ndim
        return pl.BlockSpec(a.shape, lambda i: (0,) * nd)

    consts = [w_gate, b_gate.reshape(1, -1), w_br_ret, w_br_nat, w_br_hgrn, w_br_diff, w_out,
              ret_gn.reshape(1, BR_W), hgrn_gn.reshape(1, BR_W), ln_g.reshape(1, d), ln_b.reshape(1, d)]
    return pl.pallas_call(
        _merge_kernel,
        grid=(n // tm,),
        in_specs=[rows(d), rows(d),
                  rows(BR_W), rows(BR_W), rows(BR_W, COL_RET // BR_W + 3),
                  rows(BR_W),
                  rows(BR_W), rows(BR_W), rows(BR_W, COL_HGRN // BR_W + 4),
                  rows(DIFF_W)] + [whole(a) for a in consts],
        out_specs=[rows(d), rows(d), rows(d // 2)],
        out_shape=[jax.ShapeDtypeStruct((n, d), F32), jax.ShapeDtypeStruct((n, d), BF16),
                   jax.ShapeDtypeStruct((n, d // 2), jnp.int32)],
        compiler_params=_cparams("parallel"),
        name="merge",
    )(hf, hb, ret_f, ret_b, p, y_nat, hg_f, hg_b, p, y_diff, *consts)


def _router_kernel(h_ref, w_ref, bias_ref, idx_ref, w_ref8, rank_ref, cnt_ref, carry):
    w = w_ref[...]
    w_hi = w.astype(BF16)
    w_lo = (w - w_hi.astype(F32)).astype(BF16)
    h = h_ref[...]
    logits = _dot_split(h, w_hi, terms=3) + _dot_split(h, w_lo, terms=2)
    scores = _sigmoid(logits)
    biased = scores + bias_ref[...]
    shape = scores.shape
    lane = lax.broadcasted_iota(jnp.int32, shape, 1).astype(F32)
    group = lax.shift_right_logical(lax.broadcasted_iota(jnp.int32, shape, 1), 3).astype(F32)
    big = jnp.asarray(1e9, F32)
    neg = jnp.asarray(-jnp.inf, F32)

    def first_max(x, ids):
        m = jnp.max(x, axis=-1, keepdims=True)
        return m, jnp.min(jnp.where(x == m, ids, big), axis=-1, keepdims=True)

    gscore = jnp.zeros(shape, F32)
    for g in range(N_GROUPS):
        xg = jnp.where(group == g, biased, neg)
        m1, i1 = first_max(xg, lane)
        m2 = jnp.max(jnp.where(lane == i1, neg, xg), axis=-1, keepdims=True)
        gscore = jnp.where(group == g, m1 + m2, gscore)
    gsel = jnp.zeros(shape, jnp.bool_)
    for _ in range(TOPK_GROUPS):
        _, gi = first_max(gscore, group)
        hit = group == gi
        gsel = gsel | hit
        gscore = jnp.where(hit, neg, gscore)
    cand = jnp.where(gsel, biased, MASK_VALUE)
    esel = jnp.zeros(shape, jnp.bool_)
    picks = []
    for _ in range(TOP_K):
        _, ei = first_max(cand, lane)
        hit = lane == ei
        picks.append((ei, jnp.sum(jnp.where(hit, scores, 0.0), axis=-1, keepdims=True)))
        esel = esel | hit
        cand = jnp.where(hit, neg, cand)

    @pl.when(pl.program_id(0) == 0)
    def _():
        carry[...] = jnp.zeros_like(carry)

    tm = shape[0]
    sel = jnp.where(esel, 1.0, 0.0)
    earlier = (lax.broadcasted_iota(jnp.int32, (tm, tm), 1) < lax.broadcasted_iota(jnp.int32, (tm, tm), 0))
    prefix = _dot(jnp.where(earlier, 1.0, 0.0).astype(BF16), sel.astype(BF16)) + carry[...]
    carry[...] = carry[...] + jnp.sum(sel, axis=0, keepdims=True)
    cnt_ref[...] = carry[...]

    slot = lax.broadcasted_iota(jnp.int32, (tm, TOP_K), 1)
    idx8 = jnp.zeros((tm, TOP_K), F32)
    w8 = jnp.zeros((tm, TOP_K), F32)
    rank8 = jnp.zeros((tm, TOP_K), F32)
    wsum = jnp.zeros((tm, 1), F32)
    for k, (ei, wk) in enumerate(picks):
        rk = jnp.sum(jnp.where(lane == ei, prefix, 0.0), axis=-1, keepdims=True)
        idx8 = jnp.where(slot == k, ei, idx8)
        w8 = jnp.where(slot == k, wk, w8)
        rank8 = jnp.where(slot == k, rk, rank8)
        wsum = wsum + wk
    idx_ref[...] = idx8.astype(jnp.int32)
    rank_ref[...] = rank8.astype(jnp.int32)
    w_ref8[...] = w8 / (wsum + 1e-20) * ROUTED_SCALE


def router(hf, w_router, router_bias, tm=512):
    n, d = hf.shape
    e = w_router.shape[1]
    return pl.pallas_call(
        _router_kernel,
        grid=(n // tm,),
        in_specs=[pl.BlockSpec((tm, d), lambda i: (i, 0)),
                  pl.BlockSpec((d, e), lambda i: (0, 0)),
                  pl.BlockSpec((1, e), lambda i: (0, 0))],
        out_specs=[pl.BlockSpec((tm, TOP_K), lambda i: (i, 0)),
                   pl.BlockSpec((tm, TOP_K), lambda i: (i, 0)),
                   pl.BlockSpec((tm, TOP_K), lambda i: (i, 0)),
                   pl.BlockSpec((1, e), lambda i: (0, 0))],
        out_shape=[jax.ShapeDtypeStruct((n, TOP_K), jnp.int32),
                   jax.ShapeDtypeStruct((n, TOP_K), F32),
                   jax.ShapeDtypeStruct((n, TOP_K), jnp.int32),
                   jax.ShapeDtypeStruct((1, e), F32)],
        scratch_shapes=[pltpu.VMEM((1, e), F32)],
        compiler_params=_cparams("arbitrary"),
        name="router",
    )(hf, w_router, router_bias.reshape(1, e))


HIGH_HALF = -65536


def _pack_pairs(x):
    w = x.shape[1] // 2
    lo = lax.bitcast_convert_type(x[:, :w].astype(BF16).astype(F32), jnp.int32)
    hi = lax.bitcast_convert_type(x[:, w:].astype(BF16).astype(F32), jnp.int32)
    return lax.shift_right_logical(lo, 16) | (hi & HIGH_HALF)


def _unpack_pairs(p):
    lo = lax.bitcast_convert_type(lax.shift_left(p, 16), F32)
    hi = lax.bitcast_convert_type(p & HIGH_HALF, F32)
    return jnp.concatenate([lo, hi], axis=1)


def _swiglu(x, wg, wu):
    a = _dot(x, wg)
    return a * _sigmoid(a) * _dot(x, wu)


def _dispatch_kernel(pos_ref, hp_ref, xs_ref, sem):
    td = hp_ref.shape[0]

    def row_copy(n, k):
        return pltpu.make_async_copy(hp_ref.at[pl.ds(n, 1)], xs_ref.at[pl.ds(pos_ref[n * TOP_K + k], 1)], sem.at[0])

    def start(n, carry):
        for k in range(TOP_K):
            row_copy(n, k).start()
        return carry

    def wait(n, carry):
        for k in range(TOP_K):
            row_copy(n, k).wait()
        return carry

    lax.fori_loop(0, td, start, 0)
    lax.fori_loop(0, td, wait, 0)


def dispatch(hp, pos_flat, td=256):
    n, w = hp.shape
    return pl.pallas_call(
        _dispatch_kernel,
        grid=(n // td,),
        in_specs=[pl.BlockSpec((td * TOP_K,), lambda i: (i,), memory_space=pltpu.SMEM),
                  pl.BlockSpec((td, w), lambda i: (i, 0))],
        out_specs=pl.BlockSpec(memory_space=pl.ANY),
        out_shape=jax.ShapeDtypeStruct((n * TOP_K, w), jnp.int32),
        scratch_shapes=[pltpu.SemaphoreType.DMA((1,))],
        compiler_params=_cparams("arbitrary"),
        name="dispatch",
    )(pos_flat, hp)


def _expert_kernel(item_ref, n_items_ref, xs_ref, wg_ref, wu_ref, wd_ref, ys_ref, wgb, wub, wdb):
    i = pl.program_id(0)

    @pl.when(i < n_items_ref[0])
    def _():
        expert = item_ref[0, i]
        lo = item_ref[2, i]
        hi = item_ref[3, i]

        @pl.when((i == 0) | (expert != item_ref[0, jnp.maximum(i - 1, 0)]))
        def _():
            wgb[...] = wg_ref[0, 0].astype(BF16)
            wub[...] = wu_ref[0, 0].astype(BF16)
            wdb[...] = wd_ref[0, 0].astype(BF16)

        x = _unpack_pairs(xs_ref[...]).astype(BF16)
        hid = _swiglu(x, wgb[...], wub[...])
        y = _pack_pairs(_dot(hid.astype(BF16), wdb[...]))

        @pl.when(lo == 0)
        def _():
            ys_ref[...] = y

        @pl.when(lo != 0)
        def _():
            row = lax.broadcasted_iota(jnp.int32, y.shape, 0)
            ys_ref[...] = jnp.where((row >= lo) & (row < hi), y, ys_ref[...])


def _expert_items(counts, n_rows, tr, max_items):
    counts = counts.astype(jnp.int32)
    ends = jnp.cumsum(counts)
    starts = ends - counts
    first_tile = starts // tr
    last_tile = (ends - 1) // tr
    per_expert = jnp.where(counts > 0, last_tile - first_tile + 1, 0)
    item_end = jnp.cumsum(per_expert)
    n_items = item_end[-1]
    i = jnp.minimum(jnp.arange(max_items, dtype=jnp.int32), n_items - 1)
    expert = jnp.sum((item_end[None, :] <= i[:, None]).astype(jnp.int32), axis=1)
    pick = lambda a: jnp.sum(jnp.where(jnp.arange(N_EXPERTS)[None, :] == expert[:, None], a[None, :], 0), axis=1)
    tile = pick(first_tile) + (i - (pick(item_end) - pick(per_expert)))
    lo = jnp.maximum(pick(starts), tile * tr) - tile * tr
    hi = jnp.minimum(pick(ends), (tile + 1) * tr) - tile * tr
    return jnp.stack([expert, tile, lo, hi]).astype(jnp.int32), n_items.reshape(1).astype(jnp.int32)


def grouped_experts(xs, counts, w_e_gate, w_e_up, w_e_down, layer, tr=256):
    rows, w = xs.shape
    _, ne, d, hid = w_e_gate.shape
    max_items = rows // tr + ne
    items, n_items = _expert_items(counts, rows, tr, max_items)
    grid_spec = pltpu.PrefetchScalarGridSpec(
        num_scalar_prefetch=2,
        grid=(max_items,),
        in_specs=[pl.BlockSpec((tr, w), lambda i, it, n: (it[1, i], 0)),
                  pl.BlockSpec((1, 1, d, hid), lambda i, it, n: (layer, it[0, i], 0, 0)),
                  pl.BlockSpec((1, 1, d, hid), lambda i, it, n: (layer, it[0, i], 0, 0)),
                  pl.BlockSpec((1, 1, hid, d), lambda i, it, n: (layer, it[0, i], 0, 0))],
        out_specs=pl.BlockSpec((tr, w), lambda i, it, n: (it[1, i], 0)),
        scratch_shapes=[pltpu.VMEM((d, hid), BF16), pltpu.VMEM((d, hid), BF16), pltpu.VMEM((hid, d), BF16)],
    )
    return pl.pallas_call(
        _expert_kernel,
        grid_spec=grid_spec,
        out_shape=jax.ShapeDtypeStruct((rows, w), jnp.int32),
        compiler_params=_cparams("arbitrary"),
        name="experts",
    )(items, n_items, xs, w_e_gate, w_e_up, w_e_down)


def _combine_kernel(pos_ref, ys_ref, w8_ref, hf_ref, hb_ref, sg_ref, su_ref, sd_ref, lng_ref, lnb_ref,
                    of_ref, ob_ref, buf, sem):
    tc = hf_ref.shape[0]

    def row_copy(n, k):
        return pltpu.make_async_copy(ys_ref.at[pl.ds(pos_ref[n * TOP_K + k], 1)], buf.at[k, pl.ds(n, 1)], sem.at[0])

    def start(n, carry):
        for k in range(TOP_K):
            row_copy(n, k).start()
        return carry

    def wait(n, carry):
        for k in range(TOP_K):
            row_copy(n, k).wait()
        return carry

    lax.fori_loop(0, tc, start, 0)
    acc = _dot(_swiglu(hb_ref[...], sg_ref[...], su_ref[...]).astype(BF16), sd_ref[...])
    lax.fori_loop(0, tc, wait, 0)
    w8 = w8_ref[...]
    for k in range(TOP_K):
        acc = acc + w8[:, k:k + 1] * _unpack_pairs(buf[k])
    h = _layer_norm(ALPHA * hf_ref[...] + acc, lng_ref[...], lnb_ref[...])
    of_ref[...] = h
    ob_ref[...] = h.astype(BF16)


def combine_and_norm(ys, pos_flat, w8, hf, hb, w_s_gate, w_s_up, w_s_down, ln_g, ln_b, tc=128):
    n, d = hf.shape
    w = ys.shape[1]

    def whole(a):
        return pl.BlockSpec(a.shape, lambda i: (0, 0))

    consts = [w_s_gate, w_s_up, w_s_down, ln_g.reshape(1, d), ln_b.reshape(1, d)]
    return pl.pallas_call(
        _combine_kernel,
        grid=(n // tc,),
        in_specs=[pl.BlockSpec((tc * TOP_K,), lambda i: (i,), memory_space=pltpu.SMEM),
                  pl.BlockSpec(memory_space=pl.ANY),
                  pl.BlockSpec((tc, TOP_K), lambda i: (i, 0)),
                  pl.BlockSpec((tc, d), lambda i: (i, 0)),
                  pl.BlockSpec((tc, d), lambda i: (i, 0))] + [whole(a) for a in consts],
        out_specs=[pl.BlockSpec((tc, d), lambda i: (i, 0)), pl.BlockSpec((tc, d), lambda i: (i, 0))],
        out_shape=[jax.ShapeDtypeStruct((n, d), F32), jax.ShapeDtypeStruct((n, d), BF16)],
        scratch_shapes=[pltpu.VMEM((TOP_K, tc, w), jnp.int32), pltpu.SemaphoreType.DMA((1,))],
        compiler_params=_cparams("arbitrary"),
        name="combine",
    )(pos_flat, ys, w8, hf, hb, *consts)


def moe_and_norm(hf, hb, hp, w_router, router_bias, w_e_gate, w_e_up, w_e_down,
                 w_s_gate, w_s_up, w_s_down, ln_g, ln_b, layer):
    idx8, w8, rank8, counts = router(hf, w_router, router_bias)
    counts = counts.reshape(-1).astype(jnp.int32)
    starts = jnp.cumsum(counts) - counts
    seg = jnp.sum(jnp.where(idx8[..., None] == jnp.arange(N_EXPERTS, dtype=jnp.int32), starts, 0), axis=-1)
    pos_flat = (seg + rank8).reshape(-1)
    xs = dispatch(hp, pos_flat)
    ys = grouped_experts(xs, counts, w_e_gate, w_e_up, w_e_down, layer)
    return combine_and_norm(ys, pos_flat, w8, hf, hb, w_s_gate, w_s_up, w_s_down, ln_g, ln_b)


def kernel(x, ln_in_g, ln_in_b, w_in, w_gate, b_gate, w_br_ret, w_br_nat, w_br_hgrn, w_br_diff, w_out,
           ret_gn, nat_rpb, hgrn_lb, hgrn_gn, diff_lambda, diff_subln, ln1_g, ln1_b, w_router,
           router_bias, w_e_gate, w_e_up, w_e_down, w_s_gate, w_s_up, w_s_down, ln2_g, ln2_b):
    batch, seq, d = x.shape
    bf = lambda a: a.astype(BF16)
    hf, hb = input_layer_norm(x.reshape(batch * seq, d), ln_in_g, ln_in_b)
    for l in range(DEPTH):
        p = in_projection(hb, bf(w_in[l]))
        ret_f, ret_b = retention(p, batch, seq)
        y_nat = neighbourhood_attention(p, nat_rpb[l], batch, seq)
        hg_f, hg_b = hgrn(p, hgrn_lb, l, batch, seq)
        y_diff = diff_attention(p, diff_lambda[l], diff_subln[l], l, batch, seq)
        hf, hb, hp = merge_and_norm(hf, hb, p, ret_f, ret_b, y_nat, hg_f, hg_b, y_diff,
                                bf(w_gate[l]), b_gate[l], bf(w_br_ret[l]), bf(w_br_nat[l]),
                                bf(w_br_hgrn[l]), bf(w_br_diff[l]), bf(w_out[l]),
                                ret_gn[l], hgrn_gn[l], ln1_g[l], ln1_b[l])
        hf, hb = moe_and_norm(hf, hb, hp, w_router[l], router_bias[l], w_e_gate, w_e_up, w_e_down,
                              bf(w_s_gate[l]), bf(w_s_up[l]), bf(w_s_down[l]), ln2_g[l], ln2_b[l], l)
    return hf.reshape(batch, seq, d)
```

```python
import functools
import math

import numpy as np
import jax
import jax.numpy as jnp
from jax import lax
from jax.experimental import pallas as pl
from jax.experimental.pallas import tpu as pltpu

F32 = jnp.float32
BF16 = jnp.bfloat16

D_MODEL = 1024
DEPTH = 2
GRID_W = 64
HEAD_DIM = 64
N_HEADS = 4
BR_W = N_HEADS * HEAD_DIM
NAT_KH = 8
NAT_KW = 16
DIFF_DV = 128
DIFF_W = N_HEADS * DIFF_DV
N_EXPERTS = 64
TOP_K = 8
N_GROUPS = 8
TOPK_GROUPS = 4
GROUP_SIZE = N_EXPERTS // N_GROUPS
EXPERT_HIDDEN = 256
ROUTED_SCALE = 2.5
EPS = 1e-5
MASK_VALUE = -1e30
ALPHA = (2.0 * DEPTH) ** 0.25
IN_WIDTH = 4608
COL_RET = 0
COL_NAT = 1024
COL_HGRN = 1792
COL_DIFF = 3072

VMEM_LIMIT = 56 * 1024 * 1024

RET_CHUNK = 256
HGRN_CHUNK = 64
DIFF_TILE = 512
NAT_ROWS_PER_STEP = 8
ROUTE_TILE = 128
MOE_TILE = 256
EXPERT_ROWS = 256
RUN_ALIGN = 8


def _run_length(count):
    return jnp.floor((count + (RUN_ALIGN - 1)) * (1.0 / RUN_ALIGN)) * RUN_ALIGN


def _tile_rows(tm):
    bound = TOP_K * tm + N_EXPERTS * (RUN_ALIGN - 1)
    return -(-bound // EXPERT_ROWS) * EXPERT_ROWS


def _sorted_rows(n_tokens, tm):
    bound = TOP_K * n_tokens + (n_tokens // tm) * N_EXPERTS * (RUN_ALIGN - 1)
    return -(-bound // EXPERT_ROWS) * EXPERT_ROWS


def _cparams(*sem):
    return pltpu.CompilerParams(dimension_semantics=sem, vmem_limit_bytes=VMEM_LIMIT)


def _nt(a, b):
    return lax.dot_general(a, b, (((1,), (1,)), ((), ())), preferred_element_type=F32)


def _tn(a, b):
    return lax.dot_general(a, b, (((0,), (0,)), ((), ())), preferred_element_type=F32)


def _dot(a, b):
    return jnp.dot(a, b, preferred_element_type=F32)


def _dot_split(x, w_bf16, terms=3):
    acc = None
    rem = x
    for _ in range(terms):
        piece = rem.astype(BF16)
        part = _dot(piece, w_bf16)
        acc = part if acc is None else acc + part
        rem = rem - piece.astype(F32)
    return acc


def _layer_norm(z, g, b):
    mu = jnp.mean(z, axis=-1, keepdims=True)
    zc = z - mu
    var = jnp.mean(zc * zc, axis=-1, keepdims=True)
    return zc * lax.rsqrt(var + EPS) * g + b


def _sigmoid(x):
    return 1.0 / (1.0 + jnp.exp(-x))


def _head_id(shape, axis):
    return lax.shift_right_logical(lax.broadcasted_iota(jnp.int32, shape, axis), 6)


def _block_diag_mask(n):
    return _head_id((n, n), 0) == _head_id((n, n), 1)


def _ln_kernel(x_ref, g_ref, b_ref, hf_ref, hb_ref):
    h = _layer_norm(x_ref[...], g_ref[...], b_ref[...])
    hf_ref[...] = h
    hb_ref[...] = h.astype(BF16)


def input_layer_norm(x2, g, b, tm=1024):
    n, d = x2.shape
    return pl.pallas_call(
        _ln_kernel,
        grid=(n // tm,),
        in_specs=[pl.BlockSpec((tm, d), lambda i: (i, 0)),
                  pl.BlockSpec((1, d), lambda i: (0, 0)),
                  pl.BlockSpec((1, d), lambda i: (0, 0))],
        out_specs=[pl.BlockSpec((tm, d), lambda i: (i, 0)),
                   pl.BlockSpec((tm, d), lambda i: (i, 0))],
        out_shape=[jax.ShapeDtypeStruct((n, d), F32), jax.ShapeDtypeStruct((n, d), BF16)],
        compiler_params=_cparams("parallel"),
        name="input_ln",
    )(x2, g.reshape(1, d), b.reshape(1, d))


def _matmul_kernel(h_ref, w_ref, o_ref):
    o_ref[...] = _dot(h_ref[...], w_ref[...]).astype(o_ref.dtype)


def in_projection(hb, w_bf16, tm=2048, tn=768):
    n, d = hb.shape
    w = w_bf16.shape[1]
    return pl.pallas_call(
        _matmul_kernel,
        grid=(n // tm, w // tn),
        in_specs=[pl.BlockSpec((tm, d), lambda i, j: (i, 0)),
                  pl.BlockSpec((d, tn), lambda i, j: (0, j))],
        out_specs=pl.BlockSpec((tm, tn), lambda i, j: (i, j)),
        out_shape=jax.ShapeDtypeStruct((n, w), BF16),
        compiler_params=_cparams("parallel", "arbitrary"),
        name="in_proj",
    )(hb, w_bf16)


def _retention_tables(c):
    idx = np.arange(N_HEADS, dtype=np.float64)
    lg = [np.log1p(-np.exp2(-5.0 - 2.0 * idx)), np.log1p(-np.exp2(-6.0 - 2.0 * idx))]
    i = np.arange(c, dtype=np.float64)
    diff = i[:, None] - i[None, :]
    lane_head = np.repeat(np.arange(N_HEADS), HEAD_DIM)
    dmat = np.zeros((2, N_HEADS, c, c), np.float32)
    oscale = np.zeros((2, c, BR_W), np.float32)
    kscale = np.zeros((2, c, BR_W), np.float32)
    sdecay = np.zeros((2, 1, BR_W), np.float32)
    for h in range(N_HEADS):
        dmat[0, h] = np.where(diff >= 0, np.exp(lg[0][h] * np.maximum(diff, 0)), 0.0)
        dmat[1, h] = np.where(diff <= 0, np.exp(lg[1][h] * np.maximum(-diff, 0)), 0.0)
    oscale[0] = np.exp(lg[0][lane_head][None, :] * (i[:, None] + 1.0))
    oscale[1] = np.exp(lg[1][lane_head][None, :] * (c - i[:, None]))
    kscale[0] = np.exp(lg[0][lane_head][None, :] * (c - 1.0 - i[:, None]))
    kscale[1] = np.exp(lg[1][lane_head][None, :] * i[:, None])
    sdecay[0, 0] = np.exp(lg[0][lane_head] * c)
    sdecay[1, 0] = np.exp(lg[1][lane_head] * c)
    return (jnp.asarray(dmat), jnp.asarray(oscale), jnp.asarray(kscale), jnp.asarray(sdecay))


def _retention_kernel(qf, kf, vf, qb, kb, vb, dmat, oscale, kscale, sdecay, of_ref, ob_ref, state):
    @pl.when(pl.program_id(1) == 0)
    def _():
        state[...] = jnp.zeros_like(state)

    c = qf.shape[0]
    lane_head = _head_id((c, BR_W), 1)
    bd = _block_diag_mask(BR_W)
    for dirn, (q_ref, k_ref, v_ref, o_ref) in enumerate(((qf, kf, vf, of_ref), (qb, kb, vb, ob_ref))):
        q = q_ref[...]
        k = k_ref[...] * jnp.asarray(HEAD_DIM ** -0.5, BF16)
        v = v_ref[...]
        s_old = state[dirn]
        out = _dot(q, s_old.astype(BF16)) * oscale[dirn]
        for h in range(N_HEADS):
            qz = jnp.where(lane_head == h, q, jnp.zeros_like(q))
            a = (_nt(qz, k) * dmat[dirn, h]).astype(BF16)
            out = out + jnp.where(lane_head == h, _dot(a, v), 0.0)
        o_ref[...] = out
        kw = (k.astype(F32) * kscale[dirn]).astype(BF16)
        state[dirn] = s_old * sdecay[dirn] + jnp.where(bd, _tn(kw, v), 0.0)


def retention(p, batch, seq):
    c = RET_CHUNK
    nc = seq // c
    dmat, oscale, kscale, sdecay = _retention_tables(c)
    cb = COL_RET // BR_W

    def fwd(col):
        return pl.BlockSpec((c, BR_W), lambda b, j: (b * nc + j, col))

    def bwd(col):
        return pl.BlockSpec((c, BR_W), lambda b, j: (b * nc + nc - 1 - j, col))

    def whole(a):
        nd = a.ndim
        return pl.BlockSpec(a.shape, lambda b, j: (0,) * nd)

    n = batch * seq
    return pl.pallas_call(
        _retention_kernel,
        grid=(batch, nc),
        in_specs=[fwd(cb), fwd(cb + 1), fwd(cb + 2), bwd(cb), bwd(cb + 1), bwd(cb + 2),
                  whole(dmat), whole(oscale), whole(kscale), whole(sdecay)],
        out_specs=[pl.BlockSpec((c, BR_W), lambda b, j: (b * nc + j, 0)),
                   pl.BlockSpec((c, BR_W), lambda b, j: (b * nc + nc - 1 - j, 0))],
        out_shape=[jax.ShapeDtypeStruct((n, BR_W), F32)] * 2,
        scratch_shapes=[pltpu.VMEM((2, BR_W, BR_W), F32)],
        compiler_params=_cparams("parallel", "arbitrary"),
        name="retention",
    )(p, p, p, p, p, p, dmat, oscale, kscale, sdecay)


def _hgrn_kernel(lb_ref, qf, ff, vf, qb, fb, vb, of_ref, ob_ref, state, kpad, gpad, vpad, *, layer):
    c = qf.shape[0]

    @pl.when(pl.program_id(1) == 0)
    def _():
        state[...] = jnp.zeros_like(state)
        kpad[...] = jnp.zeros_like(kpad)
        gpad[...] = jnp.zeros_like(gpad)
        vpad[...] = jnp.zeros_like(vpad)

    lb = lb_ref[...]
    e = jnp.exp(lb - jnp.max(lb, axis=0, keepdims=True))
    prob = e / jnp.sum(e, axis=0, keepdims=True)
    lower = jnp.sum(prob[:layer + 1], axis=0) - prob[0]

    row = lax.broadcasted_iota(jnp.int32, (c, c), 0)
    col = lax.broadcasted_iota(jnp.int32, (c, c), 1)
    bd = _block_diag_mask(BR_W)
    seg_ones = jnp.where(bd, 1.0, 0.0).astype(BF16)

    for dirn, (q_ref, f_ref, v_ref, o_ref) in enumerate(((qf, ff, vf, of_ref), (qb, fb, vb, ob_ref))):
        rev = dirn == 1
        lo = lower[dirn:dirn + 1, :]
        fpre = f_ref[...].astype(F32)
        logf = jnp.log(lo + (1.0 - lo) * _sigmoid(fpre))
        kk = (1.0 - lo) * _sigmoid(-fpre)
        qx = q_ref[...].astype(F32)
        q = qx * _sigmoid(qx)
        v_bf = v_ref[...]
        v = v_bf.astype(F32)
        tri = jnp.where((col >= row) if rev else (col <= row), 1.0, 0.0).astype(BF16)
        g = _tri_cumsum(tri, logf)
        g_end = g[0:1, :] if rev else g[c - 1:c, :]
        s_old = state[dirn]
        inter = _nt((q * jnp.exp(g)).astype(BF16), s_old.astype(BF16))
        kd = (kk * jnp.exp(g_end - g)).astype(BF16)
        state[dirn] = s_old * jnp.exp(g_end) + jnp.where(bd, _tn(v_bf, kd), 0.0)
        for b in range(8):
            lo_row = (c - b) if rev else (c + b)
            kpad[b, dirn, lo_row:lo_row + c, :] = kk
            gpad[b, dirn, lo_row:lo_row + c, :] = g
            vpad[b, dirn, lo_row:lo_row + c, :] = v

        intra = jnp.zeros((c, BR_W), F32)
        for b in range(8):
            for a in range(c // 8):
                start = (c + 8 * a) if rev else (c - 8 * a)
                ks = kpad[b, dirn, start:start + c, :]
                gs = gpad[b, dirn, start:start + c, :]
                vs = vpad[b, dirn, start:start + c, :]
                term = q * ks * jnp.exp(g - gs)
                intra = intra + _dot(term.astype(BF16), seg_ones) * vs
        o_ref[...] = inter + intra


def _tri_cumsum(tri_bf16, x):
    acc = None
    rem = x
    for _ in range(3):
        piece = rem.astype(BF16)
        part = _dot(tri_bf16, piece)
        acc = part if acc is None else acc + part
        rem = rem - piece.astype(F32)
    return acc


def hgrn(p, hgrn_lb, layer, batch, seq):
    c = HGRN_CHUNK
    nc = seq // c
    cb = COL_HGRN // BR_W

    def fwd(col):
        return pl.BlockSpec((c, BR_W), lambda b, j: (b * nc + j, col))

    def bwd(col):
        return pl.BlockSpec((c, BR_W), lambda b, j: (b * nc + nc - 1 - j, col))

    n = batch * seq
    return pl.pallas_call(
        functools.partial(_hgrn_kernel, layer=layer),
        grid=(batch, nc),
        in_specs=[pl.BlockSpec(hgrn_lb.shape, lambda b, j: (0, 0, 0)),
                  fwd(cb), fwd(cb + 1), fwd(cb + 3), bwd(cb), bwd(cb + 2), bwd(cb + 3)],
        out_specs=[pl.BlockSpec((c, BR_W), lambda b, j: (b * nc + j, 0)),
                   pl.BlockSpec((c, BR_W), lambda b, j: (b * nc + nc - 1 - j, 0))],
        out_shape=[jax.ShapeDtypeStruct((n, BR_W), F32)] * 2,
        scratch_shapes=[pltpu.VMEM((2, BR_W, BR_W), F32),
                        pltpu.VMEM((8, 2, 3 * c, BR_W), F32),
                        pltpu.VMEM((8, 2, 3 * c, BR_W), F32),
                        pltpu.VMEM((8, 2, 3 * c, BR_W), F32)],
        compiler_params=_cparams("parallel", "arbitrary"),
        name="hgrn",
    )(hgrn_lb, p, p, p, p, p, p)


def _nat_bias_table(rpb):
    cq = np.arange(GRID_W)
    ck = np.arange(GRID_W)
    col_start = np.clip(cq - NAT_KW // 2, 0, GRID_W - NAT_KW)
    col_mask = (ck[None, :] >= col_start[:, None]) & (ck[None, :] < col_start[:, None] + NAT_KW)
    dc = np.clip(ck[None, :] - cq[:, None], -(NAT_KW - 1), NAT_KW - 1) + (NAT_KW - 1)
    pick = (dc[None, :, :] == np.arange(2 * NAT_KW - 1)[:, None, None]).astype(np.float32)
    tiles = jnp.einsum('hrd,dqk->hrqk', rpb.astype(F32), jnp.asarray(pick),
                       precision=lax.Precision.HIGHEST)
    tiles = jnp.where(jnp.asarray(col_mask)[None, None], tiles, MASK_VALUE)
    t = jnp.stack([tiles[:, b:b + NAT_KH] for b in range(NAT_KH)], axis=1)
    return t.transpose(0, 1, 3, 2, 4).reshape(N_HEADS, NAT_KH, GRID_W, NAT_KH * GRID_W)


def _nat_kernel(q_ref, k_ref, v_ref, bias_ref, o_ref, *, rows_per_step, n_rows):
    j = pl.program_id(1)
    lane_head = _head_id((GRID_W, BR_W), 1)
    win = NAT_KH * GRID_W

    def body(i, carry):
        r = j * rows_per_step + i
        rs = jnp.clip(r - NAT_KH // 2, 0, n_rows - NAT_KH)
        base = rs - r + (NAT_KH - 1)
        q = q_ref[pl.ds(pl.multiple_of(i * GRID_W, GRID_W), GRID_W), :] * jnp.asarray(HEAD_DIM ** -0.5, BF16)
        kw = k_ref[pl.ds(pl.multiple_of(rs * GRID_W, GRID_W), win), :]
        vw = v_ref[pl.ds(pl.multiple_of(rs * GRID_W, GRID_W), win), :]
        out = jnp.zeros((GRID_W, BR_W), F32)
        for h in range(N_HEADS):
            qz = jnp.where(lane_head == h, q, jnp.zeros_like(q))
            s = _nt(qz, kw) + bias_ref[h, base]
            m = jnp.max(s, axis=-1, keepdims=True)
            pr = jnp.exp(s - m)
            l = jnp.sum(pr, axis=-1, keepdims=True)
            o = _dot(pr.astype(BF16), vw) / l
            out = out + jnp.where(lane_head == h, o, 0.0)
        o_ref[pl.ds(pl.multiple_of(i * GRID_W, GRID_W), GRID_W), :] = out.astype(o_ref.dtype)
        return carry

    lax.fori_loop(0, rows_per_step, body, 0, unroll=True)


def neighbourhood_attention(p, rpb, batch, seq):
    n_rows = seq // GRID_W
    rps = NAT_ROWS_PER_STEP
    steps = n_rows // rps
    bias = _nat_bias_table(rpb)
    cb = COL_NAT // BR_W
    tq = rps * GRID_W
    return pl.pallas_call(
        functools.partial(_nat_kernel, rows_per_step=rps, n_rows=n_rows),
        grid=(batch, steps),
        in_specs=[pl.BlockSpec((tq, BR_W), lambda b, j: (b * steps + j, cb)),
                  pl.BlockSpec((seq, BR_W), lambda b, j: (b, cb + 1)),
                  pl.BlockSpec((seq, BR_W), lambda b, j: (b, cb + 2)),
                  pl.BlockSpec(bias.shape, lambda b, j: (0, 0, 0, 0))],
        out_specs=pl.BlockSpec((tq, BR_W), lambda b, j: (b * steps + j, 0)),
        out_shape=jax.ShapeDtypeStruct((batch * seq, BR_W), BF16),
        compiler_params=_cparams("parallel", "arbitrary"),
        name="nat",
    )(p, p, p, bias)


def _diff_kernel(lam_ref, slope_ref, sub_ref, q_ref, k_ref, v_ref, o_ref,
                 acc1, acc2, m1, l1, m2, l2, *, lam_init, n_tiles):
    t = q_ref.shape[0]
    reps = t // 128
    qi = pl.program_id(2)
    neg_slope = -slope_ref[0, 0:1, 0:1]
    lp = lam_ref[...]
    lam = (jnp.exp(jnp.sum(lp[0:1] * lp[1:2], axis=-1, keepdims=True))
           - jnp.exp(jnp.sum(lp[2:3] * lp[3:4], axis=-1, keepdims=True)) + lam_init)

    q = q_ref[...] * jnp.asarray(0.125, BF16)
    lane = lax.broadcasted_iota(jnp.int32, q.shape, 1)
    q1 = jnp.where(lane < 64, q, jnp.zeros_like(q))
    q2 = jnp.where(lane >= 64, q, jnp.zeros_like(q))
    qpos = (qi * t + lax.broadcasted_iota(jnp.int32, (t, 1), 0)).astype(F32)

    acc1[...] = jnp.zeros_like(acc1)
    acc2[...] = jnp.zeros_like(acc2)
    l1[...] = jnp.zeros_like(l1)
    l2[...] = jnp.zeros_like(l2)
    m1[...] = jnp.full_like(m1, -jnp.inf)
    m2[...] = jnp.full_like(m2, -jnp.inf)

    def tile(kj, carry):
        start = pl.multiple_of(kj * t, t)
        k = k_ref[pl.ds(start, t), :]
        v = v_ref[pl.ds(start, t), :]
        kpos = (kj * t + lax.broadcasted_iota(jnp.int32, (1, t), 1)).astype(F32)
        bias = neg_slope * jnp.abs(qpos - kpos)
        for qx, acc, m_ref, l_ref in ((q1, acc1, m1, l1), (q2, acc2, m2, l2)):
            s = _nt(qx, k) + bias
            m_old = m_ref[...]
            m_new = jnp.maximum(m_old, jnp.max(s, axis=-1, keepdims=True))
            a = jnp.exp(m_old - m_new)
            pr = jnp.exp(s - jnp.concatenate([m_new] * reps, axis=1))
            l_ref[...] = a * l_ref[...] + jnp.sum(pr, axis=-1, keepdims=True)
            acc[...] = a * acc[...] + _dot(pr.astype(BF16), v)
            m_ref[...] = m_new
        return carry

    lax.fori_loop(0, n_tiles, tile, 0)

    o = acc1[...] / l1[...] - lam * (acc2[...] / l2[...])
    o = o * lax.rsqrt(jnp.mean(o * o, axis=-1, keepdims=True) + EPS) * sub_ref[0]
    o_ref[...] = (o * (1.0 - lam_init)).astype(o_ref.dtype)


def diff_attention(p, lam_params, subln, layer, batch, seq):
    t = DIFF_TILE
    nt = seq // t
    lam_init = 0.8 - 0.6 * math.exp(-0.3 * layer)
    slopes = np.exp2(-8.0 * (np.arange(N_HEADS, dtype=np.float64) + 1.0) / N_HEADS)
    slope_tab = jnp.asarray(np.broadcast_to(slopes[:, None, None], (N_HEADS, 8, 128)).astype(np.float32))
    qc, kc, vc = (COL_DIFF // DIFF_DV, (COL_DIFF + DIFF_W) // DIFF_DV, (COL_DIFF + 2 * DIFF_W) // DIFF_DV)
    return pl.pallas_call(
        functools.partial(_diff_kernel, lam_init=lam_init, n_tiles=nt),
        grid=(batch, N_HEADS, nt),
        in_specs=[pl.BlockSpec(lam_params.shape, lambda b, h, i: (0, 0)),
                  pl.BlockSpec((1, 8, 128), lambda b, h, i: (h, 0, 0)),
                  pl.BlockSpec((1, 1, DIFF_DV), lambda b, h, i: (h, 0, 0)),
                  pl.BlockSpec((t, DIFF_DV), lambda b, h, i: (b * nt + i, qc + h)),
                  pl.BlockSpec((seq, DIFF_DV), lambda b, h, i: (b, kc + h)),
                  pl.BlockSpec((seq, DIFF_DV), lambda b, h, i: (b, vc + h))],
        out_specs=pl.BlockSpec((t, DIFF_DV), lambda b, h, i: (b * nt + i, h)),
        out_shape=jax.ShapeDtypeStruct((batch * seq, DIFF_W), BF16),
        scratch_shapes=[pltpu.VMEM((t, DIFF_DV), F32)] * 6,
        compiler_params=_cparams("parallel", "parallel", "arbitrary"),
        name="diff_attn",
    )(lam_params, slope_tab, subln.reshape(N_HEADS, 1, DIFF_DV), p, p, p)


def _merge_kernel(hf_ref, hb_ref, rof, rob, rg, ynat, hof, hob, hg, ydiff,
                  wg_ref, bg_ref, wr_ref, wn_ref, wh_ref, wd_ref, wo_ref,
                  rgn_ref, hgn_ref, lng_ref, lnb_ref, of_ref, ob_ref):
    seg_mean = jnp.where(_block_diag_mask(BR_W), 1.0 / HEAD_DIM, 0.0).astype(BF16)

    def head_norm_gate(o, gn, gate):
        ms = _dot_split(o * o, seg_mean, terms=2)
        gx = gate.astype(F32)
        return (o * lax.rsqrt(ms + EPS) * gn * (gx * _sigmoid(gx))).astype(BF16)

    y_ret = head_norm_gate(rof[...] + rob[...], rgn_ref[...], rg[...])
    y_hgrn = head_norm_gate(hof[...] + hob[...], hgn_ref[...], hg[...])
    hb = hb_ref[...]
    d = hf_ref.shape[1]
    merged = None
    for i, (y, w_ref) in enumerate(((y_ret, wr_ref), (ynat[...], wn_ref), (y_hgrn, wh_ref), (ydiff[...], wd_ref))):
        gate = _sigmoid(_dot(hb, wg_ref[:, i * d:(i + 1) * d]) + bg_ref[:, i * d:(i + 1) * d])
        part = gate * _dot(y, w_ref[...])
        merged = part if merged is None else merged + part
    z = ALPHA * hf_ref[...] + _dot(merged.astype(BF16), wo_ref[...])
    h = _layer_norm(z, lng_ref[...], lnb_ref[...])
    of_ref[...] = h
    ob_ref[...] = h.astype(BF16)


def merge_and_norm(hf, hb, p, ret_f, ret_b, y_nat, hg_f, hg_b, y_diff,
                   w_gate, b_gate, w_br_ret, w_br_nat, w_br_hgrn, w_br_diff, w_out,
                   ret_gn, hgrn_gn, ln_g, ln_b, tm=512):
    n, d = hf.shape

    def rows(width, col=0):
        return pl.BlockSpec((tm, width), lambda i: (i, col))

    def whole(a):
        nd = a.ndim
        return pl.BlockSpec(a.shape, lambda i: (0,) * nd)

    consts = [w_gate, b_gate.reshape(1, -1), w_br_ret, w_br_nat, w_br_hgrn, w_br_diff, w_out,
              ret_gn.reshape(1, BR_W), hgrn_gn.reshape(1, BR_W), ln_g.reshape(1, d), ln_b.reshape(1, d)]
    return pl.pallas_call(
        _merge_kernel,
        grid=(n // tm,),
        in_specs=[rows(d), rows(d),
                  rows(BR_W), rows(BR_W), rows(BR_W, COL_RET // BR_W + 3),
                  rows(BR_W),
                  rows(BR_W), rows(BR_W), rows(BR_W, COL_HGRN // BR_W + 4),
                  rows(DIFF_W)] + [whole(a) for a in consts],
        out_specs=[rows(d), rows(d)],
        out_shape=[jax.ShapeDtypeStruct((n, d), F32), jax.ShapeDtypeStruct((n, d), BF16)],
        compiler_params=_cparams("parallel"),
        name="merge",
    )(hf, hb, ret_f, ret_b, p, y_nat, hg_f, hg_b, p, y_diff, *consts)


def _router_kernel(h_ref, w_ref, bias_ref, idx_ref, w_ref8, cnt_ref):
    w = w_ref[...]
    w_hi = w.astype(BF16)
    w_lo = (w - w_hi.astype(F32)).astype(BF16)
    h = h_ref[...]
    logits = _dot_split(h, w_hi, terms=3) + _dot_split(h, w_lo, terms=2)
    scores = _sigmoid(logits)
    biased = scores + bias_ref[...]
    shape = scores.shape
    lane = lax.broadcasted_iota(jnp.int32, shape, 1).astype(F32)
    group = lax.shift_right_logical(lax.broadcasted_iota(jnp.int32, shape, 1), 3).astype(F32)
    big = jnp.asarray(1e9, F32)
    neg = jnp.asarray(-jnp.inf, F32)

    def first_max(x, ids):
        m = jnp.max(x, axis=-1, keepdims=True)
        return m, jnp.min(jnp.where(x == m, ids, big), axis=-1, keepdims=True)

    gscore = jnp.zeros(shape, F32)
    for g in range(N_GROUPS):
        xg = jnp.where(group == g, biased, neg)
        m1, i1 = first_max(xg, lane)
        m2 = jnp.max(jnp.where(lane == i1, neg, xg), axis=-1, keepdims=True)
        gscore = jnp.where(group == g, m1 + m2, gscore)
    gsel = jnp.zeros(shape, jnp.bool_)
    for _ in range(TOPK_GROUPS):
        _, gi = first_max(gscore, group)
        hit = group == gi
        gsel = gsel | hit
        gscore = jnp.where(hit, neg, gscore)
    cand = jnp.where(gsel, biased, MASK_VALUE)
    esel = jnp.zeros(shape, jnp.bool_)
    picks = []
    for _ in range(TOP_K):
        _, ei = first_max(cand, lane)
        hit = lane == ei
        picks.append((ei, jnp.sum(jnp.where(hit, scores, 0.0), axis=-1, keepdims=True)))
        esel = esel | hit
        cand = jnp.where(hit, neg, cand)

    tm = shape[0]
    sel = jnp.where(esel, 1.0, 0.0)
    cnt_ref[...] = jnp.concatenate(
        [jnp.sum(sel[s * ROUTE_TILE:(s + 1) * ROUTE_TILE], axis=0, keepdims=True) for s in range(tm // ROUTE_TILE)],
        axis=0)

    slot = lax.broadcasted_iota(jnp.int32, (tm, TOP_K), 1)
    idx8 = jnp.zeros((tm, TOP_K), F32)
    w8 = jnp.zeros((tm, TOP_K), F32)
    wsum = jnp.zeros((tm, 1), F32)
    for k, (ei, wk) in enumerate(picks):
        idx8 = jnp.where(slot == k, ei, idx8)
        w8 = jnp.where(slot == k, wk, w8)
        wsum = wsum + wk
    idx_ref[...] = idx8.astype(jnp.int32)
    w_ref8[...] = w8 / (wsum + 1e-20) * ROUTED_SCALE


def router(hf, w_router, router_bias, tm=1024):
    n, d = hf.shape
    e = w_router.shape[1]
    sub = tm // ROUTE_TILE
    return pl.pallas_call(
        _router_kernel,
        grid=(n // tm,),
        in_specs=[pl.BlockSpec((tm, d), lambda i: (i, 0)),
                  pl.BlockSpec((d, e), lambda i: (0, 0)),
                  pl.BlockSpec((1, e), lambda i: (0, 0))],
        out_specs=[pl.BlockSpec((tm, TOP_K), lambda i: (i, 0)),
                   pl.BlockSpec((tm, TOP_K), lambda i: (i, 0)),
                   pl.BlockSpec((sub, e), lambda i: (i, 0))],
        out_shape=[jax.ShapeDtypeStruct((n, TOP_K), jnp.int32),
                   jax.ShapeDtypeStruct((n, TOP_K), F32),
                   jax.ShapeDtypeStruct((n // ROUTE_TILE, e), F32)],
        compiler_params=_cparams("arbitrary"),
        name="router",
    )(hf, w_router, router_bias.reshape(1, e))


HIGH_HALF = -65536


def _pack_pairs(x):
    w = x.shape[1] // 2
    lo = lax.bitcast_convert_type(x[:, :w].astype(BF16).astype(F32), jnp.int32)
    hi = lax.bitcast_convert_type(x[:, w:].astype(BF16).astype(F32), jnp.int32)
    return lax.shift_right_logical(lo, 16) | (hi & HIGH_HALF)


def _unpack_pairs(p):
    lo = lax.bitcast_convert_type(lax.shift_left(p, 16), F32)
    hi = lax.bitcast_convert_type(p & HIGH_HALF, F32)
    return jnp.concatenate([lo, hi], axis=1)


def _swiglu(x, wg, wu):
    a = _dot(x, wg)
    return a * _sigmoid(a) * _dot(x, wu)


def _segment_copies(tab_ref, t, n_tile_rows, make_copy, action):
    top_bit = n_tile_rows.bit_length() - 1
    low_bit = RUN_ALIGN.bit_length() - 1

    def per_expert(e, carry):
        col = t * N_EXPERTS + e
        count = tab_ref[0, col]
        local = tab_ref[1, col]
        glob = tab_ref[2, col]
        for b in range(top_bit, low_bit - 1, -1):
            size = 1 << b
            taken = count & size

            @pl.when(taken != 0)
            def _(local=local, glob=glob, size=size):
                action(make_copy(pl.multiple_of(local, RUN_ALIGN), pl.multiple_of(glob, RUN_ALIGN), size))

            local = local + taken
            glob = glob + taken
        return carry

    lax.fori_loop(0, N_EXPERTS, per_expert, 0)


def _tile_rows_used(tab_ref, t):
    last = t * N_EXPERTS + (N_EXPERTS - 1)
    return tab_ref[1, last] + tab_ref[0, last]


def _wait_rows(total, max_rows, make_copy):
    for b in range(max_rows.bit_length() - 1, RUN_ALIGN.bit_length() - 2, -1):
        size = 1 << b

        @pl.when((total & size) != 0)
        def _(size=size):
            make_copy(0, 0, size).wait()


def _block_dispatch_kernel(tab_ref, idx_t_ref, hb_ref, xs_ref, sorted_buf, sem):
    t = pl.program_id(0)
    tm = hb_ref.shape[0]
    rows = sorted_buf.shape[0]
    idx_t = idx_t_ref[...]
    expert = lax.broadcasted_iota(jnp.int32, (N_EXPERTS, tm), 0)
    sel_t = jnp.zeros((N_EXPERTS, tm), F32)
    for k in range(TOP_K):
        sel_t = sel_t + jnp.where(idx_t[k:k + 1, :] == expert, 1.0, 0.0)
    sel_b = sel_t.astype(BF16)
    before_tok = (lax.broadcasted_iota(jnp.int32, (tm, tm), 0) < lax.broadcasted_iota(jnp.int32, (tm, tm), 1))
    before_exp = (lax.broadcasted_iota(jnp.int32, (N_EXPERTS, N_EXPERTS), 1)
                  < lax.broadcasted_iota(jnp.int32, (N_EXPERTS, N_EXPERTS), 0))
    run = jnp.broadcast_to(_run_length(jnp.sum(sel_t, axis=1, keepdims=True)), (N_EXPERTS, tm)).astype(BF16)
    base = (_dot(jnp.where(before_exp, 1.0, 0.0).astype(BF16), run)
            + _dot(sel_b, jnp.where(before_tok, 1.0, 0.0).astype(BF16)))
    row = lax.broadcasted_iota(jnp.int32, (rows, tm), 0)
    perm = jnp.zeros((rows, tm), F32)
    for k in range(TOP_K):
        place = jnp.sum(jnp.where(idx_t[k:k + 1, :] == expert, base, 0.0), axis=0, keepdims=True).astype(jnp.int32)
        perm = jnp.where(row == place, 1.0, perm)
    perm = perm.astype(BF16)
    half = hb_ref.shape[1] // 2
    lo = lax.bitcast_convert_type(_dot(perm, hb_ref[:, :half]), jnp.int32)
    hi = lax.bitcast_convert_type(_dot(perm, hb_ref[:, half:]), jnp.int32)
    sorted_buf[...] = lax.shift_right_logical(lo, 16) | (hi & HIGH_HALF)

    def make_copy(local, glob, size):
        return pltpu.make_async_copy(sorted_buf.at[pl.ds(local, size)], xs_ref.at[pl.ds(glob, size)], sem.at[0])

    _segment_copies(tab_ref, t, tm, make_copy, lambda c: c.start())
    _wait_rows(_tile_rows_used(tab_ref, t), rows, make_copy)


def block_dispatch(hb, idx_t, tab, tm):
    n, d = hb.shape
    grid_spec = pltpu.PrefetchScalarGridSpec(
        num_scalar_prefetch=1,
        grid=(n // tm,),
        in_specs=[pl.BlockSpec((TOP_K, tm), lambda i, tab: (0, i)),
                  pl.BlockSpec((tm, d), lambda i, tab: (i, 0))],
        out_specs=pl.BlockSpec(memory_space=pl.ANY),
        scratch_shapes=[pltpu.VMEM((_tile_rows(tm), d // 2), jnp.int32), pltpu.SemaphoreType.DMA((1,))],
    )
    return pl.pallas_call(
        _block_dispatch_kernel,
        grid_spec=grid_spec,
        out_shape=jax.ShapeDtypeStruct((_sorted_rows(n, tm), d // 2), jnp.int32),
        compiler_params=_cparams("arbitrary"),
        name="dispatch",
    )(tab, idx_t, hb)


def _tile_tables(tile_counts, tm_fine, tm):
    g = tm // tm_fine
    cnt = tile_counts.astype(jnp.int32).reshape(-1, g, N_EXPERTS).sum(axis=1)
    cnt = (cnt + (RUN_ALIGN - 1)) // RUN_ALIGN * RUN_ALIGN
    local = jnp.cumsum(cnt, axis=1) - cnt
    totals = cnt.sum(axis=0)
    seg_start = jnp.cumsum(totals) - totals
    glob = seg_start[None, :] + jnp.cumsum(cnt, axis=0) - cnt
    return jnp.stack([cnt.reshape(-1), local.reshape(-1), glob.reshape(-1)]), totals


def _expert_kernel(item_ref, n_items_ref, xs_ref, wg_ref, wu_ref, wd_ref, ys_ref, wgb, wub, wdb):
    i = pl.program_id(0)

    @pl.when(i < n_items_ref[0])
    def _():
        expert = item_ref[0, i]
        lo = item_ref[2, i]
        hi = item_ref[3, i]

        @pl.when((i == 0) | (expert != item_ref[0, jnp.maximum(i - 1, 0)]))
        def _():
            wgb[...] = wg_ref[0, 0].astype(BF16)
            wub[...] = wu_ref[0, 0].astype(BF16)
            wdb[...] = wd_ref[0, 0].astype(BF16)

        row = lax.broadcasted_iota(jnp.int32, xs_ref.shape, 0)
        mine = (row >= lo) & (row < hi)
        x = _unpack_pairs(jnp.where(mine, xs_ref[...], 0)).astype(BF16)
        hid = _swiglu(x, wgb[...], wub[...])
        y = _pack_pairs(_dot(hid.astype(BF16), wdb[...]))

        @pl.when(lo == 0)
        def _():
            ys_ref[...] = y

        @pl.when(lo != 0)
        def _():
            ys_ref[...] = jnp.where(mine, y, ys_ref[...])


def _expert_items(counts, n_rows, tr, max_items):
    counts = counts.astype(jnp.int32)
    ends = jnp.cumsum(counts)
    starts = ends - counts
    first_tile = starts // tr
    last_tile = (ends - 1) // tr
    per_expert = jnp.where(counts > 0, last_tile - first_tile + 1, 0)
    item_end = jnp.cumsum(per_expert)
    n_items = item_end[-1]
    i = jnp.minimum(jnp.arange(max_items, dtype=jnp.int32), n_items - 1)
    expert = jnp.sum((item_end[None, :] <= i[:, None]).astype(jnp.int32), axis=1)
    pick = lambda a: jnp.sum(jnp.where(jnp.arange(N_EXPERTS)[None, :] == expert[:, None], a[None, :], 0), axis=1)
    tile = pick(first_tile) + (i - (pick(item_end) - pick(per_expert)))
    lo = jnp.maximum(pick(starts), tile * tr) - tile * tr
    hi = jnp.minimum(pick(ends), (tile + 1) * tr) - tile * tr
    return jnp.stack([expert, tile, lo, hi]).astype(jnp.int32), n_items.reshape(1).astype(jnp.int32)


def grouped_experts(xs, counts, w_e_gate, w_e_up, w_e_down, layer, tr=EXPERT_ROWS):
    rows, w = xs.shape
    _, ne, d, hid = w_e_gate.shape
    max_items = rows // tr + ne
    items, n_items = _expert_items(counts, rows, tr, max_items)
    grid_spec = pltpu.PrefetchScalarGridSpec(
        num_scalar_prefetch=2,
        grid=(max_items,),
        in_specs=[pl.BlockSpec((tr, w), lambda i, it, n: (it[1, i], 0)),
                  pl.BlockSpec((1, 1, d, hid), lambda i, it, n: (layer, it[0, i], 0, 0)),
                  pl.BlockSpec((1, 1, d, hid), lambda i, it, n: (layer, it[0, i], 0, 0)),
                  pl.BlockSpec((1, 1, hid, d), lambda i, it, n: (layer, it[0, i], 0, 0))],
        out_specs=pl.BlockSpec((tr, w), lambda i, it, n: (it[1, i], 0)),
        scratch_shapes=[pltpu.VMEM((d, hid), BF16), pltpu.VMEM((d, hid), BF16), pltpu.VMEM((hid, d), BF16)],
    )
    return pl.pallas_call(
        _expert_kernel,
        grid_spec=grid_spec,
        out_shape=jax.ShapeDtypeStruct((rows, w), jnp.int32),
        compiler_params=_cparams("arbitrary"),
        name="experts",
    )(items, n_items, xs, w_e_gate, w_e_up, w_e_down)


def _block_combine_kernel(tab_ref, ys_ref, idx_ref, w8_ref, hf_ref, hb_ref, sg_ref, su_ref, sd_ref,
                          lng_ref, lnb_ref, of_ref, ob_ref, buf, sem):
    t = pl.program_id(0)
    tm = hf_ref.shape[0]
    rows = buf.shape[0]

    def make_copy(local, glob, size):
        return pltpu.make_async_copy(ys_ref.at[pl.ds(glob, size)], buf.at[pl.ds(local, size)], sem.at[0])

    _segment_copies(tab_ref, t, tm, make_copy, lambda c: c.start())

    idx = idx_ref[...]
    w8 = w8_ref[...]
    expert = lax.broadcasted_iota(jnp.int32, (tm, N_EXPERTS), 1)
    sel = jnp.zeros((tm, N_EXPERTS), F32)
    for k in range(TOP_K):
        sel = sel + jnp.where(idx[:, k:k + 1] == expert, 1.0, 0.0)
    sel_b = sel.astype(BF16)
    before_tok = (lax.broadcasted_iota(jnp.int32, (tm, tm), 1) < lax.broadcasted_iota(jnp.int32, (tm, tm), 0))
    before_exp = (lax.broadcasted_iota(jnp.int32, (N_EXPERTS, N_EXPERTS), 0)
                  < lax.broadcasted_iota(jnp.int32, (N_EXPERTS, N_EXPERTS), 1))
    run = jnp.broadcast_to(_run_length(jnp.sum(sel, axis=0, keepdims=True)), (tm, N_EXPERTS)).astype(BF16)
    base = (_dot(run, jnp.where(before_exp, 1.0, 0.0).astype(BF16))
            + _dot(jnp.where(before_tok, 1.0, 0.0).astype(BF16), sel_b))
    col = lax.broadcasted_iota(jnp.int32, (tm, rows), 1)
    mix = jnp.zeros((tm, rows), F32)
    for k in range(TOP_K):
        place = jnp.sum(jnp.where(idx[:, k:k + 1] == expert, base, 0.0), axis=1, keepdims=True).astype(jnp.int32)
        mix = jnp.where(col == place, w8[:, k:k + 1], mix)
    mix = mix.astype(BF16)

    acc = _dot(_swiglu(hb_ref[...], sg_ref[...], su_ref[...]).astype(BF16), sd_ref[...])
    used = _tile_rows_used(tab_ref, t)
    _wait_rows(used, rows, make_copy)
    filled = lax.broadcasted_iota(jnp.int32, buf.shape, 0) < used
    y = _unpack_pairs(jnp.where(filled, buf[...], 0)).astype(BF16)
    acc = acc + _dot(mix, y)
    h = _layer_norm(ALPHA * hf_ref[...] + acc, lng_ref[...], lnb_ref[...])
    of_ref[...] = h
    ob_ref[...] = h.astype(BF16)


def block_combine(ys, tab, idx8, w8, hf, hb, w_s_gate, w_s_up, w_s_down, ln_g, ln_b, tm):
    n, d = hf.shape
    w = ys.shape[1]

    def whole(a):
        return pl.BlockSpec(a.shape, lambda i, tab: (0, 0))

    def rows(width):
        return pl.BlockSpec((tm, width), lambda i, tab: (i, 0))

    consts = [w_s_gate, w_s_up, w_s_down, ln_g.reshape(1, d), ln_b.reshape(1, d)]
    grid_spec = pltpu.PrefetchScalarGridSpec(
        num_scalar_prefetch=1,
        grid=(n // tm,),
        in_specs=[pl.BlockSpec(memory_space=pl.ANY), rows(TOP_K), rows(TOP_K), rows(d), rows(d)]
                 + [whole(a) for a in consts],
        out_specs=[rows(d), rows(d)],
        scratch_shapes=[pltpu.VMEM((_tile_rows(tm), w), jnp.int32), pltpu.SemaphoreType.DMA((1,))],
    )
    return pl.pallas_call(
        _block_combine_kernel,
        grid_spec=grid_spec,
        out_shape=[jax.ShapeDtypeStruct((n, d), F32), jax.ShapeDtypeStruct((n, d), BF16)],
        compiler_params=_cparams("arbitrary"),
        name="combine",
    )(tab, ys, idx8, w8, hf, hb, *consts)


def moe_and_norm(hf, hb, w_router, router_bias, w_e_gate, w_e_up, w_e_down,
                 w_s_gate, w_s_up, w_s_down, ln_g, ln_b, layer):
    idx8, w8, tile_counts = router(hf, w_router, router_bias)
    tab, rows_per_expert = _tile_tables(tile_counts, ROUTE_TILE, MOE_TILE)
    xs = block_dispatch(hb, idx8.T, tab, MOE_TILE)
    ys = grouped_experts(xs, rows_per_expert, w_e_gate, w_e_up, w_e_down, layer)
    return block_combine(ys, tab, idx8, w8, hf, hb, w_s_gate, w_s_up, w_s_down, ln_g, ln_b, MOE_TILE)


def kernel(x, ln_in_g, ln_in_b, w_in, w_gate, b_gate, w_br_ret, w_br_nat, w_br_hgrn, w_br_diff, w_out,
           ret_gn, nat_rpb, hgrn_lb, hgrn_gn, diff_lambda, diff_subln, ln1_g, ln1_b, w_router,
           router_bias, w_e_gate, w_e_up, w_e_down, w_s_gate, w_s_up, w_s_down, ln2_g, ln2_b):
    batch, seq, d = x.shape
    bf = lambda a: a.astype(BF16)
    hf, hb = input_layer_norm(x.reshape(batch * seq, d), ln_in_g, ln_in_b)
    for l in range(DEPTH):
        p = in_projection(hb, bf(w_in[l]))
        ret_f, ret_b = retention(p, batch, seq)
        y_nat = neighbourhood_attention(p, nat_rpb[l], batch, seq)
        hg_f, hg_b = hgrn(p, hgrn_lb, l, batch, seq)
        y_diff = diff_attention(p, diff_lambda[l], diff_subln[l], l, batch, seq)
        hf, hb = merge_and_norm(hf, hb, p, ret_f, ret_b, y_nat, hg_f, hg_b, y_diff,
                                bf(w_gate[l]), b_gate[l], bf(w_br_ret[l]), bf(w_br_nat[l]),
                                bf(w_br_hgrn[l]), bf(w_br_diff[l]), bf(w_out[l]),
                                ret_gn[l], hgrn_gn[l], ln1_g[l], ln1_b[l])
        hf, hb = moe_and_norm(hf, hb, w_router[l], router_bias[l], w_e_gate, w_e_up, w_e_down,
                              bf(w_s_gate[l]), bf(w_s_up[l]), bf(w_s_down[l]), ln2_g[l], ln2_b[l], l)
    return hf.reshape(batch, seq, d)
```

```python
import functools
import math

import numpy as np
import jax
import jax.numpy as jnp
from jax import lax
from jax.experimental import pallas as pl
from jax.experimental.pallas import tpu as pltpu

F32 = jnp.float32
BF16 = jnp.bfloat16

D_MODEL = 1024
DEPTH = 2
GRID_W = 64
HEAD_DIM = 64
N_HEADS = 4
BR_W = N_HEADS * HEAD_DIM
NAT_KH = 8
NAT_KW = 16
DIFF_DV = 128
DIFF_W = N_HEADS * DIFF_DV
N_EXPERTS = 64
TOP_K = 8
N_GROUPS = 8
TOPK_GROUPS = 4
GROUP_SIZE = N_EXPERTS // N_GROUPS
EXPERT_HIDDEN = 256
ROUTED_SCALE = 2.5
EPS = 1e-5
MASK_VALUE = -1e30
ALPHA = (2.0 * DEPTH) ** 0.25
IN_WIDTH = 4608
COL_RET = 0
COL_NAT = 1024
COL_HGRN = 1792
COL_DIFF = 3072

VMEM_LIMIT = 56 * 1024 * 1024

RET_CHUNK = 256
HGRN_CHUNK = 64
DIFF_TILE = 512
NAT_ROWS_PER_STEP = 8
ROUTE_TILE = 128
MOE_TILE = 256
EXPERT_ROWS = 512
RUN_ALIGN = 8


def _run_length(count):
    return jnp.floor((count + (RUN_ALIGN - 1)) * (1.0 / RUN_ALIGN)) * RUN_ALIGN


def _tile_rows(tm):
    bound = TOP_K * tm + N_EXPERTS * (RUN_ALIGN - 1)
    return -(-bound // 128) * 128


def _sorted_rows(n_tokens, tm):
    bound = (TOP_K * n_tokens + (n_tokens // tm) * N_EXPERTS * (RUN_ALIGN - 1)
             + N_EXPERTS * (EXPERT_ROWS - RUN_ALIGN))
    return -(-bound // EXPERT_ROWS) * EXPERT_ROWS


def _cparams(*sem):
    return pltpu.CompilerParams(dimension_semantics=sem, vmem_limit_bytes=VMEM_LIMIT)


def _nt(a, b):
    return lax.dot_general(a, b, (((1,), (1,)), ((), ())), preferred_element_type=F32)


def _tn(a, b):
    return lax.dot_general(a, b, (((0,), (0,)), ((), ())), preferred_element_type=F32)


def _dot(a, b):
    return jnp.dot(a, b, preferred_element_type=F32)


def _dot_split(x, w_bf16, terms=3):
    acc = None
    rem = x
    for _ in range(terms):
        piece = rem.astype(BF16)
        part = _dot(piece, w_bf16)
        acc = part if acc is None else acc + part
        rem = rem - piece.astype(F32)
    return acc


def _layer_norm(z, g, b):
    mu = jnp.mean(z, axis=-1, keepdims=True)
    zc = z - mu
    var = jnp.mean(zc * zc, axis=-1, keepdims=True)
    return zc * lax.rsqrt(var + EPS) * g + b


def _sigmoid(x):
    return 1.0 / (1.0 + jnp.exp(-x))


def _head_id(shape, axis):
    return lax.shift_right_logical(lax.broadcasted_iota(jnp.int32, shape, axis), 6)


def _block_diag_mask(n):
    return _head_id((n, n), 0) == _head_id((n, n), 1)


def _ln_kernel(x_ref, g_ref, b_ref, hf_ref, hb_ref):
    h = _layer_norm(x_ref[...], g_ref[...], b_ref[...])
    hf_ref[...] = h
    hb_ref[...] = h.astype(BF16)


def input_layer_norm(x2, g, b, tm=1024):
    n, d = x2.shape
    return pl.pallas_call(
        _ln_kernel,
        grid=(n // tm,),
        in_specs=[pl.BlockSpec((tm, d), lambda i: (i, 0)),
                  pl.BlockSpec((1, d), lambda i: (0, 0)),
                  pl.BlockSpec((1, d), lambda i: (0, 0))],
        out_specs=[pl.BlockSpec((tm, d), lambda i: (i, 0)),
                   pl.BlockSpec((tm, d), lambda i: (i, 0))],
        out_shape=[jax.ShapeDtypeStruct((n, d), F32), jax.ShapeDtypeStruct((n, d), BF16)],
        compiler_params=_cparams("parallel"),
        name="input_ln",
    )(x2, g.reshape(1, d), b.reshape(1, d))


def _matmul_kernel(h_ref, w_ref, o_ref):
    o_ref[...] = _dot(h_ref[...], w_ref[...]).astype(o_ref.dtype)


def in_projection(hb, w_bf16, tm=2048, tn=768):
    n, d = hb.shape
    w = w_bf16.shape[1]
    return pl.pallas_call(
        _matmul_kernel,
        grid=(n // tm, w // tn),
        in_specs=[pl.BlockSpec((tm, d), lambda i, j: (i, 0)),
                  pl.BlockSpec((d, tn), lambda i, j: (0, j))],
        out_specs=pl.BlockSpec((tm, tn), lambda i, j: (i, j)),
        out_shape=jax.ShapeDtypeStruct((n, w), BF16),
        compiler_params=_cparams("parallel", "arbitrary"),
        name="in_proj",
    )(hb, w_bf16)


def _retention_tables(c):
    idx = np.arange(N_HEADS, dtype=np.float64)
    lg = [np.log1p(-np.exp2(-5.0 - 2.0 * idx)), np.log1p(-np.exp2(-6.0 - 2.0 * idx))]
    i = np.arange(c, dtype=np.float64)
    diff = i[:, None] - i[None, :]
    lane_head = np.repeat(np.arange(N_HEADS), HEAD_DIM)
    dmat = np.zeros((2, N_HEADS, c, c), np.float32)
    oscale = np.zeros((2, c, BR_W), np.float32)
    kscale = np.zeros((2, c, BR_W), np.float32)
    sdecay = np.zeros((2, 1, BR_W), np.float32)
    for h in range(N_HEADS):
        dmat[0, h] = np.where(diff >= 0, np.exp(lg[0][h] * np.maximum(diff, 0)), 0.0)
        dmat[1, h] = np.where(diff <= 0, np.exp(lg[1][h] * np.maximum(-diff, 0)), 0.0)
    oscale[0] = np.exp(lg[0][lane_head][None, :] * (i[:, None] + 1.0))
    oscale[1] = np.exp(lg[1][lane_head][None, :] * (c - i[:, None]))
    kscale[0] = np.exp(lg[0][lane_head][None, :] * (c - 1.0 - i[:, None]))
    kscale[1] = np.exp(lg[1][lane_head][None, :] * i[:, None])
    sdecay[0, 0] = np.exp(lg[0][lane_head] * c)
    sdecay[1, 0] = np.exp(lg[1][lane_head] * c)
    return (jnp.asarray(dmat), jnp.asarray(oscale), jnp.asarray(kscale), jnp.asarray(sdecay))


def _retention_kernel(qf, kf, vf, qb, kb, vb, dmat, oscale, kscale, sdecay, of_ref, ob_ref, state):
    @pl.when(pl.program_id(1) == 0)
    def _():
        state[...] = jnp.zeros_like(state)

    c = qf.shape[0]
    lane_head = _head_id((c, BR_W), 1)
    bd = _block_diag_mask(BR_W)
    for dirn, (q_ref, k_ref, v_ref, o_ref) in enumerate(((qf, kf, vf, of_ref), (qb, kb, vb, ob_ref))):
        q = q_ref[...]
        k = k_ref[...] * jnp.asarray(HEAD_DIM ** -0.5, BF16)
        v = v_ref[...]
        s_old = state[dirn]
        out = _dot(q, s_old.astype(BF16)) * oscale[dirn]
        for h in range(N_HEADS):
            qz = jnp.where(lane_head == h, q, jnp.zeros_like(q))
            a = (_nt(qz, k) * dmat[dirn, h]).astype(BF16)
            out = out + jnp.where(lane_head == h, _dot(a, v), 0.0)
        o_ref[...] = out
        kw = (k.astype(F32) * kscale[dirn]).astype(BF16)
        state[dirn] = s_old * sdecay[dirn] + jnp.where(bd, _tn(kw, v), 0.0)


def retention(p, batch, seq):
    c = RET_CHUNK
    nc = seq // c
    dmat, oscale, kscale, sdecay = _retention_tables(c)
    cb = COL_RET // BR_W

    def fwd(col):
        return pl.BlockSpec((c, BR_W), lambda b, j: (b * nc + j, col))

    def bwd(col):
        return pl.BlockSpec((c, BR_W), lambda b, j: (b * nc + nc - 1 - j, col))

    def whole(a):
        nd = a.ndim
        return pl.BlockSpec(a.shape, lambda b, j: (0,) * nd)

    n = batch * seq
    return pl.pallas_call(
        _retention_kernel,
        grid=(batch, nc),
        in_specs=[fwd(cb), fwd(cb + 1), fwd(cb + 2), bwd(cb), bwd(cb + 1), bwd(cb + 2),
                  whole(dmat), whole(oscale), whole(kscale), whole(sdecay)],
        out_specs=[pl.BlockSpec((c, BR_W), lambda b, j: (b * nc + j, 0)),
                   pl.BlockSpec((c, BR_W), lambda b, j: (b * nc + nc - 1 - j, 0))],
        out_shape=[jax.ShapeDtypeStruct((n, BR_W), F32)] * 2,
        scratch_shapes=[pltpu.VMEM((2, BR_W, BR_W), F32)],
        compiler_params=_cparams("parallel", "arbitrary"),
        name="retention",
    )(p, p, p, p, p, p, dmat, oscale, kscale, sdecay)


def _hgrn_kernel(lb_ref, qf, ff, vf, qb, fb, vb, of_ref, ob_ref, state, kpad, gpad, vpad, *, layer):
    c = qf.shape[0]

    @pl.when(pl.program_id(1) == 0)
    def _():
        state[...] = jnp.zeros_like(state)
        kpad[...] = jnp.zeros_like(kpad)
        gpad[...] = jnp.zeros_like(gpad)
        vpad[...] = jnp.zeros_like(vpad)

    lb = lb_ref[...]
    e = jnp.exp(lb - jnp.max(lb, axis=0, keepdims=True))
    prob = e / jnp.sum(e, axis=0, keepdims=True)
    lower = jnp.sum(prob[:layer + 1], axis=0) - prob[0]

    row = lax.broadcasted_iota(jnp.int32, (c, c), 0)
    col = lax.broadcasted_iota(jnp.int32, (c, c), 1)
    bd = _block_diag_mask(BR_W)
    seg_ones = jnp.where(bd, 1.0, 0.0).astype(BF16)

    for dirn, (q_ref, f_ref, v_ref, o_ref) in enumerate(((qf, ff, vf, of_ref), (qb, fb, vb, ob_ref))):
        rev = dirn == 1
        lo = lower[dirn:dirn + 1, :]
        fpre = f_ref[...].astype(F32)
        logf = jnp.log(lo + (1.0 - lo) * _sigmoid(fpre))
        kk = (1.0 - lo) * _sigmoid(-fpre)
        qx = q_ref[...].astype(F32)
        q = qx * _sigmoid(qx)
        v_bf = v_ref[...]
        v = v_bf.astype(F32)
        tri = jnp.where((col >= row) if rev else (col <= row), 1.0, 0.0).astype(BF16)
        g = _tri_cumsum(tri, logf)
        g_end = g[0:1, :] if rev else g[c - 1:c, :]
        s_old = state[dirn]
        inter = _nt((q * jnp.exp(g)).astype(BF16), s_old.astype(BF16))
        kd = (kk * jnp.exp(g_end - g)).astype(BF16)
        state[dirn] = s_old * jnp.exp(g_end) + jnp.where(bd, _tn(v_bf, kd), 0.0)
        for b in range(8):
            lo_row = (c - b) if rev else (c + b)
            kpad[b, dirn, lo_row:lo_row + c, :] = kk
            gpad[b, dirn, lo_row:lo_row + c, :] = g
            vpad[b, dirn, lo_row:lo_row + c, :] = v

        intra = jnp.zeros((c, BR_W), F32)
        for b in range(8):
            for a in range(c // 8):
                start = (c + 8 * a) if rev else (c - 8 * a)
                ks = kpad[b, dirn, start:start + c, :]
                gs = gpad[b, dirn, start:start + c, :]
                vs = vpad[b, dirn, start:start + c, :]
                term = q * ks * jnp.exp(g - gs)
                intra = intra + _dot(term.astype(BF16), seg_ones) * vs
        o_ref[...] = inter + intra


def _tri_cumsum(tri_bf16, x):
    acc = None
    rem = x
    for _ in range(3):
        piece = rem.astype(BF16)
        part = _dot(tri_bf16, piece)
        acc = part if acc is None else acc + part
        rem = rem - piece.astype(F32)
    return acc


def hgrn(p, hgrn_lb, layer, batch, seq):
    c = HGRN_CHUNK
    nc = seq // c
    cb = COL_HGRN // BR_W

    def fwd(col):
        return pl.BlockSpec((c, BR_W), lambda b, j: (b * nc + j, col))

    def bwd(col):
        return pl.BlockSpec((c, BR_W), lambda b, j: (b * nc + nc - 1 - j, col))

    n = batch * seq
    return pl.pallas_call(
        functools.partial(_hgrn_kernel, layer=layer),
        grid=(batch, nc),
        in_specs=[pl.BlockSpec(hgrn_lb.shape, lambda b, j: (0, 0, 0)),
                  fwd(cb), fwd(cb + 1), fwd(cb + 3), bwd(cb), bwd(cb + 2), bwd(cb + 3)],
        out_specs=[pl.BlockSpec((c, BR_W), lambda b, j: (b * nc + j, 0)),
                   pl.BlockSpec((c, BR_W), lambda b, j: (b * nc + nc - 1 - j, 0))],
        out_shape=[jax.ShapeDtypeStruct((n, BR_W), F32)] * 2,
        scratch_shapes=[pltpu.VMEM((2, BR_W, BR_W), F32),
                        pltpu.VMEM((8, 2, 3 * c, BR_W), F32),
                        pltpu.VMEM((8, 2, 3 * c, BR_W), F32),
                        pltpu.VMEM((8, 2, 3 * c, BR_W), F32)],
        compiler_params=_cparams("parallel", "arbitrary"),
        name="hgrn",
    )(hgrn_lb, p, p, p, p, p, p)


def _nat_bias_table(rpb):
    cq = np.arange(GRID_W)
    ck = np.arange(GRID_W)
    col_start = np.clip(cq - NAT_KW // 2, 0, GRID_W - NAT_KW)
    col_mask = (ck[None, :] >= col_start[:, None]) & (ck[None, :] < col_start[:, None] + NAT_KW)
    dc = np.clip(ck[None, :] - cq[:, None], -(NAT_KW - 1), NAT_KW - 1) + (NAT_KW - 1)
    pick = (dc[None, :, :] == np.arange(2 * NAT_KW - 1)[:, None, None]).astype(np.float32)
    tiles = jnp.einsum('hrd,dqk->hrqk', rpb.astype(F32), jnp.asarray(pick),
                       precision=lax.Precision.HIGHEST)
    tiles = jnp.where(jnp.asarray(col_mask)[None, None], tiles, MASK_VALUE)
    t = jnp.stack([tiles[:, b:b + NAT_KH] for b in range(NAT_KH)], axis=1)
    return t.transpose(0, 1, 3, 2, 4).reshape(N_HEADS, NAT_KH, GRID_W, NAT_KH * GRID_W)


def _nat_kernel(q_ref, k_ref, v_ref, bias_ref, o_ref, *, rows_per_step, n_rows):
    j = pl.program_id(1)
    lane_head = _head_id((GRID_W, BR_W), 1)
    win = NAT_KH * GRID_W

    def body(i, carry):
        r = j * rows_per_step + i
        rs = jnp.clip(r - NAT_KH // 2, 0, n_rows - NAT_KH)
        base = rs - r + (NAT_KH - 1)
        q = q_ref[pl.ds(pl.multiple_of(i * GRID_W, GRID_W), GRID_W), :] * jnp.asarray(HEAD_DIM ** -0.5, BF16)
        kw = k_ref[pl.ds(pl.multiple_of(rs * GRID_W, GRID_W), win), :]
        vw = v_ref[pl.ds(pl.multiple_of(rs * GRID_W, GRID_W), win), :]
        out = jnp.zeros((GRID_W, BR_W), F32)
        for h in range(N_HEADS):
            qz = jnp.where(lane_head == h, q, jnp.zeros_like(q))
            s = _nt(qz, kw) + bias_ref[h, base]
            m = jnp.max(s, axis=-1, keepdims=True)
            pr = jnp.exp(s - m)
            l = jnp.sum(pr, axis=-1, keepdims=True)
            o = _dot(pr.astype(BF16), vw) / l
            out = out + jnp.where(lane_head == h, o, 0.0)
        o_ref[pl.ds(pl.multiple_of(i * GRID_W, GRID_W), GRID_W), :] = out.astype(o_ref.dtype)
        return carry

    lax.fori_loop(0, rows_per_step, body, 0, unroll=True)


def neighbourhood_attention(p, rpb, batch, seq):
    n_rows = seq // GRID_W
    rps = NAT_ROWS_PER_STEP
    steps = n_rows // rps
    bias = _nat_bias_table(rpb)
    cb = COL_NAT // BR_W
    tq = rps * GRID_W
    return pl.pallas_call(
        functools.partial(_nat_kernel, rows_per_step=rps, n_rows=n_rows),
        grid=(batch, steps),
        in_specs=[pl.BlockSpec((tq, BR_W), lambda b, j: (b * steps + j, cb)),
                  pl.BlockSpec((seq, BR_W), lambda b, j: (b, cb + 1)),
                  pl.BlockSpec((seq, BR_W), lambda b, j: (b, cb + 2)),
                  pl.BlockSpec(bias.shape, lambda b, j: (0, 0, 0, 0))],
        out_specs=pl.BlockSpec((tq, BR_W), lambda b, j: (b * steps + j, 0)),
        out_shape=jax.ShapeDtypeStruct((batch * seq, BR_W), BF16),
        compiler_params=_cparams("parallel", "arbitrary"),
        name="nat",
    )(p, p, p, bias)


def _diff_kernel(lam_ref, slope_ref, sub_ref, q_ref, k_ref, v_ref, o_ref,
                 acc1, acc2, m1, l1, m2, l2, *, lam_init, n_tiles):
    t = q_ref.shape[0]
    reps = t // 128
    qi = pl.program_id(2)
    neg_slope = -slope_ref[0, 0:1, 0:1]
    lp = lam_ref[...]
    lam = (jnp.exp(jnp.sum(lp[0:1] * lp[1:2], axis=-1, keepdims=True))
           - jnp.exp(jnp.sum(lp[2:3] * lp[3:4], axis=-1, keepdims=True)) + lam_init)

    q = q_ref[...] * jnp.asarray(0.125, BF16)
    lane = lax.broadcasted_iota(jnp.int32, q.shape, 1)
    q1 = jnp.where(lane < 64, q, jnp.zeros_like(q))
    q2 = jnp.where(lane >= 64, q, jnp.zeros_like(q))
    qpos = (qi * t + lax.broadcasted_iota(jnp.int32, (t, 1), 0)).astype(F32)

    acc1[...] = jnp.zeros_like(acc1)
    acc2[...] = jnp.zeros_like(acc2)
    l1[...] = jnp.zeros_like(l1)
    l2[...] = jnp.zeros_like(l2)
    m1[...] = jnp.full_like(m1, -jnp.inf)
    m2[...] = jnp.full_like(m2, -jnp.inf)

    def tile(kj, carry):
        start = pl.multiple_of(kj * t, t)
        k = k_ref[pl.ds(start, t), :]
        v = v_ref[pl.ds(start, t), :]
        kpos = (kj * t + lax.broadcasted_iota(jnp.int32, (1, t), 1)).astype(F32)
        bias = neg_slope * jnp.abs(qpos - kpos)
        for qx, acc, m_ref, l_ref in ((q1, acc1, m1, l1), (q2, acc2, m2, l2)):
            s = _nt(qx, k) + bias
            m_old = m_ref[...]
            m_new = jnp.maximum(m_old, jnp.max(s, axis=-1, keepdims=True))
            a = jnp.exp(m_old - m_new)
            pr = jnp.exp(s - jnp.concatenate([m_new] * reps, axis=1))
            l_ref[...] = a * l_ref[...] + jnp.sum(pr, axis=-1, keepdims=True)
            acc[...] = a * acc[...] + _dot(pr.astype(BF16), v)
            m_ref[...] = m_new
        return carry

    lax.fori_loop(0, n_tiles, tile, 0)

    o = acc1[...] / l1[...] - lam * (acc2[...] / l2[...])
    o = o * lax.rsqrt(jnp.mean(o * o, axis=-1, keepdims=True) + EPS) * sub_ref[0]
    o_ref[...] = (o * (1.0 - lam_init)).astype(o_ref.dtype)


def diff_attention(p, lam_params, subln, layer, batch, seq):
    t = DIFF_TILE
    nt = seq // t
    lam_init = 0.8 - 0.6 * math.exp(-0.3 * layer)
    slopes = np.exp2(-8.0 * (np.arange(N_HEADS, dtype=np.float64) + 1.0) / N_HEADS)
    slope_tab = jnp.asarray(np.broadcast_to(slopes[:, None, None], (N_HEADS, 8, 128)).astype(np.float32))
    qc, kc, vc = (COL_DIFF // DIFF_DV, (COL_DIFF + DIFF_W) // DIFF_DV, (COL_DIFF + 2 * DIFF_W) // DIFF_DV)
    return pl.pallas_call(
        functools.partial(_diff_kernel, lam_init=lam_init, n_tiles=nt),
        grid=(batch, N_HEADS, nt),
        in_specs=[pl.BlockSpec(lam_params.shape, lambda b, h, i: (0, 0)),
                  pl.BlockSpec((1, 8, 128), lambda b, h, i: (h, 0, 0)),
                  pl.BlockSpec((1, 1, DIFF_DV), lambda b, h, i: (h, 0, 0)),
                  pl.BlockSpec((t, DIFF_DV), lambda b, h, i: (b * nt + i, qc + h)),
                  pl.BlockSpec((seq, DIFF_DV), lambda b, h, i: (b, kc + h)),
                  pl.BlockSpec((seq, DIFF_DV), lambda b, h, i: (b, vc + h))],
        out_specs=pl.BlockSpec((t, DIFF_DV), lambda b, h, i: (b * nt + i, h)),
        out_shape=jax.ShapeDtypeStruct((batch * seq, DIFF_W), BF16),
        scratch_shapes=[pltpu.VMEM((t, DIFF_DV), F32)] * 6,
        compiler_params=_cparams("parallel", "parallel", "arbitrary"),
        name="diff_attn",
    )(lam_params, slope_tab, subln.reshape(N_HEADS, 1, DIFF_DV), p, p, p)


def _merge_kernel(hf_ref, hb_ref, rof, rob, rg, ynat, hof, hob, hg, ydiff,
                  wg_ref, bg_ref, wr_ref, wn_ref, wh_ref, wd_ref, wo_ref,
                  rgn_ref, hgn_ref, lng_ref, lnb_ref, of_ref, ob_ref):
    seg_mean = jnp.where(_block_diag_mask(BR_W), 1.0 / HEAD_DIM, 0.0).astype(BF16)

    def head_norm_gate(o, gn, gate):
        ms = _dot_split(o * o, seg_mean, terms=2)
        gx = gate.astype(F32)
        return (o * lax.rsqrt(ms + EPS) * gn * (gx * _sigmoid(gx))).astype(BF16)

    y_ret = head_norm_gate(rof[...] + rob[...], rgn_ref[...], rg[...])
    y_hgrn = head_norm_gate(hof[...] + hob[...], hgn_ref[...], hg[...])
    hb = hb_ref[...]
    d = hf_ref.shape[1]
    merged = None
    for i, (y, w_ref) in enumerate(((y_ret, wr_ref), (ynat[...], wn_ref), (y_hgrn, wh_ref), (ydiff[...], wd_ref))):
        gate = _sigmoid(_dot(hb, wg_ref[:, i * d:(i + 1) * d]) + bg_ref[:, i * d:(i + 1) * d])
        part = gate * _dot(y, w_ref[...])
        merged = part if merged is None else merged + part
    z = ALPHA * hf_ref[...] + _dot(merged.astype(BF16), wo_ref[...])
    h = _layer_norm(z, lng_ref[...], lnb_ref[...])
    of_ref[...] = h
    ob_ref[...] = h.astype(BF16)


def merge_and_norm(hf, hb, p, ret_f, ret_b, y_nat, hg_f, hg_b, y_diff,
                   w_gate, b_gate, w_br_ret, w_br_nat, w_br_hgrn, w_br_diff, w_out,
                   ret_gn, hgrn_gn, ln_g, ln_b, tm=512):
    n, d = hf.shape

    def rows(width, col=0):
        return pl.BlockSpec((tm, width), lambda i: (i, col))

    def whole(a):
        nd = a.ndim
        return pl.BlockSpec(a.shape, lambda i: (0,) * nd)

    consts = [w_gate, b_gate.reshape(1, -1), w_br_ret, w_br_nat, w_br_hgrn, w_br_diff, w_out,
              ret_gn.reshape(1, BR_W), hgrn_gn.reshape(1, BR_W), ln_g.reshape(1, d), ln_b.reshape(1, d)]
    return pl.pallas_call(
        _merge_kernel,
        grid=(n // tm,),
        in_specs=[rows(d), rows(d),
                  rows(BR_W), rows(BR_W), rows(BR_W, COL_RET // BR_W + 3),
                  rows(BR_W),
                  rows(BR_W), rows(BR_W), rows(BR_W, COL_HGRN // BR_W + 4),
                  rows(DIFF_W)] + [whole(a) for a in consts],
        out_specs=[rows(d), rows(d)],
        out_shape=[jax.ShapeDtypeStruct((n, d), F32), jax.ShapeDtypeStruct((n, d), BF16)],
        compiler_params=_cparams("parallel"),
        name="merge",
    )(hf, hb, ret_f, ret_b, p, y_nat, hg_f, hg_b, p, y_diff, *consts)


def _router_kernel(h_ref, w_ref, bias_ref, idx_ref, w_ref8, cnt_ref):
    w = w_ref[...]
    w_hi = w.astype(BF16)
    w_lo = (w - w_hi.astype(F32)).astype(BF16)
    h = h_ref[...]
    logits = _dot_split(h, w_hi, terms=3) + _dot_split(h, w_lo, terms=2)
    scores = _sigmoid(logits)
    biased = scores + bias_ref[...]
    shape = scores.shape
    lane = lax.broadcasted_iota(jnp.int32, shape, 1).astype(F32)
    group = lax.shift_right_logical(lax.broadcasted_iota(jnp.int32, shape, 1), 3).astype(F32)
    big = jnp.asarray(1e9, F32)
    neg = jnp.asarray(-jnp.inf, F32)

    def first_max(x, ids):
        m = jnp.max(x, axis=-1, keepdims=True)
        return m, jnp.min(jnp.where(x == m, ids, big), axis=-1, keepdims=True)

    gscore = jnp.zeros(shape, F32)
    for g in range(N_GROUPS):
        xg = jnp.where(group == g, biased, neg)
        m1, i1 = first_max(xg, lane)
        m2 = jnp.max(jnp.where(lane == i1, neg, xg), axis=-1, keepdims=True)
        gscore = jnp.where(group == g, m1 + m2, gscore)
    gsel = jnp.zeros(shape, jnp.bool_)
    for _ in range(TOPK_GROUPS):
        _, gi = first_max(gscore, group)
        hit = group == gi
        gsel = gsel | hit
        gscore = jnp.where(hit, neg, gscore)
    cand = jnp.where(gsel, biased, MASK_VALUE)
    esel = jnp.zeros(shape, jnp.bool_)
    picks = []
    for _ in range(TOP_K):
        _, ei = first_max(cand, lane)
        hit = lane == ei
        picks.append((ei, jnp.sum(jnp.where(hit, scores, 0.0), axis=-1, keepdims=True)))
        esel = esel | hit
        cand = jnp.where(hit, neg, cand)

    tm = shape[0]
    sel = jnp.where(esel, 1.0, 0.0)
    cnt_ref[...] = jnp.concatenate(
        [jnp.sum(sel[s * ROUTE_TILE:(s + 1) * ROUTE_TILE], axis=0, keepdims=True) for s in range(tm // ROUTE_TILE)],
        axis=0)

    slot = lax.broadcasted_iota(jnp.int32, (tm, TOP_K), 1)
    idx8 = jnp.zeros((tm, TOP_K), F32)
    w8 = jnp.zeros((tm, TOP_K), F32)
    wsum = jnp.zeros((tm, 1), F32)
    for k, (ei, wk) in enumerate(picks):
        idx8 = jnp.where(slot == k, ei, idx8)
        w8 = jnp.where(slot == k, wk, w8)
        wsum = wsum + wk
    idx_ref[...] = idx8.astype(jnp.int32)
    w_ref8[...] = w8 / (wsum + 1e-20) * ROUTED_SCALE


def router(hf, w_router, router_bias, tm=1024):
    n, d = hf.shape
    e = w_router.shape[1]
    sub = tm // ROUTE_TILE
    return pl.pallas_call(
        _router_kernel,
        grid=(n // tm,),
        in_specs=[pl.BlockSpec((tm, d), lambda i: (i, 0)),
                  pl.BlockSpec((d, e), lambda i: (0, 0)),
                  pl.BlockSpec((1, e), lambda i: (0, 0))],
        out_specs=[pl.BlockSpec((tm, TOP_K), lambda i: (i, 0)),
                   pl.BlockSpec((tm, TOP_K), lambda i: (i, 0)),
                   pl.BlockSpec((sub, e), lambda i: (i, 0))],
        out_shape=[jax.ShapeDtypeStruct((n, TOP_K), jnp.int32),
                   jax.ShapeDtypeStruct((n, TOP_K), F32),
                   jax.ShapeDtypeStruct((n // ROUTE_TILE, e), F32)],
        compiler_params=_cparams("arbitrary"),
        name="router",
    )(hf, w_router, router_bias.reshape(1, e))


HIGH_HALF = -65536


def _pack_pairs(x):
    w = x.shape[1] // 2
    lo = lax.bitcast_convert_type(x[:, :w].astype(BF16).astype(F32), jnp.int32)
    hi = lax.bitcast_convert_type(x[:, w:].astype(BF16).astype(F32), jnp.int32)
    return lax.shift_right_logical(lo, 16) | (hi & HIGH_HALF)


def _unpack_pairs(p):
    lo = lax.bitcast_convert_type(lax.shift_left(p, 16), F32)
    hi = lax.bitcast_convert_type(p & HIGH_HALF, F32)
    return jnp.concatenate([lo, hi], axis=1)


def _swiglu(x, wg, wu):
    a = _dot(x, wg)
    return a * _sigmoid(a) * _dot(x, wu)


def _segment_copies(tab_ref, first_col, max_run, make_copy, action):
    top_bit = max_run.bit_length() - 1
    low_bit = RUN_ALIGN.bit_length() - 1

    def per_expert(e, carry):
        col = first_col + e
        count = tab_ref[0, col]
        local = tab_ref[1, col]
        glob = tab_ref[2, col]
        for b in range(top_bit, low_bit - 1, -1):
            size = 1 << b
            taken = count & size

            @pl.when(taken != 0)
            def _(local=local, glob=glob, size=size):
                action(make_copy(pl.multiple_of(local, RUN_ALIGN), pl.multiple_of(glob, RUN_ALIGN), size))

            local = local + taken
            glob = glob + taken
        return carry

    lax.fori_loop(0, N_EXPERTS, per_expert, 0)


def _tile_rows_used(tab_ref, t):
    last = t * N_EXPERTS + (N_EXPERTS - 1)
    return tab_ref[1, last] + tab_ref[0, last]


def _wait_rows(total, max_rows, make_copy):
    for b in range(max_rows.bit_length() - 1, RUN_ALIGN.bit_length() - 2, -1):
        size = 1 << b

        @pl.when((total & size) != 0)
        def _(size=size):
            make_copy(0, 0, size).wait()


def _block_dispatch_kernel(tab_ref, gap_ref, tail_ref, idx_t_ref, hb_ref, xs_ref, sorted_buf, zeros_buf, sem):
    t = pl.program_id(0)
    tm = hb_ref.shape[0]
    rows = sorted_buf.shape[0]
    idx_t = idx_t_ref[...]
    expert = lax.broadcasted_iota(jnp.int32, (N_EXPERTS, tm), 0)
    sel_t = jnp.zeros((N_EXPERTS, tm), F32)
    for k in range(TOP_K):
        sel_t = sel_t + jnp.where(idx_t[k:k + 1, :] == expert, 1.0, 0.0)
    sel_b = sel_t.astype(BF16)
    before_tok = (lax.broadcasted_iota(jnp.int32, (tm, tm), 0) < lax.broadcasted_iota(jnp.int32, (tm, tm), 1))
    before_exp = (lax.broadcasted_iota(jnp.int32, (N_EXPERTS, N_EXPERTS), 1)
                  < lax.broadcasted_iota(jnp.int32, (N_EXPERTS, N_EXPERTS), 0))
    run = jnp.broadcast_to(_run_length(jnp.sum(sel_t, axis=1, keepdims=True)), (N_EXPERTS, tm)).astype(BF16)
    base = (_dot(jnp.where(before_exp, 1.0, 0.0).astype(BF16), run)
            + _dot(sel_b, jnp.where(before_tok, 1.0, 0.0).astype(BF16)))
    row = lax.broadcasted_iota(jnp.int32, (rows, tm), 0)
    perm = jnp.zeros((rows, tm), F32)
    for k in range(TOP_K):
        place = jnp.sum(jnp.where(idx_t[k:k + 1, :] == expert, base, 0.0), axis=0, keepdims=True).astype(jnp.int32)
        perm = jnp.where(row == place, 1.0, perm)
    perm = perm.astype(BF16)
    half = hb_ref.shape[1] // 2
    lo = lax.bitcast_convert_type(_dot(perm, hb_ref[:, :half]), jnp.int32)
    hi = lax.bitcast_convert_type(_dot(perm, hb_ref[:, half:]), jnp.int32)
    sorted_buf[...] = lax.shift_right_logical(lo, 16) | (hi & HIGH_HALF)

    def make_copy(local, glob, size):
        return pltpu.make_async_copy(sorted_buf.at[pl.ds(local, size)], xs_ref.at[pl.ds(glob, size)], sem.at[0])

    _segment_copies(tab_ref, t * N_EXPERTS, tm, make_copy, lambda c: c.start())
    _wait_rows(_tile_rows_used(tab_ref, t), rows, make_copy)

    @pl.when(t == pl.num_programs(0) - 1)
    def _():
        zeros_buf[...] = jnp.zeros_like(zeros_buf)
        tile_rows = zeros_buf.shape[0]

        def zero_copy(local, glob, size):
            return pltpu.make_async_copy(zeros_buf.at[pl.ds(0, size)], xs_ref.at[pl.ds(glob, size)], sem.at[0])

        def tail_copy(j):
            return zero_copy(0, pl.multiple_of(tail_ref[0] + j * tile_rows, RUN_ALIGN), tile_rows)

        _segment_copies(gap_ref, 0, tile_rows // 2, zero_copy, lambda c: c.start())
        lax.fori_loop(0, tail_ref[1], lambda j, c: (tail_copy(j).start(), c)[1], 0)
        _segment_copies(gap_ref, 0, tile_rows // 2, zero_copy, lambda c: c.wait())
        lax.fori_loop(0, tail_ref[1], lambda j, c: (tail_copy(j).wait(), c)[1], 0)


def block_dispatch(hb, idx_t, tab, gaps, tail, tm):
    n, d = hb.shape
    grid_spec = pltpu.PrefetchScalarGridSpec(
        num_scalar_prefetch=3,
        grid=(n // tm,),
        in_specs=[pl.BlockSpec((TOP_K, tm), lambda i, *_: (0, i)),
                  pl.BlockSpec((tm, d), lambda i, *_: (i, 0))],
        out_specs=pl.BlockSpec(memory_space=pl.ANY),
        scratch_shapes=[pltpu.VMEM((_tile_rows(tm), d // 2), jnp.int32),
                        pltpu.VMEM((EXPERT_ROWS, d // 2), jnp.int32),
                        pltpu.SemaphoreType.DMA((1,))],
    )
    return pl.pallas_call(
        _block_dispatch_kernel,
        grid_spec=grid_spec,
        out_shape=jax.ShapeDtypeStruct((_sorted_rows(n, tm), d // 2), jnp.int32),
        compiler_params=_cparams("arbitrary"),
        name="dispatch",
    )(tab, gaps, tail, idx_t, hb)


def _tile_tables(tile_counts, tm_fine, tm):
    g = tm // tm_fine
    n_tokens = tile_counts.shape[0] * tm_fine
    n_tiles = _sorted_rows(n_tokens, tm) // EXPERT_ROWS
    cnt = tile_counts.astype(jnp.int32).reshape(-1, g, N_EXPERTS).sum(axis=1)
    cnt = (cnt + (RUN_ALIGN - 1)) // RUN_ALIGN * RUN_ALIGN
    local = jnp.cumsum(cnt, axis=1) - cnt
    used = cnt.sum(axis=0)
    seg = (used + (EXPERT_ROWS - 1)) // EXPERT_ROWS * EXPERT_ROWS
    seg_end = jnp.cumsum(seg)
    seg_start = seg_end - seg
    glob = seg_start[None, :] + jnp.cumsum(cnt, axis=0) - cnt
    runs = jnp.stack([cnt.reshape(-1), local.reshape(-1), glob.reshape(-1)])
    gaps = jnp.stack([seg - used, jnp.zeros_like(used), seg_start + used])
    tail = jnp.stack([seg_end[-1], n_tiles - seg_end[-1] // EXPERT_ROWS])
    tile_row = jnp.arange(n_tiles, dtype=jnp.int32) * EXPERT_ROWS
    owner = jnp.sum((seg_end[None, :] <= tile_row[:, None]).astype(jnp.int32), axis=1)
    owner_c = jnp.minimum(owner, N_EXPERTS - 1)
    of_owner = lambda a: jnp.sum(jnp.where(jnp.arange(N_EXPERTS)[None, :] == owner_c[:, None], a[None, :], 0), axis=1)
    filled = jnp.clip(of_owner(seg_start + used) - tile_row, 0, EXPERT_ROWS)
    tiles = jnp.stack([owner_c, jnp.where(owner < N_EXPERTS, filled, 0)])
    return runs, gaps, tail, tiles.astype(jnp.int32)


def _expert_kernel(tile_ref, xs_ref, wg_ref, wu_ref, wd_ref, ys_ref, wgb, wub, wdb):
    i = pl.program_id(0)
    expert = tile_ref[0, i]

    @pl.when(tile_ref[1, i] > 0)
    def _():
        @pl.when((i == 0) | (expert != tile_ref[0, jnp.maximum(i - 1, 0)]))
        def _():
            wgb[...] = wg_ref[0, 0].astype(BF16)
            wub[...] = wu_ref[0, 0].astype(BF16)
            wdb[...] = wd_ref[0, 0].astype(BF16)

        x = _unpack_pairs(xs_ref[...]).astype(BF16)
        hid = _swiglu(x, wgb[...], wub[...])
        ys_ref[...] = _pack_pairs(_dot(hid.astype(BF16), wdb[...]))

    @pl.when(tile_ref[1, i] == 0)
    def _():
        ys_ref[...] = jnp.zeros_like(ys_ref)


def grouped_experts(xs, tiles, w_e_gate, w_e_up, w_e_down, layer):
    rows, w = xs.shape
    _, _, d, hid = w_e_gate.shape
    tr = EXPERT_ROWS
    grid_spec = pltpu.PrefetchScalarGridSpec(
        num_scalar_prefetch=1,
        grid=(rows // tr,),
        in_specs=[pl.BlockSpec((tr, w), lambda i, tl: (i, 0)),
                  pl.BlockSpec((1, 1, d, hid), lambda i, tl: (layer, tl[0, i], 0, 0)),
                  pl.BlockSpec((1, 1, d, hid), lambda i, tl: (layer, tl[0, i], 0, 0)),
                  pl.BlockSpec((1, 1, hid, d), lambda i, tl: (layer, tl[0, i], 0, 0))],
        out_specs=pl.BlockSpec((tr, w), lambda i, tl: (i, 0)),
        scratch_shapes=[pltpu.VMEM((d, hid), BF16), pltpu.VMEM((d, hid), BF16), pltpu.VMEM((hid, d), BF16)],
    )
    return pl.pallas_call(
        _expert_kernel,
        grid_spec=grid_spec,
        out_shape=jax.ShapeDtypeStruct((rows, w), jnp.int32),
        compiler_params=_cparams("arbitrary"),
        name="experts",
    )(tiles, xs, w_e_gate, w_e_up, w_e_down)


def _block_combine_kernel(tab_ref, ys_ref, idx_ref, w8_ref, hf_ref, hb_ref, sg_ref, su_ref, sd_ref,
                          lng_ref, lnb_ref, of_ref, ob_ref, buf, sem):
    t = pl.program_id(0)
    tm = hf_ref.shape[0]
    rows = buf.shape[0]

    def make_copy(local, glob, size):
        return pltpu.make_async_copy(ys_ref.at[pl.ds(glob, size)], buf.at[pl.ds(local, size)], sem.at[0])

    _segment_copies(tab_ref, t * N_EXPERTS, tm, make_copy, lambda c: c.start())

    idx = idx_ref[...]
    w8 = w8_ref[...]
    expert = lax.broadcasted_iota(jnp.int32, (tm, N_EXPERTS), 1)
    sel = jnp.zeros((tm, N_EXPERTS), F32)
    for k in range(TOP_K):
        sel = sel + jnp.where(idx[:, k:k + 1] == expert, 1.0, 0.0)
    sel_b = sel.astype(BF16)
    before_tok = (lax.broadcasted_iota(jnp.int32, (tm, tm), 1) < lax.broadcasted_iota(jnp.int32, (tm, tm), 0))
    before_exp = (lax.broadcasted_iota(jnp.int32, (N_EXPERTS, N_EXPERTS), 0)
                  < lax.broadcasted_iota(jnp.int32, (N_EXPERTS, N_EXPERTS), 1))
    run = jnp.broadcast_to(_run_length(jnp.sum(sel, axis=0, keepdims=True)), (tm, N_EXPERTS)).astype(BF16)
    base = (_dot(run, jnp.where(before_exp, 1.0, 0.0).astype(BF16))
            + _dot(jnp.where(before_tok, 1.0, 0.0).astype(BF16), sel_b))
    col = lax.broadcasted_iota(jnp.int32, (tm, rows), 1)
    mix = jnp.zeros((tm, rows), F32)
    for k in range(TOP_K):
        place = jnp.sum(jnp.where(idx[:, k:k + 1] == expert, base, 0.0), axis=1, keepdims=True).astype(jnp.int32)
        mix = jnp.where(col == place, w8[:, k:k + 1], mix)
    mix = mix.astype(BF16)

    acc = _dot(_swiglu(hb_ref[...], sg_ref[...], su_ref[...]).astype(BF16), sd_ref[...])
    used = _tile_rows_used(tab_ref, t)
    _wait_rows(used, rows, make_copy)
    filled = lax.broadcasted_iota(jnp.int32, buf.shape, 0) < used
    y = _unpack_pairs(jnp.where(filled, buf[...], 0)).astype(BF16)
    acc = acc + _dot(mix, y)
    h = _layer_norm(ALPHA * hf_ref[...] + acc, lng_ref[...], lnb_ref[...])
    of_ref[...] = h
    ob_ref[...] = h.astype(BF16)


def block_combine(ys, tab, idx8, w8, hf, hb, w_s_gate, w_s_up, w_s_down, ln_g, ln_b, tm):
    n, d = hf.shape
    w = ys.shape[1]

    def whole(a):
        return pl.BlockSpec(a.shape, lambda i, tab: (0, 0))

    def rows(width):
        return pl.BlockSpec((tm, width), lambda i, tab: (i, 0))

    consts = [w_s_gate, w_s_up, w_s_down, ln_g.reshape(1, d), ln_b.reshape(1, d)]
    grid_spec = pltpu.PrefetchScalarGridSpec(
        num_scalar_prefetch=1,
        grid=(n // tm,),
        in_specs=[pl.BlockSpec(memory_space=pl.ANY), rows(TOP_K), rows(TOP_K), rows(d), rows(d)]
                 + [whole(a) for a in consts],
        out_specs=[rows(d), rows(d)],
        scratch_shapes=[pltpu.VMEM((_tile_rows(tm), w), jnp.int32), pltpu.SemaphoreType.DMA((1,))],
    )
    return pl.pallas_call(
        _block_combine_kernel,
        grid_spec=grid_spec,
        out_shape=[jax.ShapeDtypeStruct((n, d), F32), jax.ShapeDtypeStruct((n, d), BF16)],
        compiler_params=_cparams("arbitrary"),
        name="combine",
    )(tab, ys, idx8, w8, hf, hb, *consts)


def moe_and_norm(hf, hb, w_router, router_bias, w_e_gate, w_e_up, w_e_down,
                 w_s_gate, w_s_up, w_s_down, ln_g, ln_b, layer):
    idx8, w8, tile_counts = router(hf, w_router, router_bias)
    runs, gaps, tail, tiles = _tile_tables(tile_counts, ROUTE_TILE, MOE_TILE)
    xs = block_dispatch(hb, idx8.T, runs, gaps, tail, MOE_TILE)
    ys = grouped_experts(xs, tiles, w_e_gate, w_e_up, w_e_down, layer)
    return block_combine(ys, runs, idx8, w8, hf, hb, w_s_gate, w_s_up, w_s_down, ln_g, ln_b, MOE_TILE)


def kernel(x, ln_in_g, ln_in_b, w_in, w_gate, b_gate, w_br_ret, w_br_nat, w_br_hgrn, w_br_diff, w_out,
           ret_gn, nat_rpb, hgrn_lb, hgrn_gn, diff_lambda, diff_subln, ln1_g, ln1_b, w_router,
           router_bias, w_e_gate, w_e_up, w_e_down, w_s_gate, w_s_up, w_s_down, ln2_g, ln2_b):
    batch, seq, d = x.shape
    bf = lambda a: a.astype(BF16)
    hf, hb = input_layer_norm(x.reshape(batch * seq, d), ln_in_g, ln_in_b)
    for l in range(DEPTH):
        p = in_projection(hb, bf(w_in[l]))
        ret_f, ret_b = retention(p, batch, seq)
        y_nat = neighbourhood_attention(p, nat_rpb[l], batch, seq)
        hg_f, hg_b = hgrn(p, hgrn_lb, l, batch, seq)
        y_diff = diff_attention(p, diff_lambda[l], diff_subln[l], l, batch, seq)
        hf, hb = merge_and_norm(hf, hb, p, ret_f, ret_b, y_nat, hg_f, hg_b, y_diff,
                                bf(w_gate[l]), b_gate[l], bf(w_br_ret[l]), bf(w_br_nat[l]),
                                bf(w_br_hgrn[l]), bf(w_br_diff[l]), bf(w_out[l]),
                                ret_gn[l], hgrn_gn[l], ln1_g[l], ln1_b[l])
        hf, hb = moe_and_norm(hf, hb, w_router[l], router_bias[l], w_e_gate, w_e_up, w_e_down,
                              bf(w_s_gate[l]), bf(w_s_up[l]), bf(w_s_down[l]), ln2_g[l], ln2_b[l], l)
    return hf.reshape(batch, seq, d)
```

```python
import functools
import math

import numpy as np
import jax
import jax.numpy as jnp
from jax import lax
from jax.experimental import pallas as pl
from jax.experimental.pallas import tpu as pltpu

F32 = jnp.float32
BF16 = jnp.bfloat16

D_MODEL = 1024
DEPTH = 2
GRID_W = 64
HEAD_DIM = 64
N_HEADS = 4
BR_W = N_HEADS * HEAD_DIM
NAT_KH = 8
NAT_KW = 16
DIFF_DV = 128
DIFF_W = N_HEADS * DIFF_DV
N_EXPERTS = 64
TOP_K = 8
N_GROUPS = 8
TOPK_GROUPS = 4
GROUP_SIZE = N_EXPERTS // N_GROUPS
EXPERT_HIDDEN = 256
ROUTED_SCALE = 2.5
EPS = 1e-5
MASK_VALUE = -1e30
LOG2E = math.log2(math.e)
DIFF_SCALE = 64 ** -0.5
ALPHA = (2.0 * DEPTH) ** 0.25
IN_WIDTH = 4608
COL_RET = 0
COL_NAT = 1024
COL_HGRN = 1792
COL_DIFF = 3072

VMEM_LIMIT = 56 * 1024 * 1024

RET_CHUNK = 256
HGRN_CHUNK = 64
HGRN_SUB = 16
DIFF_TILE = 512
NAT_ROWS_PER_STEP = 8
ROUTE_TILE = 128
MOE_TILE = 256
EXPERT_ROWS = 512
RUN_ALIGN = 8


def _run_length(count):
    return jnp.floor((count + (RUN_ALIGN - 1)) * (1.0 / RUN_ALIGN)) * RUN_ALIGN


def _tile_rows(tm):
    bound = TOP_K * tm + N_EXPERTS * (RUN_ALIGN - 1)
    return -(-bound // 128) * 128


def _sorted_rows(n_tokens, tm):
    bound = (TOP_K * n_tokens + (n_tokens // tm) * N_EXPERTS * (RUN_ALIGN - 1)
             + N_EXPERTS * (EXPERT_ROWS - RUN_ALIGN))
    return -(-bound // EXPERT_ROWS) * EXPERT_ROWS


def _cparams(*sem):
    return pltpu.CompilerParams(dimension_semantics=sem, vmem_limit_bytes=VMEM_LIMIT)


def _nt(a, b):
    return lax.dot_general(a, b, (((1,), (1,)), ((), ())), preferred_element_type=F32)


def _tn(a, b):
    return lax.dot_general(a, b, (((0,), (0,)), ((), ())), preferred_element_type=F32)


def _dot(a, b):
    return jnp.dot(a, b, preferred_element_type=F32)


def _dot_split(x, w_bf16, terms=3):
    acc = None
    rem = x
    for _ in range(terms):
        piece = rem.astype(BF16)
        part = _dot(piece, w_bf16)
        acc = part if acc is None else acc + part
        rem = rem - piece.astype(F32)
    return acc


def _layer_norm(z, g, b):
    mu = jnp.mean(z, axis=-1, keepdims=True)
    zc = z - mu
    var = jnp.mean(zc * zc, axis=-1, keepdims=True)
    return zc * lax.rsqrt(var + EPS) * g + b


def _sigmoid(x):
    return 1.0 / (1.0 + jnp.exp(-x))


def _head_id(shape, axis):
    return lax.shift_right_logical(lax.broadcasted_iota(jnp.int32, shape, axis), 6)


def _block_diag_mask(n):
    return _head_id((n, n), 0) == _head_id((n, n), 1)


def _ln_kernel(x_ref, g_ref, b_ref, hf_ref, hb_ref):
    h = _layer_norm(x_ref[...], g_ref[...], b_ref[...])
    hf_ref[...] = h
    hb_ref[...] = h.astype(BF16)


def input_layer_norm(x2, g, b, tm=1024):
    n, d = x2.shape
    return pl.pallas_call(
        _ln_kernel,
        grid=(n // tm,),
        in_specs=[pl.BlockSpec((tm, d), lambda i: (i, 0)),
                  pl.BlockSpec((1, d), lambda i: (0, 0)),
                  pl.BlockSpec((1, d), lambda i: (0, 0))],
        out_specs=[pl.BlockSpec((tm, d), lambda i: (i, 0)),
                   pl.BlockSpec((tm, d), lambda i: (i, 0))],
        out_shape=[jax.ShapeDtypeStruct((n, d), F32), jax.ShapeDtypeStruct((n, d), BF16)],
        compiler_params=_cparams("parallel"),
        name="input_ln",
    )(x2, g.reshape(1, d), b.reshape(1, d))


def _matmul_kernel(h_ref, w_ref, o_ref):
    o_ref[...] = _dot(h_ref[...], w_ref[...]).astype(o_ref.dtype)


def in_projection(hb, w_bf16, tm=2048, tn=768):
    n, d = hb.shape
    w = w_bf16.shape[1]
    return pl.pallas_call(
        _matmul_kernel,
        grid=(n // tm, w // tn),
        in_specs=[pl.BlockSpec((tm, d), lambda i, j: (i, 0)),
                  pl.BlockSpec((d, tn), lambda i, j: (0, j))],
        out_specs=pl.BlockSpec((tm, tn), lambda i, j: (i, j)),
        out_shape=jax.ShapeDtypeStruct((n, w), BF16),
        compiler_params=_cparams("parallel", "arbitrary"),
        name="in_proj",
    )(hb, w_bf16)


def _retention_tables(c):
    idx = np.arange(N_HEADS, dtype=np.float64)
    lg = [np.log1p(-np.exp2(-5.0 - 2.0 * idx)), np.log1p(-np.exp2(-6.0 - 2.0 * idx))]
    i = np.arange(c, dtype=np.float64)
    diff = i[:, None] - i[None, :]
    lane_head = np.repeat(np.arange(N_HEADS), HEAD_DIM)
    dmat = np.zeros((2, N_HEADS, c, c), np.float32)
    oscale = np.zeros((2, c, BR_W), np.float32)
    kscale = np.zeros((2, c, BR_W), np.float32)
    sdecay = np.zeros((2, 1, BR_W), np.float32)
    for h in range(N_HEADS):
        dmat[0, h] = np.where(diff >= 0, np.exp(lg[0][h] * np.maximum(diff, 0)), 0.0)
        dmat[1, h] = np.where(diff <= 0, np.exp(lg[1][h] * np.maximum(-diff, 0)), 0.0)
    oscale[0] = np.exp(lg[0][lane_head][None, :] * (i[:, None] + 1.0))
    oscale[1] = np.exp(lg[1][lane_head][None, :] * (c - i[:, None]))
    kscale[0] = np.exp(lg[0][lane_head][None, :] * (c - 1.0 - i[:, None]))
    kscale[1] = np.exp(lg[1][lane_head][None, :] * i[:, None])
    sdecay[0, 0] = np.exp(lg[0][lane_head] * c)
    sdecay[1, 0] = np.exp(lg[1][lane_head] * c)
    return (jnp.asarray(dmat), jnp.asarray(oscale), jnp.asarray(kscale), jnp.asarray(sdecay))


def _retention_kernel(qf, kf, vf, qb, kb, vb, dmat, oscale, kscale, sdecay, of_ref, ob_ref, state):
    @pl.when(pl.program_id(1) == 0)
    def _():
        state[...] = jnp.zeros_like(state)

    c = qf.shape[0]
    lane_head = _head_id((c, BR_W), 1)
    bd = _block_diag_mask(BR_W)
    for dirn, (q_ref, k_ref, v_ref, o_ref) in enumerate(((qf, kf, vf, of_ref), (qb, kb, vb, ob_ref))):
        q = q_ref[...]
        k = k_ref[...] * jnp.asarray(HEAD_DIM ** -0.5, BF16)
        v = v_ref[...]
        s_old = state[dirn]
        out = _dot(q, s_old.astype(BF16)) * oscale[dirn]
        for h in range(N_HEADS):
            qz = jnp.where(lane_head == h, q, jnp.zeros_like(q))
            a = (_nt(qz, k) * dmat[dirn, h]).astype(BF16)
            out = out + jnp.where(lane_head == h, _dot(a, v), 0.0)
        o_ref[...] = out
        kw = (k.astype(F32) * kscale[dirn]).astype(BF16)
        state[dirn] = s_old * sdecay[dirn] + jnp.where(bd, _tn(kw, v), 0.0)


def retention(p, batch, seq):
    c = RET_CHUNK
    nc = seq // c
    dmat, oscale, kscale, sdecay = _retention_tables(c)
    cb = COL_RET // BR_W

    def fwd(col):
        return pl.BlockSpec((c, BR_W), lambda b, j: (b * nc + j, col))

    def bwd(col):
        return pl.BlockSpec((c, BR_W), lambda b, j: (b * nc + nc - 1 - j, col))

    def whole(a):
        nd = a.ndim
        return pl.BlockSpec(a.shape, lambda b, j: (0,) * nd)

    n = batch * seq
    return pl.pallas_call(
        _retention_kernel,
        grid=(batch, nc),
        in_specs=[fwd(cb), fwd(cb + 1), fwd(cb + 2), bwd(cb), bwd(cb + 1), bwd(cb + 2),
                  whole(dmat), whole(oscale), whole(kscale), whole(sdecay)],
        out_specs=[pl.BlockSpec((c, BR_W), lambda b, j: (b * nc + j, 0)),
                   pl.BlockSpec((c, BR_W), lambda b, j: (b * nc + nc - 1 - j, 0))],
        out_shape=[jax.ShapeDtypeStruct((n, BR_W), F32)] * 2,
        scratch_shapes=[pltpu.VMEM((2, BR_W, BR_W), F32)],
        compiler_params=_cparams("parallel", "arbitrary"),
        name="retention",
    )(p, p, p, p, p, p, dmat, oscale, kscale, sdecay)


def _hgrn_kernel(lb_ref, qf, ff, vf, qb, fb, vb, of_ref, ob_ref, state, *, layer):
    c = qf.shape[0]
    sb = HGRN_SUB
    n_sub = c // sb

    @pl.when(pl.program_id(1) == 0)
    def _():
        state[...] = jnp.zeros_like(state)

    lb = lb_ref[...]
    e = jnp.exp(lb - jnp.max(lb, axis=0, keepdims=True))
    prob = e / jnp.sum(e, axis=0, keepdims=True)
    lower = jnp.sum(prob[:layer + 1], axis=0) - prob[0]

    row = lax.broadcasted_iota(jnp.int32, (c, c), 0)
    col = lax.broadcasted_iota(jnp.int32, (c, c), 1)
    bd = _block_diag_mask(BR_W)
    seg_ones = jnp.where(bd, 1.0, 0.0).astype(BF16)
    rowv = lax.broadcasted_iota(jnp.int32, (c, BR_W), 0)

    for dirn, (q_ref, f_ref, v_ref, o_ref) in enumerate(((qf, ff, vf, of_ref), (qb, fb, vb, ob_ref))):
        rev = dirn == 1
        lo = lower[dirn:dirn + 1, :]
        fpre = f_ref[...].astype(F32)
        logf = jnp.log(lo + (1.0 - lo) * _sigmoid(fpre))
        kk = (1.0 - lo) * _sigmoid(-fpre)
        qx = q_ref[...].astype(F32)
        q = qx * _sigmoid(qx)
        v_bf = v_ref[...]
        v = v_bf.astype(F32)
        tri = jnp.where((col >= row) if rev else (col <= row), 1.0, 0.0).astype(BF16)
        g = _tri_cumsum(tri, logf)
        g_end = g[0:1, :] if rev else g[c - 1:c, :]
        s_old = state[dirn]
        inter = _nt((q * jnp.exp(g)).astype(BF16), s_old.astype(BF16))
        kd = (kk * jnp.exp(g_end - g)).astype(BF16)
        state[dirn] = s_old * jnp.exp(g_end) + jnp.where(bd, _tn(v_bf, kd), 0.0)
        qs, ks_ = [], []
        for j in range(1, n_sub):
            if rev:
                edge = g[j * sb:j * sb + 1, :]
                key_rows = (rowv >= j * sb) & (rowv < (j + 1) * sb)
                query_rows = rowv < j * sb
            else:
                edge = g[j * sb - 1:j * sb, :]
                key_rows = (rowv >= (j - 1) * sb) & (rowv < j * sb)
                query_rows = rowv >= j * sb
            qs.append(jnp.where(query_rows, q * jnp.exp(jnp.minimum(g - edge, 0.0)), 0.0).astype(BF16))
            ks_.append(jnp.where(key_rows, kk * jnp.exp(jnp.minimum(edge - g, 0.0)), 0.0).astype(BF16))
        bd_all = jnp.concatenate([bd] * (n_sub - 1), axis=1)
        cross = jnp.where(bd_all, _tn(v_bf, jnp.concatenate(ks_, axis=1)), 0.0).astype(BF16)
        intra = _nt(jnp.concatenate(qs, axis=1), cross)
        for d in range(sb):
            shift = (c - d) % c if rev else d
            pair = ((rowv % sb) + d < sb) if rev else ((rowv % sb) >= d)
            k_d, g_d, v_d = (kk, g, v) if d == 0 else tuple(pltpu.roll(a, shift, 0) for a in (kk, g, v))
            term = jnp.where(pair, q, 0.0) * k_d * jnp.exp(jnp.minimum(g - g_d, 0.0))
            intra = intra + _dot(term.astype(BF16), seg_ones) * v_d
        o_ref[...] = inter + intra


def _tri_cumsum(tri_bf16, x):
    acc = None
    rem = x
    for _ in range(3):
        piece = rem.astype(BF16)
        part = _dot(tri_bf16, piece)
        acc = part if acc is None else acc + part
        rem = rem - piece.astype(F32)
    return acc


def hgrn(p, hgrn_lb, layer, batch, seq):
    c = HGRN_CHUNK
    nc = seq // c
    cb = COL_HGRN // BR_W

    def fwd(col):
        return pl.BlockSpec((c, BR_W), lambda b, j: (b * nc + j, col))

    def bwd(col):
        return pl.BlockSpec((c, BR_W), lambda b, j: (b * nc + nc - 1 - j, col))

    n = batch * seq
    return pl.pallas_call(
        functools.partial(_hgrn_kernel, layer=layer),
        grid=(batch, nc),
        in_specs=[pl.BlockSpec(hgrn_lb.shape, lambda b, j: (0, 0, 0)),
                  fwd(cb), fwd(cb + 1), fwd(cb + 3), bwd(cb), bwd(cb + 2), bwd(cb + 3)],
        out_specs=[pl.BlockSpec((c, BR_W), lambda b, j: (b * nc + j, 0)),
                   pl.BlockSpec((c, BR_W), lambda b, j: (b * nc + nc - 1 - j, 0))],
        out_shape=[jax.ShapeDtypeStruct((n, BR_W), F32)] * 2,
        scratch_shapes=[pltpu.VMEM((2, BR_W, BR_W), F32)],
        compiler_params=_cparams("parallel", "arbitrary"),
        name="hgrn",
    )(hgrn_lb, p, p, p, p, p, p)


def _nat_bias_table(rpb):
    cq = np.arange(GRID_W)
    ck = np.arange(GRID_W)
    col_start = np.clip(cq - NAT_KW // 2, 0, GRID_W - NAT_KW)
    col_mask = (ck[None, :] >= col_start[:, None]) & (ck[None, :] < col_start[:, None] + NAT_KW)
    dc = np.clip(ck[None, :] - cq[:, None], -(NAT_KW - 1), NAT_KW - 1) + (NAT_KW - 1)
    pick = (dc[None, :, :] == np.arange(2 * NAT_KW - 1)[:, None, None]).astype(np.float32)
    tiles = jnp.einsum('hrd,dqk->hrqk', rpb.astype(F32), jnp.asarray(pick),
                       precision=lax.Precision.HIGHEST)
    tiles = jnp.where(jnp.asarray(col_mask)[None, None], tiles, MASK_VALUE)
    t = jnp.stack([tiles[:, b:b + NAT_KH] for b in range(NAT_KH)], axis=0)
    return t.transpose(0, 1, 3, 2, 4).reshape(NAT_KH, N_HEADS * GRID_W, NAT_KH * GRID_W)


def _nat_kernel(q_ref, k_ref, v_ref, bias_ref, o_ref, *, rows_per_step, n_rows):
    j = pl.program_id(1)
    lane_head = _head_id((GRID_W, BR_W), 1)
    win = NAT_KH * GRID_W

    def body(i, carry):
        r = j * rows_per_step + i
        rs = jnp.clip(r - NAT_KH // 2, 0, n_rows - NAT_KH)
        base = rs - r + (NAT_KH - 1)
        q = q_ref[pl.ds(pl.multiple_of(i * GRID_W, GRID_W), GRID_W), :] * jnp.asarray(HEAD_DIM ** -0.5, BF16)
        kw = k_ref[pl.ds(pl.multiple_of(rs * GRID_W, GRID_W), win), :]
        vw = v_ref[pl.ds(pl.multiple_of(rs * GRID_W, GRID_W), win), :]
        q_heads = jnp.concatenate([jnp.where(lane_head == h, q, jnp.zeros_like(q)) for h in range(N_HEADS)], axis=0)
        s = _nt(q_heads, kw) + bias_ref[base]
        m = jnp.max(s, axis=-1, keepdims=True)
        pr = jnp.exp(s - m)
        l = jnp.sum(pr, axis=-1, keepdims=True)
        o = _dot(pr.astype(BF16), vw) / l
        out = jnp.zeros((GRID_W, BR_W), F32)
        for h in range(N_HEADS):
            out = out + jnp.where(lane_head == h, o[h * GRID_W:(h + 1) * GRID_W], 0.0)
        o_ref[pl.ds(pl.multiple_of(i * GRID_W, GRID_W), GRID_W), :] = out.astype(o_ref.dtype)
        return carry

    lax.fori_loop(0, rows_per_step, body, 0, unroll=True)


def neighbourhood_attention(p, rpb, batch, seq):
    n_rows = seq // GRID_W
    rps = NAT_ROWS_PER_STEP
    steps = n_rows // rps
    bias = _nat_bias_table(rpb)
    cb = COL_NAT // BR_W
    tq = rps * GRID_W
    return pl.pallas_call(
        functools.partial(_nat_kernel, rows_per_step=rps, n_rows=n_rows),
        grid=(batch, steps),
        in_specs=[pl.BlockSpec((tq, BR_W), lambda b, j: (b * steps + j, cb)),
                  pl.BlockSpec((seq, BR_W), lambda b, j: (b, cb + 1)),
                  pl.BlockSpec((seq, BR_W), lambda b, j: (b, cb + 2)),
                  pl.BlockSpec(bias.shape, lambda b, j: (0, 0, 0))],
        out_specs=pl.BlockSpec((tq, BR_W), lambda b, j: (b * steps + j, 0)),
        out_shape=jax.ShapeDtypeStruct((batch * seq, BR_W), BF16),
        compiler_params=_cparams("parallel", "arbitrary"),
        name="nat",
    )(p, p, p, bias)


def _diff_kernel(lam_ref, slope_ref, sub_ref, q_ref, k_ref, v_ref, o_ref,
                 acc1, acc2, m1, l1, m2, l2, hill, stage_a1, stage_a2, stage_b1, stage_b2, *, lam_init, n_tiles):
    t = q_ref.shape[0]
    reps = t // 128
    qi = pl.program_id(2)
    slope2 = slope_ref[0, 0:1, 0:1] * LOG2E
    lp = lam_ref[...]
    lam = (jnp.exp(jnp.sum(lp[0:1] * lp[1:2], axis=-1, keepdims=True))
           - jnp.exp(jnp.sum(lp[2:3] * lp[3:4], axis=-1, keepdims=True)) + lam_init)

    @pl.when(qi == 0)
    def _():
        rel = (lax.broadcasted_iota(jnp.int32, (t, t), 1) - lax.broadcasted_iota(jnp.int32, (t, t), 0)).astype(F32)
        hill[...] = -slope2 * jnp.abs(rel)

    q = (q_ref[...].astype(F32) * (DIFF_SCALE * LOG2E)).astype(BF16)
    lane = lax.broadcasted_iota(jnp.int32, q.shape, 1)
    q1 = jnp.where(lane < 64, q, jnp.zeros_like(q))
    q2 = jnp.where(lane >= 64, q, jnp.zeros_like(q))

    acc1[...] = jnp.zeros_like(acc1)
    acc2[...] = jnp.zeros_like(acc2)
    l1[...] = jnp.zeros_like(l1)
    l2[...] = jnp.zeros_like(l2)
    m1[...] = jnp.full_like(m1, -jnp.inf)
    m2[...] = jnp.full_like(m2, -jnp.inf)

    def scores(kj):
        k = k_ref[pl.ds(pl.multiple_of(kj * t, t), t), :]
        return _nt(q1, k), _nt(q2, k)

    def absorb(kj, raw_scores, add_local_bias, tile_bias):
        v = v_ref[pl.ds(pl.multiple_of(kj * t, t), t), :]
        for s_raw, acc, m_ref, l_ref in zip(raw_scores, (acc1, acc2), (m1, m2), (l1, l2)):
            s = add_local_bias(s_raw)
            m_old = m_ref[...]
            m_new = jnp.maximum(m_old, jnp.max(s, axis=-1, keepdims=True) + tile_bias)
            a = jnp.exp2(m_old - m_new)
            pr = jnp.exp2(s - jnp.concatenate([m_new - tile_bias] * reps, axis=1))
            l_ref[...] = a * l_ref[...] + jnp.sum(pr, axis=-1, keepdims=True)
            acc[...] = a * acc[...] + _dot(pr.astype(BF16), v)
            m_ref[...] = m_new

    diagonal = hill[...]
    absorb(qi, scores(qi), lambda s: s + diagonal, jnp.zeros((1, 1), F32))

    stage = ((stage_a1, stage_a2), (stage_b1, stage_b2))
    query_term = slope2 * lax.broadcasted_iota(jnp.int32, (t, 128), 0).astype(F32)
    others = [jnp.where(u < qi, u, u + 1) for u in range(n_tiles - 1)]
    for u, kj in enumerate(others):
        cur, nxt = stage[u % 2], stage[(u + 1) % 2]
        if u == 0:
            for ref, val in zip(cur, scores(kj)):
                ref[...] = val
        if u + 1 < len(others):
            for ref, val in zip(nxt, scores(others[u + 1])):
                ref[...] = val
        sign = jnp.where(jnp.full((1, 1), kj, jnp.int32) < qi, 1.0, -1.0)
        keys = (sign * slope2) * ((kj - qi) * t + lax.broadcasted_iota(jnp.int32, (1, t), 1)).astype(F32)
        absorb(kj, (cur[0][...], cur[1][...]), lambda s, keys=keys: s + keys, -sign * query_term)

    o = acc1[...] / l1[...] - lam * (acc2[...] / l2[...])
    o = o * lax.rsqrt(jnp.mean(o * o, axis=-1, keepdims=True) + EPS) * sub_ref[0]
    o_ref[...] = (o * (1.0 - lam_init)).astype(o_ref.dtype)


def diff_attention(p, lam_params, subln, layer, batch, seq):
    t = DIFF_TILE
    nt = seq // t
    lam_init = 0.8 - 0.6 * math.exp(-0.3 * layer)
    slopes = np.exp2(-8.0 * (np.arange(N_HEADS, dtype=np.float64) + 1.0) / N_HEADS)
    slope_tab = jnp.asarray(np.broadcast_to(slopes[:, None, None], (N_HEADS, 8, 128)).astype(np.float32))
    qc, kc, vc = (COL_DIFF // DIFF_DV, (COL_DIFF + DIFF_W) // DIFF_DV, (COL_DIFF + 2 * DIFF_W) // DIFF_DV)
    return pl.pallas_call(
        functools.partial(_diff_kernel, lam_init=lam_init, n_tiles=nt),
        grid=(batch, N_HEADS, nt),
        in_specs=[pl.BlockSpec(lam_params.shape, lambda b, h, i: (0, 0)),
                  pl.BlockSpec((1, 8, 128), lambda b, h, i: (h, 0, 0)),
                  pl.BlockSpec((1, 1, DIFF_DV), lambda b, h, i: (h, 0, 0)),
                  pl.BlockSpec((t, DIFF_DV), lambda b, h, i: (b * nt + i, qc + h)),
                  pl.BlockSpec((seq, DIFF_DV), lambda b, h, i: (b, kc + h)),
                  pl.BlockSpec((seq, DIFF_DV), lambda b, h, i: (b, vc + h))],
        out_specs=pl.BlockSpec((t, DIFF_DV), lambda b, h, i: (b * nt + i, h)),
        out_shape=jax.ShapeDtypeStruct((batch * seq, DIFF_W), BF16),
        scratch_shapes=[pltpu.VMEM((t, DIFF_DV), F32)] * 6 + [pltpu.VMEM((t, t), F32)] * 5,
        compiler_params=_cparams("parallel", "parallel", "arbitrary"),
        name="diff_attn",
    )(lam_params, slope_tab, subln.reshape(N_HEADS, 1, DIFF_DV), p, p, p)


def _merge_kernel(hf_ref, hb_ref, rof, rob, rg, ynat, hof, hob, hg, ydiff,
                  wg_ref, bg_ref, wr_ref, wn_ref, wh_ref, wd_ref, wo_ref,
                  rgn_ref, hgn_ref, lng_ref, lnb_ref, of_ref, ob_ref):
    seg_mean = jnp.where(_block_diag_mask(BR_W), 1.0 / HEAD_DIM, 0.0).astype(BF16)

    def head_norm_gate(o, gn, gate):
        ms = _dot_split(o * o, seg_mean, terms=2)
        gx = gate.astype(F32)
        return (o * lax.rsqrt(ms + EPS) * gn * (gx * _sigmoid(gx))).astype(BF16)

    y_ret = head_norm_gate(rof[...] + rob[...], rgn_ref[...], rg[...])
    y_hgrn = head_norm_gate(hof[...] + hob[...], hgn_ref[...], hg[...])
    hb = hb_ref[...]
    d = hf_ref.shape[1]
    merged = None
    for i, (y, w_ref) in enumerate(((y_ret, wr_ref), (ynat[...], wn_ref), (y_hgrn, wh_ref), (ydiff[...], wd_ref))):
        gate = _sigmoid(_dot(hb, wg_ref[:, i * d:(i + 1) * d]) + bg_ref[:, i * d:(i + 1) * d])
        part = gate * _dot(y, w_ref[...])
        merged = part if merged is None else merged + part
    z = ALPHA * hf_ref[...] + _dot(merged.astype(BF16), wo_ref[...])
    h = _layer_norm(z, lng_ref[...], lnb_ref[...])
    of_ref[...] = h
    ob_ref[...] = h.astype(BF16)


def merge_and_norm(hf, hb, p, ret_f, ret_b, y_nat, hg_f, hg_b, y_diff,
                   w_gate, b_gate, w_br_ret, w_br_nat, w_br_hgrn, w_br_diff, w_out,
                   ret_gn, hgrn_gn, ln_g, ln_b, tm=512):
    n, d = hf.shape

    def rows(width, col=0):
        return pl.BlockSpec((tm, width), lambda i: (i, col))

    def whole(a):
        nd = a.ndim
        return pl.BlockSpec(a.shape, lambda i: (0,) * nd)

    consts = [w_gate, b_gate.reshape(1, -1), w_br_ret, w_br_nat, w_br_hgrn, w_br_diff, w_out,
              ret_gn.reshape(1, BR_W), hgrn_gn.reshape(1, BR_W), ln_g.reshape(1, d), ln_b.reshape(1, d)]
    return pl.pallas_call(
        _merge_kernel,
        grid=(n // tm,),
        in_specs=[rows(d), rows(d),
                  rows(BR_W), rows(BR_W), rows(BR_W, COL_RET // BR_W + 3),
                  rows(BR_W),
                  rows(BR_W), rows(BR_W), rows(BR_W, COL_HGRN // BR_W + 4),
                  rows(DIFF_W)] + [whole(a) for a in consts],
        out_specs=[rows(d), rows(d)],
        out_shape=[jax.ShapeDtypeStruct((n, d), F32), jax.ShapeDtypeStruct((n, d), BF16)],
        compiler_params=_cparams("parallel"),
        name="merge",
    )(hf, hb, ret_f, ret_b, p, y_nat, hg_f, hg_b, p, y_diff, *consts)


def _router_kernel(h_ref, w_ref, bias_ref, idx_ref, w_ref8, cnt_ref):
    w = w_ref[...]
    w_hi = w.astype(BF16)
    w_lo = (w - w_hi.astype(F32)).astype(BF16)
    h = h_ref[...]
    logits = _dot_split(h, w_hi, terms=3) + _dot_split(h, w_lo, terms=2)
    scores = _sigmoid(logits)
    biased = scores + bias_ref[...]
    shape = scores.shape
    lane = lax.broadcasted_iota(jnp.int32, shape, 1).astype(F32)
    group = lax.shift_right_logical(lax.broadcasted_iota(jnp.int32, shape, 1), 3).astype(F32)
    big = jnp.asarray(1e9, F32)
    neg = jnp.asarray(-jnp.inf, F32)

    def first_max(x, ids):
        m = jnp.max(x, axis=-1, keepdims=True)
        return m, jnp.min(jnp.where(x == m, ids, big), axis=-1, keepdims=True)

    gscore = jnp.zeros(shape, F32)
    for g in range(N_GROUPS):
        xg = jnp.where(group == g, biased, neg)
        m1, i1 = first_max(xg, lane)
        m2 = jnp.max(jnp.where(lane == i1, neg, xg), axis=-1, keepdims=True)
        gscore = jnp.where(group == g, m1 + m2, gscore)
    gsel = jnp.zeros(shape, jnp.bool_)
    for _ in range(TOPK_GROUPS):
        _, gi = first_max(gscore, group)
        hit = group == gi
        gsel = gsel | hit
        gscore = jnp.where(hit, neg, gscore)
    cand = jnp.where(gsel, biased, MASK_VALUE)
    esel = jnp.zeros(shape, jnp.bool_)
    picks = []
    for _ in range(TOP_K):
        _, ei = first_max(cand, lane)
        hit = lane == ei
        picks.append((ei, jnp.sum(jnp.where(hit, scores, 0.0), axis=-1, keepdims=True)))
        esel = esel | hit
        cand = jnp.where(hit, neg, cand)

    tm = shape[0]
    sel = jnp.where(esel, 1.0, 0.0)
    cnt_ref[...] = jnp.concatenate(
        [jnp.sum(sel[s * ROUTE_TILE:(s + 1) * ROUTE_TILE], axis=0, keepdims=True) for s in range(tm // ROUTE_TILE)],
        axis=0)

    slot = lax.broadcasted_iota(jnp.int32, (tm, TOP_K), 1)
    idx8 = jnp.zeros((tm, TOP_K), F32)
    w8 = jnp.zeros((tm, TOP_K), F32)
    wsum = jnp.zeros((tm, 1), F32)
    for k, (ei, wk) in enumerate(picks):
        idx8 = jnp.where(slot == k, ei, idx8)
        w8 = jnp.where(slot == k, wk, w8)
        wsum = wsum + wk
    idx_ref[...] = idx8.astype(jnp.int32)
    w_ref8[...] = w8 / (wsum + 1e-20) * ROUTED_SCALE


def router(hf, w_router, router_bias, tm=1024):
    n, d = hf.shape
    e = w_router.shape[1]
    sub = tm // ROUTE_TILE
    return pl.pallas_call(
        _router_kernel,
        grid=(n // tm,),
        in_specs=[pl.BlockSpec((tm, d), lambda i: (i, 0)),
                  pl.BlockSpec((d, e), lambda i: (0, 0)),
                  pl.BlockSpec((1, e), lambda i: (0, 0))],
        out_specs=[pl.BlockSpec((tm, TOP_K), lambda i: (i, 0)),
                   pl.BlockSpec((tm, TOP_K), lambda i: (i, 0)),
                   pl.BlockSpec((sub, e), lambda i: (i, 0))],
        out_shape=[jax.ShapeDtypeStruct((n, TOP_K), jnp.int32),
                   jax.ShapeDtypeStruct((n, TOP_K), F32),
                   jax.ShapeDtypeStruct((n // ROUTE_TILE, e), F32)],
        compiler_params=_cparams("arbitrary"),
        name="router",
    )(hf, w_router, router_bias.reshape(1, e))


HIGH_HALF = -65536


def _pack_pairs(x):
    w = x.shape[1] // 2
    lo = lax.bitcast_convert_type(x[:, :w].astype(BF16).astype(F32), jnp.int32)
    hi = lax.bitcast_convert_type(x[:, w:].astype(BF16).astype(F32), jnp.int32)
    return lax.shift_right_logical(lo, 16) | (hi & HIGH_HALF)


def _unpack_pairs(p):
    lo = lax.bitcast_convert_type(lax.shift_left(p, 16), F32)
    hi = lax.bitcast_convert_type(p & HIGH_HALF, F32)
    return jnp.concatenate([lo, hi], axis=1)


def _swiglu(x, wg, wu):
    a = _dot(x, wg)
    return a * _sigmoid(a) * _dot(x, wu)


def _segment_copies(tab_ref, first_col, max_run, make_copy, action):
    top_bit = max_run.bit_length() - 1
    low_bit = RUN_ALIGN.bit_length() - 1

    def per_expert(e, carry):
        col = first_col + e
        count = tab_ref[0, col]
        local = tab_ref[1, col]
        glob = tab_ref[2, col]
        for b in range(top_bit, low_bit - 1, -1):
            size = 1 << b
            taken = count & size

            @pl.when(taken != 0)
            def _(local=local, glob=glob, size=size):
                action(make_copy(pl.multiple_of(local, RUN_ALIGN), pl.multiple_of(glob, RUN_ALIGN), size))

            local = local + taken
            glob = glob + taken
        return carry

    lax.fori_loop(0, N_EXPERTS, per_expert, 0)


def _tile_rows_used(tab_ref, t):
    last = t * N_EXPERTS + (N_EXPERTS - 1)
    return tab_ref[1, last] + tab_ref[0, last]


def _wait_rows(total, max_rows, make_copy):
    for b in range(max_rows.bit_length() - 1, RUN_ALIGN.bit_length() - 2, -1):
        size = 1 << b

        @pl.when((total & size) != 0)
        def _(size=size):
            make_copy(0, 0, size).wait()


def _block_dispatch_kernel(tab_ref, gap_ref, tail_ref, idx_t_ref, hb_ref, xs_ref, sorted_buf, zeros_buf, sem):
    t = pl.program_id(0)
    tm = hb_ref.shape[0]
    rows = sorted_buf.shape[0]
    idx_t = idx_t_ref[...]
    expert = lax.broadcasted_iota(jnp.int32, (N_EXPERTS, tm), 0)
    sel_t = jnp.zeros((N_EXPERTS, tm), F32)
    for k in range(TOP_K):
        sel_t = sel_t + jnp.where(idx_t[k:k + 1, :] == expert, 1.0, 0.0)
    sel_b = sel_t.astype(BF16)
    before_tok = (lax.broadcasted_iota(jnp.int32, (tm, tm), 0) < lax.broadcasted_iota(jnp.int32, (tm, tm), 1))
    before_exp = (lax.broadcasted_iota(jnp.int32, (N_EXPERTS, N_EXPERTS), 1)
                  < lax.broadcasted_iota(jnp.int32, (N_EXPERTS, N_EXPERTS), 0))
    run = jnp.broadcast_to(_run_length(jnp.sum(sel_t, axis=1, keepdims=True)), (N_EXPERTS, tm)).astype(BF16)
    base = (_dot(jnp.where(before_exp, 1.0, 0.0).astype(BF16), run)
            + _dot(sel_b, jnp.where(before_tok, 1.0, 0.0).astype(BF16)))
    row = lax.broadcasted_iota(jnp.int32, (rows, tm), 0)
    perm = jnp.zeros((rows, tm), F32)
    for k in range(TOP_K):
        place = jnp.sum(jnp.where(idx_t[k:k + 1, :] == expert, base, 0.0), axis=0, keepdims=True).astype(jnp.int32)
        perm = jnp.where(row == place, 1.0, perm)
    perm = perm.astype(BF16)
    half = hb_ref.shape[1] // 2
    lo = lax.bitcast_convert_type(_dot(perm, hb_ref[:, :half]), jnp.int32)
    hi = lax.bitcast_convert_type(_dot(perm, hb_ref[:, half:]), jnp.int32)
    sorted_buf[...] = lax.shift_right_logical(lo, 16) | (hi & HIGH_HALF)

    def make_copy(local, glob, size):
        return pltpu.make_async_copy(sorted_buf.at[pl.ds(local, size)], xs_ref.at[pl.ds(glob, size)], sem.at[0])

    _segment_copies(tab_ref, t * N_EXPERTS, tm, make_copy, lambda c: c.start())
    _wait_rows(_tile_rows_used(tab_ref, t), rows, make_copy)

    @pl.when(t == pl.num_programs(0) - 1)
    def _():
        zeros_buf[...] = jnp.zeros_like(zeros_buf)
        tile_rows = zeros_buf.shape[0]

        def zero_copy(local, glob, size):
            return pltpu.make_async_copy(zeros_buf.at[pl.ds(0, size)], xs_ref.at[pl.ds(glob, size)], sem.at[0])

        def tail_copy(j):
            return zero_copy(0, pl.multiple_of(tail_ref[0] + j * tile_rows, RUN_ALIGN), tile_rows)

        _segment_copies(gap_ref, 0, tile_rows // 2, zero_copy, lambda c: c.start())
        lax.fori_loop(0, tail_ref[1], lambda j, c: (tail_copy(j).start(), c)[1], 0)
        _segment_copies(gap_ref, 0, tile_rows // 2, zero_copy, lambda c: c.wait())
        lax.fori_loop(0, tail_ref[1], lambda j, c: (tail_copy(j).wait(), c)[1], 0)


def block_dispatch(hb, idx_t, tab, gaps, tail, tm):
    n, d = hb.shape
    grid_spec = pltpu.PrefetchScalarGridSpec(
        num_scalar_prefetch=3,
        grid=(n // tm,),
        in_specs=[pl.BlockSpec((TOP_K, tm), lambda i, *_: (0, i)),
                  pl.BlockSpec((tm, d), lambda i, *_: (i, 0))],
        out_specs=pl.BlockSpec(memory_space=pl.ANY),
        scratch_shapes=[pltpu.VMEM((_tile_rows(tm), d // 2), jnp.int32),
                        pltpu.VMEM((EXPERT_ROWS, d // 2), jnp.int32),
                        pltpu.SemaphoreType.DMA((1,))],
    )
    return pl.pallas_call(
        _block_dispatch_kernel,
        grid_spec=grid_spec,
        out_shape=jax.ShapeDtypeStruct((_sorted_rows(n, tm), d // 2), jnp.int32),
        compiler_params=_cparams("arbitrary"),
        name="dispatch",
    )(tab, gaps, tail, idx_t, hb)


def _tile_tables(tile_counts, tm_fine, tm):
    g = tm // tm_fine
    n_tokens = tile_counts.shape[0] * tm_fine
    n_tiles = _sorted_rows(n_tokens, tm) // EXPERT_ROWS
    cnt = tile_counts.astype(jnp.int32).reshape(-1, g, N_EXPERTS).sum(axis=1)
    cnt = (cnt + (RUN_ALIGN - 1)) // RUN_ALIGN * RUN_ALIGN
    local = jnp.cumsum(cnt, axis=1) - cnt
    used = cnt.sum(axis=0)
    seg = (used + (EXPERT_ROWS - 1)) // EXPERT_ROWS * EXPERT_ROWS
    seg_end = jnp.cumsum(seg)
    seg_start = seg_end - seg
    glob = seg_start[None, :] + jnp.cumsum(cnt, axis=0) - cnt
    runs = jnp.stack([cnt.reshape(-1), local.reshape(-1), glob.reshape(-1)])
    gaps = jnp.stack([seg - used, jnp.zeros_like(used), seg_start + used])
    tail = jnp.stack([seg_end[-1], n_tiles - seg_end[-1] // EXPERT_ROWS])
    tile_row = jnp.arange(n_tiles, dtype=jnp.int32) * EXPERT_ROWS
    owner = jnp.sum((seg_end[None, :] <= tile_row[:, None]).astype(jnp.int32), axis=1)
    owner_c = jnp.minimum(owner, N_EXPERTS - 1)
    of_owner = lambda a: jnp.sum(jnp.where(jnp.arange(N_EXPERTS)[None, :] == owner_c[:, None], a[None, :], 0), axis=1)
    filled = jnp.clip(of_owner(seg_start + used) - tile_row, 0, EXPERT_ROWS)
    tiles = jnp.stack([owner_c, jnp.where(owner < N_EXPERTS, filled, 0)])
    return runs, gaps, tail, tiles.astype(jnp.int32)


def _expert_kernel(tile_ref, xs_ref, wg_ref, wu_ref, wd_ref, ys_ref, wgb, wub, wdb):
    i = pl.program_id(0)
    expert = tile_ref[0, i]

    @pl.when(tile_ref[1, i] > 0)
    def _():
        @pl.when((i == 0) | (expert != tile_ref[0, jnp.maximum(i - 1, 0)]))
        def _():
            wgb[...] = wg_ref[0, 0].astype(BF16)
            wub[...] = wu_ref[0, 0].astype(BF16)
            wdb[...] = wd_ref[0, 0].astype(BF16)

        x = _unpack_pairs(xs_ref[...]).astype(BF16)
        hid = _swiglu(x, wgb[...], wub[...])
        ys_ref[...] = _pack_pairs(_dot(hid.astype(BF16), wdb[...]))

    @pl.when(tile_ref[1, i] == 0)
    def _():
        ys_ref[...] = jnp.zeros_like(ys_ref)


def grouped_experts(xs, tiles, w_e_gate, w_e_up, w_e_down, layer):
    rows, w = xs.shape
    _, _, d, hid = w_e_gate.shape
    tr = EXPERT_ROWS
    grid_spec = pltpu.PrefetchScalarGridSpec(
        num_scalar_prefetch=1,
        grid=(rows // tr,),
        in_specs=[pl.BlockSpec((tr, w), lambda i, tl: (i, 0)),
                  pl.BlockSpec((1, 1, d, hid), lambda i, tl: (layer, tl[0, i], 0, 0)),
                  pl.BlockSpec((1, 1, d, hid), lambda i, tl: (layer, tl[0, i], 0, 0)),
                  pl.BlockSpec((1, 1, hid, d), lambda i, tl: (layer, tl[0, i], 0, 0))],
        out_specs=pl.BlockSpec((tr, w), lambda i, tl: (i, 0)),
        scratch_shapes=[pltpu.VMEM((d, hid), BF16), pltpu.VMEM((d, hid), BF16), pltpu.VMEM((hid, d), BF16)],
    )
    return pl.pallas_call(
        _expert_kernel,
        grid_spec=grid_spec,
        out_shape=jax.ShapeDtypeStruct((rows, w), jnp.int32),
        compiler_params=_cparams("arbitrary"),
        name="experts",
    )(tiles, xs, w_e_gate, w_e_up, w_e_down)


def _block_combine_kernel(tab_ref, ys_ref, idx_ref, w8_ref, hf_ref, hb_ref, sg_ref, su_ref, sd_ref,
                          lng_ref, lnb_ref, of_ref, ob_ref, buf, sem):
    t = pl.program_id(0)
    tm = hf_ref.shape[0]
    rows = buf.shape[0]

    def make_copy(local, glob, size):
        return pltpu.make_async_copy(ys_ref.at[pl.ds(glob, size)], buf.at[pl.ds(local, size)], sem.at[0])

    _segment_copies(tab_ref, t * N_EXPERTS, tm, make_copy, lambda c: c.start())

    idx = idx_ref[...]
    w8 = w8_ref[...]
    expert = lax.broadcasted_iota(jnp.int32, (tm, N_EXPERTS), 1)
    sel = jnp.zeros((tm, N_EXPERTS), F32)
    for k in range(TOP_K):
        sel = sel + jnp.where(idx[:, k:k + 1] == expert, 1.0, 0.0)
    sel_b = sel.astype(BF16)
    before_tok = (lax.broadcasted_iota(jnp.int32, (tm, tm), 1) < lax.broadcasted_iota(jnp.int32, (tm, tm), 0))
    before_exp = (lax.broadcasted_iota(jnp.int32, (N_EXPERTS, N_EXPERTS), 0)
                  < lax.broadcasted_iota(jnp.int32, (N_EXPERTS, N_EXPERTS), 1))
    run = jnp.broadcast_to(_run_length(jnp.sum(sel, axis=0, keepdims=True)), (tm, N_EXPERTS)).astype(BF16)
    base = (_dot(run, jnp.where(before_exp, 1.0, 0.0).astype(BF16))
            + _dot(jnp.where(before_tok, 1.0, 0.0).astype(BF16), sel_b))
    col = lax.broadcasted_iota(jnp.int32, (tm, rows), 1)
    mix = jnp.zeros((tm, rows), F32)
    for k in range(TOP_K):
        place = jnp.sum(jnp.where(idx[:, k:k + 1] == expert, base, 0.0), axis=1, keepdims=True).astype(jnp.int32)
        mix = jnp.where(col == place, w8[:, k:k + 1], mix)
    mix = mix.astype(BF16)

    acc = _dot(_swiglu(hb_ref[...], sg_ref[...], su_ref[...]).astype(BF16), sd_ref[...])
    used = _tile_rows_used(tab_ref, t)
    _wait_rows(used, rows, make_copy)
    filled = lax.broadcasted_iota(jnp.int32, buf.shape, 0) < used
    y = _unpack_pairs(jnp.where(filled, buf[...], 0)).astype(BF16)
    acc = acc + _dot(mix, y)
    h = _layer_norm(ALPHA * hf_ref[...] + acc, lng_ref[...], lnb_ref[...])
    of_ref[...] = h
    ob_ref[...] = h.astype(BF16)


def block_combine(ys, tab, idx8, w8, hf, hb, w_s_gate, w_s_up, w_s_down, ln_g, ln_b, tm):
    n, d = hf.shape
    w = ys.shape[1]

    def whole(a):
        return pl.BlockSpec(a.shape, lambda i, tab: (0, 0))

    def rows(width):
        return pl.BlockSpec((tm, width), lambda i, tab: (i, 0))

    consts = [w_s_gate, w_s_up, w_s_down, ln_g.reshape(1, d), ln_b.reshape(1, d)]
    grid_spec = pltpu.PrefetchScalarGridSpec(
        num_scalar_prefetch=1,
        grid=(n // tm,),
        in_specs=[pl.BlockSpec(memory_space=pl.ANY), rows(TOP_K), rows(TOP_K), rows(d), rows(d)]
                 + [whole(a) for a in consts],
        out_specs=[rows(d), rows(d)],
        scratch_shapes=[pltpu.VMEM((_tile_rows(tm), w), jnp.int32), pltpu.SemaphoreType.DMA((1,))],
    )
    return pl.pallas_call(
        _block_combine_kernel,
        grid_spec=grid_spec,
        out_shape=[jax.ShapeDtypeStruct((n, d), F32), jax.ShapeDtypeStruct((n, d), BF16)],
        compiler_params=_cparams("arbitrary"),
        name="combine",
    )(tab, ys, idx8, w8, hf, hb, *consts)


def moe_and_norm(hf, hb, w_router, router_bias, w_e_gate, w_e_up, w_e_down,
                 w_s_gate, w_s_up, w_s_down, ln_g, ln_b, layer):
    idx8, w8, tile_counts = router(hf, w_router, router_bias)
    runs, gaps, tail, tiles = _tile_tables(tile_counts, ROUTE_TILE, MOE_TILE)
    xs = block_dispatch(hb, idx8.T, runs, gaps, tail, MOE_TILE)
    ys = grouped_experts(xs, tiles, w_e_gate, w_e_up, w_e_down, layer)
    return block_combine(ys, runs, idx8, w8, hf, hb, w_s_gate, w_s_up, w_s_down, ln_g, ln_b, MOE_TILE)


def kernel(x, ln_in_g, ln_in_b, w_in, w_gate, b_gate, w_br_ret, w_br_nat, w_br_hgrn, w_br_diff, w_out,
           ret_gn, nat_rpb, hgrn_lb, hgrn_gn, diff_lambda, diff_subln, ln1_g, ln1_b, w_router,
           router_bias, w_e_gate, w_e_up, w_e_down, w_s_gate, w_s_up, w_s_down, ln2_g, ln2_b):
    batch, seq, d = x.shape
    bf = lambda a: a.astype(BF16)
    hf, hb = input_layer_norm(x.reshape(batch * seq, d), ln_in_g, ln_in_b)
    for l in range(DEPTH):
        p = in_projection(hb, bf(w_in[l]))
        ret_f, ret_b = retention(p, batch, seq)
        y_nat = neighbourhood_attention(p, nat_rpb[l], batch, seq)
        hg_f, hg_b = hgrn(p, hgrn_lb, l, batch, seq)
        y_diff = diff_attention(p, diff_lambda[l], diff_subln[l], l, batch, seq)
        hf, hb = merge_and_norm(hf, hb, p, ret_f, ret_b, y_nat, hg_f, hg_b, y_diff,
                                bf(w_gate[l]), b_gate[l], bf(w_br_ret[l]), bf(w_br_nat[l]),
                                bf(w_br_hgrn[l]), bf(w_br_diff[l]), bf(w_out[l]),
                                ret_gn[l], hgrn_gn[l], ln1_g[l], ln1_b[l])
        hf, hb = moe_and_norm(hf, hb, w_router[l], router_bias[l], w_e_gate, w_e_up, w_e_down,
                              bf(w_s_gate[l]), bf(w_s_up[l]), bf(w_s_down[l]), ln2_g[l], ln2_b[l], l)
    return hf.reshape(batch, seq, d)
```

```python
import functools
import math

import numpy as np
import jax
import jax.numpy as jnp
from jax import lax
from jax.experimental import pallas as pl
from jax.experimental.pallas import tpu as pltpu

F32 = jnp.float32
BF16 = jnp.bfloat16

D_MODEL = 1024
DEPTH = 2
GRID_W = 64
HEAD_DIM = 64
N_HEADS = 4
BR_W = N_HEADS * HEAD_DIM
NAT_KH = 8
NAT_KW = 16
DIFF_DV = 128
DIFF_W = N_HEADS * DIFF_DV
N_EXPERTS = 64
TOP_K = 8
N_GROUPS = 8
TOPK_GROUPS = 4
GROUP_SIZE = N_EXPERTS // N_GROUPS
EXPERT_HIDDEN = 256
ROUTED_SCALE = 2.5
EPS = 1e-5
MASK_VALUE = -1e30
LOG2E = math.log2(math.e)
DIFF_SCALE = 64 ** -0.5
ALPHA = (2.0 * DEPTH) ** 0.25
IN_WIDTH = 4608
COL_RET = 0
COL_NAT = 1024
COL_HGRN = 1792
COL_DIFF = 3072

VMEM_LIMIT = 56 * 1024 * 1024

RET_CHUNK = 256
HGRN_CHUNK = 64
HGRN_SUB = 16
DIFF_TILE = 512
NAT_ROWS_PER_STEP = 8
ROUTE_TILE = 128
MOE_TILE = 256
EXPERT_ROWS = 512
RUN_ALIGN = 8
LONG_RUN = 64


def _run_length(count):
    return jnp.floor((count + (RUN_ALIGN - 1)) * (1.0 / RUN_ALIGN)) * RUN_ALIGN


def _tile_rows(tm):
    bound = TOP_K * tm + N_EXPERTS * (RUN_ALIGN - 1)
    return -(-bound // 128) * 128


def _sorted_rows(n_tokens, tm):
    bound = (TOP_K * n_tokens + (n_tokens // tm) * N_EXPERTS * (RUN_ALIGN - 1)
             + N_EXPERTS * (EXPERT_ROWS - RUN_ALIGN))
    return -(-bound // EXPERT_ROWS) * EXPERT_ROWS


def _cparams(*sem):
    return pltpu.CompilerParams(dimension_semantics=sem, vmem_limit_bytes=VMEM_LIMIT)


def _nt(a, b):
    return lax.dot_general(a, b, (((1,), (1,)), ((), ())), preferred_element_type=F32)


def _tn(a, b):
    return lax.dot_general(a, b, (((0,), (0,)), ((), ())), preferred_element_type=F32)


def _dot(a, b):
    return jnp.dot(a, b, preferred_element_type=F32)


def _dot_split(x, w_bf16, terms=3):
    acc = None
    rem = x
    for _ in range(terms):
        piece = rem.astype(BF16)
        part = _dot(piece, w_bf16)
        acc = part if acc is None else acc + part
        rem = rem - piece.astype(F32)
    return acc


def _layer_norm(z, g, b):
    mu = jnp.mean(z, axis=-1, keepdims=True)
    zc = z - mu
    var = jnp.mean(zc * zc, axis=-1, keepdims=True)
    return zc * lax.rsqrt(var + EPS) * g + b


def _sigmoid(x):
    return 1.0 / (1.0 + jnp.exp(-x))


def _head_id(shape, axis):
    return lax.shift_right_logical(lax.broadcasted_iota(jnp.int32, shape, axis), 6)


def _block_diag_mask(n):
    return _head_id((n, n), 0) == _head_id((n, n), 1)


def _ln_kernel(x_ref, g_ref, b_ref, hf_ref, hb_ref):
    h = _layer_norm(x_ref[...], g_ref[...], b_ref[...])
    hf_ref[...] = h
    hb_ref[...] = h.astype(BF16)


def input_layer_norm(x2, g, b, tm=1024):
    n, d = x2.shape
    return pl.pallas_call(
        _ln_kernel,
        grid=(n // tm,),
        in_specs=[pl.BlockSpec((tm, d), lambda i: (i, 0)),
                  pl.BlockSpec((1, d), lambda i: (0, 0)),
                  pl.BlockSpec((1, d), lambda i: (0, 0))],
        out_specs=[pl.BlockSpec((tm, d), lambda i: (i, 0)),
                   pl.BlockSpec((tm, d), lambda i: (i, 0))],
        out_shape=[jax.ShapeDtypeStruct((n, d), F32), jax.ShapeDtypeStruct((n, d), BF16)],
        compiler_params=_cparams("parallel"),
        name="input_ln",
    )(x2, g.reshape(1, d), b.reshape(1, d))


def _matmul_kernel(h_ref, w_ref, o_ref):
    o_ref[...] = _dot(h_ref[...], w_ref[...]).astype(o_ref.dtype)


def in_projection(hb, w_bf16, tm=2048, tn=768):
    n, d = hb.shape
    w = w_bf16.shape[1]
    return pl.pallas_call(
        _matmul_kernel,
        grid=(n // tm, w // tn),
        in_specs=[pl.BlockSpec((tm, d), lambda i, j: (i, 0)),
                  pl.BlockSpec((d, tn), lambda i, j: (0, j))],
        out_specs=pl.BlockSpec((tm, tn), lambda i, j: (i, j)),
        out_shape=jax.ShapeDtypeStruct((n, w), BF16),
        compiler_params=_cparams("parallel", "arbitrary"),
        name="in_proj",
    )(hb, w_bf16)


def _retention_tables(c):
    idx = np.arange(N_HEADS, dtype=np.float64)
    lg = [np.log1p(-np.exp2(-5.0 - 2.0 * idx)), np.log1p(-np.exp2(-6.0 - 2.0 * idx))]
    i = np.arange(c, dtype=np.float64)
    diff = i[:, None] - i[None, :]
    lane_head = np.repeat(np.arange(N_HEADS), HEAD_DIM)
    dmat = np.zeros((2, N_HEADS, c, c), np.float32)
    oscale = np.zeros((2, c, BR_W), np.float32)
    kscale = np.zeros((2, c, BR_W), np.float32)
    sdecay = np.zeros((2, 1, BR_W), np.float32)
    for h in range(N_HEADS):
        dmat[0, h] = np.where(diff >= 0, np.exp(lg[0][h] * np.maximum(diff, 0)), 0.0)
        dmat[1, h] = np.where(diff <= 0, np.exp(lg[1][h] * np.maximum(-diff, 0)), 0.0)
    oscale[0] = np.exp(lg[0][lane_head][None, :] * (i[:, None] + 1.0))
    oscale[1] = np.exp(lg[1][lane_head][None, :] * (c - i[:, None]))
    kscale[0] = np.exp(lg[0][lane_head][None, :] * (c - 1.0 - i[:, None]))
    kscale[1] = np.exp(lg[1][lane_head][None, :] * i[:, None])
    sdecay[0, 0] = np.exp(lg[0][lane_head] * c)
    sdecay[1, 0] = np.exp(lg[1][lane_head] * c)
    return (jnp.asarray(dmat), jnp.asarray(oscale), jnp.asarray(kscale), jnp.asarray(sdecay))


def _retention_kernel(qf, kf, vf, qb, kb, vb, dmat, oscale, kscale, sdecay, of_ref, ob_ref, state):
    @pl.when(pl.program_id(1) == 0)
    def _():
        state[...] = jnp.zeros_like(state)

    c = qf.shape[0]
    lane_head = _head_id((c, BR_W), 1)
    bd = _block_diag_mask(BR_W)
    for dirn, (q_ref, k_ref, v_ref, o_ref) in enumerate(((qf, kf, vf, of_ref), (qb, kb, vb, ob_ref))):
        q = q_ref[...]
        k = k_ref[...] * jnp.asarray(HEAD_DIM ** -0.5, BF16)
        v = v_ref[...]
        s_old = state[dirn]
        out = _dot(q, s_old.astype(BF16)) * oscale[dirn]
        for h in range(N_HEADS):
            qz = jnp.where(lane_head == h, q, jnp.zeros_like(q))
            a = (_nt(qz, k) * dmat[dirn, h]).astype(BF16)
            out = out + jnp.where(lane_head == h, _dot(a, v), 0.0)
        o_ref[...] = out
        kw = (k.astype(F32) * kscale[dirn]).astype(BF16)
        state[dirn] = s_old * sdecay[dirn] + jnp.where(bd, _tn(kw, v), 0.0)


def retention(p, batch, seq):
    c = RET_CHUNK
    nc = seq // c
    dmat, oscale, kscale, sdecay = _retention_tables(c)
    cb = COL_RET // BR_W

    def fwd(col):
        return pl.BlockSpec((c, BR_W), lambda b, j: (b * nc + j, col))

    def bwd(col):
        return pl.BlockSpec((c, BR_W), lambda b, j: (b * nc + nc - 1 - j, col))

    def whole(a):
        nd = a.ndim
        return pl.BlockSpec(a.shape, lambda b, j: (0,) * nd)

    n = batch * seq
    return pl.pallas_call(
        _retention_kernel,
        grid=(batch, nc),
        in_specs=[fwd(cb), fwd(cb + 1), fwd(cb + 2), bwd(cb), bwd(cb + 1), bwd(cb + 2),
                  whole(dmat), whole(oscale), whole(kscale), whole(sdecay)],
        out_specs=[pl.BlockSpec((c, BR_W), lambda b, j: (b * nc + j, 0)),
                   pl.BlockSpec((c, BR_W), lambda b, j: (b * nc + nc - 1 - j, 0))],
        out_shape=[jax.ShapeDtypeStruct((n, BR_W), F32)] * 2,
        scratch_shapes=[pltpu.VMEM((2, BR_W, BR_W), F32)],
        compiler_params=_cparams("parallel", "arbitrary"),
        name="retention",
    )(p, p, p, p, p, p, dmat, oscale, kscale, sdecay)


def _hgrn_kernel(lb_ref, qf, ff, vf, qb, fb, vb, of_ref, ob_ref, state, *, layer):
    c = qf.shape[0]
    sb = HGRN_SUB
    n_sub = c // sb

    @pl.when(pl.program_id(1) == 0)
    def _():
        state[...] = jnp.zeros_like(state)

    lb = lb_ref[...]
    e = jnp.exp(lb - jnp.max(lb, axis=0, keepdims=True))
    prob = e / jnp.sum(e, axis=0, keepdims=True)
    lower = jnp.sum(prob[:layer + 1], axis=0) - prob[0]

    row = lax.broadcasted_iota(jnp.int32, (c, c), 0)
    col = lax.broadcasted_iota(jnp.int32, (c, c), 1)
    bd = _block_diag_mask(BR_W)
    seg_ones = jnp.where(bd, 1.0, 0.0).astype(BF16)
    rowv = lax.broadcasted_iota(jnp.int32, (c, BR_W), 0)

    for dirn, (q_ref, f_ref, v_ref, o_ref) in enumerate(((qf, ff, vf, of_ref), (qb, fb, vb, ob_ref))):
        rev = dirn == 1
        lo = lower[dirn:dirn + 1, :]
        fpre = f_ref[...].astype(F32)
        logf = jnp.log(lo + (1.0 - lo) * _sigmoid(fpre))
        kk = (1.0 - lo) * _sigmoid(-fpre)
        qx = q_ref[...].astype(F32)
        q = qx * _sigmoid(qx)
        v_bf = v_ref[...]
        v = v_bf.astype(F32)
        tri = jnp.where((col >= row) if rev else (col <= row), 1.0, 0.0).astype(BF16)
        g = _tri_cumsum(tri, logf)
        g_end = g[0:1, :] if rev else g[c - 1:c, :]
        s_old = state[dirn]
        inter = _nt((q * jnp.exp(g)).astype(BF16), s_old.astype(BF16))
        kd = (kk * jnp.exp(g_end - g)).astype(BF16)
        state[dirn] = s_old * jnp.exp(g_end) + jnp.where(bd, _tn(v_bf, kd), 0.0)
        qs, ks_ = [], []
        for j in range(1, n_sub):
            if rev:
                edge = g[j * sb:j * sb + 1, :]
                key_rows = (rowv >= j * sb) & (rowv < (j + 1) * sb)
                query_rows = rowv < j * sb
            else:
                edge = g[j * sb - 1:j * sb, :]
                key_rows = (rowv >= (j - 1) * sb) & (rowv < j * sb)
                query_rows = rowv >= j * sb
            qs.append(jnp.where(query_rows, q * jnp.exp(jnp.minimum(g - edge, 0.0)), 0.0).astype(BF16))
            ks_.append(jnp.where(key_rows, kk * jnp.exp(jnp.minimum(edge - g, 0.0)), 0.0).astype(BF16))
        bd_all = jnp.concatenate([bd] * (n_sub - 1), axis=1)
        cross = jnp.where(bd_all, _tn(v_bf, jnp.concatenate(ks_, axis=1)), 0.0).astype(BF16)
        intra = _nt(jnp.concatenate(qs, axis=1), cross)
        for d in range(sb):
            shift = (c - d) % c if rev else d
            pair = ((rowv % sb) + d < sb) if rev else ((rowv % sb) >= d)
            k_d, g_d, v_d = (kk, g, v) if d == 0 else tuple(pltpu.roll(a, shift, 0) for a in (kk, g, v))
            term = jnp.where(pair, q, 0.0) * k_d * jnp.exp(jnp.minimum(g - g_d, 0.0))
            intra = intra + _dot(term.astype(BF16), seg_ones) * v_d
        o_ref[...] = inter + intra


def _tri_cumsum(tri_bf16, x):
    acc = None
    rem = x
    for _ in range(3):
        piece = rem.astype(BF16)
        part = _dot(tri_bf16, piece)
        acc = part if acc is None else acc + part
        rem = rem - piece.astype(F32)
    return acc


def hgrn(p, hgrn_lb, layer, batch, seq):
    c = HGRN_CHUNK
    nc = seq // c
    cb = COL_HGRN // BR_W

    def fwd(col):
        return pl.BlockSpec((c, BR_W), lambda b, j: (b * nc + j, col))

    def bwd(col):
        return pl.BlockSpec((c, BR_W), lambda b, j: (b * nc + nc - 1 - j, col))

    n = batch * seq
    return pl.pallas_call(
        functools.partial(_hgrn_kernel, layer=layer),
        grid=(batch, nc),
        in_specs=[pl.BlockSpec(hgrn_lb.shape, lambda b, j: (0, 0, 0)),
                  fwd(cb), fwd(cb + 1), fwd(cb + 3), bwd(cb), bwd(cb + 2), bwd(cb + 3)],
        out_specs=[pl.BlockSpec((c, BR_W), lambda b, j: (b * nc + j, 0)),
                   pl.BlockSpec((c, BR_W), lambda b, j: (b * nc + nc - 1 - j, 0))],
        out_shape=[jax.ShapeDtypeStruct((n, BR_W), F32)] * 2,
        scratch_shapes=[pltpu.VMEM((2, BR_W, BR_W), F32)],
        compiler_params=_cparams("parallel", "arbitrary"),
        name="hgrn",
    )(hgrn_lb, p, p, p, p, p, p)


def _nat_bias_table(rpb):
    cq = np.arange(GRID_W)
    ck = np.arange(GRID_W)
    col_start = np.clip(cq - NAT_KW // 2, 0, GRID_W - NAT_KW)
    col_mask = (ck[None, :] >= col_start[:, None]) & (ck[None, :] < col_start[:, None] + NAT_KW)
    dc = np.clip(ck[None, :] - cq[:, None], -(NAT_KW - 1), NAT_KW - 1) + (NAT_KW - 1)
    pick = (dc[None, :, :] == np.arange(2 * NAT_KW - 1)[:, None, None]).astype(np.float32)
    tiles = jnp.einsum('hrd,dqk->hrqk', rpb.astype(F32), jnp.asarray(pick),
                       precision=lax.Precision.HIGHEST)
    tiles = jnp.where(jnp.asarray(col_mask)[None, None], tiles, MASK_VALUE)
    t = jnp.stack([tiles[:, b:b + NAT_KH] for b in range(NAT_KH)], axis=0)
    return t.transpose(0, 1, 3, 2, 4).reshape(NAT_KH, N_HEADS * GRID_W, NAT_KH * GRID_W)


def _nat_kernel(q_ref, k_ref, v_ref, bias_ref, o_ref, *, rows_per_step, n_rows):
    j = pl.program_id(1)
    lane_head = _head_id((GRID_W, BR_W), 1)
    win = NAT_KH * GRID_W

    def body(i, carry):
        r = j * rows_per_step + i
        rs = jnp.clip(r - NAT_KH // 2, 0, n_rows - NAT_KH)
        base = rs - r + (NAT_KH - 1)
        q = q_ref[pl.ds(pl.multiple_of(i * GRID_W, GRID_W), GRID_W), :] * jnp.asarray(HEAD_DIM ** -0.5, BF16)
        kw = k_ref[pl.ds(pl.multiple_of(rs * GRID_W, GRID_W), win), :]
        vw = v_ref[pl.ds(pl.multiple_of(rs * GRID_W, GRID_W), win), :]
        q_heads = jnp.concatenate([jnp.where(lane_head == h, q, jnp.zeros_like(q)) for h in range(N_HEADS)], axis=0)
        s = _nt(q_heads, kw) + bias_ref[base]
        m = jnp.max(s, axis=-1, keepdims=True)
        pr = jnp.exp(s - m)
        l = jnp.sum(pr, axis=-1, keepdims=True)
        o = _dot(pr.astype(BF16), vw) / l
        out = jnp.zeros((GRID_W, BR_W), F32)
        for h in range(N_HEADS):
            out = out + jnp.where(lane_head == h, o[h * GRID_W:(h + 1) * GRID_W], 0.0)
        o_ref[pl.ds(pl.multiple_of(i * GRID_W, GRID_W), GRID_W), :] = out.astype(o_ref.dtype)
        return carry

    lax.fori_loop(0, rows_per_step, body, 0, unroll=True)


def neighbourhood_attention(p, rpb, batch, seq):
    n_rows = seq // GRID_W
    rps = NAT_ROWS_PER_STEP
    steps = n_rows // rps
    bias = _nat_bias_table(rpb)
    cb = COL_NAT // BR_W
    tq = rps * GRID_W
    return pl.pallas_call(
        functools.partial(_nat_kernel, rows_per_step=rps, n_rows=n_rows),
        grid=(batch, steps),
        in_specs=[pl.BlockSpec((tq, BR_W), lambda b, j: (b * steps + j, cb)),
                  pl.BlockSpec((seq, BR_W), lambda b, j: (b, cb + 1)),
                  pl.BlockSpec((seq, BR_W), lambda b, j: (b, cb + 2)),
                  pl.BlockSpec(bias.shape, lambda b, j: (0, 0, 0))],
        out_specs=pl.BlockSpec((tq, BR_W), lambda b, j: (b * steps + j, 0)),
        out_shape=jax.ShapeDtypeStruct((batch * seq, BR_W), BF16),
        compiler_params=_cparams("parallel", "arbitrary"),
        name="nat",
    )(p, p, p, bias)


def _diff_kernel(lam_ref, slope_ref, sub_ref, q_ref, k_ref, v_ref, o_ref,
                 acc1, acc2, m1, l1, m2, l2, hill, stage_a1, stage_a2, stage_b1, stage_b2, *, lam_init, n_tiles):
    t = q_ref.shape[0]
    reps = t // 128
    qi = pl.program_id(2)
    slope2 = slope_ref[0, 0:1, 0:1] * LOG2E
    lp = lam_ref[...]
    lam = (jnp.exp(jnp.sum(lp[0:1] * lp[1:2], axis=-1, keepdims=True))
           - jnp.exp(jnp.sum(lp[2:3] * lp[3:4], axis=-1, keepdims=True)) + lam_init)

    @pl.when(qi == 0)
    def _():
        rel = (lax.broadcasted_iota(jnp.int32, (t, t), 1) - lax.broadcasted_iota(jnp.int32, (t, t), 0)).astype(F32)
        hill[...] = -slope2 * jnp.abs(rel)

    q = (q_ref[...].astype(F32) * (DIFF_SCALE * LOG2E)).astype(BF16)
    lane = lax.broadcasted_iota(jnp.int32, q.shape, 1)
    q1 = jnp.where(lane < 64, q, jnp.zeros_like(q))
    q2 = jnp.where(lane >= 64, q, jnp.zeros_like(q))

    acc1[...] = jnp.zeros_like(acc1)
    acc2[...] = jnp.zeros_like(acc2)
    l1[...] = jnp.zeros_like(l1)
    l2[...] = jnp.zeros_like(l2)
    m1[...] = jnp.full_like(m1, -jnp.inf)
    m2[...] = jnp.full_like(m2, -jnp.inf)

    def scores(kj):
        k = k_ref[pl.ds(pl.multiple_of(kj * t, t), t), :]
        return _nt(q1, k), _nt(q2, k)

    def absorb(kj, raw_scores, add_local_bias, tile_bias):
        v = v_ref[pl.ds(pl.multiple_of(kj * t, t), t), :]
        for s_raw, acc, m_ref, l_ref in zip(raw_scores, (acc1, acc2), (m1, m2), (l1, l2)):
            s = add_local_bias(s_raw)
            m_old = m_ref[...]
            m_new = jnp.maximum(m_old, jnp.max(s, axis=-1, keepdims=True) + tile_bias)
            a = jnp.exp2(m_old - m_new)
            pr = jnp.exp2(s - jnp.concatenate([m_new - tile_bias] * reps, axis=1))
            l_ref[...] = a * l_ref[...] + jnp.sum(pr, axis=-1, keepdims=True)
            acc[...] = a * acc[...] + _dot(pr.astype(BF16), v)
            m_ref[...] = m_new

    diagonal = hill[...]
    absorb(qi, scores(qi), lambda s: s + diagonal, jnp.zeros((1, 1), F32))

    stage = ((stage_a1, stage_a2), (stage_b1, stage_b2))
    query_term = slope2 * lax.broadcasted_iota(jnp.int32, (t, 128), 0).astype(F32)
    others = [jnp.where(u < qi, u, u + 1) for u in range(n_tiles - 1)]
    for u, kj in enumerate(others):
        cur, nxt = stage[u % 2], stage[(u + 1) % 2]
        if u == 0:
            for ref, val in zip(cur, scores(kj)):
                ref[...] = val
        if u + 1 < len(others):
            for ref, val in zip(nxt, scores(others[u + 1])):
                ref[...] = val
        sign = jnp.where(jnp.full((1, 1), kj, jnp.int32) < qi, 1.0, -1.0)
        keys = (sign * slope2) * ((kj - qi) * t + lax.broadcasted_iota(jnp.int32, (1, t), 1)).astype(F32)
        absorb(kj, (cur[0][...], cur[1][...]), lambda s, keys=keys: s + keys, -sign * query_term)

    o = acc1[...] / l1[...] - lam * (acc2[...] / l2[...])
    o = o * lax.rsqrt(jnp.mean(o * o, axis=-1, keepdims=True) + EPS) * sub_ref[0]
    o_ref[...] = (o * (1.0 - lam_init)).astype(o_ref.dtype)


def diff_attention(p, lam_params, subln, layer, batch, seq):
    t = DIFF_TILE
    nt = seq // t
    lam_init = 0.8 - 0.6 * math.exp(-0.3 * layer)
    slopes = np.exp2(-8.0 * (np.arange(N_HEADS, dtype=np.float64) + 1.0) / N_HEADS)
    slope_tab = jnp.asarray(np.broadcast_to(slopes[:, None, None], (N_HEADS, 8, 128)).astype(np.float32))
    qc, kc, vc = (COL_DIFF // DIFF_DV, (COL_DIFF + DIFF_W) // DIFF_DV, (COL_DIFF + 2 * DIFF_W) // DIFF_DV)
    return pl.pallas_call(
        functools.partial(_diff_kernel, lam_init=lam_init, n_tiles=nt),
        grid=(batch, N_HEADS, nt),
        in_specs=[pl.BlockSpec(lam_params.shape, lambda b, h, i: (0, 0)),
                  pl.BlockSpec((1, 8, 128), lambda b, h, i: (h, 0, 0)),
                  pl.BlockSpec((1, 1, DIFF_DV), lambda b, h, i: (h, 0, 0)),
                  pl.BlockSpec((t, DIFF_DV), lambda b, h, i: (b * nt + i, qc + h)),
                  pl.BlockSpec((seq, DIFF_DV), lambda b, h, i: (b, kc + h)),
                  pl.BlockSpec((seq, DIFF_DV), lambda b, h, i: (b, vc + h))],
        out_specs=pl.BlockSpec((t, DIFF_DV), lambda b, h, i: (b * nt + i, h)),
        out_shape=jax.ShapeDtypeStruct((batch * seq, DIFF_W), BF16),
        scratch_shapes=[pltpu.VMEM((t, DIFF_DV), F32)] * 6 + [pltpu.VMEM((t, t), F32)] * 5,
        compiler_params=_cparams("parallel", "parallel", "arbitrary"),
        name="diff_attn",
    )(lam_params, slope_tab, subln.reshape(N_HEADS, 1, DIFF_DV), p, p, p)


def _merge_kernel(hf_ref, hb_ref, rof, rob, rg, ynat, hof, hob, hg, ydiff,
                  wg_ref, bg_ref, wr_ref, wn_ref, wh_ref, wd_ref, wo_ref,
                  rgn_ref, hgn_ref, lng_ref, lnb_ref, of_ref, ob_ref):
    seg_mean = jnp.where(_block_diag_mask(BR_W), 1.0 / HEAD_DIM, 0.0).astype(BF16)

    def head_norm_gate(o, gn, gate):
        ms = _dot_split(o * o, seg_mean, terms=2)
        gx = gate.astype(F32)
        return (o * lax.rsqrt(ms + EPS) * gn * (gx * _sigmoid(gx))).astype(BF16)

    y_ret = head_norm_gate(rof[...] + rob[...], rgn_ref[...], rg[...])
    y_hgrn = head_norm_gate(hof[...] + hob[...], hgn_ref[...], hg[...])
    hb = hb_ref[...]
    d = hf_ref.shape[1]
    merged = None
    for i, (y, w_ref) in enumerate(((y_ret, wr_ref), (ynat[...], wn_ref), (y_hgrn, wh_ref), (ydiff[...], wd_ref))):
        gate = _sigmoid(_dot(hb, wg_ref[:, i * d:(i + 1) * d]) + bg_ref[:, i * d:(i + 1) * d])
        part = gate * _dot(y, w_ref[...])
        merged = part if merged is None else merged + part
    z = ALPHA * hf_ref[...] + _dot(merged.astype(BF16), wo_ref[...])
    h = _layer_norm(z, lng_ref[...], lnb_ref[...])
    of_ref[...] = h
    ob_ref[...] = h.astype(BF16)


def merge_and_norm(hf, hb, p, ret_f, ret_b, y_nat, hg_f, hg_b, y_diff,
                   w_gate, b_gate, w_br_ret, w_br_nat, w_br_hgrn, w_br_diff, w_out,
                   ret_gn, hgrn_gn, ln_g, ln_b, tm=512):
    n, d = hf.shape

    def rows(width, col=0):
        return pl.BlockSpec((tm, width), lambda i: (i, col))

    def whole(a):
        nd = a.ndim
        return pl.BlockSpec(a.shape, lambda i: (0,) * nd)

    consts = [w_gate, b_gate.reshape(1, -1), w_br_ret, w_br_nat, w_br_hgrn, w_br_diff, w_out,
              ret_gn.reshape(1, BR_W), hgrn_gn.reshape(1, BR_W), ln_g.reshape(1, d), ln_b.reshape(1, d)]
    return pl.pallas_call(
        _merge_kernel,
        grid=(n // tm,),
        in_specs=[rows(d), rows(d),
                  rows(BR_W), rows(BR_W), rows(BR_W, COL_RET // BR_W + 3),
                  rows(BR_W),
                  rows(BR_W), rows(BR_W), rows(BR_W, COL_HGRN // BR_W + 4),
                  rows(DIFF_W)] + [whole(a) for a in consts],
        out_specs=[rows(d), rows(d)],
        out_shape=[jax.ShapeDtypeStruct((n, d), F32), jax.ShapeDtypeStruct((n, d), BF16)],
        compiler_params=_cparams("parallel"),
        name="merge",
    )(hf, hb, ret_f, ret_b, p, y_nat, hg_f, hg_b, p, y_diff, *consts)


def _route(h, w, bias):
    w_hi = w.astype(BF16)
    w_lo = (w - w_hi.astype(F32)).astype(BF16)
    logits = _dot_split(h, w_hi, terms=3) + _dot_split(h, w_lo, terms=2)
    scores = _sigmoid(logits)
    biased = scores + bias
    shape = scores.shape
    lane = lax.broadcasted_iota(jnp.int32, shape, 1).astype(F32)
    group = lax.shift_right_logical(lax.broadcasted_iota(jnp.int32, shape, 1), 3).astype(F32)
    big = jnp.asarray(1e9, F32)
    neg = jnp.asarray(-jnp.inf, F32)

    def first_max(x, ids):
        m = jnp.max(x, axis=-1, keepdims=True)
        return m, jnp.min(jnp.where(x == m, ids, big), axis=-1, keepdims=True)

    gscore = jnp.zeros(shape, F32)
    for g in range(N_GROUPS):
        xg = jnp.where(group == g, biased, neg)
        m1, i1 = first_max(xg, lane)
        m2 = jnp.max(jnp.where(lane == i1, neg, xg), axis=-1, keepdims=True)
        gscore = jnp.where(group == g, m1 + m2, gscore)
    gsel = jnp.zeros(shape, jnp.bool_)
    for _ in range(TOPK_GROUPS):
        _, gi = first_max(gscore, group)
        hit = group == gi
        gsel = gsel | hit
        gscore = jnp.where(hit, neg, gscore)
    cand = jnp.where(gsel, biased, MASK_VALUE)
    esel = jnp.zeros(shape, jnp.bool_)
    picks = []
    for _ in range(TOP_K):
        _, ei = first_max(cand, lane)
        hit = lane == ei
        picks.append((ei, jnp.sum(jnp.where(hit, scores, 0.0), axis=-1, keepdims=True)))
        esel = esel | hit
        cand = jnp.where(hit, neg, cand)

    tm = shape[0]
    sel = jnp.where(esel, 1.0, 0.0)
    counts = jnp.concatenate(
        [jnp.sum(sel[s * ROUTE_TILE:(s + 1) * ROUTE_TILE], axis=0, keepdims=True) for s in range(tm // ROUTE_TILE)],
        axis=0)

    slot = lax.broadcasted_iota(jnp.int32, (tm, TOP_K), 1)
    idx8 = jnp.zeros((tm, TOP_K), F32)
    w8 = jnp.zeros((tm, TOP_K), F32)
    wsum = jnp.zeros((tm, 1), F32)
    for k, (ei, wk) in enumerate(picks):
        idx8 = jnp.where(slot == k, ei, idx8)
        w8 = jnp.where(slot == k, wk, w8)
        wsum = wsum + wk
    return idx8.astype(jnp.int32), w8 / (wsum + 1e-20) * ROUTED_SCALE, counts


def _router_kernel(h_ref, w_ref, bias_ref, idx_ref, w8_ref, cnt_ref):
    idx_ref[...], w8_ref[...], cnt_ref[...] = _route(h_ref[...], w_ref[...], bias_ref[...])


def router(hf, w_router, router_bias, tm=1024):
    n, d = hf.shape
    e = w_router.shape[1]
    sub = tm // ROUTE_TILE
    return pl.pallas_call(
        _router_kernel,
        grid=(n // tm,),
        in_specs=[pl.BlockSpec((tm, d), lambda i: (i, 0)),
                  pl.BlockSpec((d, e), lambda i: (0, 0)),
                  pl.BlockSpec((1, e), lambda i: (0, 0))],
        out_specs=[pl.BlockSpec((tm, TOP_K), lambda i: (i, 0)),
                   pl.BlockSpec((tm, TOP_K), lambda i: (i, 0)),
                   pl.BlockSpec((sub, e), lambda i: (i, 0))],
        out_shape=[jax.ShapeDtypeStruct((n, TOP_K), jnp.int32),
                   jax.ShapeDtypeStruct((n, TOP_K), F32),
                   jax.ShapeDtypeStruct((n // ROUTE_TILE, e), F32)],
        compiler_params=_cparams("arbitrary"),
        name="router",
    )(hf, w_router, router_bias.reshape(1, e))


HIGH_HALF = -65536


def _pack_pairs(x):
    w = x.shape[1] // 2
    lo = lax.bitcast_convert_type(x[:, :w].astype(BF16).astype(F32), jnp.int32)
    hi = lax.bitcast_convert_type(x[:, w:].astype(BF16).astype(F32), jnp.int32)
    return lax.shift_right_logical(lo, 16) | (hi & HIGH_HALF)


def _unpack_pairs(p):
    lo = lax.bitcast_convert_type(lax.shift_left(p, 16), F32)
    hi = lax.bitcast_convert_type(p & HIGH_HALF, F32)
    return jnp.concatenate([lo, hi], axis=1)


def _swiglu(x, wg, wu):
    a = _dot(x, wg)
    return a * _sigmoid(a) * _dot(x, wu)


def _segment_copies(tab_ref, first_col, max_run, make_copy, action):
    top_bit = max_run.bit_length() - 1
    low_bit = RUN_ALIGN.bit_length() - 1
    mid_bit = min(top_bit + 1, LONG_RUN.bit_length() - 1)

    def pieces(count, local, glob, bits):
        for b in bits:
            size = 1 << b
            taken = count & size

            @pl.when(taken != 0)
            def _(local=local, glob=glob, size=size):
                action(make_copy(pl.multiple_of(local, RUN_ALIGN), pl.multiple_of(glob, RUN_ALIGN), size))

            local = local + taken
            glob = glob + taken

    def per_expert(e, carry):
        col = first_col + e
        count = tab_ref[0, col]
        local = tab_ref[1, col]
        glob = tab_ref[2, col]
        long_part = count & -LONG_RUN

        @pl.when(long_part != 0)
        def _():
            pieces(count, local, glob, range(top_bit, mid_bit - 1, -1))

        pieces(count, local + long_part, glob + long_part, range(mid_bit - 1, low_bit - 1, -1))
        return carry

    lax.fori_loop(0, N_EXPERTS, per_expert, 0)


def _tile_rows_used(tab_ref, t):
    last = t * N_EXPERTS + (N_EXPERTS - 1)
    return tab_ref[1, last] + tab_ref[0, last]


def _wait_rows(total, max_rows, make_copy):
    for b in range(max_rows.bit_length() - 1, RUN_ALIGN.bit_length() - 2, -1):
        size = 1 << b

        @pl.when((total & size) != 0)
        def _(size=size):
            make_copy(0, 0, size).wait()


def _block_dispatch_kernel(tab_ref, gap_ref, tail_ref, idx_t_ref, hb_ref, xs_ref, sorted_buf, zeros_buf, sem):
    t = pl.program_id(0)
    tm = hb_ref.shape[0]
    rows = sorted_buf.shape[0]
    idx_t = idx_t_ref[...]
    expert = lax.broadcasted_iota(jnp.int32, (N_EXPERTS, tm), 0)
    sel_t = jnp.zeros((N_EXPERTS, tm), F32)
    for k in range(TOP_K):
        sel_t = sel_t + jnp.where(idx_t[k:k + 1, :] == expert, 1.0, 0.0)
    sel_b = sel_t.astype(BF16)
    before_tok = (lax.broadcasted_iota(jnp.int32, (tm, tm), 0) < lax.broadcasted_iota(jnp.int32, (tm, tm), 1))
    before_exp = (lax.broadcasted_iota(jnp.int32, (N_EXPERTS, N_EXPERTS), 1)
                  < lax.broadcasted_iota(jnp.int32, (N_EXPERTS, N_EXPERTS), 0))
    run = jnp.broadcast_to(_run_length(jnp.sum(sel_t, axis=1, keepdims=True)), (N_EXPERTS, tm)).astype(BF16)
    base = (_dot(jnp.where(before_exp, 1.0, 0.0).astype(BF16), run)
            + _dot(sel_b, jnp.where(before_tok, 1.0, 0.0).astype(BF16)))
    row = lax.broadcasted_iota(jnp.int32, (rows, tm), 0).astype(jnp.int16)
    perm = jnp.zeros((rows, tm), BF16)
    for k in range(TOP_K):
        place = jnp.sum(jnp.where(idx_t[k:k + 1, :] == expert, base, 0.0), axis=0, keepdims=True)
        perm = jnp.where(row == place.astype(jnp.int32).astype(jnp.int16), jnp.ones_like(perm), perm)
    half = hb_ref.shape[1] // 2
    lo = lax.bitcast_convert_type(_dot(perm, hb_ref[:, :half]), jnp.int32)
    hi = lax.bitcast_convert_type(_dot(perm, hb_ref[:, half:]), jnp.int32)
    sorted_buf[...] = lax.shift_right_logical(lo, 16) | (hi & HIGH_HALF)

    def make_copy(local, glob, size):
        return pltpu.make_async_copy(sorted_buf.at[pl.ds(local, size)], xs_ref.at[pl.ds(glob, size)], sem.at[0])

    _segment_copies(tab_ref, t * N_EXPERTS, tm, make_copy, lambda c: c.start())
    _wait_rows(_tile_rows_used(tab_ref, t), rows, make_copy)

    @pl.when(t == pl.num_programs(0) - 1)
    def _():
        zeros_buf[...] = jnp.zeros_like(zeros_buf)
        tile_rows = zeros_buf.shape[0]

        def zero_copy(local, glob, size):
            return pltpu.make_async_copy(zeros_buf.at[pl.ds(0, size)], xs_ref.at[pl.ds(glob, size)], sem.at[0])

        def tail_copy(j):
            return zero_copy(0, pl.multiple_of(tail_ref[0] + j * tile_rows, RUN_ALIGN), tile_rows)

        _segment_copies(gap_ref, 0, tile_rows // 2, zero_copy, lambda c: c.start())
        lax.fori_loop(0, tail_ref[1], lambda j, c: (tail_copy(j).start(), c)[1], 0)
        _segment_copies(gap_ref, 0, tile_rows // 2, zero_copy, lambda c: c.wait())
        lax.fori_loop(0, tail_ref[1], lambda j, c: (tail_copy(j).wait(), c)[1], 0)


def block_dispatch(hb, idx_t, tab, gaps, tail, tm):
    n, d = hb.shape
    grid_spec = pltpu.PrefetchScalarGridSpec(
        num_scalar_prefetch=3,
        grid=(n // tm,),
        in_specs=[pl.BlockSpec((TOP_K, tm), lambda i, *_: (0, i)),
                  pl.BlockSpec((tm, d), lambda i, *_: (i, 0))],
        out_specs=pl.BlockSpec(memory_space=pl.ANY),
        scratch_shapes=[pltpu.VMEM((_tile_rows(tm), d // 2), jnp.int32),
                        pltpu.VMEM((EXPERT_ROWS, d // 2), jnp.int32),
                        pltpu.SemaphoreType.DMA((1,))],
    )
    return pl.pallas_call(
        _block_dispatch_kernel,
        grid_spec=grid_spec,
        out_shape=jax.ShapeDtypeStruct((_sorted_rows(n, tm), d // 2), jnp.int32),
        compiler_params=_cparams("arbitrary"),
        name="dispatch",
    )(tab, gaps, tail, idx_t, hb)


def _tile_tables(tile_counts, tm_fine, tm):
    g = tm // tm_fine
    n_tokens = tile_counts.shape[0] * tm_fine
    n_tiles = _sorted_rows(n_tokens, tm) // EXPERT_ROWS
    cnt = tile_counts.astype(jnp.int32).reshape(-1, g, N_EXPERTS).sum(axis=1)
    cnt = (cnt + (RUN_ALIGN - 1)) // RUN_ALIGN * RUN_ALIGN
    local = jnp.cumsum(cnt, axis=1) - cnt
    used = cnt.sum(axis=0)
    seg = (used + (EXPERT_ROWS - 1)) // EXPERT_ROWS * EXPERT_ROWS
    seg_end = jnp.cumsum(seg)
    seg_start = seg_end - seg
    glob = seg_start[None, :] + jnp.cumsum(cnt, axis=0) - cnt
    runs = jnp.stack([cnt.reshape(-1), local.reshape(-1), glob.reshape(-1)])
    gaps = jnp.stack([seg - used, jnp.zeros_like(used), seg_start + used])
    tail = jnp.stack([seg_end[-1], n_tiles - seg_end[-1] // EXPERT_ROWS])
    tile_row = jnp.arange(n_tiles, dtype=jnp.int32) * EXPERT_ROWS
    owner = jnp.sum((seg_end[None, :] <= tile_row[:, None]).astype(jnp.int32), axis=1)
    owner_c = jnp.minimum(owner, N_EXPERTS - 1)
    of_owner = lambda a: jnp.sum(jnp.where(jnp.arange(N_EXPERTS)[None, :] == owner_c[:, None], a[None, :], 0), axis=1)
    filled = jnp.clip(of_owner(seg_start + used) - tile_row, 0, EXPERT_ROWS)
    tiles = jnp.stack([owner_c, jnp.where(owner < N_EXPERTS, filled, 0)])
    return runs, gaps, tail, tiles.astype(jnp.int32)


def _expert_kernel(tile_ref, xs_ref, wg_ref, wu_ref, wd_ref, ys_ref, wgb, wub, wdb):
    i = pl.program_id(0)
    expert = tile_ref[0, i]

    @pl.when(tile_ref[1, i] > 0)
    def _():
        @pl.when((i == 0) | (expert != tile_ref[0, jnp.maximum(i - 1, 0)]))
        def _():
            wgb[...] = wg_ref[0, 0].astype(BF16)
            wub[...] = wu_ref[0, 0].astype(BF16)
            wdb[...] = wd_ref[0, 0].astype(BF16)

        x = _unpack_pairs(xs_ref[...]).astype(BF16)
        hid = _swiglu(x, wgb[...], wub[...])
        ys_ref[...] = _pack_pairs(_dot(hid.astype(BF16), wdb[...]))

    @pl.when(tile_ref[1, i] == 0)
    def _():
        ys_ref[...] = jnp.zeros_like(ys_ref)


def grouped_experts(xs, tiles, w_e_gate, w_e_up, w_e_down, layer):
    rows, w = xs.shape
    _, _, d, hid = w_e_gate.shape
    tr = EXPERT_ROWS
    grid_spec = pltpu.PrefetchScalarGridSpec(
        num_scalar_prefetch=1,
        grid=(rows // tr,),
        in_specs=[pl.BlockSpec((tr, w), lambda i, tl: (i, 0)),
                  pl.BlockSpec((1, 1, d, hid), lambda i, tl: (layer, tl[0, i], 0, 0)),
                  pl.BlockSpec((1, 1, d, hid), lambda i, tl: (layer, tl[0, i], 0, 0)),
                  pl.BlockSpec((1, 1, hid, d), lambda i, tl: (layer, tl[0, i], 0, 0))],
        out_specs=pl.BlockSpec((tr, w), lambda i, tl: (i, 0)),
        scratch_shapes=[pltpu.VMEM((d, hid), BF16), pltpu.VMEM((d, hid), BF16), pltpu.VMEM((hid, d), BF16)],
    )
    return pl.pallas_call(
        _expert_kernel,
        grid_spec=grid_spec,
        out_shape=jax.ShapeDtypeStruct((rows, w), jnp.int32),
        compiler_params=_cparams("arbitrary"),
        name="experts",
    )(tiles, xs, w_e_gate, w_e_up, w_e_down)


def _block_combine_kernel(tab_ref, ys_ref, idx_ref, w8_ref, hf_ref, hb_ref, sg_ref, su_ref, sd_ref,
                          lng_ref, lnb_ref, of_ref, ob_ref, buf, sem):
    t = pl.program_id(0)
    tm = hf_ref.shape[0]
    rows = buf.shape[0]

    def make_copy(local, glob, size):
        return pltpu.make_async_copy(ys_ref.at[pl.ds(glob, size)], buf.at[pl.ds(local, size)], sem.at[0])

    @pl.when(t == 0)
    def _():
        buf[...] = jnp.zeros_like(buf)

    _segment_copies(tab_ref, t * N_EXPERTS, tm, make_copy, lambda c: c.start())

    idx = idx_ref[...]
    w8 = w8_ref[...]
    expert = lax.broadcasted_iota(jnp.int32, (tm, N_EXPERTS), 1)
    sel = jnp.zeros((tm, N_EXPERTS), F32)
    for k in range(TOP_K):
        sel = sel + jnp.where(idx[:, k:k + 1] == expert, 1.0, 0.0)
    sel_b = sel.astype(BF16)
    before_tok = (lax.broadcasted_iota(jnp.int32, (tm, tm), 1) < lax.broadcasted_iota(jnp.int32, (tm, tm), 0))
    before_exp = (lax.broadcasted_iota(jnp.int32, (N_EXPERTS, N_EXPERTS), 0)
                  < lax.broadcasted_iota(jnp.int32, (N_EXPERTS, N_EXPERTS), 1))
    run = jnp.broadcast_to(_run_length(jnp.sum(sel, axis=0, keepdims=True)), (tm, N_EXPERTS)).astype(BF16)
    base = (_dot(run, jnp.where(before_exp, 1.0, 0.0).astype(BF16))
            + _dot(jnp.where(before_tok, 1.0, 0.0).astype(BF16), sel_b))
    col = lax.broadcasted_iota(jnp.int32, (tm, rows), 1).astype(jnp.int16)
    w8_b = w8.astype(BF16)
    mix = jnp.zeros((tm, rows), BF16)
    for k in range(TOP_K):
        place = jnp.sum(jnp.where(idx[:, k:k + 1] == expert, base, 0.0), axis=1, keepdims=True)
        mix = jnp.where(col == place.astype(jnp.int32).astype(jnp.int16),
                        jnp.broadcast_to(w8_b[:, k:k + 1], mix.shape), mix)

    acc = _dot(_swiglu(hb_ref[...], sg_ref[...], su_ref[...]).astype(BF16), sd_ref[...])
    _wait_rows(_tile_rows_used(tab_ref, t), rows, make_copy)
    y = _unpack_pairs(buf[...]).astype(BF16)
    acc = acc + _dot(mix, y)
    h = _layer_norm(ALPHA * hf_ref[...] + acc, lng_ref[...], lnb_ref[...])
    of_ref[...] = h
    ob_ref[...] = h.astype(BF16)


def block_combine(ys, tab, idx8, w8, hf, hb, w_s_gate, w_s_up, w_s_down, ln_g, ln_b, tm):
    n, d = hf.shape
    w = ys.shape[1]

    def whole(a):
        return pl.BlockSpec(a.shape, lambda i, tab: (0, 0))

    def rows(width):
        return pl.BlockSpec((tm, width), lambda i, tab: (i, 0))

    consts = [w_s_gate, w_s_up, w_s_down, ln_g.reshape(1, d), ln_b.reshape(1, d)]
    grid_spec = pltpu.PrefetchScalarGridSpec(
        num_scalar_prefetch=1,
        grid=(n // tm,),
        in_specs=[pl.BlockSpec(memory_space=pl.ANY), rows(TOP_K), rows(TOP_K), rows(d), rows(d)]
                 + [whole(a) for a in consts],
        out_specs=[rows(d), rows(d)],
        scratch_shapes=[pltpu.VMEM((_tile_rows(tm), w), jnp.int32), pltpu.SemaphoreType.DMA((1,))],
    )
    return pl.pallas_call(
        _block_combine_kernel,
        grid_spec=grid_spec,
        out_shape=[jax.ShapeDtypeStruct((n, d), F32), jax.ShapeDtypeStruct((n, d), BF16)],
        compiler_params=_cparams("arbitrary"),
        name="combine",
    )(tab, ys, idx8, w8, hf, hb, *consts)


def moe_and_norm(hf, hb, w_router, router_bias, w_e_gate, w_e_up, w_e_down,
                 w_s_gate, w_s_up, w_s_down, ln_g, ln_b, layer):
    idx8, w8, tile_counts = router(hf, w_router, router_bias)
    runs, gaps, tail, tiles = _tile_tables(tile_counts, ROUTE_TILE, MOE_TILE)
    xs = block_dispatch(hb, idx8.T, runs, gaps, tail, MOE_TILE)
    ys = grouped_experts(xs, tiles, w_e_gate, w_e_up, w_e_down, layer)
    return block_combine(ys, runs, idx8, w8, hf, hb, w_s_gate, w_s_up, w_s_down, ln_g, ln_b, MOE_TILE)


def kernel(x, ln_in_g, ln_in_b, w_in, w_gate, b_gate, w_br_ret, w_br_nat, w_br_hgrn, w_br_diff, w_out,
           ret_gn, nat_rpb, hgrn_lb, hgrn_gn, diff_lambda, diff_subln, ln1_g, ln1_b, w_router,
           router_bias, w_e_gate, w_e_up, w_e_down, w_s_gate, w_s_up, w_s_down, ln2_g, ln2_b):
    batch, seq, d = x.shape
    bf = lambda a: a.astype(BF16)
    hf, hb = input_layer_norm(x.reshape(batch * seq, d), ln_in_g, ln_in_b)
    for l in range(DEPTH):
        p = in_projection(hb, bf(w_in[l]))
        ret_f, ret_b = retention(p, batch, seq)
        y_nat = neighbourhood_attention(p, nat_rpb[l], batch, seq)
        hg_f, hg_b = hgrn(p, hgrn_lb, l, batch, seq)
        y_diff = diff_attention(p, diff_lambda[l], diff_subln[l], l, batch, seq)
        hf, hb = merge_and_norm(hf, hb, p, ret_f, ret_b, y_nat, hg_f, hg_b, y_diff,
                                bf(w_gate[l]), b_gate[l], bf(w_br_ret[l]), bf(w_br_nat[l]),
                                bf(w_br_hgrn[l]), bf(w_br_diff[l]), bf(w_out[l]),
                                ret_gn[l], hgrn_gn[l], ln1_g[l], ln1_b[l])
        hf, hb = moe_and_norm(hf, hb, w_router[l], router_bias[l], w_e_gate, w_e_up, w_e_down,
                              bf(w_s_gate[l]), bf(w_s_up[l]), bf(w_s_down[l]), ln2_g[l], ln2_b[l], l)
    return hf.reshape(batch, seq, d)
```

```python
import functools
import math

import numpy as np
import jax
import jax.numpy as jnp
from jax import lax
from jax.experimental import pallas as pl
from jax.experimental.pallas import tpu as pltpu

F32 = jnp.float32
BF16 = jnp.bfloat16

D_MODEL = 1024
DEPTH = 2
GRID_W = 64
HEAD_DIM = 64
N_HEADS = 4
BR_W = N_HEADS * HEAD_DIM
NAT_KH = 8
NAT_KW = 16
DIFF_DV = 128
DIFF_W = N_HEADS * DIFF_DV
N_EXPERTS = 64
TOP_K = 8
N_GROUPS = 8
TOPK_GROUPS = 4
GROUP_SIZE = N_EXPERTS // N_GROUPS
EXPERT_HIDDEN = 256
ROUTED_SCALE = 2.5
EPS = 1e-5
MASK_VALUE = -1e30
LOG2E = math.log2(math.e)
DIFF_SCALE = 64 ** -0.5
ALPHA = (2.0 * DEPTH) ** 0.25
IN_WIDTH = 4608
COL_RET = 0
COL_NAT = 1024
COL_HGRN = 1792
COL_DIFF = 3072

VMEM_LIMIT = 56 * 1024 * 1024

RET_CHUNK = 256
HGRN_CHUNK = 128
HGRN_SUB = 16
DIFF_TILE = 1024
NAT_ROWS_PER_STEP = 8
ROUTE_TILE = 128
MOE_TILE = 256
EXPERT_ROWS = 512
RUN_ALIGN = 8
LONG_RUN = 64


def _run_length(count):
    return jnp.floor((count + (RUN_ALIGN - 1)) * (1.0 / RUN_ALIGN)) * RUN_ALIGN


def _tile_rows(tm):
    bound = TOP_K * tm + N_EXPERTS * (RUN_ALIGN - 1)
    return -(-bound // 128) * 128


def _sorted_rows(n_tokens, tm):
    bound = (TOP_K * n_tokens + (n_tokens // tm) * N_EXPERTS * (RUN_ALIGN - 1)
             + N_EXPERTS * (EXPERT_ROWS - RUN_ALIGN))
    return -(-bound // EXPERT_ROWS) * EXPERT_ROWS


def _cparams(*sem):
    return pltpu.CompilerParams(dimension_semantics=sem, vmem_limit_bytes=VMEM_LIMIT)


def _nt(a, b):
    return lax.dot_general(a, b, (((1,), (1,)), ((), ())), preferred_element_type=F32)


def _tn(a, b):
    return lax.dot_general(a, b, (((0,), (0,)), ((), ())), preferred_element_type=F32)


def _dot(a, b):
    return jnp.dot(a, b, preferred_element_type=F32)


def _dot_split(x, w_bf16, terms=3):
    acc = None
    rem = x
    for _ in range(terms):
        piece = rem.astype(BF16)
        part = _dot(piece, w_bf16)
        acc = part if acc is None else acc + part
        rem = rem - piece.astype(F32)
    return acc


def _layer_norm(z, g, b):
    mu = jnp.mean(z, axis=-1, keepdims=True)
    zc = z - mu
    var = jnp.mean(zc * zc, axis=-1, keepdims=True)
    return zc * lax.rsqrt(var + EPS) * g + b


def _sigmoid(x):
    return 1.0 / (1.0 + jnp.exp(-x))


def _head_id(shape, axis):
    return lax.shift_right_logical(lax.broadcasted_iota(jnp.int32, shape, axis), 6)


def _block_diag_mask(n):
    return _head_id((n, n), 0) == _head_id((n, n), 1)


def _ln_kernel(x_ref, g_ref, b_ref, hf_ref, hb_ref):
    h = _layer_norm(x_ref[...], g_ref[...], b_ref[...])
    hf_ref[...] = h
    hb_ref[...] = h.astype(BF16)


def input_layer_norm(x2, g, b, tm=1024):
    n, d = x2.shape
    return pl.pallas_call(
        _ln_kernel,
        grid=(n // tm,),
        in_specs=[pl.BlockSpec((tm, d), lambda i: (i, 0)),
                  pl.BlockSpec((1, d), lambda i: (0, 0)),
                  pl.BlockSpec((1, d), lambda i: (0, 0))],
        out_specs=[pl.BlockSpec((tm, d), lambda i: (i, 0)),
                   pl.BlockSpec((tm, d), lambda i: (i, 0))],
        out_shape=[jax.ShapeDtypeStruct((n, d), F32), jax.ShapeDtypeStruct((n, d), BF16)],
        compiler_params=_cparams("parallel"),
        name="input_ln",
    )(x2, g.reshape(1, d), b.reshape(1, d))


def _matmul_kernel(h_ref, w_ref, o_ref):
    o_ref[...] = _dot(h_ref[...], w_ref[...]).astype(o_ref.dtype)


def in_projection(hb, w_bf16, tm=2048, tn=768):
    n, d = hb.shape
    w = w_bf16.shape[1]
    return pl.pallas_call(
        _matmul_kernel,
        grid=(n // tm, w // tn),
        in_specs=[pl.BlockSpec((tm, d), lambda i, j: (i, 0)),
                  pl.BlockSpec((d, tn), lambda i, j: (0, j))],
        out_specs=pl.BlockSpec((tm, tn), lambda i, j: (i, j)),
        out_shape=jax.ShapeDtypeStruct((n, w), BF16),
        compiler_params=_cparams("parallel", "arbitrary"),
        name="in_proj",
    )(hb, w_bf16)


def _retention_tables(c):
    idx = np.arange(N_HEADS, dtype=np.float64)
    lg = [np.log1p(-np.exp2(-5.0 - 2.0 * idx)), np.log1p(-np.exp2(-6.0 - 2.0 * idx))]
    i = np.arange(c, dtype=np.float64)
    diff = i[:, None] - i[None, :]
    lane_head = np.repeat(np.arange(N_HEADS), HEAD_DIM)
    dmat = np.zeros((2, N_HEADS, c, c), np.float32)
    oscale = np.zeros((2, c, BR_W), np.float32)
    kscale = np.zeros((2, c, BR_W), np.float32)
    sdecay = np.zeros((2, 1, BR_W), np.float32)
    for h in range(N_HEADS):
        dmat[0, h] = np.where(diff >= 0, np.exp(lg[0][h] * np.maximum(diff, 0)), 0.0)
        dmat[1, h] = np.where(diff <= 0, np.exp(lg[1][h] * np.maximum(-diff, 0)), 0.0)
    oscale[0] = np.exp(lg[0][lane_head][None, :] * (i[:, None] + 1.0))
    oscale[1] = np.exp(lg[1][lane_head][None, :] * (c - i[:, None]))
    kscale[0] = np.exp(lg[0][lane_head][None, :] * (c - 1.0 - i[:, None]))
    kscale[1] = np.exp(lg[1][lane_head][None, :] * i[:, None])
    sdecay[0, 0] = np.exp(lg[0][lane_head] * c)
    sdecay[1, 0] = np.exp(lg[1][lane_head] * c)
    return (jnp.asarray(dmat), jnp.asarray(oscale), jnp.asarray(kscale), jnp.asarray(sdecay))


def _retention_kernel(qf, kf, vf, qb, kb, vb, dmat, oscale, kscale, sdecay, of_ref, ob_ref, state):
    @pl.when(pl.program_id(1) == 0)
    def _():
        state[...] = jnp.zeros_like(state)

    c = qf.shape[0]
    lane_head = _head_id((c, BR_W), 1)
    bd = _block_diag_mask(BR_W)
    for dirn, (q_ref, k_ref, v_ref, o_ref) in enumerate(((qf, kf, vf, of_ref), (qb, kb, vb, ob_ref))):
        q = q_ref[...]
        k = k_ref[...] * jnp.asarray(HEAD_DIM ** -0.5, BF16)
        v = v_ref[...]
        s_old = state[dirn]
        out = _dot(q, s_old.astype(BF16)) * oscale[dirn]
        for h in range(N_HEADS):
            qz = jnp.where(lane_head == h, q, jnp.zeros_like(q))
            a = (_nt(qz, k) * dmat[dirn, h]).astype(BF16)
            out = out + jnp.where(lane_head == h, _dot(a, v), 0.0)
        o_ref[...] = out
        kw = (k.astype(F32) * kscale[dirn]).astype(BF16)
        state[dirn] = s_old * sdecay[dirn] + jnp.where(bd, _tn(kw, v), 0.0)


def retention(p, batch, seq):
    c = RET_CHUNK
    nc = seq // c
    dmat, oscale, kscale, sdecay = _retention_tables(c)
    cb = COL_RET // BR_W

    def fwd(col):
        return pl.BlockSpec((c, BR_W), lambda b, j: (b * nc + j, col))

    def bwd(col):
        return pl.BlockSpec((c, BR_W), lambda b, j: (b * nc + nc - 1 - j, col))

    def whole(a):
        nd = a.ndim
        return pl.BlockSpec(a.shape, lambda b, j: (0,) * nd)

    n = batch * seq
    return pl.pallas_call(
        _retention_kernel,
        grid=(batch, nc),
        in_specs=[fwd(cb), fwd(cb + 1), fwd(cb + 2), bwd(cb), bwd(cb + 1), bwd(cb + 2),
                  whole(dmat), whole(oscale), whole(kscale), whole(sdecay)],
        out_specs=[pl.BlockSpec((c, BR_W), lambda b, j: (b * nc + j, 0)),
                   pl.BlockSpec((c, BR_W), lambda b, j: (b * nc + nc - 1 - j, 0))],
        out_shape=[jax.ShapeDtypeStruct((n, BR_W), F32)] * 2,
        scratch_shapes=[pltpu.VMEM((2, BR_W, BR_W), F32)],
        compiler_params=_cparams("parallel", "arbitrary"),
        name="retention",
    )(p, p, p, p, p, p, dmat, oscale, kscale, sdecay)


def _hgrn_kernel(lb_ref, qf, ff, vf, qb, fb, vb, of_ref, ob_ref, state, *, layer):
    c = qf.shape[0]
    sb = HGRN_SUB
    n_sub = c // sb

    @pl.when(pl.program_id(1) == 0)
    def _():
        state[...] = jnp.zeros_like(state)

    lb = lb_ref[...]
    e = jnp.exp(lb - jnp.max(lb, axis=0, keepdims=True))
    prob = e / jnp.sum(e, axis=0, keepdims=True)
    lower = jnp.sum(prob[:layer + 1], axis=0) - prob[0]

    row = lax.broadcasted_iota(jnp.int32, (c, c), 0)
    col = lax.broadcasted_iota(jnp.int32, (c, c), 1)
    bd = _block_diag_mask(BR_W)
    seg_ones = jnp.where(bd, 1.0, 0.0).astype(BF16)
    rowv = lax.broadcasted_iota(jnp.int32, (c, BR_W), 0)

    for dirn, (q_ref, f_ref, v_ref, o_ref) in enumerate(((qf, ff, vf, of_ref), (qb, fb, vb, ob_ref))):
        rev = dirn == 1
        lo = lower[dirn:dirn + 1, :]
        fpre = f_ref[...].astype(F32)
        logf = jnp.log(lo + (1.0 - lo) * _sigmoid(fpre))
        kk = (1.0 - lo) * _sigmoid(-fpre)
        qx = q_ref[...].astype(F32)
        q = qx * _sigmoid(qx)
        v_bf = v_ref[...]
        v = v_bf.astype(F32)
        tri = jnp.where((col >= row) if rev else (col <= row), 1.0, 0.0).astype(BF16)
        g = _tri_cumsum(tri, logf)
        g_end = g[0:1, :] if rev else g[c - 1:c, :]
        s_old = state[dirn]
        inter = _nt((q * jnp.exp(g)).astype(BF16), s_old.astype(BF16))
        kd = (kk * jnp.exp(g_end - g)).astype(BF16)
        state[dirn] = s_old * jnp.exp(g_end) + jnp.where(bd, _tn(v_bf, kd), 0.0)
        qs, ks_ = [], []
        for j in range(1, n_sub):
            if rev:
                edge = g[j * sb:j * sb + 1, :]
                key_rows = (rowv >= j * sb) & (rowv < (j + 1) * sb)
                query_rows = rowv < j * sb
            else:
                edge = g[j * sb - 1:j * sb, :]
                key_rows = (rowv >= (j - 1) * sb) & (rowv < j * sb)
                query_rows = rowv >= j * sb
            qs.append(jnp.where(query_rows, q * jnp.exp(jnp.minimum(g - edge, 0.0)), 0.0).astype(BF16))
            ks_.append(jnp.where(key_rows, kk * jnp.exp(jnp.minimum(edge - g, 0.0)), 0.0).astype(BF16))
        bd_all = jnp.concatenate([bd] * (n_sub - 1), axis=1)
        cross = jnp.where(bd_all, _tn(v_bf, jnp.concatenate(ks_, axis=1)), 0.0).astype(BF16)
        intra = _nt(jnp.concatenate(qs, axis=1), cross)
        for d in range(sb):
            shift = (c - d) % c if rev else d
            pair = ((rowv % sb) + d < sb) if rev else ((rowv % sb) >= d)
            k_d, g_d, v_d = (kk, g, v) if d == 0 else tuple(pltpu.roll(a, shift, 0) for a in (kk, g, v))
            term = jnp.where(pair, q, 0.0) * k_d * jnp.exp(jnp.minimum(g - g_d, 0.0))
            intra = intra + _dot(term.astype(BF16), seg_ones) * v_d
        o_ref[...] = inter + intra


def _tri_cumsum(tri_bf16, x):
    acc = None
    rem = x
    for _ in range(3):
        piece = rem.astype(BF16)
        part = _dot(tri_bf16, piece)
        acc = part if acc is None else acc + part
        rem = rem - piece.astype(F32)
    return acc


def hgrn(p, hgrn_lb, layer, batch, seq):
    c = HGRN_CHUNK
    nc = seq // c
    cb = COL_HGRN // BR_W

    def fwd(col):
        return pl.BlockSpec((c, BR_W), lambda b, j: (b * nc + j, col))

    def bwd(col):
        return pl.BlockSpec((c, BR_W), lambda b, j: (b * nc + nc - 1 - j, col))

    n = batch * seq
    return pl.pallas_call(
        functools.partial(_hgrn_kernel, layer=layer),
        grid=(batch, nc),
        in_specs=[pl.BlockSpec(hgrn_lb.shape, lambda b, j: (0, 0, 0)),
                  fwd(cb), fwd(cb + 1), fwd(cb + 3), bwd(cb), bwd(cb + 2), bwd(cb + 3)],
        out_specs=[pl.BlockSpec((c, BR_W), lambda b, j: (b * nc + j, 0)),
                   pl.BlockSpec((c, BR_W), lambda b, j: (b * nc + nc - 1 - j, 0))],
        out_shape=[jax.ShapeDtypeStruct((n, BR_W), F32)] * 2,
        scratch_shapes=[pltpu.VMEM((2, BR_W, BR_W), F32)],
        compiler_params=_cparams("parallel", "arbitrary"),
        name="hgrn",
    )(hgrn_lb, p, p, p, p, p, p)


def _nat_bias_table(rpb):
    cq = np.arange(GRID_W)
    ck = np.arange(GRID_W)
    col_start = np.clip(cq - NAT_KW // 2, 0, GRID_W - NAT_KW)
    col_mask = (ck[None, :] >= col_start[:, None]) & (ck[None, :] < col_start[:, None] + NAT_KW)
    dc = np.clip(ck[None, :] - cq[:, None], -(NAT_KW - 1), NAT_KW - 1) + (NAT_KW - 1)
    pick = (dc[None, :, :] == np.arange(2 * NAT_KW - 1)[:, None, None]).astype(np.float32)
    tiles = jnp.einsum('hrd,dqk->hrqk', rpb.astype(F32), jnp.asarray(pick),
                       precision=lax.Precision.HIGHEST)
    tiles = jnp.where(jnp.asarray(col_mask)[None, None], tiles, MASK_VALUE)
    t = jnp.stack([tiles[:, b:b + NAT_KH] for b in range(NAT_KH)], axis=0)
    return t.transpose(0, 1, 3, 2, 4).reshape(NAT_KH, N_HEADS * GRID_W, NAT_KH * GRID_W)


def _nat_kernel(q_ref, k_ref, v_ref, bias_ref, o_ref, *, rows_per_step, n_rows):
    j = pl.program_id(1)
    lane_head = _head_id((GRID_W, BR_W), 1)
    win = NAT_KH * GRID_W

    def body(i, carry):
        r = j * rows_per_step + i
        rs = jnp.clip(r - NAT_KH // 2, 0, n_rows - NAT_KH)
        base = rs - r + (NAT_KH - 1)
        q = q_ref[pl.ds(pl.multiple_of(i * GRID_W, GRID_W), GRID_W), :] * jnp.asarray(HEAD_DIM ** -0.5, BF16)
        kw = k_ref[pl.ds(pl.multiple_of(rs * GRID_W, GRID_W), win), :]
        vw = v_ref[pl.ds(pl.multiple_of(rs * GRID_W, GRID_W), win), :]
        q_heads = jnp.concatenate([jnp.where(lane_head == h, q, jnp.zeros_like(q)) for h in range(N_HEADS)], axis=0)
        s = _nt(q_heads, kw) + bias_ref[base]
        m = jnp.max(s, axis=-1, keepdims=True)
        pr = jnp.exp(s - m)
        l = jnp.sum(pr, axis=-1, keepdims=True)
        o = _dot(pr.astype(BF16), vw) / l
        out = jnp.zeros((GRID_W, BR_W), F32)
        for h in range(N_HEADS):
            out = out + jnp.where(lane_head == h, o[h * GRID_W:(h + 1) * GRID_W], 0.0)
        o_ref[pl.ds(pl.multiple_of(i * GRID_W, GRID_W), GRID_W), :] = out.astype(o_ref.dtype)
        return carry

    lax.fori_loop(0, rows_per_step, body, 0, unroll=True)


def neighbourhood_attention(p, rpb, batch, seq):
    n_rows = seq // GRID_W
    rps = NAT_ROWS_PER_STEP
    steps = n_rows // rps
    bias = _nat_bias_table(rpb)
    cb = COL_NAT // BR_W
    tq = rps * GRID_W
    return pl.pallas_call(
        functools.partial(_nat_kernel, rows_per_step=rps, n_rows=n_rows),
        grid=(batch, steps),
        in_specs=[pl.BlockSpec((tq, BR_W), lambda b, j: (b * steps + j, cb)),
                  pl.BlockSpec((seq, BR_W), lambda b, j: (b, cb + 1)),
                  pl.BlockSpec((seq, BR_W), lambda b, j: (b, cb + 2)),
                  pl.BlockSpec(bias.shape, lambda b, j: (0, 0, 0))],
        out_specs=pl.BlockSpec((tq, BR_W), lambda b, j: (b * steps + j, 0)),
        out_shape=jax.ShapeDtypeStruct((batch * seq, BR_W), BF16),
        compiler_params=_cparams("parallel", "arbitrary"),
        name="nat",
    )(p, p, p, bias)


def _diff_kernel(lam_ref, slope_ref, sub_ref, q_ref, k_ref, v_ref, o_ref,
                 acc1, acc2, m1, l1, m2, l2, hill, stage_a1, stage_a2, stage_b1, stage_b2, *, lam_init, n_tiles):
    t = q_ref.shape[0]
    reps = t // 128
    qi = pl.program_id(2)
    slope2 = slope_ref[0, 0:1, 0:1] * LOG2E
    lp = lam_ref[...]
    lam = (jnp.exp(jnp.sum(lp[0:1] * lp[1:2], axis=-1, keepdims=True))
           - jnp.exp(jnp.sum(lp[2:3] * lp[3:4], axis=-1, keepdims=True)) + lam_init)

    @pl.when(qi == 0)
    def _():
        rel = (lax.broadcasted_iota(jnp.int32, (t, t), 1) - lax.broadcasted_iota(jnp.int32, (t, t), 0)).astype(F32)
        hill[...] = -slope2 * jnp.abs(rel)

    q = (q_ref[...].astype(F32) * (DIFF_SCALE * LOG2E)).astype(BF16)
    lane = lax.broadcasted_iota(jnp.int32, q.shape, 1)
    q1 = jnp.where(lane < 64, q, jnp.zeros_like(q))
    q2 = jnp.where(lane >= 64, q, jnp.zeros_like(q))

    acc1[...] = jnp.zeros_like(acc1)
    acc2[...] = jnp.zeros_like(acc2)
    l1[...] = jnp.zeros_like(l1)
    l2[...] = jnp.zeros_like(l2)
    m1[...] = jnp.full_like(m1, -jnp.inf)
    m2[...] = jnp.full_like(m2, -jnp.inf)

    def scores(kj):
        k = k_ref[pl.ds(pl.multiple_of(kj * t, t), t), :]
        return _nt(q1, k), _nt(q2, k)

    def absorb(kj, raw_scores, add_local_bias, tile_bias):
        v = v_ref[pl.ds(pl.multiple_of(kj * t, t), t), :]
        for s_raw, acc, m_ref, l_ref in zip(raw_scores, (acc1, acc2), (m1, m2), (l1, l2)):
            s = add_local_bias(s_raw)
            m_old = m_ref[...]
            m_new = jnp.maximum(m_old, jnp.max(s, axis=-1, keepdims=True) + tile_bias)
            a = jnp.exp2(m_old - m_new)
            pr = jnp.exp2(s - jnp.concatenate([m_new - tile_bias] * reps, axis=1))
            l_ref[...] = a * l_ref[...] + jnp.sum(pr, axis=-1, keepdims=True)
            acc[...] = a * acc[...] + _dot(pr.astype(BF16), v)
            m_ref[...] = m_new

    stage = ((stage_a1, stage_a2), (stage_b1, stage_b2))
    query_term = slope2 * lax.broadcasted_iota(jnp.int32, (t, 128), 0).astype(F32)
    order = [qi] + [jnp.where(u < qi, u, u + 1) for u in range(n_tiles - 1)]
    for ref, val in zip(stage[0], scores(order[0])):
        ref[...] = val
    for u, kj in enumerate(order):
        cur, nxt = stage[u % 2], stage[(u + 1) % 2]
        if u + 1 < len(order):
            for ref, val in zip(nxt, scores(order[u + 1])):
                ref[...] = val
        raw = (cur[0][...], cur[1][...])
        if u == 0:
            diagonal = hill[...]
            absorb(kj, raw, lambda s: s + diagonal, jnp.zeros((1, 1), F32))
        else:
            sign = jnp.where(jnp.full((1, 1), kj, jnp.int32) < qi, 1.0, -1.0)
            keys = (sign * slope2) * ((kj - qi) * t + lax.broadcasted_iota(jnp.int32, (1, t), 1)).astype(F32)
            absorb(kj, raw, lambda s, keys=keys: s + keys, -sign * query_term)

    o = acc1[...] / l1[...] - lam * (acc2[...] / l2[...])
    o = o * lax.rsqrt(jnp.mean(o * o, axis=-1, keepdims=True) + EPS) * sub_ref[0]
    o_ref[...] = (o * (1.0 - lam_init)).astype(o_ref.dtype)


def diff_attention(p, lam_params, subln, layer, batch, seq):
    t = DIFF_TILE
    nt = seq // t
    lam_init = 0.8 - 0.6 * math.exp(-0.3 * layer)
    slopes = np.exp2(-8.0 * (np.arange(N_HEADS, dtype=np.float64) + 1.0) / N_HEADS)
    slope_tab = jnp.asarray(np.broadcast_to(slopes[:, None, None], (N_HEADS, 8, 128)).astype(np.float32))
    qc, kc, vc = (COL_DIFF // DIFF_DV, (COL_DIFF + DIFF_W) // DIFF_DV, (COL_DIFF + 2 * DIFF_W) // DIFF_DV)
    return pl.pallas_call(
        functools.partial(_diff_kernel, lam_init=lam_init, n_tiles=nt),
        grid=(batch, N_HEADS, nt),
        in_specs=[pl.BlockSpec(lam_params.shape, lambda b, h, i: (0, 0)),
                  pl.BlockSpec((1, 8, 128), lambda b, h, i: (h, 0, 0)),
                  pl.BlockSpec((1, 1, DIFF_DV), lambda b, h, i: (h, 0, 0)),
                  pl.BlockSpec((t, DIFF_DV), lambda b, h, i: (b * nt + i, qc + h)),
                  pl.BlockSpec((seq, DIFF_DV), lambda b, h, i: (b, kc + h)),
                  pl.BlockSpec((seq, DIFF_DV), lambda b, h, i: (b, vc + h))],
        out_specs=pl.BlockSpec((t, DIFF_DV), lambda b, h, i: (b * nt + i, h)),
        out_shape=jax.ShapeDtypeStruct((batch * seq, DIFF_W), BF16),
        scratch_shapes=[pltpu.VMEM((t, DIFF_DV), F32)] * 6 + [pltpu.VMEM((t, t), F32)] * 5,
        compiler_params=_cparams("parallel", "parallel", "arbitrary"),
        name="diff_attn",
    )(lam_params, slope_tab, subln.reshape(N_HEADS, 1, DIFF_DV), p, p, p)


def _merge_kernel(hf_ref, hb_ref, rof, rob, rg, ynat, hof, hob, hg, ydiff,
                  wg_ref, bg_ref, wr_ref, wn_ref, wh_ref, wd_ref, wo_ref,
                  rgn_ref, hgn_ref, lng_ref, lnb_ref, of_ref, ob_ref):
    seg_mean = jnp.where(_block_diag_mask(BR_W), 1.0 / HEAD_DIM, 0.0).astype(BF16)

    def head_norm_gate(o, gn, gate):
        ms = _dot_split(o * o, seg_mean, terms=2)
        gx = gate.astype(F32)
        return (o * lax.rsqrt(ms + EPS) * gn * (gx * _sigmoid(gx))).astype(BF16)

    y_ret = head_norm_gate(rof[...] + rob[...], rgn_ref[...], rg[...])
    y_hgrn = head_norm_gate(hof[...] + hob[...], hgn_ref[...], hg[...])
    hb = hb_ref[...]
    d = hf_ref.shape[1]
    merged = None
    for i, (y, w_ref) in enumerate(((y_ret, wr_ref), (ynat[...], wn_ref), (y_hgrn, wh_ref), (ydiff[...], wd_ref))):
        gate = _sigmoid(_dot(hb, wg_ref[:, i * d:(i + 1) * d]) + bg_ref[:, i * d:(i + 1) * d])
        part = gate * _dot(y, w_ref[...])
        merged = part if merged is None else merged + part
    z = ALPHA * hf_ref[...] + _dot(merged.astype(BF16), wo_ref[...])
    h = _layer_norm(z, lng_ref[...], lnb_ref[...])
    of_ref[...] = h
    ob_ref[...] = h.astype(BF16)


def merge_and_norm(hf, hb, p, ret_f, ret_b, y_nat, hg_f, hg_b, y_diff,
                   w_gate, b_gate, w_br_ret, w_br_nat, w_br_hgrn, w_br_diff, w_out,
                   ret_gn, hgrn_gn, ln_g, ln_b, tm=512):
    n, d = hf.shape

    def rows(width, col=0):
        return pl.BlockSpec((tm, width), lambda i: (i, col))

    def whole(a):
        nd = a.ndim
        return pl.BlockSpec(a.shape, lambda i: (0,) * nd)

    consts = [w_gate, b_gate.reshape(1, -1), w_br_ret, w_br_nat, w_br_hgrn, w_br_diff, w_out,
              ret_gn.reshape(1, BR_W), hgrn_gn.reshape(1, BR_W), ln_g.reshape(1, d), ln_b.reshape(1, d)]
    return pl.pallas_call(
        _merge_kernel,
        grid=(n // tm,),
        in_specs=[rows(d), rows(d),
                  rows(BR_W), rows(BR_W), rows(BR_W, COL_RET // BR_W + 3),
                  rows(BR_W),
                  rows(BR_W), rows(BR_W), rows(BR_W, COL_HGRN // BR_W + 4),
                  rows(DIFF_W)] + [whole(a) for a in consts],
        out_specs=[rows(d), rows(d)],
        out_shape=[jax.ShapeDtypeStruct((n, d), F32), jax.ShapeDtypeStruct((n, d), BF16)],
        compiler_params=_cparams("parallel"),
        name="merge",
    )(hf, hb, ret_f, ret_b, p, y_nat, hg_f, hg_b, p, y_diff, *consts)


def _route(h, w, bias):
    w_hi = w.astype(BF16)
    w_lo = (w - w_hi.astype(F32)).astype(BF16)
    logits = _dot_split(h, w_hi, terms=3) + _dot_split(h, w_lo, terms=2)
    scores = _sigmoid(logits)
    biased = scores + bias
    shape = scores.shape
    lane = lax.broadcasted_iota(jnp.int32, shape, 1).astype(F32)
    group = lax.shift_right_logical(lax.broadcasted_iota(jnp.int32, shape, 1), 3).astype(F32)
    big = jnp.asarray(1e9, F32)
    neg = jnp.asarray(-jnp.inf, F32)

    def first_max(x, ids):
        m = jnp.max(x, axis=-1, keepdims=True)
        return m, jnp.min(jnp.where(x == m, ids, big), axis=-1, keepdims=True)

    gscore = jnp.zeros(shape, F32)
    for g in range(N_GROUPS):
        xg = jnp.where(group == g, biased, neg)
        m1, i1 = first_max(xg, lane)
        m2 = jnp.max(jnp.where(lane == i1, neg, xg), axis=-1, keepdims=True)
        gscore = jnp.where(group == g, m1 + m2, gscore)
    gsel = jnp.zeros(shape, jnp.bool_)
    for _ in range(TOPK_GROUPS):
        _, gi = first_max(gscore, group)
        hit = group == gi
        gsel = gsel | hit
        gscore = jnp.where(hit, neg, gscore)
    cand = jnp.where(gsel, biased, MASK_VALUE)
    esel = jnp.zeros(shape, jnp.bool_)
    picks = []
    for _ in range(TOP_K):
        _, ei = first_max(cand, lane)
        hit = lane == ei
        picks.append((ei, jnp.sum(jnp.where(hit, scores, 0.0), axis=-1, keepdims=True)))
        esel = esel | hit
        cand = jnp.where(hit, neg, cand)

    tm = shape[0]
    sel = jnp.where(esel, 1.0, 0.0)
    counts = jnp.concatenate(
        [jnp.sum(sel[s * ROUTE_TILE:(s + 1) * ROUTE_TILE], axis=0, keepdims=True) for s in range(tm // ROUTE_TILE)],
        axis=0)

    slot = lax.broadcasted_iota(jnp.int32, (tm, TOP_K), 1)
    idx8 = jnp.zeros((tm, TOP_K), F32)
    w8 = jnp.zeros((tm, TOP_K), F32)
    wsum = jnp.zeros((tm, 1), F32)
    for k, (ei, wk) in enumerate(picks):
        idx8 = jnp.where(slot == k, ei, idx8)
        w8 = jnp.where(slot == k, wk, w8)
        wsum = wsum + wk
    return idx8.astype(jnp.int32), w8 / (wsum + 1e-20) * ROUTED_SCALE, counts


def _router_kernel(h_ref, w_ref, bias_ref, idx_ref, w8_ref, cnt_ref):
    idx_ref[...], w8_ref[...], cnt_ref[...] = _route(h_ref[...], w_ref[...], bias_ref[...])


def router(hf, w_router, router_bias, tm=1024):
    n, d = hf.shape
    e = w_router.shape[1]
    sub = tm // ROUTE_TILE
    return pl.pallas_call(
        _router_kernel,
        grid=(n // tm,),
        in_specs=[pl.BlockSpec((tm, d), lambda i: (i, 0)),
                  pl.BlockSpec((d, e), lambda i: (0, 0)),
                  pl.BlockSpec((1, e), lambda i: (0, 0))],
        out_specs=[pl.BlockSpec((tm, TOP_K), lambda i: (i, 0)),
                   pl.BlockSpec((tm, TOP_K), lambda i: (i, 0)),
                   pl.BlockSpec((sub, e), lambda i: (i, 0))],
        out_shape=[jax.ShapeDtypeStruct((n, TOP_K), jnp.int32),
                   jax.ShapeDtypeStruct((n, TOP_K), F32),
                   jax.ShapeDtypeStruct((n // ROUTE_TILE, e), F32)],
        compiler_params=_cparams("arbitrary"),
        name="router",
    )(hf, w_router, router_bias.reshape(1, e))


HIGH_HALF = -65536


def _pack_pairs(x):
    w = x.shape[1] // 2
    lo = lax.bitcast_convert_type(x[:, :w].astype(BF16).astype(F32), jnp.int32)
    hi = lax.bitcast_convert_type(x[:, w:].astype(BF16).astype(F32), jnp.int32)
    return lax.shift_right_logical(lo, 16) | (hi & HIGH_HALF)


def _unpack_pairs(p):
    lo = lax.bitcast_convert_type(lax.shift_left(p, 16), F32)
    hi = lax.bitcast_convert_type(p & HIGH_HALF, F32)
    return jnp.concatenate([lo, hi], axis=1)


def _swiglu(x, wg, wu):
    a = _dot(x, wg)
    return a * _sigmoid(a) * _dot(x, wu)


def _segment_copies(tab_ref, first_col, max_run, make_copy, action):
    top_bit = max_run.bit_length() - 1
    low_bit = RUN_ALIGN.bit_length() - 1
    mid_bit = min(top_bit + 1, LONG_RUN.bit_length() - 1)

    def pieces(count, local, glob, bits):
        for b in bits:
            size = 1 << b
            taken = count & size

            @pl.when(taken != 0)
            def _(local=local, glob=glob, size=size):
                action(make_copy(pl.multiple_of(local, RUN_ALIGN), pl.multiple_of(glob, RUN_ALIGN), size))

            local = local + taken
            glob = glob + taken

    def per_expert(e, carry):
        col = first_col + e
        count = tab_ref[0, col]
        local = tab_ref[1, col]
        glob = tab_ref[2, col]
        long_part = count & -LONG_RUN

        @pl.when(long_part != 0)
        def _():
            pieces(count, local, glob, range(top_bit, mid_bit - 1, -1))

        pieces(count, local + long_part, glob + long_part, range(mid_bit - 1, low_bit - 1, -1))
        return carry

    lax.fori_loop(0, N_EXPERTS, per_expert, 0)


def _tile_rows_used(tab_ref, t):
    last = t * N_EXPERTS + (N_EXPERTS - 1)
    return tab_ref[1, last] + tab_ref[0, last]


def _wait_rows(total, max_rows, make_copy):
    for b in range(max_rows.bit_length() - 1, RUN_ALIGN.bit_length() - 2, -1):
        size = 1 << b

        @pl.when((total & size) != 0)
        def _(size=size):
            make_copy(0, 0, size).wait()


def _block_dispatch_kernel(tab_ref, gap_ref, tail_ref, idx_t_ref, hb_ref, xs_ref, sorted_buf, zeros_buf, sem):
    t = pl.program_id(0)
    tm = hb_ref.shape[0]
    rows = sorted_buf.shape[0]
    idx_t = idx_t_ref[...]
    expert = lax.broadcasted_iota(jnp.int32, (N_EXPERTS, tm), 0)
    sel_t = jnp.zeros((N_EXPERTS, tm), F32)
    for k in range(TOP_K):
        sel_t = sel_t + jnp.where(idx_t[k:k + 1, :] == expert, 1.0, 0.0)
    sel_b = sel_t.astype(BF16)
    before_tok = (lax.broadcasted_iota(jnp.int32, (tm, tm), 0) < lax.broadcasted_iota(jnp.int32, (tm, tm), 1))
    before_exp = (lax.broadcasted_iota(jnp.int32, (N_EXPERTS, N_EXPERTS), 1)
                  < lax.broadcasted_iota(jnp.int32, (N_EXPERTS, N_EXPERTS), 0))
    run = jnp.broadcast_to(_run_length(jnp.sum(sel_t, axis=1, keepdims=True)), (N_EXPERTS, tm)).astype(BF16)
    base = (_dot(jnp.where(before_exp, 1.0, 0.0).astype(BF16), run)
            + _dot(sel_b, jnp.where(before_tok, 1.0, 0.0).astype(BF16)))
    row = lax.broadcasted_iota(jnp.int32, (rows, tm), 0).astype(jnp.int16)
    perm = jnp.zeros((rows, tm), BF16)
    for k in range(TOP_K):
        place = jnp.sum(jnp.where(idx_t[k:k + 1, :] == expert, base, 0.0), axis=0, keepdims=True)
        perm = jnp.where(row == place.astype(jnp.int32).astype(jnp.int16), jnp.ones_like(perm), perm)
    half = hb_ref.shape[1] // 2
    lo = lax.bitcast_convert_type(_dot(perm, hb_ref[:, :half]), jnp.int32)
    hi = lax.bitcast_convert_type(_dot(perm, hb_ref[:, half:]), jnp.int32)
    sorted_buf[...] = lax.shift_right_logical(lo, 16) | (hi & HIGH_HALF)

    def make_copy(local, glob, size):
        return pltpu.make_async_copy(sorted_buf.at[pl.ds(local, size)], xs_ref.at[pl.ds(glob, size)], sem.at[0])

    _segment_copies(tab_ref, t * N_EXPERTS, tm, make_copy, lambda c: c.start())
    _wait_rows(_tile_rows_used(tab_ref, t), rows, make_copy)

    @pl.when(t == pl.num_programs(0) - 1)
    def _():
        zeros_buf[...] = jnp.zeros_like(zeros_buf)
        tile_rows = zeros_buf.shape[0]

        def zero_copy(local, glob, size):
            return pltpu.make_async_copy(zeros_buf.at[pl.ds(0, size)], xs_ref.at[pl.ds(glob, size)], sem.at[0])

        def tail_copy(j):
            return zero_copy(0, pl.multiple_of(tail_ref[0] + j * tile_rows, RUN_ALIGN), tile_rows)

        _segment_copies(gap_ref, 0, tile_rows // 2, zero_copy, lambda c: c.start())
        lax.fori_loop(0, tail_ref[1], lambda j, c: (tail_copy(j).start(), c)[1], 0)
        _segment_copies(gap_ref, 0, tile_rows // 2, zero_copy, lambda c: c.wait())
        lax.fori_loop(0, tail_ref[1], lambda j, c: (tail_copy(j).wait(), c)[1], 0)


def block_dispatch(hb, idx_t, tab, gaps, tail, tm):
    n, d = hb.shape
    grid_spec = pltpu.PrefetchScalarGridSpec(
        num_scalar_prefetch=3,
        grid=(n // tm,),
        in_specs=[pl.BlockSpec((TOP_K, tm), lambda i, *_: (0, i)),
                  pl.BlockSpec((tm, d), lambda i, *_: (i, 0))],
        out_specs=pl.BlockSpec(memory_space=pl.ANY),
        scratch_shapes=[pltpu.VMEM((_tile_rows(tm), d // 2), jnp.int32),
                        pltpu.VMEM((EXPERT_ROWS, d // 2), jnp.int32),
                        pltpu.SemaphoreType.DMA((1,))],
    )
    return pl.pallas_call(
        _block_dispatch_kernel,
        grid_spec=grid_spec,
        out_shape=jax.ShapeDtypeStruct((_sorted_rows(n, tm), d // 2), jnp.int32),
        compiler_params=_cparams("arbitrary"),
        name="dispatch",
    )(tab, gaps, tail, idx_t, hb)


def _tile_tables(tile_counts, tm_fine, tm):
    g = tm // tm_fine
    n_tokens = tile_counts.shape[0] * tm_fine
    n_tiles = _sorted_rows(n_tokens, tm) // EXPERT_ROWS
    cnt = tile_counts.astype(jnp.int32).reshape(-1, g, N_EXPERTS).sum(axis=1)
    cnt = (cnt + (RUN_ALIGN - 1)) // RUN_ALIGN * RUN_ALIGN
    local = jnp.cumsum(cnt, axis=1) - cnt
    used = cnt.sum(axis=0)
    seg = (used + (EXPERT_ROWS - 1)) // EXPERT_ROWS * EXPERT_ROWS
    seg_end = jnp.cumsum(seg)
    seg_start = seg_end - seg
    glob = seg_start[None, :] + jnp.cumsum(cnt, axis=0) - cnt
    runs = jnp.stack([cnt.reshape(-1), local.reshape(-1), glob.reshape(-1)])
    gaps = jnp.stack([seg - used, jnp.zeros_like(used), seg_start + used])
    tail = jnp.stack([seg_end[-1], n_tiles - seg_end[-1] // EXPERT_ROWS])
    tile_row = jnp.arange(n_tiles, dtype=jnp.int32) * EXPERT_ROWS
    owner = jnp.sum((seg_end[None, :] <= tile_row[:, None]).astype(jnp.int32), axis=1)
    owner_c = jnp.minimum(owner, N_EXPERTS - 1)
    of_owner = lambda a: jnp.sum(jnp.where(jnp.arange(N_EXPERTS)[None, :] == owner_c[:, None], a[None, :], 0), axis=1)
    filled = jnp.clip(of_owner(seg_start + used) - tile_row, 0, EXPERT_ROWS)
    tiles = jnp.stack([owner_c, jnp.where(owner < N_EXPERTS, filled, 0)])
    return runs, gaps, tail, tiles.astype(jnp.int32)


def _expert_kernel(tile_ref, xs_ref, wg_ref, wu_ref, wd_ref, ys_ref, wgb, wub, wdb):
    i = pl.program_id(0)
    expert = tile_ref[0, i]

    @pl.when(tile_ref[1, i] > 0)
    def _():
        @pl.when((i == 0) | (expert != tile_ref[0, jnp.maximum(i - 1, 0)]))
        def _():
            wgb[...] = wg_ref[0, 0].astype(BF16)
            wub[...] = wu_ref[0, 0].astype(BF16)
            wdb[...] = wd_ref[0, 0].astype(BF16)

        x = _unpack_pairs(xs_ref[...]).astype(BF16)
        hid = _swiglu(x, wgb[...], wub[...])
        ys_ref[...] = _pack_pairs(_dot(hid.astype(BF16), wdb[...]))

    @pl.when(tile_ref[1, i] == 0)
    def _():
        ys_ref[...] = jnp.zeros_like(ys_ref)


def grouped_experts(xs, tiles, w_e_gate, w_e_up, w_e_down, layer):
    rows, w = xs.shape
    _, _, d, hid = w_e_gate.shape
    tr = EXPERT_ROWS
    grid_spec = pltpu.PrefetchScalarGridSpec(
        num_scalar_prefetch=1,
        grid=(rows // tr,),
        in_specs=[pl.BlockSpec((tr, w), lambda i, tl: (i, 0)),
                  pl.BlockSpec((1, 1, d, hid), lambda i, tl: (layer, tl[0, i], 0, 0)),
                  pl.BlockSpec((1, 1, d, hid), lambda i, tl: (layer, tl[0, i], 0, 0)),
                  pl.BlockSpec((1, 1, hid, d), lambda i, tl: (layer, tl[0, i], 0, 0))],
        out_specs=pl.BlockSpec((tr, w), lambda i, tl: (i, 0)),
        scratch_shapes=[pltpu.VMEM((d, hid), BF16), pltpu.VMEM((d, hid), BF16), pltpu.VMEM((hid, d), BF16)],
    )
    return pl.pallas_call(
        _expert_kernel,
        grid_spec=grid_spec,
        out_shape=jax.ShapeDtypeStruct((rows, w), jnp.int32),
        compiler_params=_cparams("arbitrary"),
        name="experts",
    )(tiles, xs, w_e_gate, w_e_up, w_e_down)


def _block_combine_kernel(tab_ref, ys_ref, idx_ref, w8_ref, hf_ref, hb_ref, sg_ref, su_ref, sd_ref,
                          lng_ref, lnb_ref, of_ref, ob_ref, buf, sem):
    t = pl.program_id(0)
    tm = hf_ref.shape[0]
    rows = buf.shape[0]

    def make_copy(local, glob, size):
        return pltpu.make_async_copy(ys_ref.at[pl.ds(glob, size)], buf.at[pl.ds(local, size)], sem.at[0])

    @pl.when(t == 0)
    def _():
        buf[...] = jnp.zeros_like(buf)

    _segment_copies(tab_ref, t * N_EXPERTS, tm, make_copy, lambda c: c.start())

    idx = idx_ref[...]
    w8 = w8_ref[...]
    expert = lax.broadcasted_iota(jnp.int32, (tm, N_EXPERTS), 1)
    sel = jnp.zeros((tm, N_EXPERTS), F32)
    for k in range(TOP_K):
        sel = sel + jnp.where(idx[:, k:k + 1] == expert, 1.0, 0.0)
    sel_b = sel.astype(BF16)
    before_tok = (lax.broadcasted_iota(jnp.int32, (tm, tm), 1) < lax.broadcasted_iota(jnp.int32, (tm, tm), 0))
    before_exp = (lax.broadcasted_iota(jnp.int32, (N_EXPERTS, N_EXPERTS), 0)
                  < lax.broadcasted_iota(jnp.int32, (N_EXPERTS, N_EXPERTS), 1))
    run = jnp.broadcast_to(_run_length(jnp.sum(sel, axis=0, keepdims=True)), (tm, N_EXPERTS)).astype(BF16)
    base = (_dot(run, jnp.where(before_exp, 1.0, 0.0).astype(BF16))
            + _dot(jnp.where(before_tok, 1.0, 0.0).astype(BF16), sel_b))
    col = lax.broadcasted_iota(jnp.int32, (tm, rows), 1).astype(jnp.int16)
    w8_b = w8.astype(BF16)
    mix = jnp.zeros((tm, rows), BF16)
    for k in range(TOP_K):
        place = jnp.sum(jnp.where(idx[:, k:k + 1] == expert, base, 0.0), axis=1, keepdims=True)
        mix = jnp.where(col == place.astype(jnp.int32).astype(jnp.int16),
                        jnp.broadcast_to(w8_b[:, k:k + 1], mix.shape), mix)

    acc = _dot(_swiglu(hb_ref[...], sg_ref[...], su_ref[...]).astype(BF16), sd_ref[...])
    _wait_rows(_tile_rows_used(tab_ref, t), rows, make_copy)
    y = _unpack_pairs(buf[...]).astype(BF16)
    acc = acc + _dot(mix, y)
    h = _layer_norm(ALPHA * hf_ref[...] + acc, lng_ref[...], lnb_ref[...])
    of_ref[...] = h
    ob_ref[...] = h.astype(BF16)


def block_combine(ys, tab, idx8, w8, hf, hb, w_s_gate, w_s_up, w_s_down, ln_g, ln_b, tm):
    n, d = hf.shape
    w = ys.shape[1]

    def whole(a):
        return pl.BlockSpec(a.shape, lambda i, tab: (0, 0))

    def rows(width):
        return pl.BlockSpec((tm, width), lambda i, tab: (i, 0))

    consts = [w_s_gate, w_s_up, w_s_down, ln_g.reshape(1, d), ln_b.reshape(1, d)]
    grid_spec = pltpu.PrefetchScalarGridSpec(
        num_scalar_prefetch=1,
        grid=(n // tm,),
        in_specs=[pl.BlockSpec(memory_space=pl.ANY), rows(TOP_K), rows(TOP_K), rows(d), rows(d)]
                 + [whole(a) for a in consts],
        out_specs=[rows(d), rows(d)],
        scratch_shapes=[pltpu.VMEM((_tile_rows(tm), w), jnp.int32), pltpu.SemaphoreType.DMA((1,))],
    )
    return pl.pallas_call(
        _block_combine_kernel,
        grid_spec=grid_spec,
        out_shape=[jax.ShapeDtypeStruct((n, d), F32), jax.ShapeDtypeStruct((n, d), BF16)],
        compiler_params=_cparams("arbitrary"),
        name="combine",
    )(tab, ys, idx8, w8, hf, hb, *consts)


def moe_and_norm(hf, hb, w_router, router_bias, w_e_gate, w_e_up, w_e_down,
                 w_s_gate, w_s_up, w_s_down, ln_g, ln_b, layer):
    idx8, w8, tile_counts = router(hf, w_router, router_bias)
    runs, gaps, tail, tiles = _tile_tables(tile_counts, ROUTE_TILE, MOE_TILE)
    xs = block_dispatch(hb, idx8.T, runs, gaps, tail, MOE_TILE)
    ys = grouped_experts(xs, tiles, w_e_gate, w_e_up, w_e_down, layer)
    return block_combine(ys, runs, idx8, w8, hf, hb, w_s_gate, w_s_up, w_s_down, ln_g, ln_b, MOE_TILE)


def kernel(x, ln_in_g, ln_in_b, w_in, w_gate, b_gate, w_br_ret, w_br_nat, w_br_hgrn, w_br_diff, w_out,
           ret_gn, nat_rpb, hgrn_lb, hgrn_gn, diff_lambda, diff_subln, ln1_g, ln1_b, w_router,
           router_bias, w_e_gate, w_e_up, w_e_down, w_s_gate, w_s_up, w_s_down, ln2_g, ln2_b):
    batch, seq, d = x.shape
    bf = lambda a: a.astype(BF16)
    hf, hb = input_layer_norm(x.reshape(batch * seq, d), ln_in_g, ln_in_b)
    for l in range(DEPTH):
        p = in_projection(hb, bf(w_in[l]))
        ret_f, ret_b = retention(p, batch, seq)
        y_nat = neighbourhood_attention(p, nat_rpb[l], batch, seq)
        hg_f, hg_b = hgrn(p, hgrn_lb, l, batch, seq)
        y_diff = diff_attention(p, diff_lambda[l], diff_subln[l], l, batch, seq)
        hf, hb = merge_and_norm(hf, hb, p, ret_f, ret_b, y_nat, hg_f, hg_b, y_diff,
                                bf(w_gate[l]), b_gate[l], bf(w_br_ret[l]), bf(w_br_nat[l]),
                                bf(w_br_hgrn[l]), bf(w_br_diff[l]), bf(w_out[l]),
                                ret_gn[l], hgrn_gn[l], ln1_g[l], ln1_b[l])
        hf, hb = moe_and_norm(hf, hb, w_router[l], router_bias[l], w_e_gate, w_e_up, w_e_down,
                              bf(w_s_gate[l]), bf(w_s_up[l]), bf(w_s_down[l]), ln2_g[l], ln2_b[l], l)
    return hf.reshape(batch, seq, d)
```

```python
import functools
import math

import numpy as np
import jax
import jax.numpy as jnp
from jax import lax
from jax.experimental import pallas as pl
from jax.experimental.pallas import tpu as pltpu

F32 = jnp.float32
BF16 = jnp.bfloat16

DEPTH = 2
GRID_W = 64
HEAD_DIM = 64
N_HEADS = 4
BR_W = N_HEADS * HEAD_DIM
NAT_KH = 8
NAT_KW = 16
DIFF_DV = 128
DIFF_W = N_HEADS * DIFF_DV
N_EXPERTS = 64
TOP_K = 8
N_GROUPS = 8
TOPK_GROUPS = 4
GROUP_SIZE = N_EXPERTS // N_GROUPS
ROUTED_SCALE = 2.5
EPS = 1e-5
MASK_VALUE = -1e30
LOG2E = math.log2(math.e)
DIFF_SCALE = 64 ** -0.5
ALPHA = (2.0 * DEPTH) ** 0.25
COL_RET = 0
COL_NAT = 1024
COL_HGRN = 1792
COL_DIFF = 3072

VMEM_LIMIT = 56 * 1024 * 1024

RET_CHUNK = 256
HGRN_CHUNK = 128
HGRN_SUB = 16
DIFF_TILE = 1024
NAT_ROWS_PER_STEP = 8
ROUTE_TILE = 128
MOE_TILE = 256
EXPERT_ROWS = 512
RUN_ALIGN = 8
LONG_RUN = 64


def _run_length(count):
    return jnp.floor((count + (RUN_ALIGN - 1)) * (1.0 / RUN_ALIGN)) * RUN_ALIGN


def _tile_rows(tm):
    bound = TOP_K * tm + N_EXPERTS * (RUN_ALIGN - 1)
    return -(-bound // 128) * 128


def _sorted_rows(n_tokens, tm):
    bound = (TOP_K * n_tokens + (n_tokens // tm) * N_EXPERTS * (RUN_ALIGN - 1)
             + N_EXPERTS * (EXPERT_ROWS - RUN_ALIGN))
    return -(-bound // EXPERT_ROWS) * EXPERT_ROWS


def _cparams(*sem):
    return pltpu.CompilerParams(dimension_semantics=sem, vmem_limit_bytes=VMEM_LIMIT)


def _nt(a, b):
    return lax.dot_general(a, b, (((1,), (1,)), ((), ())), preferred_element_type=F32)


def _tn(a, b):
    return lax.dot_general(a, b, (((0,), (0,)), ((), ())), preferred_element_type=F32)


def _dot(a, b):
    return jnp.dot(a, b, preferred_element_type=F32)


def _dot_split(x, w_bf16, terms=3):
    acc = None
    rem = x
    for _ in range(terms):
        piece = rem.astype(BF16)
        part = _dot(piece, w_bf16)
        acc = part if acc is None else acc + part
        rem = rem - piece.astype(F32)
    return acc


def _layer_norm(z, g, b):
    mu = jnp.mean(z, axis=-1, keepdims=True)
    zc = z - mu
    var = jnp.mean(zc * zc, axis=-1, keepdims=True)
    return zc * lax.rsqrt(var + EPS) * g + b


def _sigmoid(x):
    return 1.0 / (1.0 + jnp.exp(-x))


def _head_id(shape, axis):
    return lax.shift_right_logical(lax.broadcasted_iota(jnp.int32, shape, axis), HEAD_DIM.bit_length() - 1)


def _block_diag_mask(n):
    return _head_id((n, n), 0) == _head_id((n, n), 1)


def _ln_kernel(x_ref, g_ref, b_ref, hf_ref, hb_ref):
    h = _layer_norm(x_ref[...], g_ref[...], b_ref[...])
    hf_ref[...] = h
    hb_ref[...] = h.astype(BF16)


def input_layer_norm(x2, g, b, tm=1024):
    n, d = x2.shape
    return pl.pallas_call(
        _ln_kernel,
        grid=(n // tm,),
        in_specs=[pl.BlockSpec((tm, d), lambda i: (i, 0)),
                  pl.BlockSpec((1, d), lambda i: (0, 0)),
                  pl.BlockSpec((1, d), lambda i: (0, 0))],
        out_specs=[pl.BlockSpec((tm, d), lambda i: (i, 0)),
                   pl.BlockSpec((tm, d), lambda i: (i, 0))],
        out_shape=[jax.ShapeDtypeStruct((n, d), F32), jax.ShapeDtypeStruct((n, d), BF16)],
        compiler_params=_cparams("parallel"),
        name="input_ln",
    )(x2, g.reshape(1, d), b.reshape(1, d))


def _matmul_kernel(h_ref, w_ref, o_ref):
    o_ref[...] = _dot(h_ref[...], w_ref[...]).astype(o_ref.dtype)


def in_projection(hb, w_bf16, tm=2048, tn=768):
    n, d = hb.shape
    w = w_bf16.shape[1]
    return pl.pallas_call(
        _matmul_kernel,
        grid=(n // tm, w // tn),
        in_specs=[pl.BlockSpec((tm, d), lambda i, j: (i, 0)),
                  pl.BlockSpec((d, tn), lambda i, j: (0, j))],
        out_specs=pl.BlockSpec((tm, tn), lambda i, j: (i, j)),
        out_shape=jax.ShapeDtypeStruct((n, w), BF16),
        compiler_params=_cparams("parallel", "arbitrary"),
        name="in_proj",
    )(hb, w_bf16)


def _retention_tables(c):
    idx = np.arange(N_HEADS, dtype=np.float64)
    lg = [np.log1p(-np.exp2(-5.0 - 2.0 * idx)), np.log1p(-np.exp2(-6.0 - 2.0 * idx))]
    i = np.arange(c, dtype=np.float64)
    diff = i[:, None] - i[None, :]
    lane_head = np.repeat(np.arange(N_HEADS), HEAD_DIM)
    dmat = np.zeros((2, N_HEADS, c, c), np.float32)
    oscale = np.zeros((2, c, BR_W), np.float32)
    kscale = np.zeros((2, c, BR_W), np.float32)
    sdecay = np.zeros((2, 1, BR_W), np.float32)
    for h in range(N_HEADS):
        dmat[0, h] = np.where(diff >= 0, np.exp(lg[0][h] * np.maximum(diff, 0)), 0.0)
        dmat[1, h] = np.where(diff <= 0, np.exp(lg[1][h] * np.maximum(-diff, 0)), 0.0)
    oscale[0] = np.exp(lg[0][lane_head][None, :] * (i[:, None] + 1.0))
    oscale[1] = np.exp(lg[1][lane_head][None, :] * (c - i[:, None]))
    kscale[0] = np.exp(lg[0][lane_head][None, :] * (c - 1.0 - i[:, None]))
    kscale[1] = np.exp(lg[1][lane_head][None, :] * i[:, None])
    sdecay[0, 0] = np.exp(lg[0][lane_head] * c)
    sdecay[1, 0] = np.exp(lg[1][lane_head] * c)
    return (jnp.asarray(dmat), jnp.asarray(oscale), jnp.asarray(kscale), jnp.asarray(sdecay))


def _retention_kernel(qf, kf, vf, qb, kb, vb, dmat, oscale, kscale, sdecay, of_ref, ob_ref, state):
    @pl.when(pl.program_id(1) == 0)
    def _():
        state[...] = jnp.zeros_like(state)

    c = qf.shape[0]
    lane_head = _head_id((c, BR_W), 1)
    bd = _block_diag_mask(BR_W)
    for dirn, (q_ref, k_ref, v_ref, o_ref) in enumerate(((qf, kf, vf, of_ref), (qb, kb, vb, ob_ref))):
        q = q_ref[...]
        k = k_ref[...] * jnp.asarray(HEAD_DIM ** -0.5, BF16)
        v = v_ref[...]
        s_old = state[dirn]
        out = _dot(q, s_old.astype(BF16)) * oscale[dirn]
        for h in range(N_HEADS):
            qz = jnp.where(lane_head == h, q, jnp.zeros_like(q))
            a = (_nt(qz, k) * dmat[dirn, h]).astype(BF16)
            out = out + jnp.where(lane_head == h, _dot(a, v), 0.0)
        o_ref[...] = out
        kw = (k.astype(F32) * kscale[dirn]).astype(BF16)
        state[dirn] = s_old * sdecay[dirn] + jnp.where(bd, _tn(kw, v), 0.0)


def retention(p, batch, seq):
    c = RET_CHUNK
    nc = seq // c
    dmat, oscale, kscale, sdecay = _retention_tables(c)
    cb = COL_RET // BR_W

    def fwd(col):
        return pl.BlockSpec((c, BR_W), lambda b, j: (b * nc + j, col))

    def bwd(col):
        return pl.BlockSpec((c, BR_W), lambda b, j: (b * nc + nc - 1 - j, col))

    def whole(a):
        nd = a.ndim
        return pl.BlockSpec(a.shape, lambda b, j: (0,) * nd)

    n = batch * seq
    return pl.pallas_call(
        _retention_kernel,
        grid=(batch, nc),
        in_specs=[fwd(cb), fwd(cb + 1), fwd(cb + 2), bwd(cb), bwd(cb + 1), bwd(cb + 2),
                  whole(dmat), whole(oscale), whole(kscale), whole(sdecay)],
        out_specs=[pl.BlockSpec((c, BR_W), lambda b, j: (b * nc + j, 0)),
                   pl.BlockSpec((c, BR_W), lambda b, j: (b * nc + nc - 1 - j, 0))],
        out_shape=[jax.ShapeDtypeStruct((n, BR_W), F32)] * 2,
        scratch_shapes=[pltpu.VMEM((2, BR_W, BR_W), F32)],
        compiler_params=_cparams("parallel", "arbitrary"),
        name="retention",
    )(p, p, p, p, p, p, dmat, oscale, kscale, sdecay)


def _hgrn_kernel(lb_ref, qf, ff, vf, qb, fb, vb, of_ref, ob_ref, state, *, layer):
    c = qf.shape[0]
    sb = HGRN_SUB
    n_sub = c // sb

    @pl.when(pl.program_id(1) == 0)
    def _():
        state[...] = jnp.zeros_like(state)

    lb = lb_ref[...]
    e = jnp.exp(lb - jnp.max(lb, axis=0, keepdims=True))
    prob = e / jnp.sum(e, axis=0, keepdims=True)
    lower = jnp.sum(prob[:layer + 1], axis=0) - prob[0]

    row = lax.broadcasted_iota(jnp.int32, (c, c), 0)
    col = lax.broadcasted_iota(jnp.int32, (c, c), 1)
    bd = _block_diag_mask(BR_W)
    seg_ones = jnp.where(bd, 1.0, 0.0).astype(BF16)
    rowv = lax.broadcasted_iota(jnp.int32, (c, BR_W), 0)

    for dirn, (q_ref, f_ref, v_ref, o_ref) in enumerate(((qf, ff, vf, of_ref), (qb, fb, vb, ob_ref))):
        rev = dirn == 1
        lo = lower[dirn:dirn + 1, :]
        fpre = f_ref[...].astype(F32)
        logf = jnp.log(lo + (1.0 - lo) * _sigmoid(fpre))
        kk = (1.0 - lo) * _sigmoid(-fpre)
        qx = q_ref[...].astype(F32)
        q = qx * _sigmoid(qx)
        v_bf = v_ref[...]
        v = v_bf.astype(F32)
        tri = jnp.where((col >= row) if rev else (col <= row), 1.0, 0.0).astype(BF16)
        g = _tri_cumsum(tri, logf)
        g_end = g[0:1, :] if rev else g[c - 1:c, :]
        s_old = state[dirn]
        inter = _nt((q * jnp.exp(g)).astype(BF16), s_old.astype(BF16))
        kd = (kk * jnp.exp(g_end - g)).astype(BF16)
        state[dirn] = s_old * jnp.exp(g_end) + jnp.where(bd, _tn(v_bf, kd), 0.0)
        qs, ks_ = [], []
        for j in range(1, n_sub):
            if rev:
                edge = g[j * sb:j * sb + 1, :]
                key_rows = (rowv >= j * sb) & (rowv < (j + 1) * sb)
                query_rows = rowv < j * sb
            else:
                edge = g[j * sb - 1:j * sb, :]
                key_rows = (rowv >= (j - 1) * sb) & (rowv < j * sb)
                query_rows = rowv >= j * sb
            qs.append(jnp.where(query_rows, q * jnp.exp(jnp.minimum(g - edge, 0.0)), 0.0).astype(BF16))
            ks_.append(jnp.where(key_rows, kk * jnp.exp(jnp.minimum(edge - g, 0.0)), 0.0).astype(BF16))
        bd_all = jnp.concatenate([bd] * (n_sub - 1), axis=1)
        cross = jnp.where(bd_all, _tn(v_bf, jnp.concatenate(ks_, axis=1)), 0.0).astype(BF16)
        intra = _nt(jnp.concatenate(qs, axis=1), cross)
        for d in range(sb):
            shift = (c - d) % c if rev else d
            pair = ((rowv % sb) + d < sb) if rev else ((rowv % sb) >= d)
            k_d, g_d, v_d = (kk, g, v) if d == 0 else tuple(pltpu.roll(a, shift, 0) for a in (kk, g, v))
            term = jnp.where(pair, q, 0.0) * k_d * jnp.exp(jnp.minimum(g - g_d, 0.0))
            intra = intra + _dot(term.astype(BF16), seg_ones) * v_d
        o_ref[...] = inter + intra


def _tri_cumsum(tri_bf16, x):
    acc = None
    rem = x
    for _ in range(3):
        piece = rem.astype(BF16)
        part = _dot(tri_bf16, piece)
        acc = part if acc is None else acc + part
        rem = rem - piece.astype(F32)
    return acc


def hgrn(p, hgrn_lb, layer, batch, seq):
    c = HGRN_CHUNK
    nc = seq // c
    cb = COL_HGRN // BR_W

    def fwd(col):
        return pl.BlockSpec((c, BR_W), lambda b, j: (b * nc + j, col))

    def bwd(col):
        return pl.BlockSpec((c, BR_W), lambda b, j: (b * nc + nc - 1 - j, col))

    n = batch * seq
    return pl.pallas_call(
        functools.partial(_hgrn_kernel, layer=layer),
        grid=(batch, nc),
        in_specs=[pl.BlockSpec(hgrn_lb.shape, lambda b, j: (0, 0, 0)),
                  fwd(cb), fwd(cb + 1), fwd(cb + 3), bwd(cb), bwd(cb + 2), bwd(cb + 3)],
        out_specs=[pl.BlockSpec((c, BR_W), lambda b, j: (b * nc + j, 0)),
                   pl.BlockSpec((c, BR_W), lambda b, j: (b * nc + nc - 1 - j, 0))],
        out_shape=[jax.ShapeDtypeStruct((n, BR_W), F32)] * 2,
        scratch_shapes=[pltpu.VMEM((2, BR_W, BR_W), F32)],
        compiler_params=_cparams("parallel", "arbitrary"),
        name="hgrn",
    )(hgrn_lb, p, p, p, p, p, p)


def _nat_bias_table(rpb):
    cq = np.arange(GRID_W)
    ck = np.arange(GRID_W)
    col_start = np.clip(cq - NAT_KW // 2, 0, GRID_W - NAT_KW)
    col_mask = (ck[None, :] >= col_start[:, None]) & (ck[None, :] < col_start[:, None] + NAT_KW)
    dc = np.clip(ck[None, :] - cq[:, None], -(NAT_KW - 1), NAT_KW - 1) + (NAT_KW - 1)
    pick = (dc[None, :, :] == np.arange(2 * NAT_KW - 1)[:, None, None]).astype(np.float32)
    tiles = jnp.einsum('hrd,dqk->hrqk', rpb.astype(F32), jnp.asarray(pick),
                       precision=lax.Precision.HIGHEST)
    tiles = jnp.where(jnp.asarray(col_mask)[None, None], tiles, MASK_VALUE)
    t = jnp.stack([tiles[:, b:b + NAT_KH] for b in range(NAT_KH)], axis=0)
    return t.transpose(0, 1, 3, 2, 4).reshape(NAT_KH, N_HEADS * GRID_W, NAT_KH * GRID_W)


def _nat_kernel(q_ref, k_ref, v_ref, bias_ref, o_ref, *, rows_per_step, n_rows):
    j = pl.program_id(1)
    lane_head = _head_id((GRID_W, BR_W), 1)
    win = NAT_KH * GRID_W

    def body(i, carry):
        r = j * rows_per_step + i
        rs = jnp.clip(r - NAT_KH // 2, 0, n_rows - NAT_KH)
        base = rs - r + (NAT_KH - 1)
        q = q_ref[pl.ds(pl.multiple_of(i * GRID_W, GRID_W), GRID_W), :] * jnp.asarray(HEAD_DIM ** -0.5, BF16)
        kw = k_ref[pl.ds(pl.multiple_of(rs * GRID_W, GRID_W), win), :]
        vw = v_ref[pl.ds(pl.multiple_of(rs * GRID_W, GRID_W), win), :]
        q_heads = jnp.concatenate([jnp.where(lane_head == h, q, jnp.zeros_like(q)) for h in range(N_HEADS)], axis=0)
        s = _nt(q_heads, kw) + bias_ref[base]
        m = jnp.max(s, axis=-1, keepdims=True)
        pr = jnp.exp(s - m)
        l = jnp.sum(pr, axis=-1, keepdims=True)
        o = _dot(pr.astype(BF16), vw) / l
        out = jnp.zeros((GRID_W, BR_W), F32)
        for h in range(N_HEADS):
            out = out + jnp.where(lane_head == h, o[h * GRID_W:(h + 1) * GRID_W], 0.0)
        o_ref[pl.ds(pl.multiple_of(i * GRID_W, GRID_W), GRID_W), :] = out.astype(o_ref.dtype)
        return carry

    lax.fori_loop(0, rows_per_step, body, 0, unroll=True)


def neighbourhood_attention(p, rpb, batch, seq):
    n_rows = seq // GRID_W
    rps = NAT_ROWS_PER_STEP
    steps = n_rows // rps
    bias = _nat_bias_table(rpb)
    cb = COL_NAT // BR_W
    tq = rps * GRID_W
    return pl.pallas_call(
        functools.partial(_nat_kernel, rows_per_step=rps, n_rows=n_rows),
        grid=(batch, steps),
        in_specs=[pl.BlockSpec((tq, BR_W), lambda b, j: (b * steps + j, cb)),
                  pl.BlockSpec((seq, BR_W), lambda b, j: (b, cb + 1)),
                  pl.BlockSpec((seq, BR_W), lambda b, j: (b, cb + 2)),
                  pl.BlockSpec(bias.shape, lambda b, j: (0, 0, 0))],
        out_specs=pl.BlockSpec((tq, BR_W), lambda b, j: (b * steps + j, 0)),
        out_shape=jax.ShapeDtypeStruct((batch * seq, BR_W), BF16),
        compiler_params=_cparams("parallel", "arbitrary"),
        name="nat",
    )(p, p, p, bias)


def _diff_kernel(lam_ref, slope_ref, sub_ref, q_ref, k_ref, v_ref, o_ref,
                 acc1, acc2, m1, l1, m2, l2, hill, stage_a1, stage_a2, stage_b1, stage_b2, *, lam_init, n_tiles):
    t = q_ref.shape[0]
    reps = t // 128
    qi = pl.program_id(2)
    slope2 = slope_ref[0, 0:1, 0:1] * LOG2E
    lp = lam_ref[...]
    lam = (jnp.exp(jnp.sum(lp[0:1] * lp[1:2], axis=-1, keepdims=True))
           - jnp.exp(jnp.sum(lp[2:3] * lp[3:4], axis=-1, keepdims=True)) + lam_init)

    @pl.when(qi == 0)
    def _():
        rel = (lax.broadcasted_iota(jnp.int32, (t, t), 1) - lax.broadcasted_iota(jnp.int32, (t, t), 0)).astype(F32)
        hill[...] = -slope2 * jnp.abs(rel)

    q = (q_ref[...].astype(F32) * (DIFF_SCALE * LOG2E)).astype(BF16)
    lane = lax.broadcasted_iota(jnp.int32, q.shape, 1)
    q1 = jnp.where(lane < 64, q, jnp.zeros_like(q))
    q2 = jnp.where(lane >= 64, q, jnp.zeros_like(q))

    acc1[...] = jnp.zeros_like(acc1)
    acc2[...] = jnp.zeros_like(acc2)
    l1[...] = jnp.zeros_like(l1)
    l2[...] = jnp.zeros_like(l2)
    m1[...] = jnp.full_like(m1, -jnp.inf)
    m2[...] = jnp.full_like(m2, -jnp.inf)

    def scores(kj):
        k = k_ref[pl.ds(pl.multiple_of(kj * t, t), t), :]
        return _nt(q1, k), _nt(q2, k)

    def absorb(kj, raw_scores, add_local_bias, tile_bias):
        v = v_ref[pl.ds(pl.multiple_of(kj * t, t), t), :]
        for s_raw, acc, m_ref, l_ref in zip(raw_scores, (acc1, acc2), (m1, m2), (l1, l2)):
            s = add_local_bias(s_raw)
            m_old = m_ref[...]
            m_new = jnp.maximum(m_old, jnp.max(s, axis=-1, keepdims=True) + tile_bias)
            a = jnp.exp2(m_old - m_new)
            pr = jnp.exp2(s - jnp.concatenate([m_new - tile_bias] * reps, axis=1))
            l_ref[...] = a * l_ref[...] + jnp.sum(pr, axis=-1, keepdims=True)
            acc[...] = a * acc[...] + _dot(pr.astype(BF16), v)
            m_ref[...] = m_new

    stage = ((stage_a1, stage_a2), (stage_b1, stage_b2))
    query_term = slope2 * lax.broadcasted_iota(jnp.int32, (t, 128), 0).astype(F32)
    order = [qi] + [jnp.where(u < qi, u, u + 1) for u in range(n_tiles - 1)]
    for ref, val in zip(stage[0], scores(order[0])):
        ref[...] = val
    for u, kj in enumerate(order):
        cur, nxt = stage[u % 2], stage[(u + 1) % 2]
        if u + 1 < len(order):
            for ref, val in zip(nxt, scores(order[u + 1])):
                ref[...] = val
        raw = (cur[0][...], cur[1][...])
        if u == 0:
            diagonal = hill[...]
            absorb(kj, raw, lambda s: s + diagonal, jnp.zeros((1, 1), F32))
        else:
            sign = jnp.where(jnp.full((1, 1), kj, jnp.int32) < qi, 1.0, -1.0)
            keys = (sign * slope2) * ((kj - qi) * t + lax.broadcasted_iota(jnp.int32, (1, t), 1)).astype(F32)
            absorb(kj, raw, lambda s, keys=keys: s + keys, -sign * query_term)

    o = acc1[...] / l1[...] - lam * (acc2[...] / l2[...])
    o = o * lax.rsqrt(jnp.mean(o * o, axis=-1, keepdims=True) + EPS) * sub_ref[0]
    o_ref[...] = (o * (1.0 - lam_init)).astype(o_ref.dtype)


def diff_attention(p, lam_params, subln, layer, batch, seq):
    t = DIFF_TILE
    nt = seq // t
    lam_init = 0.8 - 0.6 * math.exp(-0.3 * layer)
    slopes = np.exp2(-8.0 * (np.arange(N_HEADS, dtype=np.float64) + 1.0) / N_HEADS)
    slope_tab = jnp.asarray(np.broadcast_to(slopes[:, None, None], (N_HEADS, 8, 128)).astype(np.float32))
    qc, kc, vc = (COL_DIFF // DIFF_DV, (COL_DIFF + DIFF_W) // DIFF_DV, (COL_DIFF + 2 * DIFF_W) // DIFF_DV)
    return pl.pallas_call(
        functools.partial(_diff_kernel, lam_init=lam_init, n_tiles=nt),
        grid=(batch, N_HEADS, nt),
        in_specs=[pl.BlockSpec(lam_params.shape, lambda b, h, i: (0, 0)),
                  pl.BlockSpec((1, 8, 128), lambda b, h, i: (h, 0, 0)),
                  pl.BlockSpec((1, 1, DIFF_DV), lambda b, h, i: (h, 0, 0)),
                  pl.BlockSpec((t, DIFF_DV), lambda b, h, i: (b * nt + i, qc + h)),
                  pl.BlockSpec((seq, DIFF_DV), lambda b, h, i: (b, kc + h)),
                  pl.BlockSpec((seq, DIFF_DV), lambda b, h, i: (b, vc + h))],
        out_specs=pl.BlockSpec((t, DIFF_DV), lambda b, h, i: (b * nt + i, h)),
        out_shape=jax.ShapeDtypeStruct((batch * seq, DIFF_W), BF16),
        scratch_shapes=[pltpu.VMEM((t, DIFF_DV), F32)] * 6 + [pltpu.VMEM((t, t), F32)] * 5,
        compiler_params=_cparams("parallel", "parallel", "arbitrary"),
        name="diff_attn",
    )(lam_params, slope_tab, subln.reshape(N_HEADS, 1, DIFF_DV), p, p, p)


def _merge_kernel(hf_ref, hb_ref, rof, rob, rg, ynat, hof, hob, hg, ydiff,
                  wg_ref, bg_ref, wr_ref, wn_ref, wh_ref, wd_ref, wo_ref,
                  rgn_ref, hgn_ref, lng_ref, lnb_ref, of_ref, ob_ref):
    seg_mean = jnp.where(_block_diag_mask(BR_W), 1.0 / HEAD_DIM, 0.0).astype(BF16)

    def head_norm_gate(o, gn, gate):
        ms = _dot_split(o * o, seg_mean, terms=2)
        gx = gate.astype(F32)
        return (o * lax.rsqrt(ms + EPS) * gn * (gx * _sigmoid(gx))).astype(BF16)

    y_ret = head_norm_gate(rof[...] + rob[...], rgn_ref[...], rg[...])
    y_hgrn = head_norm_gate(hof[...] + hob[...], hgn_ref[...], hg[...])
    hb = hb_ref[...]
    d = hf_ref.shape[1]
    merged = None
    for i, (y, w_ref) in enumerate(((y_ret, wr_ref), (ynat[...], wn_ref), (y_hgrn, wh_ref), (ydiff[...], wd_ref))):
        gate = _sigmoid(_dot(hb, wg_ref[:, i * d:(i + 1) * d]) + bg_ref[:, i * d:(i + 1) * d])
        part = gate * _dot(y, w_ref[...])
        merged = part if merged is None else merged + part
    z = ALPHA * hf_ref[...] + _dot(merged.astype(BF16), wo_ref[...])
    h = _layer_norm(z, lng_ref[...], lnb_ref[...])
    of_ref[...] = h
    ob_ref[...] = h.astype(BF16)


def merge_and_norm(hf, hb, p, ret_f, ret_b, y_nat, hg_f, hg_b, y_diff,
                   w_gate, b_gate, w_br_ret, w_br_nat, w_br_hgrn, w_br_diff, w_out,
                   ret_gn, hgrn_gn, ln_g, ln_b, tm=512):
    n, d = hf.shape

    def rows(width, col=0):
        return pl.BlockSpec((tm, width), lambda i: (i, col))

    def whole(a):
        nd = a.ndim
        return pl.BlockSpec(a.shape, lambda i: (0,) * nd)

    consts = [w_gate, b_gate.reshape(1, -1), w_br_ret, w_br_nat, w_br_hgrn, w_br_diff, w_out,
              ret_gn.reshape(1, BR_W), hgrn_gn.reshape(1, BR_W), ln_g.reshape(1, d), ln_b.reshape(1, d)]
    return pl.pallas_call(
        _merge_kernel,
        grid=(n // tm,),
        in_specs=[rows(d), rows(d),
                  rows(BR_W), rows(BR_W), rows(BR_W, COL_RET // BR_W + 3),
                  rows(BR_W),
                  rows(BR_W), rows(BR_W), rows(BR_W, COL_HGRN // BR_W + 4),
                  rows(DIFF_W)] + [whole(a) for a in consts],
        out_specs=[rows(d), rows(d)],
        out_shape=[jax.ShapeDtypeStruct((n, d), F32), jax.ShapeDtypeStruct((n, d), BF16)],
        compiler_params=_cparams("parallel"),
        name="merge",
    )(hf, hb, ret_f, ret_b, p, y_nat, hg_f, hg_b, p, y_diff, *consts)


def _route(h, w, bias):
    w_hi = w.astype(BF16)
    w_lo = (w - w_hi.astype(F32)).astype(BF16)
    logits = _dot_split(h, w_hi, terms=3) + _dot_split(h, w_lo, terms=2)
    scores = _sigmoid(logits)
    biased = scores + bias
    shape = scores.shape
    lane = lax.broadcasted_iota(jnp.int32, shape, 1).astype(F32)
    group = lax.shift_right_logical(lax.broadcasted_iota(jnp.int32, shape, 1),
                                    GROUP_SIZE.bit_length() - 1).astype(F32)
    big = jnp.asarray(1e9, F32)
    neg = jnp.asarray(-jnp.inf, F32)

    def first_max(x, ids):
        m = jnp.max(x, axis=-1, keepdims=True)
        return m, jnp.min(jnp.where(x == m, ids, big), axis=-1, keepdims=True)

    gscore = jnp.zeros(shape, F32)
    for g in range(N_GROUPS):
        xg = jnp.where(group == g, biased, neg)
        m1, i1 = first_max(xg, lane)
        m2 = jnp.max(jnp.where(lane == i1, neg, xg), axis=-1, keepdims=True)
        gscore = jnp.where(group == g, m1 + m2, gscore)
    gsel = jnp.zeros(shape, jnp.bool_)
    for _ in range(TOPK_GROUPS):
        _, gi = first_max(gscore, group)
        hit = group == gi
        gsel = gsel | hit
        gscore = jnp.where(hit, neg, gscore)
    cand = jnp.where(gsel, biased, MASK_VALUE)
    esel = jnp.zeros(shape, jnp.bool_)
    picks = []
    for _ in range(TOP_K):
        _, ei = first_max(cand, lane)
        hit = lane == ei
        picks.append((ei, jnp.sum(jnp.where(hit, scores, 0.0), axis=-1, keepdims=True)))
        esel = esel | hit
        cand = jnp.where(hit, neg, cand)

    tm = shape[0]
    sel = jnp.where(esel, 1.0, 0.0)
    counts = jnp.concatenate(
        [jnp.sum(sel[s * ROUTE_TILE:(s + 1) * ROUTE_TILE], axis=0, keepdims=True) for s in range(tm // ROUTE_TILE)],
        axis=0)

    slot = lax.broadcasted_iota(jnp.int32, (tm, TOP_K), 1)
    idx8 = jnp.zeros((tm, TOP_K), F32)
    w8 = jnp.zeros((tm, TOP_K), F32)
    wsum = jnp.zeros((tm, 1), F32)
    for k, (ei, wk) in enumerate(picks):
        idx8 = jnp.where(slot == k, ei, idx8)
        w8 = jnp.where(slot == k, wk, w8)
        wsum = wsum + wk
    return idx8.astype(jnp.int32), w8 / (wsum + 1e-20) * ROUTED_SCALE, counts


def _router_kernel(h_ref, w_ref, bias_ref, idx_ref, w8_ref, cnt_ref):
    idx_ref[...], w8_ref[...], cnt_ref[...] = _route(h_ref[...], w_ref[...], bias_ref[...])


def router(hf, w_router, router_bias, tm=1024):
    n, d = hf.shape
    e = w_router.shape[1]
    sub = tm // ROUTE_TILE
    return pl.pallas_call(
        _router_kernel,
        grid=(n // tm,),
        in_specs=[pl.BlockSpec((tm, d), lambda i: (i, 0)),
                  pl.BlockSpec((d, e), lambda i: (0, 0)),
                  pl.BlockSpec((1, e), lambda i: (0, 0))],
        out_specs=[pl.BlockSpec((tm, TOP_K), lambda i: (i, 0)),
                   pl.BlockSpec((tm, TOP_K), lambda i: (i, 0)),
                   pl.BlockSpec((sub, e), lambda i: (i, 0))],
        out_shape=[jax.ShapeDtypeStruct((n, TOP_K), jnp.int32),
                   jax.ShapeDtypeStruct((n, TOP_K), F32),
                   jax.ShapeDtypeStruct((n // ROUTE_TILE, e), F32)],
        compiler_params=_cparams("arbitrary"),
        name="router",
    )(hf, w_router, router_bias.reshape(1, e))


HIGH_HALF = -65536


def _pack_pairs(x):
    w = x.shape[1] // 2
    lo = lax.bitcast_convert_type(x[:, :w].astype(BF16).astype(F32), jnp.int32)
    hi = lax.bitcast_convert_type(x[:, w:].astype(BF16).astype(F32), jnp.int32)
    return lax.shift_right_logical(lo, 16) | (hi & HIGH_HALF)


def _unpack_pairs(p):
    lo = lax.bitcast_convert_type(lax.shift_left(p, 16), F32)
    hi = lax.bitcast_convert_type(p & HIGH_HALF, F32)
    return jnp.concatenate([lo, hi], axis=1)


def _swiglu(x, wg, wu):
    a = _dot(x, wg)
    return a * _sigmoid(a) * _dot(x, wu)


def _segment_copies(tab_ref, first_col, max_run, make_copy, action):
    top_bit = max_run.bit_length() - 1
    low_bit = RUN_ALIGN.bit_length() - 1
    mid_bit = min(top_bit + 1, LONG_RUN.bit_length() - 1)

    def pieces(count, local, glob, bits):
        for b in bits:
            size = 1 << b
            taken = count & size

            @pl.when(taken != 0)
            def _(local=local, glob=glob, size=size):
                action(make_copy(pl.multiple_of(local, RUN_ALIGN), pl.multiple_of(glob, RUN_ALIGN), size))

            local = local + taken
            glob = glob + taken

    def per_expert(e, carry):
        col = first_col + e
        count = tab_ref[0, col]
        local = tab_ref[1, col]
        glob = tab_ref[2, col]
        long_part = count & -LONG_RUN

        @pl.when(long_part != 0)
        def _():
            pieces(count, local, glob, range(top_bit, mid_bit - 1, -1))

        pieces(count, local + long_part, glob + long_part, range(mid_bit - 1, low_bit - 1, -1))
        return carry

    lax.fori_loop(0, N_EXPERTS, per_expert, 0)


def _tile_rows_used(tab_ref, t):
    last = t * N_EXPERTS + (N_EXPERTS - 1)
    return tab_ref[1, last] + tab_ref[0, last]


def _wait_rows(total, max_rows, make_copy):
    for b in range(max_rows.bit_length() - 1, RUN_ALIGN.bit_length() - 2, -1):
        size = 1 << b

        @pl.when((total & size) != 0)
        def _(size=size):
            make_copy(0, 0, size).wait()


def _block_dispatch_kernel(tab_ref, gap_ref, tail_ref, idx_t_ref, hb_ref, xs_ref, sorted_buf, zeros_buf, sem):
    t = pl.program_id(0)
    tm = hb_ref.shape[0]
    rows = sorted_buf.shape[0]
    idx_t = idx_t_ref[...]
    expert = lax.broadcasted_iota(jnp.int32, (N_EXPERTS, tm), 0)
    sel_t = jnp.zeros((N_EXPERTS, tm), F32)
    for k in range(TOP_K):
        sel_t = sel_t + jnp.where(idx_t[k:k + 1, :] == expert, 1.0, 0.0)
    sel_b = sel_t.astype(BF16)
    before_tok = (lax.broadcasted_iota(jnp.int32, (tm, tm), 0) < lax.broadcasted_iota(jnp.int32, (tm, tm), 1))
    before_exp = (lax.broadcasted_iota(jnp.int32, (N_EXPERTS, N_EXPERTS), 1)
                  < lax.broadcasted_iota(jnp.int32, (N_EXPERTS, N_EXPERTS), 0))
    run = jnp.broadcast_to(_run_length(jnp.sum(sel_t, axis=1, keepdims=True)), (N_EXPERTS, tm)).astype(BF16)
    base = (_dot(jnp.where(before_exp, 1.0, 0.0).astype(BF16), run)
            + _dot(sel_b, jnp.where(before_tok, 1.0, 0.0).astype(BF16)))
    row = lax.broadcasted_iota(jnp.int32, (rows, tm), 0).astype(jnp.int16)
    perm = jnp.zeros((rows, tm), BF16)
    for k in range(TOP_K):
        place = jnp.sum(jnp.where(idx_t[k:k + 1, :] == expert, base, 0.0), axis=0, keepdims=True)
        perm = jnp.where(row == place.astype(jnp.int32).astype(jnp.int16), jnp.ones_like(perm), perm)
    half = hb_ref.shape[1] // 2
    lo = lax.bitcast_convert_type(_dot(perm, hb_ref[:, :half]), jnp.int32)
    hi = lax.bitcast_convert_type(_dot(perm, hb_ref[:, half:]), jnp.int32)
    sorted_buf[...] = lax.shift_right_logical(lo, 16) | (hi & HIGH_HALF)

    def make_copy(local, glob, size):
        return pltpu.make_async_copy(sorted_buf.at[pl.ds(local, size)], xs_ref.at[pl.ds(glob, size)], sem.at[0])

    _segment_copies(tab_ref, t * N_EXPERTS, tm, make_copy, lambda c: c.start())
    _wait_rows(_tile_rows_used(tab_ref, t), rows, make_copy)

    @pl.when(t == pl.num_programs(0) - 1)
    def _():
        zeros_buf[...] = jnp.zeros_like(zeros_buf)
        tile_rows = zeros_buf.shape[0]

        def zero_copy(local, glob, size):
            return pltpu.make_async_copy(zeros_buf.at[pl.ds(0, size)], xs_ref.at[pl.ds(glob, size)], sem.at[0])

        def tail_copy(j):
            return zero_copy(0, pl.multiple_of(tail_ref[0] + j * tile_rows, RUN_ALIGN), tile_rows)

        _segment_copies(gap_ref, 0, tile_rows // 2, zero_copy, lambda c: c.start())
        lax.fori_loop(0, tail_ref[1], lambda j, c: (tail_copy(j).start(), c)[1], 0)
        _segment_copies(gap_ref, 0, tile_rows // 2, zero_copy, lambda c: c.wait())
        lax.fori_loop(0, tail_ref[1], lambda j, c: (tail_copy(j).wait(), c)[1], 0)


def block_dispatch(hb, idx_t, tab, gaps, tail, tm):
    n, d = hb.shape
    grid_spec = pltpu.PrefetchScalarGridSpec(
        num_scalar_prefetch=3,
        grid=(n // tm,),
        in_specs=[pl.BlockSpec((TOP_K, tm), lambda i, *_: (0, i)),
                  pl.BlockSpec((tm, d), lambda i, *_: (i, 0))],
        out_specs=pl.BlockSpec(memory_space=pl.ANY),
        scratch_shapes=[pltpu.VMEM((_tile_rows(tm), d // 2), jnp.int32),
                        pltpu.VMEM((EXPERT_ROWS, d // 2), jnp.int32),
                        pltpu.SemaphoreType.DMA((1,))],
    )
    return pl.pallas_call(
        _block_dispatch_kernel,
        grid_spec=grid_spec,
        out_shape=jax.ShapeDtypeStruct((_sorted_rows(n, tm), d // 2), jnp.int32),
        compiler_params=_cparams("arbitrary"),
        name="dispatch",
    )(tab, gaps, tail, idx_t, hb)


def _tile_tables(tile_counts, tm_fine, tm):
    g = tm // tm_fine
    n_tokens = tile_counts.shape[0] * tm_fine
    n_tiles = _sorted_rows(n_tokens, tm) // EXPERT_ROWS
    cnt = tile_counts.astype(jnp.int32).reshape(-1, g, N_EXPERTS).sum(axis=1)
    cnt = (cnt + (RUN_ALIGN - 1)) // RUN_ALIGN * RUN_ALIGN
    local = jnp.cumsum(cnt, axis=1) - cnt
    used = cnt.sum(axis=0)
    seg = (used + (EXPERT_ROWS - 1)) // EXPERT_ROWS * EXPERT_ROWS
    seg_end = jnp.cumsum(seg)
    seg_start = seg_end - seg
    glob = seg_start[None, :] + jnp.cumsum(cnt, axis=0) - cnt
    runs = jnp.stack([cnt.reshape(-1), local.reshape(-1), glob.reshape(-1)])
    gaps = jnp.stack([seg - used, jnp.zeros_like(used), seg_start + used])
    tail = jnp.stack([seg_end[-1], n_tiles - seg_end[-1] // EXPERT_ROWS])
    tile_row = jnp.arange(n_tiles, dtype=jnp.int32) * EXPERT_ROWS
    owner = jnp.sum((seg_end[None, :] <= tile_row[:, None]).astype(jnp.int32), axis=1)
    owner_c = jnp.minimum(owner, N_EXPERTS - 1)
    of_owner = lambda a: jnp.sum(jnp.where(jnp.arange(N_EXPERTS)[None, :] == owner_c[:, None], a[None, :], 0), axis=1)
    filled = jnp.clip(of_owner(seg_start + used) - tile_row, 0, EXPERT_ROWS)
    source = jnp.minimum(jnp.arange(n_tiles, dtype=jnp.int32), jnp.maximum(seg_end[-1] // EXPERT_ROWS - 1, 0))
    tiles = jnp.stack([owner_c, jnp.where(owner < N_EXPERTS, filled, 0), source])
    return runs, gaps, tail, tiles.astype(jnp.int32)


def _expert_kernel(tile_ref, xs_ref, wg_ref, wu_ref, wd_ref, ys_ref, wgb, wub, wdb):
    i = pl.program_id(0)
    expert = tile_ref[0, i]

    @pl.when(tile_ref[1, i] > 0)
    def _():
        @pl.when((i == 0) | (expert != tile_ref[0, jnp.maximum(i - 1, 0)]))
        def _():
            wgb[...] = wg_ref[0, 0].astype(BF16)
            wub[...] = wu_ref[0, 0].astype(BF16)
            wdb[...] = wd_ref[0, 0].astype(BF16)

        x = _unpack_pairs(xs_ref[...]).astype(BF16)
        hid = _swiglu(x, wgb[...], wub[...])
        ys_ref[...] = _pack_pairs(_dot(hid.astype(BF16), wdb[...]))

    @pl.when(tile_ref[1, i] == 0)
    def _():
        ys_ref[...] = jnp.zeros_like(ys_ref)


def grouped_experts(xs, tiles, w_e_gate, w_e_up, w_e_down, layer):
    rows, w = xs.shape
    _, _, d, hid = w_e_gate.shape
    tr = EXPERT_ROWS
    grid_spec = pltpu.PrefetchScalarGridSpec(
        num_scalar_prefetch=1,
        grid=(rows // tr,),
        in_specs=[pl.BlockSpec((tr, w), lambda i, tl: (tl[2, i], 0)),
                  pl.BlockSpec((1, 1, d, hid), lambda i, tl: (layer, tl[0, i], 0, 0)),
                  pl.BlockSpec((1, 1, d, hid), lambda i, tl: (layer, tl[0, i], 0, 0)),
                  pl.BlockSpec((1, 1, hid, d), lambda i, tl: (layer, tl[0, i], 0, 0))],
        out_specs=pl.BlockSpec((tr, w), lambda i, tl: (i, 0)),
        scratch_shapes=[pltpu.VMEM((d, hid), BF16), pltpu.VMEM((d, hid), BF16), pltpu.VMEM((hid, d), BF16)],
    )
    return pl.pallas_call(
        _expert_kernel,
        grid_spec=grid_spec,
        out_shape=jax.ShapeDtypeStruct((rows, w), jnp.int32),
        compiler_params=_cparams("arbitrary"),
        name="experts",
    )(tiles, xs, w_e_gate, w_e_up, w_e_down)


def _block_combine_kernel(tab_ref, ys_ref, idx_ref, w8_ref, hf_ref, hb_ref, sg_ref, su_ref, sd_ref,
                          lng_ref, lnb_ref, of_ref, ob_ref, buf, sem):
    t = pl.program_id(0)
    tm = hf_ref.shape[0]
    rows = buf.shape[0]

    def make_copy(local, glob, size):
        return pltpu.make_async_copy(ys_ref.at[pl.ds(glob, size)], buf.at[pl.ds(local, size)], sem.at[0])

    @pl.when(t == 0)
    def _():
        buf[...] = jnp.zeros_like(buf)

    _segment_copies(tab_ref, t * N_EXPERTS, tm, make_copy, lambda c: c.start())

    idx = idx_ref[...]
    w8 = w8_ref[...]
    expert = lax.broadcasted_iota(jnp.int32, (tm, N_EXPERTS), 1)
    sel = jnp.zeros((tm, N_EXPERTS), F32)
    for k in range(TOP_K):
        sel = sel + jnp.where(idx[:, k:k + 1] == expert, 1.0, 0.0)
    sel_b = sel.astype(BF16)
    before_tok = (lax.broadcasted_iota(jnp.int32, (tm, tm), 1) < lax.broadcasted_iota(jnp.int32, (tm, tm), 0))
    before_exp = (lax.broadcasted_iota(jnp.int32, (N_EXPERTS, N_EXPERTS), 0)
                  < lax.broadcasted_iota(jnp.int32, (N_EXPERTS, N_EXPERTS), 1))
    run = jnp.broadcast_to(_run_length(jnp.sum(sel, axis=0, keepdims=True)), (tm, N_EXPERTS)).astype(BF16)
    base = (_dot(run, jnp.where(before_exp, 1.0, 0.0).astype(BF16))
            + _dot(jnp.where(before_tok, 1.0, 0.0).astype(BF16), sel_b))
    col = lax.broadcasted_iota(jnp.int32, (tm, rows), 1).astype(jnp.int16)
    w8_b = w8.astype(BF16)
    mix = jnp.zeros((tm, rows), BF16)
    for k in range(TOP_K):
        place = jnp.sum(jnp.where(idx[:, k:k + 1] == expert, base, 0.0), axis=1, keepdims=True)
        mix = jnp.where(col == place.astype(jnp.int32).astype(jnp.int16),
                        jnp.broadcast_to(w8_b[:, k:k + 1], mix.shape), mix)

    acc = _dot(_swiglu(hb_ref[...], sg_ref[...], su_ref[...]).astype(BF16), sd_ref[...])
    _wait_rows(_tile_rows_used(tab_ref, t), rows, make_copy)
    y = _unpack_pairs(buf[...]).astype(BF16)
    acc = acc + _dot(mix, y)
    h = _layer_norm(ALPHA * hf_ref[...] + acc, lng_ref[...], lnb_ref[...])
    of_ref[...] = h
    ob_ref[...] = h.astype(BF16)


def block_combine(ys, tab, idx8, w8, hf, hb, w_s_gate, w_s_up, w_s_down, ln_g, ln_b, tm):
    n, d = hf.shape
    w = ys.shape[1]

    def whole(a):
        return pl.BlockSpec(a.shape, lambda i, tab: (0, 0))

    def rows(width):
        return pl.BlockSpec((tm, width), lambda i, tab: (i, 0))

    consts = [w_s_gate, w_s_up, w_s_down, ln_g.reshape(1, d), ln_b.reshape(1, d)]
    grid_spec = pltpu.PrefetchScalarGridSpec(
        num_scalar_prefetch=1,
        grid=(n // tm,),
        in_specs=[pl.BlockSpec(memory_space=pl.ANY), rows(TOP_K), rows(TOP_K), rows(d), rows(d)]
                 + [whole(a) for a in consts],
        out_specs=[rows(d), rows(d)],
        scratch_shapes=[pltpu.VMEM((_tile_rows(tm), w), jnp.int32), pltpu.SemaphoreType.DMA((1,))],
    )
    return pl.pallas_call(
        _block_combine_kernel,
        grid_spec=grid_spec,
        out_shape=[jax.ShapeDtypeStruct((n, d), F32), jax.ShapeDtypeStruct((n, d), BF16)],
        compiler_params=_cparams("arbitrary"),
        name="combine",
    )(tab, ys, idx8, w8, hf, hb, *consts)


def moe_and_norm(hf, hb, w_router, router_bias, w_e_gate, w_e_up, w_e_down,
                 w_s_gate, w_s_up, w_s_down, ln_g, ln_b, layer):
    idx8, w8, tile_counts = router(hf, w_router, router_bias)
    runs, gaps, tail, tiles = _tile_tables(tile_counts, ROUTE_TILE, MOE_TILE)
    xs = block_dispatch(hb, idx8.T, runs, gaps, tail, MOE_TILE)
    ys = grouped_experts(xs, tiles, w_e_gate, w_e_up, w_e_down, layer)
    return block_combine(ys, runs, idx8, w8, hf, hb, w_s_gate, w_s_up, w_s_down, ln_g, ln_b, MOE_TILE)


def kernel(x, ln_in_g, ln_in_b, w_in, w_gate, b_gate, w_br_ret, w_br_nat, w_br_hgrn, w_br_diff, w_out,
           ret_gn, nat_rpb, hgrn_lb, hgrn_gn, diff_lambda, diff_subln, ln1_g, ln1_b, w_router,
           router_bias, w_e_gate, w_e_up, w_e_down, w_s_gate, w_s_up, w_s_down, ln2_g, ln2_b):
    batch, seq, d = x.shape
    bf = lambda a: a.astype(BF16)
    hf, hb = input_layer_norm(x.reshape(batch * seq, d), ln_in_g, ln_in_b)
    for l in range(DEPTH):
        p = in_projection(hb, bf(w_in[l]))
        ret_f, ret_b = retention(p, batch, seq)
        y_nat = neighbourhood_attention(p, nat_rpb[l], batch, seq)
        hg_f, hg_b = hgrn(p, hgrn_lb, l, batch, seq)
        y_diff = diff_attention(p, diff_lambda[l], diff_subln[l], l, batch, seq)
        hf, hb = merge_and_norm(hf, hb, p, ret_f, ret_b, y_nat, hg_f, hg_b, y_diff,
                                bf(w_gate[l]), b_gate[l], bf(w_br_ret[l]), bf(w_br_nat[l]),
                                bf(w_br_hgrn[l]), bf(w_br_diff[l]), bf(w_out[l]),
                                ret_gn[l], hgrn_gn[l], ln1_g[l], ln1_b[l])
        hf, hb = moe_and_norm(hf, hb, w_router[l], router_bias[l], w_e_gate, w_e_up, w_e_down,
                              bf(w_s_gate[l]), bf(w_s_up[l]), bf(w_s_down[l]), ln2_g[l], ln2_b[l], l)
    return hf.reshape(batch, seq, d)
```

```python
import functools
import math

import numpy as np
import jax
import jax.numpy as jnp
from jax import lax
from jax.experimental import pallas as pl
from jax.experimental.pallas import tpu as pltpu

F32 = jnp.float32
BF16 = jnp.bfloat16

DEPTH = 2
GRID_W = 64
HEAD_DIM = 64
N_HEADS = 4
BR_W = N_HEADS * HEAD_DIM
NAT_KH = 8
NAT_KW = 16
DIFF_DV = 128
DIFF_W = N_HEADS * DIFF_DV
N_EXPERTS = 64
TOP_K = 8
N_GROUPS = 8
TOPK_GROUPS = 4
GROUP_SIZE = N_EXPERTS // N_GROUPS
ROUTED_SCALE = 2.5
EPS = 1e-5
MASK_VALUE = -1e30
LOG2E = math.log2(math.e)
DIFF_SCALE = 64 ** -0.5
ALPHA = (2.0 * DEPTH) ** 0.25
COL_RET = 0
COL_NAT = 1024
COL_HGRN = 1792
COL_DIFF = 3072

VMEM_LIMIT = 56 * 1024 * 1024

RET_CHUNK = 256
HGRN_CHUNK = 128
HGRN_SUB = 16
DIFF_TILE = 1024
NAT_ROWS_PER_STEP = 8
ROUTE_TILE = 128
MOE_TILE = 256
EXPERT_ROWS = 1024
RUN_ALIGN = 8
LONG_RUN = 64


def _run_length(count):
    return jnp.floor((count + (RUN_ALIGN - 1)) * (1.0 / RUN_ALIGN)) * RUN_ALIGN


def _tile_rows(tm):
    bound = TOP_K * tm + N_EXPERTS * (RUN_ALIGN - 1)
    return -(-bound // 128) * 128


def _sorted_rows(n_tokens, tm):
    bound = (TOP_K * n_tokens + (n_tokens // tm) * N_EXPERTS * (RUN_ALIGN - 1)
             + N_EXPERTS * (EXPERT_ROWS - RUN_ALIGN))
    return -(-bound // EXPERT_ROWS) * EXPERT_ROWS


def _cparams(*sem):
    return pltpu.CompilerParams(dimension_semantics=sem, vmem_limit_bytes=VMEM_LIMIT)


def _nt(a, b):
    return lax.dot_general(a, b, (((1,), (1,)), ((), ())), preferred_element_type=F32)


def _tn(a, b):
    return lax.dot_general(a, b, (((0,), (0,)), ((), ())), preferred_element_type=F32)


def _dot(a, b):
    return jnp.dot(a, b, preferred_element_type=F32)


def _dot_split(x, w_bf16, terms=3):
    acc = None
    rem = x
    for _ in range(terms):
        piece = rem.astype(BF16)
        part = _dot(piece, w_bf16)
        acc = part if acc is None else acc + part
        rem = rem - piece.astype(F32)
    return acc


def _layer_norm(z, g, b):
    mu = jnp.mean(z, axis=-1, keepdims=True)
    zc = z - mu
    var = jnp.mean(zc * zc, axis=-1, keepdims=True)
    return zc * lax.rsqrt(var + EPS) * g + b


def _sigmoid(x):
    return 1.0 / (1.0 + jnp.exp(-x))


def _head_id(shape, axis):
    return lax.shift_right_logical(lax.broadcasted_iota(jnp.int32, shape, axis), HEAD_DIM.bit_length() - 1)


def _block_diag_mask(n):
    return _head_id((n, n), 0) == _head_id((n, n), 1)


def _ln_kernel(x_ref, g_ref, b_ref, hf_ref, hb_ref):
    h = _layer_norm(x_ref[...], g_ref[...], b_ref[...])
    hf_ref[...] = h
    hb_ref[...] = h.astype(BF16)


def input_layer_norm(x2, g, b, tm=1024):
    n, d = x2.shape
    return pl.pallas_call(
        _ln_kernel,
        grid=(n // tm,),
        in_specs=[pl.BlockSpec((tm, d), lambda i: (i, 0)),
                  pl.BlockSpec((1, d), lambda i: (0, 0)),
                  pl.BlockSpec((1, d), lambda i: (0, 0))],
        out_specs=[pl.BlockSpec((tm, d), lambda i: (i, 0)),
                   pl.BlockSpec((tm, d), lambda i: (i, 0))],
        out_shape=[jax.ShapeDtypeStruct((n, d), F32), jax.ShapeDtypeStruct((n, d), BF16)],
        compiler_params=_cparams("parallel"),
        name="input_ln",
    )(x2, g.reshape(1, d), b.reshape(1, d))


def _matmul_kernel(h_ref, w_ref, o_ref):
    o_ref[...] = _dot(h_ref[...], w_ref[...]).astype(o_ref.dtype)


def in_projection(hb, w_bf16, tm=2048, tn=768):
    n, d = hb.shape
    w = w_bf16.shape[1]
    return pl.pallas_call(
        _matmul_kernel,
        grid=(n // tm, w // tn),
        in_specs=[pl.BlockSpec((tm, d), lambda i, j: (i, 0)),
                  pl.BlockSpec((d, tn), lambda i, j: (0, j))],
        out_specs=pl.BlockSpec((tm, tn), lambda i, j: (i, j)),
        out_shape=jax.ShapeDtypeStruct((n, w), BF16),
        compiler_params=_cparams("parallel", "arbitrary"),
        name="in_proj",
    )(hb, w_bf16)


def _retention_tables(c):
    idx = np.arange(N_HEADS, dtype=np.float64)
    lg = [np.log1p(-np.exp2(-5.0 - 2.0 * idx)), np.log1p(-np.exp2(-6.0 - 2.0 * idx))]
    i = np.arange(c, dtype=np.float64)
    diff = i[:, None] - i[None, :]
    lane_head = np.repeat(np.arange(N_HEADS), HEAD_DIM)
    dmat = np.zeros((2, N_HEADS, c, c), np.float32)
    oscale = np.zeros((2, c, BR_W), np.float32)
    kscale = np.zeros((2, c, BR_W), np.float32)
    sdecay = np.zeros((2, 1, BR_W), np.float32)
    for h in range(N_HEADS):
        dmat[0, h] = np.where(diff >= 0, np.exp(lg[0][h] * np.maximum(diff, 0)), 0.0)
        dmat[1, h] = np.where(diff <= 0, np.exp(lg[1][h] * np.maximum(-diff, 0)), 0.0)
    oscale[0] = np.exp(lg[0][lane_head][None, :] * (i[:, None] + 1.0))
    oscale[1] = np.exp(lg[1][lane_head][None, :] * (c - i[:, None]))
    kscale[0] = np.exp(lg[0][lane_head][None, :] * (c - 1.0 - i[:, None]))
    kscale[1] = np.exp(lg[1][lane_head][None, :] * i[:, None])
    sdecay[0, 0] = np.exp(lg[0][lane_head] * c)
    sdecay[1, 0] = np.exp(lg[1][lane_head] * c)
    return (jnp.asarray(dmat), jnp.asarray(oscale), jnp.asarray(kscale), jnp.asarray(sdecay))


def _retention_kernel(qf, kf, vf, qb, kb, vb, dmat, oscale, kscale, sdecay, of_ref, ob_ref, state):
    @pl.when(pl.program_id(1) == 0)
    def _():
        state[...] = jnp.zeros_like(state)

    c = qf.shape[0]
    lane_head = _head_id((c, BR_W), 1)
    bd = _block_diag_mask(BR_W)
    for dirn, (q_ref, k_ref, v_ref, o_ref) in enumerate(((qf, kf, vf, of_ref), (qb, kb, vb, ob_ref))):
        q = q_ref[...]
        k = k_ref[...] * jnp.asarray(HEAD_DIM ** -0.5, BF16)
        v = v_ref[...]
        s_old = state[dirn]
        out = _dot(q, s_old.astype(BF16)) * oscale[dirn]
        for h in range(N_HEADS):
            qz = jnp.where(lane_head == h, q, jnp.zeros_like(q))
            a = (_nt(qz, k) * dmat[dirn, h]).astype(BF16)
            out = out + jnp.where(lane_head == h, _dot(a, v), 0.0)
        o_ref[...] = out
        kw = (k.astype(F32) * kscale[dirn]).astype(BF16)
        state[dirn] = s_old * sdecay[dirn] + jnp.where(bd, _tn(kw, v), 0.0)


def retention(p, batch, seq):
    c = RET_CHUNK
    nc = seq // c
    dmat, oscale, kscale, sdecay = _retention_tables(c)
    cb = COL_RET // BR_W

    def fwd(col):
        return pl.BlockSpec((c, BR_W), lambda b, j: (b * nc + j, col))

    def bwd(col):
        return pl.BlockSpec((c, BR_W), lambda b, j: (b * nc + nc - 1 - j, col))

    def whole(a):
        nd = a.ndim
        return pl.BlockSpec(a.shape, lambda b, j: (0,) * nd)

    n = batch * seq
    return pl.pallas_call(
        _retention_kernel,
        grid=(batch, nc),
        in_specs=[fwd(cb), fwd(cb + 1), fwd(cb + 2), bwd(cb), bwd(cb + 1), bwd(cb + 2),
                  whole(dmat), whole(oscale), whole(kscale), whole(sdecay)],
        out_specs=[pl.BlockSpec((c, BR_W), lambda b, j: (b * nc + j, 0)),
                   pl.BlockSpec((c, BR_W), lambda b, j: (b * nc + nc - 1 - j, 0))],
        out_shape=[jax.ShapeDtypeStruct((n, BR_W), F32)] * 2,
        scratch_shapes=[pltpu.VMEM((2, BR_W, BR_W), F32)],
        compiler_params=_cparams("parallel", "arbitrary"),
        name="retention",
    )(p, p, p, p, p, p, dmat, oscale, kscale, sdecay)


def _hgrn_kernel(lb_ref, qf, ff, vf, qb, fb, vb, of_ref, ob_ref, state, *, layer):
    c = qf.shape[0]
    sb = HGRN_SUB
    n_sub = c // sb

    @pl.when(pl.program_id(1) == 0)
    def _():
        state[...] = jnp.zeros_like(state)

    lb = lb_ref[...]
    e = jnp.exp(lb - jnp.max(lb, axis=0, keepdims=True))
    prob = e / jnp.sum(e, axis=0, keepdims=True)
    lower = jnp.sum(prob[:layer + 1], axis=0) - prob[0]

    row = lax.broadcasted_iota(jnp.int32, (c, c), 0)
    col = lax.broadcasted_iota(jnp.int32, (c, c), 1)
    bd = _block_diag_mask(BR_W)
    seg_ones = jnp.where(bd, 1.0, 0.0).astype(BF16)
    rowv = lax.broadcasted_iota(jnp.int32, (c, BR_W), 0)

    for dirn, (q_ref, f_ref, v_ref, o_ref) in enumerate(((qf, ff, vf, of_ref), (qb, fb, vb, ob_ref))):
        rev = dirn == 1
        lo = lower[dirn:dirn + 1, :]
        fpre = f_ref[...].astype(F32)
        logf = jnp.log(lo + (1.0 - lo) * _sigmoid(fpre))
        kk = (1.0 - lo) * _sigmoid(-fpre)
        qx = q_ref[...].astype(F32)
        q = qx * _sigmoid(qx)
        v_bf = v_ref[...]
        v = v_bf.astype(F32)
        tri = jnp.where((col >= row) if rev else (col <= row), 1.0, 0.0).astype(BF16)
        g = _tri_cumsum(tri, logf)
        g_end = g[0:1, :] if rev else g[c - 1:c, :]
        s_old = state[dirn]
        inter = _nt((q * jnp.exp(g)).astype(BF16), s_old.astype(BF16))
        kd = (kk * jnp.exp(g_end - g)).astype(BF16)
        state[dirn] = s_old * jnp.exp(g_end) + jnp.where(bd, _tn(v_bf, kd), 0.0)
        qs, ks_ = [], []
        for j in range(1, n_sub):
            if rev:
                edge = g[j * sb:j * sb + 1, :]
                key_rows = (rowv >= j * sb) & (rowv < (j + 1) * sb)
                query_rows = rowv < j * sb
            else:
                edge = g[j * sb - 1:j * sb, :]
                key_rows = (rowv >= (j - 1) * sb) & (rowv < j * sb)
                query_rows = rowv >= j * sb
            qs.append(jnp.where(query_rows, q * jnp.exp(jnp.minimum(g - edge, 0.0)), 0.0).astype(BF16))
            ks_.append(jnp.where(key_rows, kk * jnp.exp(jnp.minimum(edge - g, 0.0)), 0.0).astype(BF16))
        bd_all = jnp.concatenate([bd] * (n_sub - 1), axis=1)
        cross = jnp.where(bd_all, _tn(v_bf, jnp.concatenate(ks_, axis=1)), 0.0).astype(BF16)
        intra = _nt(jnp.concatenate(qs, axis=1), cross)
        for d in range(sb):
            shift = (c - d) % c if rev else d
            pair = ((rowv % sb) + d < sb) if rev else ((rowv % sb) >= d)
            k_d, g_d, v_d = (kk, g, v) if d == 0 else tuple(pltpu.roll(a, shift, 0) for a in (kk, g, v))
            term = jnp.where(pair, q, 0.0) * k_d * jnp.exp(jnp.minimum(g - g_d, 0.0))
            intra = intra + _dot(term.astype(BF16), seg_ones) * v_d
        o_ref[...] = inter + intra


def _tri_cumsum(tri_bf16, x):
    acc = None
    rem = x
    for _ in range(3):
        piece = rem.astype(BF16)
        part = _dot(tri_bf16, piece)
        acc = part if acc is None else acc + part
        rem = rem - piece.astype(F32)
    return acc


def hgrn(p, hgrn_lb, layer, batch, seq):
    c = HGRN_CHUNK
    nc = seq // c
    cb = COL_HGRN // BR_W

    def fwd(col):
        return pl.BlockSpec((c, BR_W), lambda b, j: (b * nc + j, col))

    def bwd(col):
        return pl.BlockSpec((c, BR_W), lambda b, j: (b * nc + nc - 1 - j, col))

    n = batch * seq
    return pl.pallas_call(
        functools.partial(_hgrn_kernel, layer=layer),
        grid=(batch, nc),
        in_specs=[pl.BlockSpec(hgrn_lb.shape, lambda b, j: (0, 0, 0)),
                  fwd(cb), fwd(cb + 1), fwd(cb + 3), bwd(cb), bwd(cb + 2), bwd(cb + 3)],
        out_specs=[pl.BlockSpec((c, BR_W), lambda b, j: (b * nc + j, 0)),
                   pl.BlockSpec((c, BR_W), lambda b, j: (b * nc + nc - 1 - j, 0))],
        out_shape=[jax.ShapeDtypeStruct((n, BR_W), F32)] * 2,
        scratch_shapes=[pltpu.VMEM((2, BR_W, BR_W), F32)],
        compiler_params=_cparams("parallel", "arbitrary"),
        name="hgrn",
    )(hgrn_lb, p, p, p, p, p, p)


def _nat_bias_table(rpb):
    cq = np.arange(GRID_W)
    ck = np.arange(GRID_W)
    col_start = np.clip(cq - NAT_KW // 2, 0, GRID_W - NAT_KW)
    col_mask = (ck[None, :] >= col_start[:, None]) & (ck[None, :] < col_start[:, None] + NAT_KW)
    dc = np.clip(ck[None, :] - cq[:, None], -(NAT_KW - 1), NAT_KW - 1) + (NAT_KW - 1)
    pick = (dc[None, :, :] == np.arange(2 * NAT_KW - 1)[:, None, None]).astype(np.float32)
    tiles = jnp.einsum('hrd,dqk->hrqk', rpb.astype(F32), jnp.asarray(pick),
                       precision=lax.Precision.HIGHEST)
    tiles = jnp.where(jnp.asarray(col_mask)[None, None], tiles, MASK_VALUE)
    t = jnp.stack([tiles[:, b:b + NAT_KH] for b in range(NAT_KH)], axis=0)
    return t.transpose(0, 1, 3, 2, 4).reshape(NAT_KH, N_HEADS * GRID_W, NAT_KH * GRID_W)


def _nat_kernel(q_ref, k_ref, v_ref, bias_ref, o_ref, *, rows_per_step, n_rows):
    j = pl.program_id(1)
    lane_head = _head_id((GRID_W, BR_W), 1)
    win = NAT_KH * GRID_W

    def body(i, carry):
        r = j * rows_per_step + i
        rs = jnp.clip(r - NAT_KH // 2, 0, n_rows - NAT_KH)
        base = rs - r + (NAT_KH - 1)
        q = q_ref[pl.ds(pl.multiple_of(i * GRID_W, GRID_W), GRID_W), :] * jnp.asarray(HEAD_DIM ** -0.5, BF16)
        kw = k_ref[pl.ds(pl.multiple_of(rs * GRID_W, GRID_W), win), :]
        vw = v_ref[pl.ds(pl.multiple_of(rs * GRID_W, GRID_W), win), :]
        q_heads = jnp.concatenate([jnp.where(lane_head == h, q, jnp.zeros_like(q)) for h in range(N_HEADS)], axis=0)
        s = _nt(q_heads, kw) + bias_ref[base]
        m = jnp.max(s, axis=-1, keepdims=True)
        pr = jnp.exp(s - m)
        l = jnp.sum(pr, axis=-1, keepdims=True)
        o = _dot(pr.astype(BF16), vw) / l
        out = jnp.zeros((GRID_W, BR_W), F32)
        for h in range(N_HEADS):
            out = out + jnp.where(lane_head == h, o[h * GRID_W:(h + 1) * GRID_W], 0.0)
        o_ref[pl.ds(pl.multiple_of(i * GRID_W, GRID_W), GRID_W), :] = out.astype(o_ref.dtype)
        return carry

    lax.fori_loop(0, rows_per_step, body, 0, unroll=True)


def neighbourhood_attention(p, rpb, batch, seq):
    n_rows = seq // GRID_W
    rps = NAT_ROWS_PER_STEP
    steps = n_rows // rps
    bias = _nat_bias_table(rpb)
    cb = COL_NAT // BR_W
    tq = rps * GRID_W
    return pl.pallas_call(
        functools.partial(_nat_kernel, rows_per_step=rps, n_rows=n_rows),
        grid=(batch, steps),
        in_specs=[pl.BlockSpec((tq, BR_W), lambda b, j: (b * steps + j, cb)),
                  pl.BlockSpec((seq, BR_W), lambda b, j: (b, cb + 1)),
                  pl.BlockSpec((seq, BR_W), lambda b, j: (b, cb + 2)),
                  pl.BlockSpec(bias.shape, lambda b, j: (0, 0, 0))],
        out_specs=pl.BlockSpec((tq, BR_W), lambda b, j: (b * steps + j, 0)),
        out_shape=jax.ShapeDtypeStruct((batch * seq, BR_W), BF16),
        compiler_params=_cparams("parallel", "arbitrary"),
        name="nat",
    )(p, p, p, bias)


def _diff_kernel(lam_ref, slope_ref, sub_ref, q_ref, k_ref, v_ref, o_ref,
                 acc1, acc2, m1, l1, m2, l2, hill, stage_a1, stage_a2, stage_b1, stage_b2, *, lam_init, n_tiles):
    t = q_ref.shape[0]
    reps = t // 128
    qi = pl.program_id(2)
    slope2 = slope_ref[0, 0:1, 0:1] * LOG2E
    lp = lam_ref[...]
    lam = (jnp.exp(jnp.sum(lp[0:1] * lp[1:2], axis=-1, keepdims=True))
           - jnp.exp(jnp.sum(lp[2:3] * lp[3:4], axis=-1, keepdims=True)) + lam_init)

    @pl.when(qi == 0)
    def _():
        rel = (lax.broadcasted_iota(jnp.int32, (t, t), 1) - lax.broadcasted_iota(jnp.int32, (t, t), 0)).astype(F32)
        hill[...] = -slope2 * jnp.abs(rel)

    q = (q_ref[...].astype(F32) * (DIFF_SCALE * LOG2E)).astype(BF16)
    lane = lax.broadcasted_iota(jnp.int32, q.shape, 1)
    q1 = jnp.where(lane < 64, q, jnp.zeros_like(q))
    q2 = jnp.where(lane >= 64, q, jnp.zeros_like(q))

    acc1[...] = jnp.zeros_like(acc1)
    acc2[...] = jnp.zeros_like(acc2)
    l1[...] = jnp.zeros_like(l1)
    l2[...] = jnp.zeros_like(l2)
    m1[...] = jnp.full_like(m1, -jnp.inf)
    m2[...] = jnp.full_like(m2, -jnp.inf)

    def scores(kj):
        k = k_ref[pl.ds(pl.multiple_of(kj * t, t), t), :]
        return _nt(q1, k), _nt(q2, k)

    def absorb(kj, raw_scores, add_local_bias, tile_bias):
        v = v_ref[pl.ds(pl.multiple_of(kj * t, t), t), :]
        for s_raw, acc, m_ref, l_ref in zip(raw_scores, (acc1, acc2), (m1, m2), (l1, l2)):
            s = add_local_bias(s_raw)
            m_old = m_ref[...]
            m_new = jnp.maximum(m_old, jnp.max(s, axis=-1, keepdims=True) + tile_bias)
            a = jnp.exp2(m_old - m_new)
            pr = jnp.exp2(s - jnp.concatenate([m_new - tile_bias] * reps, axis=1))
            l_ref[...] = a * l_ref[...] + jnp.sum(pr, axis=-1, keepdims=True)
            acc[...] = a * acc[...] + _dot(pr.astype(BF16), v)
            m_ref[...] = m_new

    stage = ((stage_a1, stage_a2), (stage_b1, stage_b2))
    query_term = slope2 * lax.broadcasted_iota(jnp.int32, (t, 128), 0).astype(F32)
    order = [qi] + [jnp.where(u < qi, u, u + 1) for u in range(n_tiles - 1)]
    for ref, val in zip(stage[0], scores(order[0])):
        ref[...] = val
    for u, kj in enumerate(order):
        cur, nxt = stage[u % 2], stage[(u + 1) % 2]
        if u + 1 < len(order):
            for ref, val in zip(nxt, scores(order[u + 1])):
                ref[...] = val
        raw = (cur[0][...], cur[1][...])
        if u == 0:
            diagonal = hill[...]
            absorb(kj, raw, lambda s: s + diagonal, jnp.zeros((1, 1), F32))
        else:
            sign = jnp.where(jnp.full((1, 1), kj, jnp.int32) < qi, 1.0, -1.0)
            keys = (sign * slope2) * ((kj - qi) * t + lax.broadcasted_iota(jnp.int32, (1, t), 1)).astype(F32)
            absorb(kj, raw, lambda s, keys=keys: s + keys, -sign * query_term)

    o = acc1[...] / l1[...] - lam * (acc2[...] / l2[...])
    o = o * lax.rsqrt(jnp.mean(o * o, axis=-1, keepdims=True) + EPS) * sub_ref[0]
    o_ref[...] = (o * (1.0 - lam_init)).astype(o_ref.dtype)


def diff_attention(p, lam_params, subln, layer, batch, seq):
    t = DIFF_TILE
    nt = seq // t
    lam_init = 0.8 - 0.6 * math.exp(-0.3 * layer)
    slopes = np.exp2(-8.0 * (np.arange(N_HEADS, dtype=np.float64) + 1.0) / N_HEADS)
    slope_tab = jnp.asarray(np.broadcast_to(slopes[:, None, None], (N_HEADS, 8, 128)).astype(np.float32))
    qc, kc, vc = (COL_DIFF // DIFF_DV, (COL_DIFF + DIFF_W) // DIFF_DV, (COL_DIFF + 2 * DIFF_W) // DIFF_DV)
    return pl.pallas_call(
        functools.partial(_diff_kernel, lam_init=lam_init, n_tiles=nt),
        grid=(batch, N_HEADS, nt),
        in_specs=[pl.BlockSpec(lam_params.shape, lambda b, h, i: (0, 0)),
                  pl.BlockSpec((1, 8, 128), lambda b, h, i: (h, 0, 0)),
                  pl.BlockSpec((1, 1, DIFF_DV), lambda b, h, i: (h, 0, 0)),
                  pl.BlockSpec((t, DIFF_DV), lambda b, h, i: (b * nt + i, qc + h)),
                  pl.BlockSpec((seq, DIFF_DV), lambda b, h, i: (b, kc + h)),
                  pl.BlockSpec((seq, DIFF_DV), lambda b, h, i: (b, vc + h))],
        out_specs=pl.BlockSpec((t, DIFF_DV), lambda b, h, i: (b * nt + i, h)),
        out_shape=jax.ShapeDtypeStruct((batch * seq, DIFF_W), BF16),
        scratch_shapes=[pltpu.VMEM((t, DIFF_DV), F32)] * 6 + [pltpu.VMEM((t, t), F32)] * 5,
        compiler_params=_cparams("parallel", "parallel", "arbitrary"),
        name="diff_attn",
    )(lam_params, slope_tab, subln.reshape(N_HEADS, 1, DIFF_DV), p, p, p)


def _merge_kernel(hf_ref, hb_ref, rof, rob, rg, ynat, hof, hob, hg, ydiff,
                  wg_ref, bg_ref, wr_ref, wn_ref, wh_ref, wd_ref, wo_ref,
                  rgn_ref, hgn_ref, lng_ref, lnb_ref, of_ref, ob_ref):
    seg_mean = jnp.where(_block_diag_mask(BR_W), 1.0 / HEAD_DIM, 0.0).astype(BF16)

    def head_norm_gate(o, gn, gate):
        ms = _dot_split(o * o, seg_mean, terms=2)
        gx = gate.astype(F32)
        return (o * lax.rsqrt(ms + EPS) * gn * (gx * _sigmoid(gx))).astype(BF16)

    y_ret = head_norm_gate(rof[...] + rob[...], rgn_ref[...], rg[...])
    y_hgrn = head_norm_gate(hof[...] + hob[...], hgn_ref[...], hg[...])
    hb = hb_ref[...]
    d = hf_ref.shape[1]
    merged = None
    for i, (y, w_ref) in enumerate(((y_ret, wr_ref), (ynat[...], wn_ref), (y_hgrn, wh_ref), (ydiff[...], wd_ref))):
        gate = _sigmoid(_dot(hb, wg_ref[:, i * d:(i + 1) * d]) + bg_ref[:, i * d:(i + 1) * d])
        part = gate * _dot(y, w_ref[...])
        merged = part if merged is None else merged + part
    z = ALPHA * hf_ref[...] + _dot(merged.astype(BF16), wo_ref[...])
    h = _layer_norm(z, lng_ref[...], lnb_ref[...])
    of_ref[...] = h
    ob_ref[...] = h.astype(BF16)


def merge_and_norm(hf, hb, p, ret_f, ret_b, y_nat, hg_f, hg_b, y_diff,
                   w_gate, b_gate, w_br_ret, w_br_nat, w_br_hgrn, w_br_diff, w_out,
                   ret_gn, hgrn_gn, ln_g, ln_b, tm=512):
    n, d = hf.shape

    def rows(width, col=0):
        return pl.BlockSpec((tm, width), lambda i: (i, col))

    def whole(a):
        nd = a.ndim
        return pl.BlockSpec(a.shape, lambda i: (0,) * nd)

    consts = [w_gate, b_gate.reshape(1, -1), w_br_ret, w_br_nat, w_br_hgrn, w_br_diff, w_out,
              ret_gn.reshape(1, BR_W), hgrn_gn.reshape(1, BR_W), ln_g.reshape(1, d), ln_b.reshape(1, d)]
    return pl.pallas_call(
        _merge_kernel,
        grid=(n // tm,),
        in_specs=[rows(d), rows(d),
                  rows(BR_W), rows(BR_W), rows(BR_W, COL_RET // BR_W + 3),
                  rows(BR_W),
                  rows(BR_W), rows(BR_W), rows(BR_W, COL_HGRN // BR_W + 4),
                  rows(DIFF_W)] + [whole(a) for a in consts],
        out_specs=[rows(d), rows(d)],
        out_shape=[jax.ShapeDtypeStruct((n, d), F32), jax.ShapeDtypeStruct((n, d), BF16)],
        compiler_params=_cparams("parallel"),
        name="merge",
    )(hf, hb, ret_f, ret_b, p, y_nat, hg_f, hg_b, p, y_diff, *consts)


def _route(h, w, bias):
    w_hi = w.astype(BF16)
    w_lo = (w - w_hi.astype(F32)).astype(BF16)
    logits = _dot_split(h, w_hi, terms=3) + _dot_split(h, w_lo, terms=2)
    scores = _sigmoid(logits)
    biased = scores + bias
    shape = scores.shape
    lane = lax.broadcasted_iota(jnp.int32, shape, 1).astype(F32)
    group = lax.shift_right_logical(lax.broadcasted_iota(jnp.int32, shape, 1),
                                    GROUP_SIZE.bit_length() - 1).astype(F32)
    big = jnp.asarray(1e9, F32)
    neg = jnp.asarray(-jnp.inf, F32)

    def first_max(x, ids):
        m = jnp.max(x, axis=-1, keepdims=True)
        return m, jnp.min(jnp.where(x == m, ids, big), axis=-1, keepdims=True)

    gscore = jnp.zeros(shape, F32)
    for g in range(N_GROUPS):
        xg = jnp.where(group == g, biased, neg)
        m1, i1 = first_max(xg, lane)
        m2 = jnp.max(jnp.where(lane == i1, neg, xg), axis=-1, keepdims=True)
        gscore = jnp.where(group == g, m1 + m2, gscore)
    gsel = jnp.zeros(shape, jnp.bool_)
    for _ in range(TOPK_GROUPS):
        _, gi = first_max(gscore, group)
        hit = group == gi
        gsel = gsel | hit
        gscore = jnp.where(hit, neg, gscore)
    cand = jnp.where(gsel, biased, MASK_VALUE)
    esel = jnp.zeros(shape, jnp.bool_)
    picks = []
    for _ in range(TOP_K):
        _, ei = first_max(cand, lane)
        hit = lane == ei
        picks.append((ei, jnp.sum(jnp.where(hit, scores, 0.0), axis=-1, keepdims=True)))
        esel = esel | hit
        cand = jnp.where(hit, neg, cand)

    tm = shape[0]
    sel = jnp.where(esel, 1.0, 0.0)
    counts = jnp.concatenate(
        [jnp.sum(sel[s * ROUTE_TILE:(s + 1) * ROUTE_TILE], axis=0, keepdims=True) for s in range(tm // ROUTE_TILE)],
        axis=0)

    slot = lax.broadcasted_iota(jnp.int32, (tm, TOP_K), 1)
    idx8 = jnp.zeros((tm, TOP_K), F32)
    w8 = jnp.zeros((tm, TOP_K), F32)
    wsum = jnp.zeros((tm, 1), F32)
    for k, (ei, wk) in enumerate(picks):
        idx8 = jnp.where(slot == k, ei, idx8)
        w8 = jnp.where(slot == k, wk, w8)
        wsum = wsum + wk
    return idx8.astype(jnp.int32), w8 / (wsum + 1e-20) * ROUTED_SCALE, counts


def _router_kernel(h_ref, w_ref, bias_ref, idx_ref, w8_ref, cnt_ref):
    idx_ref[...], w8_ref[...], cnt_ref[...] = _route(h_ref[...], w_ref[...], bias_ref[...])


def router(hf, w_router, router_bias, tm=1024):
    n, d = hf.shape
    e = w_router.shape[1]
    sub = tm // ROUTE_TILE
    return pl.pallas_call(
        _router_kernel,
        grid=(n // tm,),
        in_specs=[pl.BlockSpec((tm, d), lambda i: (i, 0)),
                  pl.BlockSpec((d, e), lambda i: (0, 0)),
                  pl.BlockSpec((1, e), lambda i: (0, 0))],
        out_specs=[pl.BlockSpec((tm, TOP_K), lambda i: (i, 0)),
                   pl.BlockSpec((tm, TOP_K), lambda i: (i, 0)),
                   pl.BlockSpec((sub, e), lambda i: (i, 0))],
        out_shape=[jax.ShapeDtypeStruct((n, TOP_K), jnp.int32),
                   jax.ShapeDtypeStruct((n, TOP_K), F32),
                   jax.ShapeDtypeStruct((n // ROUTE_TILE, e), F32)],
        compiler_params=_cparams("arbitrary"),
        name="router",
    )(hf, w_router, router_bias.reshape(1, e))


HIGH_HALF = -65536


def _pack_pairs(x):
    w = x.shape[1] // 2
    lo = lax.bitcast_convert_type(x[:, :w].astype(BF16).astype(F32), jnp.int32)
    hi = lax.bitcast_convert_type(x[:, w:].astype(BF16).astype(F32), jnp.int32)
    return lax.shift_right_logical(lo, 16) | (hi & HIGH_HALF)


def _unpack_pairs(p):
    lo = lax.bitcast_convert_type(lax.shift_left(p, 16), F32)
    hi = lax.bitcast_convert_type(p & HIGH_HALF, F32)
    return jnp.concatenate([lo, hi], axis=1)


def _swiglu(x, wg, wu):
    a = _dot(x, wg)
    return a * _sigmoid(a) * _dot(x, wu)


def _segment_copies(tab_ref, first_col, max_run, make_copy, action):
    top_bit = max_run.bit_length() - 1
    low_bit = RUN_ALIGN.bit_length() - 1
    mid_bit = min(top_bit + 1, LONG_RUN.bit_length() - 1)

    def pieces(count, local, glob, bits):
        for b in bits:
            size = 1 << b
            taken = count & size

            @pl.when(taken != 0)
            def _(local=local, glob=glob, size=size):
                action(make_copy(pl.multiple_of(local, RUN_ALIGN), pl.multiple_of(glob, RUN_ALIGN), size))

            local = local + taken
            glob = glob + taken

    def per_expert(e, carry):
        col = first_col + e
        count = tab_ref[0, col]
        local = tab_ref[1, col]
        glob = tab_ref[2, col]
        long_part = count & -LONG_RUN

        @pl.when(long_part != 0)
        def _():
            pieces(count, local, glob, range(top_bit, mid_bit - 1, -1))

        pieces(count, local + long_part, glob + long_part, range(mid_bit - 1, low_bit - 1, -1))
        return carry

    lax.fori_loop(0, N_EXPERTS, per_expert, 0)


def _tile_rows_used(tab_ref, t):
    last = t * N_EXPERTS + (N_EXPERTS - 1)
    return tab_ref[1, last] + tab_ref[0, last]


def _wait_rows(total, max_rows, make_copy):
    for b in range(max_rows.bit_length() - 1, RUN_ALIGN.bit_length() - 2, -1):
        size = 1 << b

        @pl.when((total & size) != 0)
        def _(size=size):
            make_copy(0, 0, size).wait()


def _block_dispatch_kernel(tab_ref, gap_ref, tail_ref, idx_t_ref, hb_ref, xs_ref, sorted_buf, zeros_buf, sem):
    t = pl.program_id(0)
    tm = hb_ref.shape[0]
    rows = sorted_buf.shape[0]
    idx_t = idx_t_ref[...]
    expert = lax.broadcasted_iota(jnp.int32, (N_EXPERTS, tm), 0)
    sel_t = jnp.zeros((N_EXPERTS, tm), F32)
    for k in range(TOP_K):
        sel_t = sel_t + jnp.where(idx_t[k:k + 1, :] == expert, 1.0, 0.0)
    sel_b = sel_t.astype(BF16)
    before_tok = (lax.broadcasted_iota(jnp.int32, (tm, tm), 0) < lax.broadcasted_iota(jnp.int32, (tm, tm), 1))
    before_exp = (lax.broadcasted_iota(jnp.int32, (N_EXPERTS, N_EXPERTS), 1)
                  < lax.broadcasted_iota(jnp.int32, (N_EXPERTS, N_EXPERTS), 0))
    run = jnp.broadcast_to(_run_length(jnp.sum(sel_t, axis=1, keepdims=True)), (N_EXPERTS, tm)).astype(BF16)
    base = (_dot(jnp.where(before_exp, 1.0, 0.0).astype(BF16), run)
            + _dot(sel_b, jnp.where(before_tok, 1.0, 0.0).astype(BF16)))
    row = lax.broadcasted_iota(jnp.int32, (rows, tm), 0).astype(jnp.int16)
    perm = jnp.zeros((rows, tm), BF16)
    for k in range(TOP_K):
        place = jnp.sum(jnp.where(idx_t[k:k + 1, :] == expert, base, 0.0), axis=0, keepdims=True)
        perm = jnp.where(row == place.astype(jnp.int32).astype(jnp.int16), jnp.ones_like(perm), perm)
    half = hb_ref.shape[1] // 2
    lo = lax.bitcast_convert_type(_dot(perm, hb_ref[:, :half]), jnp.int32)
    hi = lax.bitcast_convert_type(_dot(perm, hb_ref[:, half:]), jnp.int32)
    sorted_buf[...] = lax.shift_right_logical(lo, 16) | (hi & HIGH_HALF)

    def make_copy(local, glob, size):
        return pltpu.make_async_copy(sorted_buf.at[pl.ds(local, size)], xs_ref.at[pl.ds(glob, size)], sem.at[0])

    _segment_copies(tab_ref, t * N_EXPERTS, tm, make_copy, lambda c: c.start())
    _wait_rows(_tile_rows_used(tab_ref, t), rows, make_copy)

    @pl.when(t == pl.num_programs(0) - 1)
    def _():
        zeros_buf[...] = jnp.zeros_like(zeros_buf)
        tile_rows = zeros_buf.shape[0]

        def zero_copy(local, glob, size):
            return pltpu.make_async_copy(zeros_buf.at[pl.ds(0, size)], xs_ref.at[pl.ds(glob, size)], sem.at[0])

        def tail_copy(j):
            return zero_copy(0, pl.multiple_of(tail_ref[0] + j * tile_rows, RUN_ALIGN), tile_rows)

        _segment_copies(gap_ref, 0, tile_rows // 2, zero_copy, lambda c: c.start())
        lax.fori_loop(0, tail_ref[1], lambda j, c: (tail_copy(j).start(), c)[1], 0)
        _segment_copies(gap_ref, 0, tile_rows // 2, zero_copy, lambda c: c.wait())
        lax.fori_loop(0, tail_ref[1], lambda j, c: (tail_copy(j).wait(), c)[1], 0)


def block_dispatch(hb, idx_t, tab, gaps, tail, tm):
    n, d = hb.shape
    grid_spec = pltpu.PrefetchScalarGridSpec(
        num_scalar_prefetch=3,
        grid=(n // tm,),
        in_specs=[pl.BlockSpec((TOP_K, tm), lambda i, *_: (0, i)),
                  pl.BlockSpec((tm, d), lambda i, *_: (i, 0))],
        out_specs=pl.BlockSpec(memory_space=pl.ANY),
        scratch_shapes=[pltpu.VMEM((_tile_rows(tm), d // 2), jnp.int32),
                        pltpu.VMEM((EXPERT_ROWS, d // 2), jnp.int32),
                        pltpu.SemaphoreType.DMA((1,))],
    )
    return pl.pallas_call(
        _block_dispatch_kernel,
        grid_spec=grid_spec,
        out_shape=jax.ShapeDtypeStruct((_sorted_rows(n, tm), d // 2), jnp.int32),
        compiler_params=_cparams("arbitrary"),
        name="dispatch",
    )(tab, gaps, tail, idx_t, hb)


def _tile_tables(tile_counts, tm_fine, tm):
    g = tm // tm_fine
    n_tokens = tile_counts.shape[0] * tm_fine
    n_tiles = _sorted_rows(n_tokens, tm) // EXPERT_ROWS
    cnt = tile_counts.astype(jnp.int32).reshape(-1, g, N_EXPERTS).sum(axis=1)
    cnt = (cnt + (RUN_ALIGN - 1)) // RUN_ALIGN * RUN_ALIGN
    local = jnp.cumsum(cnt, axis=1) - cnt
    used = cnt.sum(axis=0)
    seg = (used + (EXPERT_ROWS - 1)) // EXPERT_ROWS * EXPERT_ROWS
    seg_end = jnp.cumsum(seg)
    seg_start = seg_end - seg
    glob = seg_start[None, :] + jnp.cumsum(cnt, axis=0) - cnt
    runs = jnp.stack([cnt.reshape(-1), local.reshape(-1), glob.reshape(-1)])
    gaps = jnp.stack([seg - used, jnp.zeros_like(used), seg_start + used])
    tail = jnp.stack([seg_end[-1], n_tiles - seg_end[-1] // EXPERT_ROWS])
    tile_row = jnp.arange(n_tiles, dtype=jnp.int32) * EXPERT_ROWS
    owner = jnp.sum((seg_end[None, :] <= tile_row[:, None]).astype(jnp.int32), axis=1)
    owner_c = jnp.minimum(owner, N_EXPERTS - 1)
    of_owner = lambda a: jnp.sum(jnp.where(jnp.arange(N_EXPERTS)[None, :] == owner_c[:, None], a[None, :], 0), axis=1)
    filled = jnp.clip(of_owner(seg_start + used) - tile_row, 0, EXPERT_ROWS)
    source = jnp.minimum(jnp.arange(n_tiles, dtype=jnp.int32), jnp.maximum(seg_end[-1] // EXPERT_ROWS - 1, 0))
    tiles = jnp.stack([owner_c, jnp.where(owner < N_EXPERTS, filled, 0), source])
    return runs, gaps, tail, tiles.astype(jnp.int32)


def _expert_kernel(tile_ref, xs_ref, wg_ref, wu_ref, wd_ref, ys_ref, wgb, wub, wdb):
    i = pl.program_id(0)
    expert = tile_ref[0, i]

    @pl.when(tile_ref[1, i] > 0)
    def _():
        @pl.when((i == 0) | (expert != tile_ref[0, jnp.maximum(i - 1, 0)]))
        def _():
            wgb[...] = wg_ref[0, 0].astype(BF16)
            wub[...] = wu_ref[0, 0].astype(BF16)
            wdb[...] = wd_ref[0, 0].astype(BF16)

        x = _unpack_pairs(xs_ref[...]).astype(BF16)
        hid = _swiglu(x, wgb[...], wub[...])
        ys_ref[...] = _pack_pairs(_dot(hid.astype(BF16), wdb[...]))

    @pl.when(tile_ref[1, i] == 0)
    def _():
        ys_ref[...] = jnp.zeros_like(ys_ref)


def grouped_experts(xs, tiles, w_e_gate, w_e_up, w_e_down, layer):
    rows, w = xs.shape
    _, _, d, hid = w_e_gate.shape
    tr = EXPERT_ROWS
    grid_spec = pltpu.PrefetchScalarGridSpec(
        num_scalar_prefetch=1,
        grid=(rows // tr,),
        in_specs=[pl.BlockSpec((tr, w), lambda i, tl: (tl[2, i], 0)),
                  pl.BlockSpec((1, 1, d, hid), lambda i, tl: (layer, tl[0, i], 0, 0)),
                  pl.BlockSpec((1, 1, d, hid), lambda i, tl: (layer, tl[0, i], 0, 0)),
                  pl.BlockSpec((1, 1, hid, d), lambda i, tl: (layer, tl[0, i], 0, 0))],
        out_specs=pl.BlockSpec((tr, w), lambda i, tl: (i, 0)),
        scratch_shapes=[pltpu.VMEM((d, hid), BF16), pltpu.VMEM((d, hid), BF16), pltpu.VMEM((hid, d), BF16)],
    )
    return pl.pallas_call(
        _expert_kernel,
        grid_spec=grid_spec,
        out_shape=jax.ShapeDtypeStruct((rows, w), jnp.int32),
        compiler_params=_cparams("arbitrary"),
        name="experts",
    )(tiles, xs, w_e_gate, w_e_up, w_e_down)


def _block_combine_kernel(tab_ref, ys_ref, idx_ref, w8_ref, hf_ref, hb_ref, sg_ref, su_ref, sd_ref,
                          lng_ref, lnb_ref, of_ref, ob_ref, buf, sem):
    t = pl.program_id(0)
    tm = hf_ref.shape[0]
    rows = buf.shape[0]

    def make_copy(local, glob, size):
        return pltpu.make_async_copy(ys_ref.at[pl.ds(glob, size)], buf.at[pl.ds(local, size)], sem.at[0])

    @pl.when(t == 0)
    def _():
        buf[...] = jnp.zeros_like(buf)

    _segment_copies(tab_ref, t * N_EXPERTS, tm, make_copy, lambda c: c.start())

    idx = idx_ref[...]
    w8 = w8_ref[...]
    expert = lax.broadcasted_iota(jnp.int32, (tm, N_EXPERTS), 1)
    sel = jnp.zeros((tm, N_EXPERTS), F32)
    for k in range(TOP_K):
        sel = sel + jnp.where(idx[:, k:k + 1] == expert, 1.0, 0.0)
    sel_b = sel.astype(BF16)
    before_tok = (lax.broadcasted_iota(jnp.int32, (tm, tm), 1) < lax.broadcasted_iota(jnp.int32, (tm, tm), 0))
    before_exp = (lax.broadcasted_iota(jnp.int32, (N_EXPERTS, N_EXPERTS), 0)
                  < lax.broadcasted_iota(jnp.int32, (N_EXPERTS, N_EXPERTS), 1))
    run = jnp.broadcast_to(_run_length(jnp.sum(sel, axis=0, keepdims=True)), (tm, N_EXPERTS)).astype(BF16)
    base = (_dot(run, jnp.where(before_exp, 1.0, 0.0).astype(BF16))
            + _dot(jnp.where(before_tok, 1.0, 0.0).astype(BF16), sel_b))
    col = lax.broadcasted_iota(jnp.int32, (tm, rows), 1).astype(jnp.int16)
    w8_b = w8.astype(BF16)
    mix = jnp.zeros((tm, rows), BF16)
    for k in range(TOP_K):
        place = jnp.sum(jnp.where(idx[:, k:k + 1] == expert, base, 0.0), axis=1, keepdims=True)
        mix = jnp.where(col == place.astype(jnp.int32).astype(jnp.int16),
                        jnp.broadcast_to(w8_b[:, k:k + 1], mix.shape), mix)

    acc = _dot(_swiglu(hb_ref[...], sg_ref[...], su_ref[...]).astype(BF16), sd_ref[...])
    _wait_rows(_tile_rows_used(tab_ref, t), rows, make_copy)
    y = _unpack_pairs(buf[...]).astype(BF16)
    acc = acc + _dot(mix, y)
    h = _layer_norm(ALPHA * hf_ref[...] + acc, lng_ref[...], lnb_ref[...])
    of_ref[...] = h
    ob_ref[...] = h.astype(BF16)


def block_combine(ys, tab, idx8, w8, hf, hb, w_s_gate, w_s_up, w_s_down, ln_g, ln_b, tm):
    n, d = hf.shape
    w = ys.shape[1]

    def whole(a):
        return pl.BlockSpec(a.shape, lambda i, tab: (0, 0))

    def rows(width):
        return pl.BlockSpec((tm, width), lambda i, tab: (i, 0))

    consts = [w_s_gate, w_s_up, w_s_down, ln_g.reshape(1, d), ln_b.reshape(1, d)]
    grid_spec = pltpu.PrefetchScalarGridSpec(
        num_scalar_prefetch=1,
        grid=(n // tm,),
        in_specs=[pl.BlockSpec(memory_space=pl.ANY), rows(TOP_K), rows(TOP_K), rows(d), rows(d)]
                 + [whole(a) for a in consts],
        out_specs=[rows(d), rows(d)],
        scratch_shapes=[pltpu.VMEM((_tile_rows(tm), w), jnp.int32), pltpu.SemaphoreType.DMA((1,))],
    )
    return pl.pallas_call(
        _block_combine_kernel,
        grid_spec=grid_spec,
        out_shape=[jax.ShapeDtypeStruct((n, d), F32), jax.ShapeDtypeStruct((n, d), BF16)],
        compiler_params=_cparams("arbitrary"),
        name="combine",
    )(tab, ys, idx8, w8, hf, hb, *consts)


def moe_and_norm(hf, hb, w_router, router_bias, w_e_gate, w_e_up, w_e_down,
                 w_s_gate, w_s_up, w_s_down, ln_g, ln_b, layer):
    idx8, w8, tile_counts = router(hf, w_router, router_bias)
    runs, gaps, tail, tiles = _tile_tables(tile_counts, ROUTE_TILE, MOE_TILE)
    xs = block_dispatch(hb, idx8.T, runs, gaps, tail, MOE_TILE)
    ys = grouped_experts(xs, tiles, w_e_gate, w_e_up, w_e_down, layer)
    return block_combine(ys, runs, idx8, w8, hf, hb, w_s_gate, w_s_up, w_s_down, ln_g, ln_b, MOE_TILE)


def kernel(x, ln_in_g, ln_in_b, w_in, w_gate, b_gate, w_br_ret, w_br_nat, w_br_hgrn, w_br_diff, w_out,
           ret_gn, nat_rpb, hgrn_lb, hgrn_gn, diff_lambda, diff_subln, ln1_g, ln1_b, w_router,
           router_bias, w_e_gate, w_e_up, w_e_down, w_s_gate, w_s_up, w_s_down, ln2_g, ln2_b):
    batch, seq, d = x.shape
    bf = lambda a: a.astype(BF16)
    hf, hb = input_layer_norm(x.reshape(batch * seq, d), ln_in_g, ln_in_b)
    for l in range(DEPTH):
        p = in_projection(hb, bf(w_in[l]))
        ret_f, ret_b = retention(p, batch, seq)
        y_nat = neighbourhood_attention(p, nat_rpb[l], batch, seq)
        hg_f, hg_b = hgrn(p, hgrn_lb, l, batch, seq)
        y_diff = diff_attention(p, diff_lambda[l], diff_subln[l], l, batch, seq)
        hf, hb = merge_and_norm(hf, hb, p, ret_f, ret_b, y_nat, hg_f, hg_b, y_diff,
                                bf(w_gate[l]), b_gate[l], bf(w_br_ret[l]), bf(w_br_nat[l]),
                                bf(w_br_hgrn[l]), bf(w_br_diff[l]), bf(w_out[l]),
                                ret_gn[l], hgrn_gn[l], ln1_g[l], ln1_b[l])
        hf, hb = moe_and_norm(hf, hb, w_router[l], router_bias[l], w_e_gate, w_e_up, w_e_down,
                              bf(w_s_gate[l]), bf(w_s_up[l]), bf(w_s_down[l]), ln2_g[l], ln2_b[l], l)
    return hf.reshape(batch, seq, d)
```

```python
import functools
import math

import numpy as np
import jax
import jax.numpy as jnp
from jax import lax
from jax.experimental import pallas as pl
from jax.experimental.pallas import tpu as pltpu

F32 = jnp.float32
BF16 = jnp.bfloat16

DEPTH = 2
GRID_W = 64
HEAD_DIM = 64
N_HEADS = 4
BR_W = N_HEADS * HEAD_DIM
NAT_KH = 8
NAT_KW = 16
DIFF_DV = 128
DIFF_W = N_HEADS * DIFF_DV
N_EXPERTS = 64
TOP_K = 8
N_GROUPS = 8
TOPK_GROUPS = 4
GROUP_SIZE = N_EXPERTS // N_GROUPS
ROUTED_SCALE = 2.5
EPS = 1e-5
MASK_VALUE = -1e30
LOG2E = math.log2(math.e)
DIFF_SCALE = 64 ** -0.5
ALPHA = (2.0 * DEPTH) ** 0.25
COL_RET = 0
COL_NAT = 1024
COL_HGRN = 1792
COL_DIFF = 3072

VMEM_LIMIT = 56 * 1024 * 1024

RET_CHUNK = 256
HGRN_CHUNK = 128
HGRN_SUB = 16
DIFF_TILE = 1024
NAT_ROWS_PER_STEP = 16
ROUTE_TILE = 128
MOE_TILE = 256
EXPERT_ROWS = 1024
RUN_ALIGN = 8
LONG_RUN = 64


def _run_length(count):
    return jnp.floor((count + (RUN_ALIGN - 1)) * (1.0 / RUN_ALIGN)) * RUN_ALIGN


def _tile_rows(tm):
    bound = TOP_K * tm + N_EXPERTS * (RUN_ALIGN - 1)
    return -(-bound // 128) * 128


def _sorted_rows(n_tokens, tm):
    bound = (TOP_K * n_tokens + (n_tokens // tm) * N_EXPERTS * (RUN_ALIGN - 1)
             + N_EXPERTS * (EXPERT_ROWS - RUN_ALIGN))
    return -(-bound // EXPERT_ROWS) * EXPERT_ROWS


def _cparams(*sem):
    return pltpu.CompilerParams(dimension_semantics=sem, vmem_limit_bytes=VMEM_LIMIT)


def _nt(a, b):
    return lax.dot_general(a, b, (((1,), (1,)), ((), ())), preferred_element_type=F32)


def _tn(a, b):
    return lax.dot_general(a, b, (((0,), (0,)), ((), ())), preferred_element_type=F32)


def _dot(a, b):
    return jnp.dot(a, b, preferred_element_type=F32)


def _dot_split(x, w_bf16, terms=3):
    acc = None
    rem = x
    for _ in range(terms):
        piece = rem.astype(BF16)
        part = _dot(piece, w_bf16)
        acc = part if acc is None else acc + part
        rem = rem - piece.astype(F32)
    return acc


def _layer_norm(z, g, b):
    mu = jnp.mean(z, axis=-1, keepdims=True)
    zc = z - mu
    var = jnp.mean(zc * zc, axis=-1, keepdims=True)
    return zc * lax.rsqrt(var + EPS) * g + b


def _sigmoid(x):
    return 1.0 / (1.0 + jnp.exp(-x))


def _head_id(shape, axis):
    return lax.shift_right_logical(lax.broadcasted_iota(jnp.int32, shape, axis), HEAD_DIM.bit_length() - 1)


def _block_diag_mask(n):
    return _head_id((n, n), 0) == _head_id((n, n), 1)


def _ln_kernel(x_ref, g_ref, b_ref, hf_ref, hb_ref):
    h = _layer_norm(x_ref[...], g_ref[...], b_ref[...])
    hf_ref[...] = h
    hb_ref[...] = h.astype(BF16)


def input_layer_norm(x2, g, b, tm=1024):
    n, d = x2.shape
    return pl.pallas_call(
        _ln_kernel,
        grid=(n // tm,),
        in_specs=[pl.BlockSpec((tm, d), lambda i: (i, 0)),
                  pl.BlockSpec((1, d), lambda i: (0, 0)),
                  pl.BlockSpec((1, d), lambda i: (0, 0))],
        out_specs=[pl.BlockSpec((tm, d), lambda i: (i, 0)),
                   pl.BlockSpec((tm, d), lambda i: (i, 0))],
        out_shape=[jax.ShapeDtypeStruct((n, d), F32), jax.ShapeDtypeStruct((n, d), BF16)],
        compiler_params=_cparams("parallel"),
        name="input_ln",
    )(x2, g.reshape(1, d), b.reshape(1, d))


def _matmul_kernel(h_ref, w_ref, o_ref):
    o_ref[...] = _dot(h_ref[...], w_ref[...]).astype(o_ref.dtype)


def in_projection(hb, w_bf16, tm=2048, tn=1536):
    n, d = hb.shape
    w = w_bf16.shape[1]
    return pl.pallas_call(
        _matmul_kernel,
        grid=(n // tm, w // tn),
        in_specs=[pl.BlockSpec((tm, d), lambda i, j: (i, 0)),
                  pl.BlockSpec((d, tn), lambda i, j: (0, j))],
        out_specs=pl.BlockSpec((tm, tn), lambda i, j: (i, j)),
        out_shape=jax.ShapeDtypeStruct((n, w), BF16),
        compiler_params=_cparams("parallel", "arbitrary"),
        name="in_proj",
    )(hb, w_bf16)


def _retention_tables(c):
    idx = np.arange(N_HEADS, dtype=np.float64)
    lg = [np.log1p(-np.exp2(-5.0 - 2.0 * idx)), np.log1p(-np.exp2(-6.0 - 2.0 * idx))]
    i = np.arange(c, dtype=np.float64)
    diff = i[:, None] - i[None, :]
    lane_head = np.repeat(np.arange(N_HEADS), HEAD_DIM)
    dmat = np.zeros((2, N_HEADS, c, c), np.float32)
    oscale = np.zeros((2, c, BR_W), np.float32)
    kscale = np.zeros((2, c, BR_W), np.float32)
    sdecay = np.zeros((2, 1, BR_W), np.float32)
    for h in range(N_HEADS):
        dmat[0, h] = np.where(diff >= 0, np.exp(lg[0][h] * np.maximum(diff, 0)), 0.0)
        dmat[1, h] = np.where(diff <= 0, np.exp(lg[1][h] * np.maximum(-diff, 0)), 0.0)
    oscale[0] = np.exp(lg[0][lane_head][None, :] * (i[:, None] + 1.0))
    oscale[1] = np.exp(lg[1][lane_head][None, :] * (c - i[:, None]))
    kscale[0] = np.exp(lg[0][lane_head][None, :] * (c - 1.0 - i[:, None]))
    kscale[1] = np.exp(lg[1][lane_head][None, :] * i[:, None])
    sdecay[0, 0] = np.exp(lg[0][lane_head] * c)
    sdecay[1, 0] = np.exp(lg[1][lane_head] * c)
    return (jnp.asarray(dmat), jnp.asarray(oscale), jnp.asarray(kscale), jnp.asarray(sdecay))


def _retention_kernel(qf, kf, vf, qb, kb, vb, dmat, oscale, kscale, sdecay, of_ref, ob_ref, state):
    @pl.when(pl.program_id(1) == 0)
    def _():
        state[...] = jnp.zeros_like(state)

    c = qf.shape[0]
    lane_head = _head_id((c, BR_W), 1)
    bd = _block_diag_mask(BR_W)
    for dirn, (q_ref, k_ref, v_ref, o_ref) in enumerate(((qf, kf, vf, of_ref), (qb, kb, vb, ob_ref))):
        q = q_ref[...]
        k = k_ref[...] * jnp.asarray(HEAD_DIM ** -0.5, BF16)
        v = v_ref[...]
        s_old = state[dirn]
        out = _dot(q, s_old.astype(BF16)) * oscale[dirn]
        for h in range(N_HEADS):
            qz = jnp.where(lane_head == h, q, jnp.zeros_like(q))
            a = (_nt(qz, k) * dmat[dirn, h]).astype(BF16)
            out = out + jnp.where(lane_head == h, _dot(a, v), 0.0)
        o_ref[...] = out
        kw = (k.astype(F32) * kscale[dirn]).astype(BF16)
        state[dirn] = s_old * sdecay[dirn] + jnp.where(bd, _tn(kw, v), 0.0)


def retention(p, batch, seq):
    c = RET_CHUNK
    nc = seq // c
    dmat, oscale, kscale, sdecay = _retention_tables(c)
    cb = COL_RET // BR_W

    def fwd(col):
        return pl.BlockSpec((c, BR_W), lambda b, j: (b * nc + j, col))

    def bwd(col):
        return pl.BlockSpec((c, BR_W), lambda b, j: (b * nc + nc - 1 - j, col))

    def whole(a):
        nd = a.ndim
        return pl.BlockSpec(a.shape, lambda b, j: (0,) * nd)

    n = batch * seq
    return pl.pallas_call(
        _retention_kernel,
        grid=(batch, nc),
        in_specs=[fwd(cb), fwd(cb + 1), fwd(cb + 2), bwd(cb), bwd(cb + 1), bwd(cb + 2),
                  whole(dmat), whole(oscale), whole(kscale), whole(sdecay)],
        out_specs=[pl.BlockSpec((c, BR_W), lambda b, j: (b * nc + j, 0)),
                   pl.BlockSpec((c, BR_W), lambda b, j: (b * nc + nc - 1 - j, 0))],
        out_shape=[jax.ShapeDtypeStruct((n, BR_W), F32)] * 2,
        scratch_shapes=[pltpu.VMEM((2, BR_W, BR_W), F32)],
        compiler_params=_cparams("parallel", "arbitrary"),
        name="retention",
    )(p, p, p, p, p, p, dmat, oscale, kscale, sdecay)


def _hgrn_kernel(lb_ref, qf, ff, vf, qb, fb, vb, of_ref, ob_ref, state, *, layer):
    c = qf.shape[0]
    sb = HGRN_SUB
    n_sub = c // sb

    @pl.when(pl.program_id(1) == 0)
    def _():
        state[...] = jnp.zeros_like(state)

    lb = lb_ref[...]
    e = jnp.exp(lb - jnp.max(lb, axis=0, keepdims=True))
    prob = e / jnp.sum(e, axis=0, keepdims=True)
    lower = jnp.sum(prob[:layer + 1], axis=0) - prob[0]

    row = lax.broadcasted_iota(jnp.int32, (c, c), 0)
    col = lax.broadcasted_iota(jnp.int32, (c, c), 1)
    bd = _block_diag_mask(BR_W)
    seg_ones = jnp.where(bd, 1.0, 0.0).astype(BF16)
    rowv = lax.broadcasted_iota(jnp.int32, (c, BR_W), 0)

    for dirn, (q_ref, f_ref, v_ref, o_ref) in enumerate(((qf, ff, vf, of_ref), (qb, fb, vb, ob_ref))):
        rev = dirn == 1
        lo = lower[dirn:dirn + 1, :]
        fpre = f_ref[...].astype(F32)
        logf = jnp.log(lo + (1.0 - lo) * _sigmoid(fpre))
        kk = (1.0 - lo) * _sigmoid(-fpre)
        qx = q_ref[...].astype(F32)
        q = qx * _sigmoid(qx)
        v_bf = v_ref[...]
        v = v_bf.astype(F32)
        tri = jnp.where((col >= row) if rev else (col <= row), 1.0, 0.0).astype(BF16)
        g = _tri_cumsum(tri, logf)
        g_end = g[0:1, :] if rev else g[c - 1:c, :]
        s_old = state[dirn]
        inter = _nt((q * jnp.exp(g)).astype(BF16), s_old.astype(BF16))
        kd = (kk * jnp.exp(g_end - g)).astype(BF16)
        state[dirn] = s_old * jnp.exp(g_end) + jnp.where(bd, _tn(v_bf, kd), 0.0)
        qs, ks_ = [], []
        for j in range(1, n_sub):
            if rev:
                edge = g[j * sb:j * sb + 1, :]
                key_rows = (rowv >= j * sb) & (rowv < (j + 1) * sb)
                query_rows = rowv < j * sb
            else:
                edge = g[j * sb - 1:j * sb, :]
                key_rows = (rowv >= (j - 1) * sb) & (rowv < j * sb)
                query_rows = rowv >= j * sb
            qs.append(jnp.where(query_rows, q * jnp.exp(jnp.minimum(g - edge, 0.0)), 0.0).astype(BF16))
            ks_.append(jnp.where(key_rows, kk * jnp.exp(jnp.minimum(edge - g, 0.0)), 0.0).astype(BF16))
        bd_all = jnp.concatenate([bd] * (n_sub - 1), axis=1)
        cross = jnp.where(bd_all, _tn(v_bf, jnp.concatenate(ks_, axis=1)), 0.0).astype(BF16)
        intra = _nt(jnp.concatenate(qs, axis=1), cross)
        for d in range(sb):
            shift = (c - d) % c if rev else d
            pair = ((rowv % sb) + d < sb) if rev else ((rowv % sb) >= d)
            k_d, g_d, v_d = (kk, g, v) if d == 0 else tuple(pltpu.roll(a, shift, 0) for a in (kk, g, v))
            term = jnp.where(pair, q, 0.0) * k_d * jnp.exp(jnp.minimum(g - g_d, 0.0))
            intra = intra + _dot(term.astype(BF16), seg_ones) * v_d
        o_ref[...] = inter + intra


def _tri_cumsum(tri_bf16, x):
    acc = None
    rem = x
    for _ in range(3):
        piece = rem.astype(BF16)
        part = _dot(tri_bf16, piece)
        acc = part if acc is None else acc + part
        rem = rem - piece.astype(F32)
    return acc


def hgrn(p, hgrn_lb, layer, batch, seq):
    c = HGRN_CHUNK
    nc = seq // c
    cb = COL_HGRN // BR_W

    def fwd(col):
        return pl.BlockSpec((c, BR_W), lambda b, j: (b * nc + j, col))

    def bwd(col):
        return pl.BlockSpec((c, BR_W), lambda b, j: (b * nc + nc - 1 - j, col))

    n = batch * seq
    return pl.pallas_call(
        functools.partial(_hgrn_kernel, layer=layer),
        grid=(batch, nc),
        in_specs=[pl.BlockSpec(hgrn_lb.shape, lambda b, j: (0, 0, 0)),
                  fwd(cb), fwd(cb + 1), fwd(cb + 3), bwd(cb), bwd(cb + 2), bwd(cb + 3)],
        out_specs=[pl.BlockSpec((c, BR_W), lambda b, j: (b * nc + j, 0)),
                   pl.BlockSpec((c, BR_W), lambda b, j: (b * nc + nc - 1 - j, 0))],
        out_shape=[jax.ShapeDtypeStruct((n, BR_W), F32)] * 2,
        scratch_shapes=[pltpu.VMEM((2, BR_W, BR_W), F32)],
        compiler_params=_cparams("parallel", "arbitrary"),
        name="hgrn",
    )(hgrn_lb, p, p, p, p, p, p)


def _nat_bias_table(rpb):
    cq = np.arange(GRID_W)
    ck = np.arange(GRID_W)
    col_start = np.clip(cq - NAT_KW // 2, 0, GRID_W - NAT_KW)
    col_mask = (ck[None, :] >= col_start[:, None]) & (ck[None, :] < col_start[:, None] + NAT_KW)
    dc = np.clip(ck[None, :] - cq[:, None], -(NAT_KW - 1), NAT_KW - 1) + (NAT_KW - 1)
    pick = (dc[None, :, :] == np.arange(2 * NAT_KW - 1)[:, None, None]).astype(np.float32)
    tiles = jnp.einsum('hrd,dqk->hrqk', rpb.astype(F32), jnp.asarray(pick),
                       precision=lax.Precision.HIGHEST)
    tiles = jnp.where(jnp.asarray(col_mask)[None, None], tiles, MASK_VALUE)
    t = jnp.stack([tiles[:, b:b + NAT_KH] for b in range(NAT_KH)], axis=0)
    return t.transpose(0, 1, 3, 2, 4).reshape(NAT_KH, N_HEADS * GRID_W, NAT_KH * GRID_W)


def _nat_kernel(q_ref, k_ref, v_ref, bias_ref, o_ref, *, rows_per_step, n_rows):
    j = pl.program_id(1)
    lane_head = _head_id((GRID_W, BR_W), 1)
    win = NAT_KH * GRID_W

    def body(i, carry):
        r = j * rows_per_step + i
        rs = jnp.clip(r - NAT_KH // 2, 0, n_rows - NAT_KH)
        base = rs - r + (NAT_KH - 1)
        q = q_ref[pl.ds(pl.multiple_of(i * GRID_W, GRID_W), GRID_W), :] * jnp.asarray(HEAD_DIM ** -0.5, BF16)
        kw = k_ref[pl.ds(pl.multiple_of(rs * GRID_W, GRID_W), win), :]
        vw = v_ref[pl.ds(pl.multiple_of(rs * GRID_W, GRID_W), win), :]
        q_heads = jnp.concatenate([jnp.where(lane_head == h, q, jnp.zeros_like(q)) for h in range(N_HEADS)], axis=0)
        s = _nt(q_heads, kw) + bias_ref[base]
        m = jnp.max(s, axis=-1, keepdims=True)
        pr = jnp.exp(s - m)
        l = jnp.sum(pr, axis=-1, keepdims=True)
        o = _dot(pr.astype(BF16), vw) / l
        out = jnp.zeros((GRID_W, BR_W), F32)
        for h in range(N_HEADS):
            out = out + jnp.where(lane_head == h, o[h * GRID_W:(h + 1) * GRID_W], 0.0)
        o_ref[pl.ds(pl.multiple_of(i * GRID_W, GRID_W), GRID_W), :] = out.astype(o_ref.dtype)
        return carry

    lax.fori_loop(0, rows_per_step, body, 0, unroll=True)


def neighbourhood_attention(p, rpb, batch, seq):
    n_rows = seq // GRID_W
    rps = NAT_ROWS_PER_STEP
    steps = n_rows // rps
    bias = _nat_bias_table(rpb)
    cb = COL_NAT // BR_W
    tq = rps * GRID_W
    return pl.pallas_call(
        functools.partial(_nat_kernel, rows_per_step=rps, n_rows=n_rows),
        grid=(batch, steps),
        in_specs=[pl.BlockSpec((tq, BR_W), lambda b, j: (b * steps + j, cb)),
                  pl.BlockSpec((seq, BR_W), lambda b, j: (b, cb + 1)),
                  pl.BlockSpec((seq, BR_W), lambda b, j: (b, cb + 2)),
                  pl.BlockSpec(bias.shape, lambda b, j: (0, 0, 0))],
        out_specs=pl.BlockSpec((tq, BR_W), lambda b, j: (b * steps + j, 0)),
        out_shape=jax.ShapeDtypeStruct((batch * seq, BR_W), BF16),
        compiler_params=_cparams("parallel", "arbitrary"),
        name="nat",
    )(p, p, p, bias)


def _diff_kernel(lam_ref, slope_ref, sub_ref, q_ref, k_ref, v_ref, o_ref,
                 acc1, acc2, m1, l1, m2, l2, hill, stage_a1, stage_a2, stage_b1, stage_b2, *, lam_init, n_tiles):
    t = q_ref.shape[0]
    reps = t // 128
    qi = pl.program_id(2)
    slope2 = slope_ref[0, 0:1, 0:1] * LOG2E
    lp = lam_ref[...]
    lam = (jnp.exp(jnp.sum(lp[0:1] * lp[1:2], axis=-1, keepdims=True))
           - jnp.exp(jnp.sum(lp[2:3] * lp[3:4], axis=-1, keepdims=True)) + lam_init)

    @pl.when(qi == 0)
    def _():
        rel = (lax.broadcasted_iota(jnp.int32, (t, t), 1) - lax.broadcasted_iota(jnp.int32, (t, t), 0)).astype(F32)
        hill[...] = -slope2 * jnp.abs(rel)

    q = (q_ref[...].astype(F32) * (DIFF_SCALE * LOG2E)).astype(BF16)
    lane = lax.broadcasted_iota(jnp.int32, q.shape, 1)
    q1 = jnp.where(lane < 64, q, jnp.zeros_like(q))
    q2 = jnp.where(lane >= 64, q, jnp.zeros_like(q))

    acc1[...] = jnp.zeros_like(acc1)
    acc2[...] = jnp.zeros_like(acc2)
    l1[...] = jnp.zeros_like(l1)
    l2[...] = jnp.zeros_like(l2)
    m1[...] = jnp.full_like(m1, -jnp.inf)
    m2[...] = jnp.full_like(m2, -jnp.inf)

    def scores(kj):
        k = k_ref[pl.ds(pl.multiple_of(kj * t, t), t), :]
        return _nt(q1, k), _nt(q2, k)

    def absorb(kj, raw_scores, add_local_bias, tile_bias):
        v = v_ref[pl.ds(pl.multiple_of(kj * t, t), t), :]
        for s_raw, acc, m_ref, l_ref in zip(raw_scores, (acc1, acc2), (m1, m2), (l1, l2)):
            s = add_local_bias(s_raw)
            m_old = m_ref[...]
            m_new = jnp.maximum(m_old, jnp.max(s, axis=-1, keepdims=True) + tile_bias)
            a = jnp.exp2(m_old - m_new)
            pr = jnp.exp2(s - jnp.concatenate([m_new - tile_bias] * reps, axis=1))
            l_ref[...] = a * l_ref[...] + jnp.sum(pr, axis=-1, keepdims=True)
            acc[...] = a * acc[...] + _dot(pr.astype(BF16), v)
            m_ref[...] = m_new

    stage = ((stage_a1, stage_a2), (stage_b1, stage_b2))
    query_term = slope2 * lax.broadcasted_iota(jnp.int32, (t, 128), 0).astype(F32)
    order = [qi] + [jnp.where(u < qi, u, u + 1) for u in range(n_tiles - 1)]
    for ref, val in zip(stage[0], scores(order[0])):
        ref[...] = val
    for u, kj in enumerate(order):
        cur, nxt = stage[u % 2], stage[(u + 1) % 2]
        if u + 1 < len(order):
            for ref, val in zip(nxt, scores(order[u + 1])):
                ref[...] = val
        raw = (cur[0][...], cur[1][...])
        if u == 0:
            diagonal = hill[...]
            absorb(kj, raw, lambda s: s + diagonal, jnp.zeros((1, 1), F32))
        else:
            sign = jnp.where(jnp.full((1, 1), kj, jnp.int32) < qi, 1.0, -1.0)
            keys = (sign * slope2) * ((kj - qi) * t + lax.broadcasted_iota(jnp.int32, (1, t), 1)).astype(F32)
            absorb(kj, raw, lambda s, keys=keys: s + keys, -sign * query_term)

    o = acc1[...] / l1[...] - lam * (acc2[...] / l2[...])
    o = o * lax.rsqrt(jnp.mean(o * o, axis=-1, keepdims=True) + EPS) * sub_ref[0]
    o_ref[...] = (o * (1.0 - lam_init)).astype(o_ref.dtype)


def diff_attention(p, lam_params, subln, layer, batch, seq):
    t = DIFF_TILE
    nt = seq // t
    lam_init = 0.8 - 0.6 * math.exp(-0.3 * layer)
    slopes = np.exp2(-8.0 * (np.arange(N_HEADS, dtype=np.float64) + 1.0) / N_HEADS)
    slope_tab = jnp.asarray(np.broadcast_to(slopes[:, None, None], (N_HEADS, 8, 128)).astype(np.float32))
    qc, kc, vc = (COL_DIFF // DIFF_DV, (COL_DIFF + DIFF_W) // DIFF_DV, (COL_DIFF + 2 * DIFF_W) // DIFF_DV)
    return pl.pallas_call(
        functools.partial(_diff_kernel, lam_init=lam_init, n_tiles=nt),
        grid=(batch, N_HEADS, nt),
        in_specs=[pl.BlockSpec(lam_params.shape, lambda b, h, i: (0, 0)),
                  pl.BlockSpec((1, 8, 128), lambda b, h, i: (h, 0, 0)),
                  pl.BlockSpec((1, 1, DIFF_DV), lambda b, h, i: (h, 0, 0)),
                  pl.BlockSpec((t, DIFF_DV), lambda b, h, i: (b * nt + i, qc + h)),
                  pl.BlockSpec((seq, DIFF_DV), lambda b, h, i: (b, kc + h)),
                  pl.BlockSpec((seq, DIFF_DV), lambda b, h, i: (b, vc + h))],
        out_specs=pl.BlockSpec((t, DIFF_DV), lambda b, h, i: (b * nt + i, h)),
        out_shape=jax.ShapeDtypeStruct((batch * seq, DIFF_W), BF16),
        scratch_shapes=[pltpu.VMEM((t, DIFF_DV), F32)] * 6 + [pltpu.VMEM((t, t), F32)] * 5,
        compiler_params=_cparams("parallel", "parallel", "arbitrary"),
        name="diff_attn",
    )(lam_params, slope_tab, subln.reshape(N_HEADS, 1, DIFF_DV), p, p, p)


def _merge_kernel(hf_ref, hb_ref, rof, rob, rg, ynat, hof, hob, hg, ydiff,
                  wg_ref, bg_ref, wr_ref, wn_ref, wh_ref, wd_ref, wo_ref,
                  rgn_ref, hgn_ref, lng_ref, lnb_ref, of_ref, ob_ref):
    seg_mean = jnp.where(_block_diag_mask(BR_W), 1.0 / HEAD_DIM, 0.0).astype(BF16)

    def head_norm_gate(o, gn, gate):
        ms = _dot_split(o * o, seg_mean, terms=2)
        gx = gate.astype(F32)
        return (o * lax.rsqrt(ms + EPS) * gn * (gx * _sigmoid(gx))).astype(BF16)

    y_ret = head_norm_gate(rof[...] + rob[...], rgn_ref[...], rg[...])
    y_hgrn = head_norm_gate(hof[...] + hob[...], hgn_ref[...], hg[...])
    hb = hb_ref[...]
    d = hf_ref.shape[1]
    merged = None
    for i, (y, w_ref) in enumerate(((y_ret, wr_ref), (ynat[...], wn_ref), (y_hgrn, wh_ref), (ydiff[...], wd_ref))):
        gate = _sigmoid(_dot(hb, wg_ref[:, i * d:(i + 1) * d]) + bg_ref[:, i * d:(i + 1) * d])
        part = gate * _dot(y, w_ref[...])
        merged = part if merged is None else merged + part
    z = ALPHA * hf_ref[...] + _dot(merged.astype(BF16), wo_ref[...])
    h = _layer_norm(z, lng_ref[...], lnb_ref[...])
    of_ref[...] = h
    ob_ref[...] = h.astype(BF16)


def merge_and_norm(hf, hb, p, ret_f, ret_b, y_nat, hg_f, hg_b, y_diff,
                   w_gate, b_gate, w_br_ret, w_br_nat, w_br_hgrn, w_br_diff, w_out,
                   ret_gn, hgrn_gn, ln_g, ln_b, tm=512):
    n, d = hf.shape

    def rows(width, col=0):
        return pl.BlockSpec((tm, width), lambda i: (i, col))

    def whole(a):
        nd = a.ndim
        return pl.BlockSpec(a.shape, lambda i: (0,) * nd)

    consts = [w_gate, b_gate.reshape(1, -1), w_br_ret, w_br_nat, w_br_hgrn, w_br_diff, w_out,
              ret_gn.reshape(1, BR_W), hgrn_gn.reshape(1, BR_W), ln_g.reshape(1, d), ln_b.reshape(1, d)]
    return pl.pallas_call(
        _merge_kernel,
        grid=(n // tm,),
        in_specs=[rows(d), rows(d),
                  rows(BR_W), rows(BR_W), rows(BR_W, COL_RET // BR_W + 3),
                  rows(BR_W),
                  rows(BR_W), rows(BR_W), rows(BR_W, COL_HGRN // BR_W + 4),
                  rows(DIFF_W)] + [whole(a) for a in consts],
        out_specs=[rows(d), rows(d)],
        out_shape=[jax.ShapeDtypeStruct((n, d), F32), jax.ShapeDtypeStruct((n, d), BF16)],
        compiler_params=_cparams("parallel"),
        name="merge",
    )(hf, hb, ret_f, ret_b, p, y_nat, hg_f, hg_b, p, y_diff, *consts)


def _route(h, w, bias):
    w_hi = w.astype(BF16)
    w_lo = (w - w_hi.astype(F32)).astype(BF16)
    logits = _dot_split(h, w_hi, terms=3) + _dot_split(h, w_lo, terms=2)
    scores = _sigmoid(logits)
    biased = scores + bias
    shape = scores.shape
    lane = lax.broadcasted_iota(jnp.int32, shape, 1).astype(F32)
    group = lax.shift_right_logical(lax.broadcasted_iota(jnp.int32, shape, 1),
                                    GROUP_SIZE.bit_length() - 1).astype(F32)
    big = jnp.asarray(1e9, F32)
    neg = jnp.asarray(-jnp.inf, F32)

    def first_max(x, ids):
        m = jnp.max(x, axis=-1, keepdims=True)
        return m, jnp.min(jnp.where(x == m, ids, big), axis=-1, keepdims=True)

    gscore = jnp.zeros(shape, F32)
    for g in range(N_GROUPS):
        xg = jnp.where(group == g, biased, neg)
        m1, i1 = first_max(xg, lane)
        m2 = jnp.max(jnp.where(lane == i1, neg, xg), axis=-1, keepdims=True)
        gscore = jnp.where(group == g, m1 + m2, gscore)
    gsel = jnp.zeros(shape, jnp.bool_)
    for _ in range(TOPK_GROUPS):
        _, gi = first_max(gscore, group)
        hit = group == gi
        gsel = gsel | hit
        gscore = jnp.where(hit, neg, gscore)
    cand = jnp.where(gsel, biased, MASK_VALUE)
    esel = jnp.zeros(shape, jnp.bool_)
    picks = []
    for _ in range(TOP_K):
        _, ei = first_max(cand, lane)
        hit = lane == ei
        picks.append((ei, jnp.sum(jnp.where(hit, scores, 0.0), axis=-1, keepdims=True)))
        esel = esel | hit
        cand = jnp.where(hit, neg, cand)

    tm = shape[0]
    sel = jnp.where(esel, 1.0, 0.0)
    counts = jnp.concatenate(
        [jnp.sum(sel[s * ROUTE_TILE:(s + 1) * ROUTE_TILE], axis=0, keepdims=True) for s in range(tm // ROUTE_TILE)],
        axis=0)

    slot = lax.broadcasted_iota(jnp.int32, (tm, TOP_K), 1)
    idx8 = jnp.zeros((tm, TOP_K), F32)
    w8 = jnp.zeros((tm, TOP_K), F32)
    wsum = jnp.zeros((tm, 1), F32)
    for k, (ei, wk) in enumerate(picks):
        idx8 = jnp.where(slot == k, ei, idx8)
        w8 = jnp.where(slot == k, wk, w8)
        wsum = wsum + wk
    return idx8.astype(jnp.int32), w8 / (wsum + 1e-20) * ROUTED_SCALE, counts


def _router_kernel(h_ref, w_ref, bias_ref, idx_ref, w8_ref, cnt_ref):
    idx_ref[...], w8_ref[...], cnt_ref[...] = _route(h_ref[...], w_ref[...], bias_ref[...])


def router(hf, w_router, router_bias, tm=1024):
    n, d = hf.shape
    e = w_router.shape[1]
    sub = tm // ROUTE_TILE
    return pl.pallas_call(
        _router_kernel,
        grid=(n // tm,),
        in_specs=[pl.BlockSpec((tm, d), lambda i: (i, 0)),
                  pl.BlockSpec((d, e), lambda i: (0, 0)),
                  pl.BlockSpec((1, e), lambda i: (0, 0))],
        out_specs=[pl.BlockSpec((tm, TOP_K), lambda i: (i, 0)),
                   pl.BlockSpec((tm, TOP_K), lambda i: (i, 0)),
                   pl.BlockSpec((sub, e), lambda i: (i, 0))],
        out_shape=[jax.ShapeDtypeStruct((n, TOP_K), jnp.int32),
                   jax.ShapeDtypeStruct((n, TOP_K), F32),
                   jax.ShapeDtypeStruct((n // ROUTE_TILE, e), F32)],
        compiler_params=_cparams("arbitrary"),
        name="router",
    )(hf, w_router, router_bias.reshape(1, e))


HIGH_HALF = -65536


def _pack_pairs(x):
    w = x.shape[1] // 2
    lo = lax.bitcast_convert_type(x[:, :w].astype(BF16).astype(F32), jnp.int32)
    hi = lax.bitcast_convert_type(x[:, w:].astype(BF16).astype(F32), jnp.int32)
    return lax.shift_right_logical(lo, 16) | (hi & HIGH_HALF)


def _unpack_pairs(p):
    lo = lax.bitcast_convert_type(lax.shift_left(p, 16), F32)
    hi = lax.bitcast_convert_type(p & HIGH_HALF, F32)
    return jnp.concatenate([lo, hi], axis=1)


def _swiglu(x, wg, wu):
    a = _dot(x, wg)
    return a * _sigmoid(a) * _dot(x, wu)


def _segment_copies(tab_ref, first_col, max_run, make_copy, action):
    top_bit = max_run.bit_length() - 1
    low_bit = RUN_ALIGN.bit_length() - 1
    mid_bit = min(top_bit + 1, LONG_RUN.bit_length() - 1)

    def pieces(count, local, glob, bits):
        for b in bits:
            size = 1 << b
            taken = count & size

            @pl.when(taken != 0)
            def _(local=local, glob=glob, size=size):
                action(make_copy(pl.multiple_of(local, RUN_ALIGN), pl.multiple_of(glob, RUN_ALIGN), size))

            local = local + taken
            glob = glob + taken

    def per_expert(e, carry):
        col = first_col + e
        count = tab_ref[0, col]
        local = tab_ref[1, col]
        glob = tab_ref[2, col]
        long_part = count & -LONG_RUN

        @pl.when(long_part != 0)
        def _():
            pieces(count, local, glob, range(top_bit, mid_bit - 1, -1))

        pieces(count, local + long_part, glob + long_part, range(mid_bit - 1, low_bit - 1, -1))
        return carry

    lax.fori_loop(0, N_EXPERTS, per_expert, 0)


def _tile_rows_used(tab_ref, t):
    last = t * N_EXPERTS + (N_EXPERTS - 1)
    return tab_ref[1, last] + tab_ref[0, last]


def _wait_rows(total, max_rows, make_copy):
    for b in range(max_rows.bit_length() - 1, RUN_ALIGN.bit_length() - 2, -1):
        size = 1 << b

        @pl.when((total & size) != 0)
        def _(size=size):
            make_copy(0, 0, size).wait()


def _block_dispatch_kernel(tab_ref, gap_ref, tail_ref, idx_t_ref, hb_ref, xs_ref, sorted_buf, zeros_buf, sem):
    t = pl.program_id(0)
    tm = hb_ref.shape[0]
    rows = sorted_buf.shape[0]
    idx_t = idx_t_ref[...]
    expert = lax.broadcasted_iota(jnp.int32, (N_EXPERTS, tm), 0)
    sel_t = jnp.zeros((N_EXPERTS, tm), F32)
    for k in range(TOP_K):
        sel_t = sel_t + jnp.where(idx_t[k:k + 1, :] == expert, 1.0, 0.0)
    sel_b = sel_t.astype(BF16)
    before_tok = (lax.broadcasted_iota(jnp.int32, (tm, tm), 0) < lax.broadcasted_iota(jnp.int32, (tm, tm), 1))
    before_exp = (lax.broadcasted_iota(jnp.int32, (N_EXPERTS, N_EXPERTS), 1)
                  < lax.broadcasted_iota(jnp.int32, (N_EXPERTS, N_EXPERTS), 0))
    run = jnp.broadcast_to(_run_length(jnp.sum(sel_t, axis=1, keepdims=True)), (N_EXPERTS, tm)).astype(BF16)
    base = (_dot(jnp.where(before_exp, 1.0, 0.0).astype(BF16), run)
            + _dot(sel_b, jnp.where(before_tok, 1.0, 0.0).astype(BF16)))
    row = lax.broadcasted_iota(jnp.int32, (rows, tm), 0).astype(jnp.int16)
    perm = jnp.zeros((rows, tm), BF16)
    for k in range(TOP_K):
        place = jnp.sum(jnp.where(idx_t[k:k + 1, :] == expert, base, 0.0), axis=0, keepdims=True)
        perm = jnp.where(row == place.astype(jnp.int32).astype(jnp.int16), jnp.ones_like(perm), perm)
    half = hb_ref.shape[1] // 2
    lo = lax.bitcast_convert_type(_dot(perm, hb_ref[:, :half]), jnp.int32)
    hi = lax.bitcast_convert_type(_dot(perm, hb_ref[:, half:]), jnp.int32)
    sorted_buf[...] = lax.shift_right_logical(lo, 16) | (hi & HIGH_HALF)

    def make_copy(local, glob, size):
        return pltpu.make_async_copy(sorted_buf.at[pl.ds(local, size)], xs_ref.at[pl.ds(glob, size)], sem.at[0])

    _segment_copies(tab_ref, t * N_EXPERTS, tm, make_copy, lambda c: c.start())
    _wait_rows(_tile_rows_used(tab_ref, t), rows, make_copy)

    @pl.when(t == pl.num_programs(0) - 1)
    def _():
        zeros_buf[...] = jnp.zeros_like(zeros_buf)
        tile_rows = zeros_buf.shape[0]

        def zero_copy(local, glob, size):
            return pltpu.make_async_copy(zeros_buf.at[pl.ds(0, size)], xs_ref.at[pl.ds(glob, size)], sem.at[0])

        def tail_copy(j):
            return zero_copy(0, pl.multiple_of(tail_ref[0] + j * tile_rows, RUN_ALIGN), tile_rows)

        _segment_copies(gap_ref, 0, tile_rows // 2, zero_copy, lambda c: c.start())
        lax.fori_loop(0, tail_ref[1], lambda j, c: (tail_copy(j).start(), c)[1], 0)
        _segment_copies(gap_ref, 0, tile_rows // 2, zero_copy, lambda c: c.wait())
        lax.fori_loop(0, tail_ref[1], lambda j, c: (tail_copy(j).wait(), c)[1], 0)


def block_dispatch(hb, idx_t, tab, gaps, tail, tm):
    n, d = hb.shape
    grid_spec = pltpu.PrefetchScalarGridSpec(
        num_scalar_prefetch=3,
        grid=(n // tm,),
        in_specs=[pl.BlockSpec((TOP_K, tm), lambda i, *_: (0, i)),
                  pl.BlockSpec((tm, d), lambda i, *_: (i, 0))],
        out_specs=pl.BlockSpec(memory_space=pl.ANY),
        scratch_shapes=[pltpu.VMEM((_tile_rows(tm), d // 2), jnp.int32),
                        pltpu.VMEM((EXPERT_ROWS, d // 2), jnp.int32),
                        pltpu.SemaphoreType.DMA((1,))],
    )
    return pl.pallas_call(
        _block_dispatch_kernel,
        grid_spec=grid_spec,
        out_shape=jax.ShapeDtypeStruct((_sorted_rows(n, tm), d // 2), jnp.int32),
        compiler_params=_cparams("arbitrary"),
        name="dispatch",
    )(tab, gaps, tail, idx_t, hb)


def _tile_tables(tile_counts, tm_fine, tm):
    g = tm // tm_fine
    n_tokens = tile_counts.shape[0] * tm_fine
    n_tiles = _sorted_rows(n_tokens, tm) // EXPERT_ROWS
    cnt = tile_counts.astype(jnp.int32).reshape(-1, g, N_EXPERTS).sum(axis=1)
    cnt = (cnt + (RUN_ALIGN - 1)) // RUN_ALIGN * RUN_ALIGN
    local = jnp.cumsum(cnt, axis=1) - cnt
    used = cnt.sum(axis=0)
    seg = (used + (EXPERT_ROWS - 1)) // EXPERT_ROWS * EXPERT_ROWS
    seg_end = jnp.cumsum(seg)
    seg_start = seg_end - seg
    glob = seg_start[None, :] + jnp.cumsum(cnt, axis=0) - cnt
    runs = jnp.stack([cnt.reshape(-1), local.reshape(-1), glob.reshape(-1)])
    gaps = jnp.stack([seg - used, jnp.zeros_like(used), seg_start + used])
    tail = jnp.stack([seg_end[-1], n_tiles - seg_end[-1] // EXPERT_ROWS])
    tile_row = jnp.arange(n_tiles, dtype=jnp.int32) * EXPERT_ROWS
    owner = jnp.sum((seg_end[None, :] <= tile_row[:, None]).astype(jnp.int32), axis=1)
    owner_c = jnp.minimum(owner, N_EXPERTS - 1)
    of_owner = lambda a: jnp.sum(jnp.where(jnp.arange(N_EXPERTS)[None, :] == owner_c[:, None], a[None, :], 0), axis=1)
    filled = jnp.clip(of_owner(seg_start + used) - tile_row, 0, EXPERT_ROWS)
    source = jnp.minimum(jnp.arange(n_tiles, dtype=jnp.int32), jnp.maximum(seg_end[-1] // EXPERT_ROWS - 1, 0))
    tiles = jnp.stack([owner_c, jnp.where(owner < N_EXPERTS, filled, 0), source])
    return runs, gaps, tail, tiles.astype(jnp.int32)


def _expert_kernel(tile_ref, xs_ref, wg_ref, wu_ref, wd_ref, ys_ref, wgb, wub, wdb):
    i = pl.program_id(0)
    expert = tile_ref[0, i]

    @pl.when(tile_ref[1, i] > 0)
    def _():
        @pl.when((i == 0) | (expert != tile_ref[0, jnp.maximum(i - 1, 0)]))
        def _():
            wgb[...] = wg_ref[0, 0].astype(BF16)
            wub[...] = wu_ref[0, 0].astype(BF16)
            wdb[...] = wd_ref[0, 0].astype(BF16)

        x = _unpack_pairs(xs_ref[...]).astype(BF16)
        hid = _swiglu(x, wgb[...], wub[...])
        ys_ref[...] = _pack_pairs(_dot(hid.astype(BF16), wdb[...]))

    @pl.when(tile_ref[1, i] == 0)
    def _():
        ys_ref[...] = jnp.zeros_like(ys_ref)


def grouped_experts(xs, tiles, w_e_gate, w_e_up, w_e_down, layer):
    rows, w = xs.shape
    _, _, d, hid = w_e_gate.shape
    tr = EXPERT_ROWS
    grid_spec = pltpu.PrefetchScalarGridSpec(
        num_scalar_prefetch=1,
        grid=(rows // tr,),
        in_specs=[pl.BlockSpec((tr, w), lambda i, tl: (tl[2, i], 0)),
                  pl.BlockSpec((1, 1, d, hid), lambda i, tl: (layer, tl[0, i], 0, 0)),
                  pl.BlockSpec((1, 1, d, hid), lambda i, tl: (layer, tl[0, i], 0, 0)),
                  pl.BlockSpec((1, 1, hid, d), lambda i, tl: (layer, tl[0, i], 0, 0))],
        out_specs=pl.BlockSpec((tr, w), lambda i, tl: (i, 0)),
        scratch_shapes=[pltpu.VMEM((d, hid), BF16), pltpu.VMEM((d, hid), BF16), pltpu.VMEM((hid, d), BF16)],
    )
    return pl.pallas_call(
        _expert_kernel,
        grid_spec=grid_spec,
        out_shape=jax.ShapeDtypeStruct((rows, w), jnp.int32),
        compiler_params=_cparams("arbitrary"),
        name="experts",
    )(tiles, xs, w_e_gate, w_e_up, w_e_down)


def _block_combine_kernel(tab_ref, ys_ref, idx_ref, w8_ref, hf_ref, hb_ref, sg_ref, su_ref, sd_ref,
                          lng_ref, lnb_ref, of_ref, ob_ref, buf, sem):
    t = pl.program_id(0)
    tm = hf_ref.shape[0]
    rows = buf.shape[0]

    def make_copy(local, glob, size):
        return pltpu.make_async_copy(ys_ref.at[pl.ds(glob, size)], buf.at[pl.ds(local, size)], sem.at[0])

    @pl.when(t == 0)
    def _():
        buf[...] = jnp.zeros_like(buf)

    _segment_copies(tab_ref, t * N_EXPERTS, tm, make_copy, lambda c: c.start())

    idx = idx_ref[...]
    w8 = w8_ref[...]
    expert = lax.broadcasted_iota(jnp.int32, (tm, N_EXPERTS), 1)
    sel = jnp.zeros((tm, N_EXPERTS), F32)
    for k in range(TOP_K):
        sel = sel + jnp.where(idx[:, k:k + 1] == expert, 1.0, 0.0)
    sel_b = sel.astype(BF16)
    before_tok = (lax.broadcasted_iota(jnp.int32, (tm, tm), 1) < lax.broadcasted_iota(jnp.int32, (tm, tm), 0))
    before_exp = (lax.broadcasted_iota(jnp.int32, (N_EXPERTS, N_EXPERTS), 0)
                  < lax.broadcasted_iota(jnp.int32, (N_EXPERTS, N_EXPERTS), 1))
    run = jnp.broadcast_to(_run_length(jnp.sum(sel, axis=0, keepdims=True)), (tm, N_EXPERTS)).astype(BF16)
    base = (_dot(run, jnp.where(before_exp, 1.0, 0.0).astype(BF16))
            + _dot(jnp.where(before_tok, 1.0, 0.0).astype(BF16), sel_b))
    col = lax.broadcasted_iota(jnp.int32, (tm, rows), 1).astype(jnp.int16)
    w8_b = w8.astype(BF16)
    mix = jnp.zeros((tm, rows), BF16)
    for k in range(TOP_K):
        place = jnp.sum(jnp.where(idx[:, k:k + 1] == expert, base, 0.0), axis=1, keepdims=True)
        mix = jnp.where(col == place.astype(jnp.int32).astype(jnp.int16),
                        jnp.broadcast_to(w8_b[:, k:k + 1], mix.shape), mix)

    acc = _dot(_swiglu(hb_ref[...], sg_ref[...], su_ref[...]).astype(BF16), sd_ref[...])
    _wait_rows(_tile_rows_used(tab_ref, t), rows, make_copy)
    y = _unpack_pairs(buf[...]).astype(BF16)
    acc = acc + _dot(mix, y)
    h = _layer_norm(ALPHA * hf_ref[...] + acc, lng_ref[...], lnb_ref[...])
    of_ref[...] = h
    ob_ref[...] = h.astype(BF16)


def block_combine(ys, tab, idx8, w8, hf, hb, w_s_gate, w_s_up, w_s_down, ln_g, ln_b, tm):
    n, d = hf.shape
    w = ys.shape[1]

    def whole(a):
        return pl.BlockSpec(a.shape, lambda i, tab: (0, 0))

    def rows(width):
        return pl.BlockSpec((tm, width), lambda i, tab: (i, 0))

    consts = [w_s_gate, w_s_up, w_s_down, ln_g.reshape(1, d), ln_b.reshape(1, d)]
    grid_spec = pltpu.PrefetchScalarGridSpec(
        num_scalar_prefetch=1,
        grid=(n // tm,),
        in_specs=[pl.BlockSpec(memory_space=pl.ANY), rows(TOP_K), rows(TOP_K), rows(d), rows(d)]
                 + [whole(a) for a in consts],
        out_specs=[rows(d), rows(d)],
        scratch_shapes=[pltpu.VMEM((_tile_rows(tm), w), jnp.int32), pltpu.SemaphoreType.DMA((1,))],
    )
    return pl.pallas_call(
        _block_combine_kernel,
        grid_spec=grid_spec,
        out_shape=[jax.ShapeDtypeStruct((n, d), F32), jax.ShapeDtypeStruct((n, d), BF16)],
        compiler_params=_cparams("arbitrary"),
        name="combine",
    )(tab, ys, idx8, w8, hf, hb, *consts)


def moe_and_norm(hf, hb, w_router, router_bias, w_e_gate, w_e_up, w_e_down,
                 w_s_gate, w_s_up, w_s_down, ln_g, ln_b, layer):
    idx8, w8, tile_counts = router(hf, w_router, router_bias)
    runs, gaps, tail, tiles = _tile_tables(tile_counts, ROUTE_TILE, MOE_TILE)
    xs = block_dispatch(hb, idx8.T, runs, gaps, tail, MOE_TILE)
    ys = grouped_experts(xs, tiles, w_e_gate, w_e_up, w_e_down, layer)
    return block_combine(ys, runs, idx8, w8, hf, hb, w_s_gate, w_s_up, w_s_down, ln_g, ln_b, MOE_TILE)


def kernel(x, ln_in_g, ln_in_b, w_in, w_gate, b_gate, w_br_ret, w_br_nat, w_br_hgrn, w_br_diff, w_out,
           ret_gn, nat_rpb, hgrn_lb, hgrn_gn, diff_lambda, diff_subln, ln1_g, ln1_b, w_router,
           router_bias, w_e_gate, w_e_up, w_e_down, w_s_gate, w_s_up, w_s_down, ln2_g, ln2_b):
    batch, seq, d = x.shape
    bf = lambda a: a.astype(BF16)
    hf, hb = input_layer_norm(x.reshape(batch * seq, d), ln_in_g, ln_in_b)
    for l in range(DEPTH):
        p = in_projection(hb, bf(w_in[l]))
        ret_f, ret_b = retention(p, batch, seq)
        y_nat = neighbourhood_attention(p, nat_rpb[l], batch, seq)
        hg_f, hg_b = hgrn(p, hgrn_lb, l, batch, seq)
        y_diff = diff_attention(p, diff_lambda[l], diff_subln[l], l, batch, seq)
        hf, hb = merge_and_norm(hf, hb, p, ret_f, ret_b, y_nat, hg_f, hg_b, y_diff,
                                bf(w_gate[l]), b_gate[l], bf(w_br_ret[l]), bf(w_br_nat[l]),
                                bf(w_br_hgrn[l]), bf(w_br_diff[l]), bf(w_out[l]),
                                ret_gn[l], hgrn_gn[l], ln1_g[l], ln1_b[l])
        hf, hb = moe_and_norm(hf, hb, w_router[l], router_bias[l], w_e_gate, w_e_up, w_e_down,
                              bf(w_s_gate[l]), bf(w_s_up[l]), bf(w_s_down[l]), ln2_g[l], ln2_b[l], l)
    return hf.reshape(batch, seq, d)
```

```python
import functools
import math

import numpy as np
import jax
import jax.numpy as jnp
from jax import lax
from jax.experimental import pallas as pl
from jax.experimental.pallas import tpu as pltpu

F32 = jnp.float32
BF16 = jnp.bfloat16

DEPTH = 2
GRID_W = 64
HEAD_DIM = 64
N_HEADS = 4
BR_W = N_HEADS * HEAD_DIM
NAT_KH = 8
NAT_KW = 16
DIFF_DV = 128
DIFF_W = N_HEADS * DIFF_DV
N_EXPERTS = 64
TOP_K = 8
N_GROUPS = 8
TOPK_GROUPS = 4
GROUP_SIZE = N_EXPERTS // N_GROUPS
ROUTED_SCALE = 2.5
EPS = 1e-5
MASK_VALUE = -1e30
LOG2E = math.log2(math.e)
DIFF_SCALE = 64 ** -0.5
ALPHA = (2.0 * DEPTH) ** 0.25
COL_RET = 0
COL_NAT = 1024
COL_HGRN = 1792
COL_DIFF = 3072

VMEM_LIMIT = 56 * 1024 * 1024

RET_CHUNK = 256
HGRN_CHUNK = 128
HGRN_SUB = 16
DIFF_TILE = 1024
NAT_ROWS_PER_STEP = 16
ROUTE_TILE = 128
MOE_TILE = 256
EXPERT_ROWS = 1024
RUN_ALIGN = 8
LONG_RUN = 64


def _run_length(count):
    return jnp.floor((count + (RUN_ALIGN - 1)) * (1.0 / RUN_ALIGN)) * RUN_ALIGN


def _tile_rows(tm):
    bound = TOP_K * tm + N_EXPERTS * (RUN_ALIGN - 1)
    return -(-bound // 128) * 128


def _sorted_rows(n_tokens, tm):
    bound = (TOP_K * n_tokens + (n_tokens // tm) * N_EXPERTS * (RUN_ALIGN - 1)
             + N_EXPERTS * (EXPERT_ROWS - RUN_ALIGN))
    return -(-bound // EXPERT_ROWS) * EXPERT_ROWS


def _cparams(*sem):
    return pltpu.CompilerParams(dimension_semantics=sem, vmem_limit_bytes=VMEM_LIMIT)


def _nt(a, b):
    return lax.dot_general(a, b, (((1,), (1,)), ((), ())), preferred_element_type=F32)


def _tn(a, b):
    return lax.dot_general(a, b, (((0,), (0,)), ((), ())), preferred_element_type=F32)


def _dot(a, b):
    return jnp.dot(a, b, preferred_element_type=F32)


def _dot_split(x, w_bf16, terms=3):
    acc = None
    rem = x
    for _ in range(terms):
        piece = rem.astype(BF16)
        part = _dot(piece, w_bf16)
        acc = part if acc is None else acc + part
        rem = rem - piece.astype(F32)
    return acc


def _layer_norm(z, g, b):
    mu = jnp.mean(z, axis=-1, keepdims=True)
    zc = z - mu
    var = jnp.mean(zc * zc, axis=-1, keepdims=True)
    return zc * lax.rsqrt(var + EPS) * g + b


def _sigmoid(x):
    return 1.0 / (1.0 + jnp.exp(-x))


def _head_id(shape, axis):
    return lax.shift_right_logical(lax.broadcasted_iota(jnp.int32, shape, axis), HEAD_DIM.bit_length() - 1)


def _block_diag_mask(n):
    return _head_id((n, n), 0) == _head_id((n, n), 1)


def _ln_kernel(x_ref, g_ref, b_ref, hf_ref, hb_ref):
    h = _layer_norm(x_ref[...], g_ref[...], b_ref[...])
    hf_ref[...] = h
    hb_ref[...] = h.astype(BF16)


def input_layer_norm(x2, g, b, tm=1024):
    n, d = x2.shape
    return pl.pallas_call(
        _ln_kernel,
        grid=(n // tm,),
        in_specs=[pl.BlockSpec((tm, d), lambda i: (i, 0)),
                  pl.BlockSpec((1, d), lambda i: (0, 0)),
                  pl.BlockSpec((1, d), lambda i: (0, 0))],
        out_specs=[pl.BlockSpec((tm, d), lambda i: (i, 0)),
                   pl.BlockSpec((tm, d), lambda i: (i, 0))],
        out_shape=[jax.ShapeDtypeStruct((n, d), F32), jax.ShapeDtypeStruct((n, d), BF16)],
        compiler_params=_cparams("parallel"),
        name="input_ln",
    )(x2, g.reshape(1, d), b.reshape(1, d))


def _matmul_kernel(h_ref, w_ref, o_ref):
    o_ref[...] = _dot(h_ref[...], w_ref[...]).astype(o_ref.dtype)


def in_projection(hb, w_bf16, tm=2048, tn=1536):
    n, d = hb.shape
    w = w_bf16.shape[1]
    return pl.pallas_call(
        _matmul_kernel,
        grid=(n // tm, w // tn),
        in_specs=[pl.BlockSpec((tm, d), lambda i, j: (i, 0)),
                  pl.BlockSpec((d, tn), lambda i, j: (0, j))],
        out_specs=pl.BlockSpec((tm, tn), lambda i, j: (i, j)),
        out_shape=jax.ShapeDtypeStruct((n, w), BF16),
        compiler_params=_cparams("parallel", "arbitrary"),
        name="in_proj",
    )(hb, w_bf16)


def _retention_tables(c):
    idx = np.arange(N_HEADS, dtype=np.float64)
    lg = [np.log1p(-np.exp2(-5.0 - 2.0 * idx)), np.log1p(-np.exp2(-6.0 - 2.0 * idx))]
    i = np.arange(c, dtype=np.float64)
    diff = i[:, None] - i[None, :]
    lane_head = np.repeat(np.arange(N_HEADS), HEAD_DIM)
    dmat = np.zeros((2, N_HEADS, c, c), np.float32)
    oscale = np.zeros((2, c, BR_W), np.float32)
    kscale = np.zeros((2, c, BR_W), np.float32)
    sdecay = np.zeros((2, 1, BR_W), np.float32)
    for h in range(N_HEADS):
        dmat[0, h] = np.where(diff >= 0, np.exp(lg[0][h] * np.maximum(diff, 0)), 0.0)
        dmat[1, h] = np.where(diff <= 0, np.exp(lg[1][h] * np.maximum(-diff, 0)), 0.0)
    oscale[0] = np.exp(lg[0][lane_head][None, :] * (i[:, None] + 1.0))
    oscale[1] = np.exp(lg[1][lane_head][None, :] * (c - i[:, None]))
    kscale[0] = np.exp(lg[0][lane_head][None, :] * (c - 1.0 - i[:, None]))
    kscale[1] = np.exp(lg[1][lane_head][None, :] * i[:, None])
    sdecay[0, 0] = np.exp(lg[0][lane_head] * c)
    sdecay[1, 0] = np.exp(lg[1][lane_head] * c)
    return (jnp.asarray(dmat), jnp.asarray(oscale), jnp.asarray(kscale), jnp.asarray(sdecay))


def _retention_kernel(qf, kf, vf, qb, kb, vb, dmat, oscale, kscale, sdecay, of_ref, ob_ref, state):
    @pl.when(pl.program_id(1) == 0)
    def _():
        state[...] = jnp.zeros_like(state)

    c = qf.shape[0]
    lane_head = _head_id((c, BR_W), 1)
    bd = _block_diag_mask(BR_W)
    for dirn, (q_ref, k_ref, v_ref, o_ref) in enumerate(((qf, kf, vf, of_ref), (qb, kb, vb, ob_ref))):
        q = q_ref[...]
        k = k_ref[...] * jnp.asarray(HEAD_DIM ** -0.5, BF16)
        v = v_ref[...]
        s_old = state[dirn]
        out = _dot(q, s_old.astype(BF16)) * oscale[dirn]
        for h in range(N_HEADS):
            qz = jnp.where(lane_head == h, q, jnp.zeros_like(q))
            a = (_nt(qz, k) * dmat[dirn, h]).astype(BF16)
            out = out + jnp.where(lane_head == h, _dot(a, v), 0.0)
        o_ref[...] = out
        kw = (k.astype(F32) * kscale[dirn]).astype(BF16)
        state[dirn] = s_old * sdecay[dirn] + jnp.where(bd, _tn(kw, v), 0.0)


def retention(p, batch, seq):
    c = RET_CHUNK
    nc = seq // c
    dmat, oscale, kscale, sdecay = _retention_tables(c)
    cb = COL_RET // BR_W

    def fwd(col):
        return pl.BlockSpec((c, BR_W), lambda b, j: (b * nc + j, col))

    def bwd(col):
        return pl.BlockSpec((c, BR_W), lambda b, j: (b * nc + nc - 1 - j, col))

    def whole(a):
        nd = a.ndim
        return pl.BlockSpec(a.shape, lambda b, j: (0,) * nd)

    n = batch * seq
    return pl.pallas_call(
        _retention_kernel,
        grid=(batch, nc),
        in_specs=[fwd(cb), fwd(cb + 1), fwd(cb + 2), bwd(cb), bwd(cb + 1), bwd(cb + 2),
                  whole(dmat), whole(oscale), whole(kscale), whole(sdecay)],
        out_specs=[pl.BlockSpec((c, BR_W), lambda b, j: (b * nc + j, 0)),
                   pl.BlockSpec((c, BR_W), lambda b, j: (b * nc + nc - 1 - j, 0))],
        out_shape=[jax.ShapeDtypeStruct((n, BR_W), F32)] * 2,
        scratch_shapes=[pltpu.VMEM((2, BR_W, BR_W), F32)],
        compiler_params=_cparams("parallel", "arbitrary"),
        name="retention",
    )(p, p, p, p, p, p, dmat, oscale, kscale, sdecay)


def _hgrn_kernel(lb_ref, qf, ff, vf, qb, fb, vb, of_ref, ob_ref, state, *, layer):
    c = qf.shape[0]
    sb = HGRN_SUB
    n_sub = c // sb

    @pl.when(pl.program_id(1) == 0)
    def _():
        state[...] = jnp.zeros_like(state)

    lb = lb_ref[...]
    e = jnp.exp(lb - jnp.max(lb, axis=0, keepdims=True))
    prob = e / jnp.sum(e, axis=0, keepdims=True)
    lower = jnp.sum(prob[:layer + 1], axis=0) - prob[0]

    row = lax.broadcasted_iota(jnp.int32, (c, c), 0)
    col = lax.broadcasted_iota(jnp.int32, (c, c), 1)
    bd = _block_diag_mask(BR_W)
    seg_ones = jnp.where(bd, 1.0, 0.0).astype(BF16)
    rowv = lax.broadcasted_iota(jnp.int32, (c, BR_W), 0)

    for dirn, (q_ref, f_ref, v_ref, o_ref) in enumerate(((qf, ff, vf, of_ref), (qb, fb, vb, ob_ref))):
        rev = dirn == 1
        lo = lower[dirn:dirn + 1, :]
        fpre = f_ref[...].astype(F32)
        logf = jnp.log(lo + (1.0 - lo) * _sigmoid(fpre))
        kk = (1.0 - lo) * _sigmoid(-fpre)
        qx = q_ref[...].astype(F32)
        q = qx * _sigmoid(qx)
        v_bf = v_ref[...]
        v = v_bf.astype(F32)
        tri = jnp.where((col >= row) if rev else (col <= row), 1.0, 0.0).astype(BF16)
        g = _tri_cumsum(tri, logf)
        g_end = g[0:1, :] if rev else g[c - 1:c, :]
        s_old = state[dirn]
        inter = _nt((q * jnp.exp(g)).astype(BF16), s_old.astype(BF16))
        kd = (kk * jnp.exp(g_end - g)).astype(BF16)
        state[dirn] = s_old * jnp.exp(g_end) + jnp.where(bd, _tn(v_bf, kd), 0.0)
        qs, ks_ = [], []
        for j in range(1, n_sub):
            if rev:
                edge = g[j * sb:j * sb + 1, :]
                key_rows = (rowv >= j * sb) & (rowv < (j + 1) * sb)
                query_rows = rowv < j * sb
            else:
                edge = g[j * sb - 1:j * sb, :]
                key_rows = (rowv >= (j - 1) * sb) & (rowv < j * sb)
                query_rows = rowv >= j * sb
            qs.append(jnp.where(query_rows, q * jnp.exp(jnp.minimum(g - edge, 0.0)), 0.0).astype(BF16))
            ks_.append(jnp.where(key_rows, kk * jnp.exp(jnp.minimum(edge - g, 0.0)), 0.0).astype(BF16))
        bd_all = jnp.concatenate([bd] * (n_sub - 1), axis=1)
        cross = jnp.where(bd_all, _tn(v_bf, jnp.concatenate(ks_, axis=1)), 0.0).astype(BF16)
        intra = _nt(jnp.concatenate(qs, axis=1), cross)
        for d in range(sb):
            shift = (c - d) % c if rev else d
            pair = ((rowv % sb) + d < sb) if rev else ((rowv % sb) >= d)
            k_d, g_d, v_d = (kk, g, v) if d == 0 else tuple(pltpu.roll(a, shift, 0) for a in (kk, g, v))
            term = jnp.where(pair, q, 0.0) * k_d * jnp.exp(jnp.minimum(g - g_d, 0.0))
            intra = intra + _dot(term.astype(BF16), seg_ones) * v_d
        o_ref[...] = inter + intra


def _tri_cumsum(tri_bf16, x):
    acc = None
    rem = x
    for _ in range(3):
        piece = rem.astype(BF16)
        part = _dot(tri_bf16, piece)
        acc = part if acc is None else acc + part
        rem = rem - piece.astype(F32)
    return acc


def hgrn(p, hgrn_lb, layer, batch, seq):
    c = HGRN_CHUNK
    nc = seq // c
    cb = COL_HGRN // BR_W

    def fwd(col):
        return pl.BlockSpec((c, BR_W), lambda b, j: (b * nc + j, col))

    def bwd(col):
        return pl.BlockSpec((c, BR_W), lambda b, j: (b * nc + nc - 1 - j, col))

    n = batch * seq
    return pl.pallas_call(
        functools.partial(_hgrn_kernel, layer=layer),
        grid=(batch, nc),
        in_specs=[pl.BlockSpec(hgrn_lb.shape, lambda b, j: (0, 0, 0)),
                  fwd(cb), fwd(cb + 1), fwd(cb + 3), bwd(cb), bwd(cb + 2), bwd(cb + 3)],
        out_specs=[pl.BlockSpec((c, BR_W), lambda b, j: (b * nc + j, 0)),
                   pl.BlockSpec((c, BR_W), lambda b, j: (b * nc + nc - 1 - j, 0))],
        out_shape=[jax.ShapeDtypeStruct((n, BR_W), F32)] * 2,
        scratch_shapes=[pltpu.VMEM((2, BR_W, BR_W), F32)],
        compiler_params=_cparams("parallel", "arbitrary"),
        name="hgrn",
    )(hgrn_lb, p, p, p, p, p, p)


def _nat_bias_table(rpb):
    cq = np.arange(GRID_W)
    ck = np.arange(GRID_W)
    col_start = np.clip(cq - NAT_KW // 2, 0, GRID_W - NAT_KW)
    col_mask = (ck[None, :] >= col_start[:, None]) & (ck[None, :] < col_start[:, None] + NAT_KW)
    dc = np.clip(ck[None, :] - cq[:, None], -(NAT_KW - 1), NAT_KW - 1) + (NAT_KW - 1)
    pick = (dc[None, :, :] == np.arange(2 * NAT_KW - 1)[:, None, None]).astype(np.float32)
    tiles = jnp.einsum('hrd,dqk->hrqk', rpb.astype(F32), jnp.asarray(pick),
                       precision=lax.Precision.HIGHEST)
    tiles = jnp.where(jnp.asarray(col_mask)[None, None], tiles, MASK_VALUE)
    t = jnp.stack([tiles[:, b:b + NAT_KH] for b in range(NAT_KH)], axis=0)
    return t.transpose(0, 1, 3, 2, 4).reshape(NAT_KH, N_HEADS * GRID_W, NAT_KH * GRID_W)


def _nat_kernel(q_ref, k_ref, v_ref, bias_ref, o_ref, *, rows_per_step, n_rows):
    j = pl.program_id(1)
    lane_head = _head_id((GRID_W, BR_W), 1)
    win = NAT_KH * GRID_W

    def body(i, carry):
        r = j * rows_per_step + i
        rs = jnp.clip(r - NAT_KH // 2, 0, n_rows - NAT_KH)
        base = rs - r + (NAT_KH - 1)
        q = q_ref[pl.ds(pl.multiple_of(i * GRID_W, GRID_W), GRID_W), :] * jnp.asarray(HEAD_DIM ** -0.5, BF16)
        kw = k_ref[pl.ds(pl.multiple_of(rs * GRID_W, GRID_W), win), :]
        vw = v_ref[pl.ds(pl.multiple_of(rs * GRID_W, GRID_W), win), :]
        q_heads = jnp.concatenate([jnp.where(lane_head == h, q, jnp.zeros_like(q)) for h in range(N_HEADS)], axis=0)
        s = _nt(q_heads, kw) + bias_ref[base]
        m = jnp.max(s, axis=-1, keepdims=True)
        pr = jnp.exp(s - m)
        l = jnp.sum(pr, axis=-1, keepdims=True)
        o = _dot(pr.astype(BF16), vw) / l
        out = jnp.zeros((GRID_W, BR_W), F32)
        for h in range(N_HEADS):
            out = out + jnp.where(lane_head == h, o[h * GRID_W:(h + 1) * GRID_W], 0.0)
        o_ref[pl.ds(pl.multiple_of(i * GRID_W, GRID_W), GRID_W), :] = out.astype(o_ref.dtype)
        return carry

    lax.fori_loop(0, rows_per_step, body, 0, unroll=True)


def neighbourhood_attention(p, rpb, batch, seq):
    n_rows = seq // GRID_W
    rps = NAT_ROWS_PER_STEP
    steps = n_rows // rps
    bias = _nat_bias_table(rpb)
    cb = COL_NAT // BR_W
    tq = rps * GRID_W
    return pl.pallas_call(
        functools.partial(_nat_kernel, rows_per_step=rps, n_rows=n_rows),
        grid=(batch, steps),
        in_specs=[pl.BlockSpec((tq, BR_W), lambda b, j: (b * steps + j, cb)),
                  pl.BlockSpec((seq, BR_W), lambda b, j: (b, cb + 1)),
                  pl.BlockSpec((seq, BR_W), lambda b, j: (b, cb + 2)),
                  pl.BlockSpec(bias.shape, lambda b, j: (0, 0, 0))],
        out_specs=pl.BlockSpec((tq, BR_W), lambda b, j: (b * steps + j, 0)),
        out_shape=jax.ShapeDtypeStruct((batch * seq, BR_W), BF16),
        compiler_params=_cparams("parallel", "arbitrary"),
        name="nat",
    )(p, p, p, bias)


def _diff_kernel(lam_ref, slope_ref, sub_ref, q_ref, k_ref, v_ref, o_ref,
                 acc1, acc2, m1, l1, m2, l2, hill, stage_a1, stage_a2, stage_b1, stage_b2, *, lam_init, n_tiles):
    t = q_ref.shape[0]
    reps = t // 128
    qi = pl.program_id(2)
    slope2 = slope_ref[0, 0:1, 0:1] * LOG2E
    lp = lam_ref[...]
    lam = (jnp.exp(jnp.sum(lp[0:1] * lp[1:2], axis=-1, keepdims=True))
           - jnp.exp(jnp.sum(lp[2:3] * lp[3:4], axis=-1, keepdims=True)) + lam_init)

    @pl.when(qi == 0)
    def _():
        rel = (lax.broadcasted_iota(jnp.int32, (t, t), 1) - lax.broadcasted_iota(jnp.int32, (t, t), 0)).astype(F32)
        hill[...] = -slope2 * jnp.abs(rel)

    q = (q_ref[...].astype(F32) * (DIFF_SCALE * LOG2E)).astype(BF16)
    lane = lax.broadcasted_iota(jnp.int32, q.shape, 1)
    q1 = jnp.where(lane < 64, q, jnp.zeros_like(q))
    q2 = jnp.where(lane >= 64, q, jnp.zeros_like(q))

    acc1[...] = jnp.zeros_like(acc1)
    acc2[...] = jnp.zeros_like(acc2)
    l1[...] = jnp.zeros_like(l1)
    l2[...] = jnp.zeros_like(l2)
    m1[...] = jnp.full_like(m1, -jnp.inf)
    m2[...] = jnp.full_like(m2, -jnp.inf)

    def scores(kj):
        k = k_ref[pl.ds(pl.multiple_of(kj * t, t), t), :]
        return _nt(q1, k), _nt(q2, k)

    def absorb(kj, raw_scores, add_local_bias, tile_bias):
        v = v_ref[pl.ds(pl.multiple_of(kj * t, t), t), :]
        for s_raw, acc, m_ref, l_ref in zip(raw_scores, (acc1, acc2), (m1, m2), (l1, l2)):
            s = add_local_bias(s_raw)
            m_old = m_ref[...]
            m_new = jnp.maximum(m_old, jnp.max(s, axis=-1, keepdims=True) + tile_bias)
            a = jnp.exp2(m_old - m_new)
            pr = jnp.exp2(s - jnp.concatenate([m_new - tile_bias] * reps, axis=1))
            l_ref[...] = a * l_ref[...] + jnp.sum(pr, axis=-1, keepdims=True)
            acc[...] = a * acc[...] + _dot(pr.astype(BF16), v)
            m_ref[...] = m_new

    stage = ((stage_a1, stage_a2), (stage_b1, stage_b2))
    query_term = slope2 * lax.broadcasted_iota(jnp.int32, (t, 128), 0).astype(F32)
    order = [qi] + [jnp.where(u < qi, u, u + 1) for u in range(n_tiles - 1)]
    for ref, val in zip(stage[0], scores(order[0])):
        ref[...] = val
    for u, kj in enumerate(order):
        cur, nxt = stage[u % 2], stage[(u + 1) % 2]
        if u + 1 < len(order):
            for ref, val in zip(nxt, scores(order[u + 1])):
                ref[...] = val
        raw = (cur[0][...], cur[1][...])
        if u == 0:
            diagonal = hill[...]
            absorb(kj, raw, lambda s: s + diagonal, jnp.zeros((1, 1), F32))
        else:
            sign = jnp.where(jnp.full((1, 1), kj, jnp.int32) < qi, 1.0, -1.0)
            keys = (sign * slope2) * ((kj - qi) * t + lax.broadcasted_iota(jnp.int32, (1, t), 1)).astype(F32)
            absorb(kj, raw, lambda s, keys=keys: s + keys, -sign * query_term)

    o = acc1[...] / l1[...] - lam * (acc2[...] / l2[...])
    o = o * lax.rsqrt(jnp.mean(o * o, axis=-1, keepdims=True) + EPS) * sub_ref[0]
    o_ref[...] = (o * (1.0 - lam_init)).astype(o_ref.dtype)


def diff_attention(p, lam_params, subln, layer, batch, seq):
    t = DIFF_TILE
    nt = seq // t
    lam_init = 0.8 - 0.6 * math.exp(-0.3 * layer)
    slopes = np.exp2(-8.0 * (np.arange(N_HEADS, dtype=np.float64) + 1.0) / N_HEADS)
    slope_tab = jnp.asarray(np.broadcast_to(slopes[:, None, None], (N_HEADS, 8, 128)).astype(np.float32))
    qc, kc, vc = (COL_DIFF // DIFF_DV, (COL_DIFF + DIFF_W) // DIFF_DV, (COL_DIFF + 2 * DIFF_W) // DIFF_DV)
    return pl.pallas_call(
        functools.partial(_diff_kernel, lam_init=lam_init, n_tiles=nt),
        grid=(batch, N_HEADS, nt),
        in_specs=[pl.BlockSpec(lam_params.shape, lambda b, h, i: (0, 0)),
                  pl.BlockSpec((1, 8, 128), lambda b, h, i: (h, 0, 0)),
                  pl.BlockSpec((1, 1, DIFF_DV), lambda b, h, i: (h, 0, 0)),
                  pl.BlockSpec((t, DIFF_DV), lambda b, h, i: (b * nt + i, qc + h)),
                  pl.BlockSpec((seq, DIFF_DV), lambda b, h, i: (b, kc + h)),
                  pl.BlockSpec((seq, DIFF_DV), lambda b, h, i: (b, vc + h))],
        out_specs=pl.BlockSpec((t, DIFF_DV), lambda b, h, i: (b * nt + i, h)),
        out_shape=jax.ShapeDtypeStruct((batch * seq, DIFF_W), BF16),
        scratch_shapes=[pltpu.VMEM((t, DIFF_DV), F32)] * 6 + [pltpu.VMEM((t, t), F32)] * 5,
        compiler_params=_cparams("parallel", "parallel", "arbitrary"),
        name="diff_attn",
    )(lam_params, slope_tab, subln.reshape(N_HEADS, 1, DIFF_DV), p, p, p)


def _merge_kernel(hf_ref, hb_ref, rof, rob, rg, ynat, hof, hob, hg, ydiff,
                  wg_ref, bg_ref, wr_ref, wn_ref, wh_ref, wd_ref, wo_ref,
                  rgn_ref, hgn_ref, lng_ref, lnb_ref, of_ref, ob_ref):
    seg_mean = jnp.where(_block_diag_mask(BR_W), 1.0 / HEAD_DIM, 0.0).astype(BF16)

    def head_norm_gate(o, gn, gate):
        ms = _dot_split(o * o, seg_mean, terms=2)
        gx = gate.astype(F32)
        return (o * lax.rsqrt(ms + EPS) * gn * (gx * _sigmoid(gx))).astype(BF16)

    y_ret = head_norm_gate(rof[...] + rob[...], rgn_ref[...], rg[...])
    y_hgrn = head_norm_gate(hof[...] + hob[...], hgn_ref[...], hg[...])
    hb = hb_ref[...]
    d = hf_ref.shape[1]
    merged = None
    for i, (y, w_ref) in enumerate(((y_ret, wr_ref), (ynat[...], wn_ref), (y_hgrn, wh_ref), (ydiff[...], wd_ref))):
        gate = _sigmoid(_dot(hb, wg_ref[:, i * d:(i + 1) * d]) + bg_ref[:, i * d:(i + 1) * d])
        part = gate * _dot(y, w_ref[...])
        merged = part if merged is None else merged + part
    z = ALPHA * hf_ref[...] + _dot(merged.astype(BF16), wo_ref[...])
    h = _layer_norm(z, lng_ref[...], lnb_ref[...])
    of_ref[...] = h
    ob_ref[...] = h.astype(BF16)


def merge_and_norm(hf, hb, p, ret_f, ret_b, y_nat, hg_f, hg_b, y_diff,
                   w_gate, b_gate, w_br_ret, w_br_nat, w_br_hgrn, w_br_diff, w_out,
                   ret_gn, hgrn_gn, ln_g, ln_b, tm=512):
    n, d = hf.shape

    def rows(width, col=0):
        return pl.BlockSpec((tm, width), lambda i: (i, col))

    def whole(a):
        nd = a.ndim
        return pl.BlockSpec(a.shape, lambda i: (0,) * nd)

    consts = [w_gate, b_gate.reshape(1, -1), w_br_ret, w_br_nat, w_br_hgrn, w_br_diff, w_out,
              ret_gn.reshape(1, BR_W), hgrn_gn.reshape(1, BR_W), ln_g.reshape(1, d), ln_b.reshape(1, d)]
    return pl.pallas_call(
        _merge_kernel,
        grid=(n // tm,),
        in_specs=[rows(d), rows(d),
                  rows(BR_W), rows(BR_W), rows(BR_W, COL_RET // BR_W + 3),
                  rows(BR_W),
                  rows(BR_W), rows(BR_W), rows(BR_W, COL_HGRN // BR_W + 4),
                  rows(DIFF_W)] + [whole(a) for a in consts],
        out_specs=[rows(d), rows(d)],
        out_shape=[jax.ShapeDtypeStruct((n, d), F32), jax.ShapeDtypeStruct((n, d), BF16)],
        compiler_params=_cparams("parallel"),
        name="merge",
    )(hf, hb, ret_f, ret_b, p, y_nat, hg_f, hg_b, p, y_diff, *consts)


def _route(h, w, bias):
    w_hi = w.astype(BF16)
    w_lo = (w - w_hi.astype(F32)).astype(BF16)
    logits = _dot_split(h, w_hi, terms=3) + _dot_split(h, w_lo, terms=2)
    scores = _sigmoid(logits)
    biased = scores + bias
    shape = scores.shape
    lane = lax.broadcasted_iota(jnp.int32, shape, 1).astype(F32)
    group = lax.shift_right_logical(lax.broadcasted_iota(jnp.int32, shape, 1),
                                    GROUP_SIZE.bit_length() - 1).astype(F32)
    big = jnp.asarray(1e9, F32)
    neg = jnp.asarray(-jnp.inf, F32)

    def first_max(x, ids):
        m = jnp.max(x, axis=-1, keepdims=True)
        return m, jnp.min(jnp.where(x == m, ids, big), axis=-1, keepdims=True)

    gscore = jnp.zeros(shape, F32)
    for g in range(N_GROUPS):
        xg = jnp.where(group == g, biased, neg)
        m1, i1 = first_max(xg, lane)
        m2 = jnp.max(jnp.where(lane == i1, neg, xg), axis=-1, keepdims=True)
        gscore = jnp.where(group == g, m1 + m2, gscore)
    gsel = jnp.zeros(shape, jnp.bool_)
    for _ in range(TOPK_GROUPS):
        _, gi = first_max(gscore, group)
        hit = group == gi
        gsel = gsel | hit
        gscore = jnp.where(hit, neg, gscore)
    cand = jnp.where(gsel, biased, MASK_VALUE)
    esel = jnp.zeros(shape, jnp.bool_)
    picks = []
    for _ in range(TOP_K):
        _, ei = first_max(cand, lane)
        hit = lane == ei
        picks.append((ei, jnp.sum(jnp.where(hit, scores, 0.0), axis=-1, keepdims=True)))
        esel = esel | hit
        cand = jnp.where(hit, neg, cand)

    tm = shape[0]
    sel = jnp.where(esel, 1.0, 0.0)
    counts = jnp.concatenate(
        [jnp.sum(sel[s * ROUTE_TILE:(s + 1) * ROUTE_TILE], axis=0, keepdims=True) for s in range(tm // ROUTE_TILE)],
        axis=0)

    slot = lax.broadcasted_iota(jnp.int32, (tm, TOP_K), 1)
    idx8 = jnp.zeros((tm, TOP_K), F32)
    w8 = jnp.zeros((tm, TOP_K), F32)
    wsum = jnp.zeros((tm, 1), F32)
    for k, (ei, wk) in enumerate(picks):
        idx8 = jnp.where(slot == k, ei, idx8)
        w8 = jnp.where(slot == k, wk, w8)
        wsum = wsum + wk
    return idx8.astype(jnp.int32), w8 / (wsum + 1e-20) * ROUTED_SCALE, counts


def _router_kernel(h_ref, w_ref, bias_ref, idx_ref, w8_ref, cnt_ref):
    idx_ref[...], w8_ref[...], cnt_ref[...] = _route(h_ref[...], w_ref[...], bias_ref[...])


def router(hf, w_router, router_bias, tm=1024):
    n, d = hf.shape
    e = w_router.shape[1]
    sub = tm // ROUTE_TILE
    return pl.pallas_call(
        _router_kernel,
        grid=(n // tm,),
        in_specs=[pl.BlockSpec((tm, d), lambda i: (i, 0)),
                  pl.BlockSpec((d, e), lambda i: (0, 0)),
                  pl.BlockSpec((1, e), lambda i: (0, 0))],
        out_specs=[pl.BlockSpec((tm, TOP_K), lambda i: (i, 0)),
                   pl.BlockSpec((tm, TOP_K), lambda i: (i, 0)),
                   pl.BlockSpec((sub, e), lambda i: (i, 0))],
        out_shape=[jax.ShapeDtypeStruct((n, TOP_K), jnp.int32),
                   jax.ShapeDtypeStruct((n, TOP_K), F32),
                   jax.ShapeDtypeStruct((n // ROUTE_TILE, e), F32)],
        compiler_params=_cparams("arbitrary"),
        name="router",
    )(hf, w_router, router_bias.reshape(1, e))


HIGH_HALF = -65536


def _pack_pairs(x):
    w = x.shape[1] // 2
    lo = lax.bitcast_convert_type(x[:, :w].astype(BF16).astype(F32), jnp.int32)
    hi = lax.bitcast_convert_type(x[:, w:].astype(BF16).astype(F32), jnp.int32)
    return lax.shift_right_logical(lo, 16) | (hi & HIGH_HALF)


def _unpack_pairs(p):
    lo = lax.bitcast_convert_type(lax.shift_left(p, 16), F32)
    hi = lax.bitcast_convert_type(p & HIGH_HALF, F32)
    return jnp.concatenate([lo, hi], axis=1)


def _swiglu(x, wg, wu):
    a = _dot(x, wg)
    return a * _sigmoid(a) * _dot(x, wu)


def _segment_copies(tab_ref, first_col, max_run, make_copy, action):
    top_bit = max_run.bit_length() - 1
    low_bit = RUN_ALIGN.bit_length() - 1
    mid_bit = min(top_bit + 1, LONG_RUN.bit_length() - 1)

    def pieces(count, local, glob, bits):
        for b in bits:
            size = 1 << b
            taken = count & size

            @pl.when(taken != 0)
            def _(local=local, glob=glob, size=size, b=b):
                action(make_copy(pl.multiple_of(local, RUN_ALIGN), pl.multiple_of(glob, RUN_ALIGN), size), b % 2)

            local = local + taken
            glob = glob + taken

    def per_expert(e, carry):
        col = first_col + e
        count = tab_ref[0, col]
        local = tab_ref[1, col]
        glob = tab_ref[2, col]
        long_part = count & -LONG_RUN

        @pl.when(long_part != 0)
        def _():
            pieces(count, local, glob, range(top_bit, mid_bit - 1, -1))

        pieces(count, local + long_part, glob + long_part, range(mid_bit - 1, low_bit - 1, -1))
        return carry

    lax.fori_loop(0, N_EXPERTS, per_expert, 0)


def _tile_rows_used(tab_ref, t):
    last = t * N_EXPERTS + (N_EXPERTS - 1)
    return tab_ref[1, last] + tab_ref[0, last]


def _wait_rows(total, max_rows, make_copy):
    for b in range(max_rows.bit_length() - 1, RUN_ALIGN.bit_length() - 2, -1):
        size = 1 << b

        @pl.when((total & size) != 0)
        def _(size=size):
            make_copy(0, 0, size).wait()


def _block_dispatch_kernel(tab_ref, gap_ref, tail_ref, idx_t_ref, hb_ref, xs_ref, sorted_buf, zeros_buf, sem):
    t = pl.program_id(0)
    tm = hb_ref.shape[0]
    rows = sorted_buf.shape[0]
    idx_t = idx_t_ref[...]
    expert = lax.broadcasted_iota(jnp.int32, (N_EXPERTS, tm), 0)
    sel_t = jnp.zeros((N_EXPERTS, tm), F32)
    for k in range(TOP_K):
        sel_t = sel_t + jnp.where(idx_t[k:k + 1, :] == expert, 1.0, 0.0)
    sel_b = sel_t.astype(BF16)
    before_tok = (lax.broadcasted_iota(jnp.int32, (tm, tm), 0) < lax.broadcasted_iota(jnp.int32, (tm, tm), 1))
    before_exp = (lax.broadcasted_iota(jnp.int32, (N_EXPERTS, N_EXPERTS), 1)
                  < lax.broadcasted_iota(jnp.int32, (N_EXPERTS, N_EXPERTS), 0))
    run = jnp.broadcast_to(_run_length(jnp.sum(sel_t, axis=1, keepdims=True)), (N_EXPERTS, tm)).astype(BF16)
    base = (_dot(jnp.where(before_exp, 1.0, 0.0).astype(BF16), run)
            + _dot(sel_b, jnp.where(before_tok, 1.0, 0.0).astype(BF16)))
    row = lax.broadcasted_iota(jnp.int32, (rows, tm), 0).astype(jnp.int16)
    perm = jnp.zeros((rows, tm), BF16)
    for k in range(TOP_K):
        place = jnp.sum(jnp.where(idx_t[k:k + 1, :] == expert, base, 0.0), axis=0, keepdims=True)
        perm = jnp.where(row == place.astype(jnp.int32).astype(jnp.int16), jnp.ones_like(perm), perm)
    half = hb_ref.shape[1] // 2
    lo = lax.bitcast_convert_type(_dot(perm, hb_ref[:, :half]), jnp.int32)
    hi = lax.bitcast_convert_type(_dot(perm, hb_ref[:, half:]), jnp.int32)
    sorted_buf[...] = lax.shift_right_logical(lo, 16) | (hi & HIGH_HALF)

    def make_copy(local, glob, size):
        return pltpu.make_async_copy(sorted_buf.at[pl.ds(local, size)], xs_ref.at[pl.ds(glob, size)], sem.at[0])

    _segment_copies(tab_ref, t * N_EXPERTS, tm, make_copy, lambda c, pr: c.start(priority=pr))
    _wait_rows(_tile_rows_used(tab_ref, t), rows, make_copy)

    @pl.when(t == pl.num_programs(0) - 1)
    def _():
        zeros_buf[...] = jnp.zeros_like(zeros_buf)
        tile_rows = zeros_buf.shape[0]

        def zero_copy(local, glob, size):
            return pltpu.make_async_copy(zeros_buf.at[pl.ds(0, size)], xs_ref.at[pl.ds(glob, size)], sem.at[0])

        def tail_copy(j):
            return zero_copy(0, pl.multiple_of(tail_ref[0] + j * tile_rows, RUN_ALIGN), tile_rows)

        _segment_copies(gap_ref, 0, tile_rows // 2, zero_copy, lambda c, pr: c.start(priority=pr))
        lax.fori_loop(0, tail_ref[1], lambda j, c: (tail_copy(j).start(), c)[1], 0)
        _segment_copies(gap_ref, 0, tile_rows // 2, zero_copy, lambda c, pr: c.wait())
        lax.fori_loop(0, tail_ref[1], lambda j, c: (tail_copy(j).wait(), c)[1], 0)


def block_dispatch(hb, idx_t, tab, gaps, tail, tm):
    n, d = hb.shape
    grid_spec = pltpu.PrefetchScalarGridSpec(
        num_scalar_prefetch=3,
        grid=(n // tm,),
        in_specs=[pl.BlockSpec((TOP_K, tm), lambda i, *_: (0, i)),
                  pl.BlockSpec((tm, d), lambda i, *_: (i, 0))],
        out_specs=pl.BlockSpec(memory_space=pl.ANY),
        scratch_shapes=[pltpu.VMEM((_tile_rows(tm), d // 2), jnp.int32),
                        pltpu.VMEM((EXPERT_ROWS, d // 2), jnp.int32),
                        pltpu.SemaphoreType.DMA((1,))],
    )
    return pl.pallas_call(
        _block_dispatch_kernel,
        grid_spec=grid_spec,
        out_shape=jax.ShapeDtypeStruct((_sorted_rows(n, tm), d // 2), jnp.int32),
        compiler_params=_cparams("arbitrary"),
        name="dispatch",
    )(tab, gaps, tail, idx_t, hb)


def _tile_tables(tile_counts, tm_fine, tm):
    g = tm // tm_fine
    n_tokens = tile_counts.shape[0] * tm_fine
    n_tiles = _sorted_rows(n_tokens, tm) // EXPERT_ROWS
    cnt = tile_counts.astype(jnp.int32).reshape(-1, g, N_EXPERTS).sum(axis=1)
    cnt = (cnt + (RUN_ALIGN - 1)) // RUN_ALIGN * RUN_ALIGN
    local = jnp.cumsum(cnt, axis=1) - cnt
    used = cnt.sum(axis=0)
    seg = (used + (EXPERT_ROWS - 1)) // EXPERT_ROWS * EXPERT_ROWS
    seg_end = jnp.cumsum(seg)
    seg_start = seg_end - seg
    glob = seg_start[None, :] + jnp.cumsum(cnt, axis=0) - cnt
    runs = jnp.stack([cnt.reshape(-1), local.reshape(-1), glob.reshape(-1)])
    gaps = jnp.stack([seg - used, jnp.zeros_like(used), seg_start + used])
    tail = jnp.stack([seg_end[-1], n_tiles - seg_end[-1] // EXPERT_ROWS])
    tile_row = jnp.arange(n_tiles, dtype=jnp.int32) * EXPERT_ROWS
    owner = jnp.sum((seg_end[None, :] <= tile_row[:, None]).astype(jnp.int32), axis=1)
    owner_c = jnp.minimum(owner, N_EXPERTS - 1)
    of_owner = lambda a: jnp.sum(jnp.where(jnp.arange(N_EXPERTS)[None, :] == owner_c[:, None], a[None, :], 0), axis=1)
    filled = jnp.clip(of_owner(seg_start + used) - tile_row, 0, EXPERT_ROWS)
    source = jnp.minimum(jnp.arange(n_tiles, dtype=jnp.int32), jnp.maximum(seg_end[-1] // EXPERT_ROWS - 1, 0))
    tiles = jnp.stack([owner_c, jnp.where(owner < N_EXPERTS, filled, 0), source])
    return runs, gaps, tail, tiles.astype(jnp.int32)


def _expert_kernel(tile_ref, xs_ref, wg_ref, wu_ref, wd_ref, ys_ref, wgb, wub, wdb):
    i = pl.program_id(0)
    expert = tile_ref[0, i]

    @pl.when(tile_ref[1, i] > 0)
    def _():
        @pl.when((i == 0) | (expert != tile_ref[0, jnp.maximum(i - 1, 0)]))
        def _():
            wgb[...] = wg_ref[0, 0].astype(BF16)
            wub[...] = wu_ref[0, 0].astype(BF16)
            wdb[...] = wd_ref[0, 0].astype(BF16)

        x = _unpack_pairs(xs_ref[...]).astype(BF16)
        hid = _swiglu(x, wgb[...], wub[...])
        ys_ref[...] = _pack_pairs(_dot(hid.astype(BF16), wdb[...]))

    @pl.when(tile_ref[1, i] == 0)
    def _():
        ys_ref[...] = jnp.zeros_like(ys_ref)


def grouped_experts(xs, tiles, w_e_gate, w_e_up, w_e_down, layer):
    rows, w = xs.shape
    _, _, d, hid = w_e_gate.shape
    tr = EXPERT_ROWS
    grid_spec = pltpu.PrefetchScalarGridSpec(
        num_scalar_prefetch=1,
        grid=(rows // tr,),
        in_specs=[pl.BlockSpec((tr, w), lambda i, tl: (tl[2, i], 0)),
                  pl.BlockSpec((1, 1, d, hid), lambda i, tl: (layer, tl[0, i], 0, 0)),
                  pl.BlockSpec((1, 1, d, hid), lambda i, tl: (layer, tl[0, i], 0, 0)),
                  pl.BlockSpec((1, 1, hid, d), lambda i, tl: (layer, tl[0, i], 0, 0))],
        out_specs=pl.BlockSpec((tr, w), lambda i, tl: (i, 0)),
        scratch_shapes=[pltpu.VMEM((d, hid), BF16), pltpu.VMEM((d, hid), BF16), pltpu.VMEM((hid, d), BF16)],
    )
    return pl.pallas_call(
        _expert_kernel,
        grid_spec=grid_spec,
        out_shape=jax.ShapeDtypeStruct((rows, w), jnp.int32),
        compiler_params=_cparams("arbitrary"),
        name="experts",
    )(tiles, xs, w_e_gate, w_e_up, w_e_down)


def _block_combine_kernel(tab_ref, ys_ref, idx_ref, w8_ref, hf_ref, hb_ref, sg_ref, su_ref, sd_ref,
                          lng_ref, lnb_ref, of_ref, ob_ref, buf, sem):
    t = pl.program_id(0)
    tm = hf_ref.shape[0]
    rows = buf.shape[0]

    def make_copy(local, glob, size):
        return pltpu.make_async_copy(ys_ref.at[pl.ds(glob, size)], buf.at[pl.ds(local, size)], sem.at[0])

    @pl.when(t == 0)
    def _():
        buf[...] = jnp.zeros_like(buf)

    _segment_copies(tab_ref, t * N_EXPERTS, tm, make_copy, lambda c, pr: c.start(priority=pr))

    idx = idx_ref[...]
    w8 = w8_ref[...]
    expert = lax.broadcasted_iota(jnp.int32, (tm, N_EXPERTS), 1)
    sel = jnp.zeros((tm, N_EXPERTS), F32)
    for k in range(TOP_K):
        sel = sel + jnp.where(idx[:, k:k + 1] == expert, 1.0, 0.0)
    sel_b = sel.astype(BF16)
    before_tok = (lax.broadcasted_iota(jnp.int32, (tm, tm), 1) < lax.broadcasted_iota(jnp.int32, (tm, tm), 0))
    before_exp = (lax.broadcasted_iota(jnp.int32, (N_EXPERTS, N_EXPERTS), 0)
                  < lax.broadcasted_iota(jnp.int32, (N_EXPERTS, N_EXPERTS), 1))
    run = jnp.broadcast_to(_run_length(jnp.sum(sel, axis=0, keepdims=True)), (tm, N_EXPERTS)).astype(BF16)
    base = (_dot(run, jnp.where(before_exp, 1.0, 0.0).astype(BF16))
            + _dot(jnp.where(before_tok, 1.0, 0.0).astype(BF16), sel_b))
    col = lax.broadcasted_iota(jnp.int32, (tm, rows), 1).astype(jnp.int16)
    w8_b = w8.astype(BF16)
    mix = jnp.zeros((tm, rows), BF16)
    for k in range(TOP_K):
        place = jnp.sum(jnp.where(idx[:, k:k + 1] == expert, base, 0.0), axis=1, keepdims=True)
        mix = jnp.where(col == place.astype(jnp.int32).astype(jnp.int16),
                        jnp.broadcast_to(w8_b[:, k:k + 1], mix.shape), mix)

    acc = _dot(_swiglu(hb_ref[...], sg_ref[...], su_ref[...]).astype(BF16), sd_ref[...])
    _wait_rows(_tile_rows_used(tab_ref, t), rows, make_copy)
    y = _unpack_pairs(buf[...]).astype(BF16)
    acc = acc + _dot(mix, y)
    h = _layer_norm(ALPHA * hf_ref[...] + acc, lng_ref[...], lnb_ref[...])
    of_ref[...] = h
    ob_ref[...] = h.astype(BF16)


def block_combine(ys, tab, idx8, w8, hf, hb, w_s_gate, w_s_up, w_s_down, ln_g, ln_b, tm):
    n, d = hf.shape
    w = ys.shape[1]

    def whole(a):
        return pl.BlockSpec(a.shape, lambda i, tab: (0, 0))

    def rows(width):
        return pl.BlockSpec((tm, width), lambda i, tab: (i, 0))

    consts = [w_s_gate, w_s_up, w_s_down, ln_g.reshape(1, d), ln_b.reshape(1, d)]
    grid_spec = pltpu.PrefetchScalarGridSpec(
        num_scalar_prefetch=1,
        grid=(n // tm,),
        in_specs=[pl.BlockSpec(memory_space=pl.ANY), rows(TOP_K), rows(TOP_K), rows(d), rows(d)]
                 + [whole(a) for a in consts],
        out_specs=[rows(d), rows(d)],
        scratch_shapes=[pltpu.VMEM((_tile_rows(tm), w), jnp.int32), pltpu.SemaphoreType.DMA((1,))],
    )
    return pl.pallas_call(
        _block_combine_kernel,
        grid_spec=grid_spec,
        out_shape=[jax.ShapeDtypeStruct((n, d), F32), jax.ShapeDtypeStruct((n, d), BF16)],
        compiler_params=_cparams("arbitrary"),
        name="combine",
    )(tab, ys, idx8, w8, hf, hb, *consts)


def moe_and_norm(hf, hb, w_router, router_bias, w_e_gate, w_e_up, w_e_down,
                 w_s_gate, w_s_up, w_s_down, ln_g, ln_b, layer):
    idx8, w8, tile_counts = router(hf, w_router, router_bias)
    runs, gaps, tail, tiles = _tile_tables(tile_counts, ROUTE_TILE, MOE_TILE)
    xs = block_dispatch(hb, idx8.T, runs, gaps, tail, MOE_TILE)
    ys = grouped_experts(xs, tiles, w_e_gate, w_e_up, w_e_down, layer)
    return block_combine(ys, runs, idx8, w8, hf, hb, w_s_gate, w_s_up, w_s_down, ln_g, ln_b, MOE_TILE)


def kernel(x, ln_in_g, ln_in_b, w_in, w_gate, b_gate, w_br_ret, w_br_nat, w_br_hgrn, w_br_diff, w_out,
           ret_gn, nat_rpb, hgrn_lb, hgrn_gn, diff_lambda, diff_subln, ln1_g, ln1_b, w_router,
           router_bias, w_e_gate, w_e_up, w_e_down, w_s_gate, w_s_up, w_s_down, ln2_g, ln2_b):
    batch, seq, d = x.shape
    bf = lambda a: a.astype(BF16)
    hf, hb = input_layer_norm(x.reshape(batch * seq, d), ln_in_g, ln_in_b)
    for l in range(DEPTH):
        p = in_projection(hb, bf(w_in[l]))
        ret_f, ret_b = retention(p, batch, seq)
        y_nat = neighbourhood_attention(p, nat_rpb[l], batch, seq)
        hg_f, hg_b = hgrn(p, hgrn_lb, l, batch, seq)
        y_diff = diff_attention(p, diff_lambda[l], diff_subln[l], l, batch, seq)
        hf, hb = merge_and_norm(hf, hb, p, ret_f, ret_b, y_nat, hg_f, hg_b, y_diff,
                                bf(w_gate[l]), b_gate[l], bf(w_br_ret[l]), bf(w_br_nat[l]),
                                bf(w_br_hgrn[l]), bf(w_br_diff[l]), bf(w_out[l]),
                                ret_gn[l], hgrn_gn[l], ln1_g[l], ln1_b[l])
        hf, hb = moe_and_norm(hf, hb, w_router[l], router_bias[l], w_e_gate, w_e_up, w_e_down,
                              bf(w_s_gate[l]), bf(w_s_up[l]), bf(w_s_down[l]), ln2_g[l], ln2_b[l], l)
    return hf.reshape(batch, seq, d)
```
